```python
import jax
import jax.numpy as jnp
from jax import lax
import numpy as np

D_MODEL = 1024
BATCH = 2
SEQ = 8192
DEPTH = 1
DEC_BATCH = 32
DEC_SEQ = 4
PAST_LEN = 8192
PAGE_SIZE = 128

R_HEADS = 8
R_HEAD = 64
R_WIDTH = R_HEADS * R_HEAD
LORA_W = 64
LORA_A = 64
LORA_G = 128
R_COLS = 3 * R_WIDTH + LORA_W + LORA_A + LORA_G
GN_EPS = 64e-5

N_HEADS = 8
N_KV = 2
HPG = N_HEADS // N_KV
HEAD_DIM = 64
A_WIDTH = N_HEADS * HEAD_DIM
KV_WIDTH = N_KV * HEAD_DIM
A_COLS = A_WIDTH + 6 * KV_WIDTH + 3 * N_HEADS
ROT_DIM = HEAD_DIM // 4
ROPE_THETA = 500000.0
CMP_STRIDE = 16
CMP_LEN = 2 * CMP_STRIDE
CMP_HIDDEN = 256
SEL_BLOCK = 64
SEL_TOP = 16
WINDOW = 512
Q_BLOCK = 128

IN_COLS = R_COLS + A_COLS + 2 * D_MODEL

N_EXPERTS = 32
TOP_K = 4
D_FF = 1024
SWIGLU_LIMIT = 7.0
SWIGLU_ALPHA = 1.702
EXPERT_BLOCK = 128

DN_ALPHA = (2 * DEPTH) ** 0.25
DN_BETA = (8 * DEPTH) ** -0.25
LN_EPS = 1e-5
NEG = -1e30

kernel_name = 'rwkv7_nsa_gated_moe_decoder_step'


def layer_norm(x, g, b):
    xf = x.astype(jnp.float32)
    mu = xf.mean(-1, keepdims=True)
    var = jnp.square(xf - mu).mean(-1, keepdims=True)
    return ((xf - mu) * lax.rsqrt(var + LN_EPS) * g + b).astype(x.dtype)


def rope(x, pos):
    half = ROT_DIM // 2
    inv = ROPE_THETA ** (-jnp.arange(half, dtype=jnp.float32) * 2.0 / ROT_DIM)
    ang = pos.astype(jnp.float32)[:, None] * inv
    cos, sin = jnp.cos(ang)[:, None, :], jnp.sin(ang)[:, None, :]
    xf = x.astype(jnp.float32)
    x1, x2 = xf[..., :half], xf[..., half:ROT_DIM]
    out = jnp.concatenate([x1 * cos - x2 * sin, x2 * cos + x1 * sin, xf[..., ROT_DIM:]], axis=-1)
    return out.astype(x.dtype)


def project(x, w_in, pos):
    B, T, _ = x.shape
    p = x @ w_in
    pr = p[..., :R_COLS]
    pa = p[..., R_COLS:R_COLS + A_COLS]
    mg = p[..., R_COLS + A_COLS:]
    q = rope(pa[..., :A_WIDTH].reshape(B, T, N_HEADS, HEAD_DIM), pos)
    kvs = [pa[..., A_WIDTH + i * KV_WIDTH:A_WIDTH + (i + 1) * KV_WIDTH].reshape(B, T, N_KV, HEAD_DIM)
           for i in range(6)]
    kvs = [rope(z, pos) if i % 2 == 0 else z for i, z in enumerate(kvs)]
    gates = jax.nn.sigmoid(pa[..., A_WIDTH + 6 * KV_WIDTH:]).reshape(B, T, N_KV, HPG, 3)
    return pr, q, kvs, gates, mg


def wkv_scan(s0, r, w, k, v, a, b):
    def step(s, inp):
        r_t, w_t, k_t, v_t, a_t, b_t = inp
        sa = jnp.einsum('bhij,bhj->bhi', s, a_t)
        s = s * w_t[:, :, None, :] + sa[..., None] * b_t[:, :, None, :] + v_t[..., None] * k_t[:, :, None, :]
        return s, jnp.einsum('bhij,bhj->bhi', s, r_t)
    xs = tuple(jnp.swapaxes(z, 0, 1) for z in (r, w, k, v, a, b))
    s, ys = lax.scan(step, s0, xs)
    return jnp.swapaxes(ys, 0, 1), s


def rwkv_mixer(pr, shift_prev, wkv0, mu, w0, w_w2, a0, w_a2, g_w2, k_k, k_a, r_k, gn_g, gn_b):
    B, T, _ = pr.shape
    prev = jnp.concatenate([shift_prev[:, None, :].astype(pr.dtype), pr[:, :-1]], axis=1)
    xm = (pr + (prev - pr) * mu).astype(jnp.float32)
    o1, o2, o3 = R_WIDTH, 2 * R_WIDTH, 3 * R_WIDTH
    o4 = o3 + LORA_W
    o5 = o4 + LORA_A
    r, k, v = xm[..., :o1], xm[..., o1:o2], xm[..., o2:o3]
    xw, xa, xg = xm[..., o3:o4], xm[..., o4:o5], xm[..., o5:]
    w_log = -jax.nn.softplus(-(w0 + jnp.tanh(xw) @ w_w2)) - 0.5
    a = jax.nn.sigmoid(a0 + xa @ w_a2)
    g = jax.nn.sigmoid(xg) @ g_w2
    heads = lambda z: z.reshape(B, T, R_HEADS, R_HEAD)
    kk = heads(k * k_k)
    kk = kk / jnp.maximum(jnp.linalg.norm(kk, axis=-1, keepdims=True), 1e-12)
    k_h = heads(k * (1.0 + (a - 1.0) * k_a))
    r_h, v_h, a_h = heads(r), heads(v), heads(a)
    decay = jnp.exp(-jnp.exp(heads(w_log)))
    y, wkv = wkv_scan(wkv0.astype(jnp.float32), r_h, decay, k_h, v_h, -kk, kk * a_h)
    mean = y.mean(-1, keepdims=True)
    var = jnp.square(y - mean).mean(-1, keepdims=True)
    yn = ((y - mean) * lax.rsqrt(var + GN_EPS)).reshape(B, T, R_WIDTH) * gn_g + gn_b
    bonus = (jnp.sum(r_h * k_h * r_k, axis=-1, keepdims=True) * v_h).reshape(B, T, R_WIDTH)
    out = ((yn + bonus) * g).astype(pr.dtype)
    return out, wkv.astype(pr.dtype), pr[:, -1]


def compress(kv, pe, w1, b1, w2, b2):
    B, L = kv.shape[:2]
    n_chunk = L // CMP_STRIDE
    ch = kv[:, :n_chunk * CMP_STRIDE].reshape(B, n_chunk, CMP_STRIDE, N_KV, HEAD_DIM)
    blk = jnp.concatenate([ch[:, :-1], ch[:, 1:]], axis=2) + pe[:, None, :]
    blk = jnp.transpose(blk, (0, 3, 1, 2, 4)).reshape(B, N_KV, n_chunk - 1, CMP_LEN * HEAD_DIM)
    return jax.nn.gelu(blk @ w1 + b1) @ w2 + b2


def nsa_attend(q, q_pos, kc, vc, kb, vb, kw, vw, w_pos, gates):
    f32 = jnp.float32
    B, Tq = q.shape[:2]
    NC, NS = kc.shape[2], kb.shape[2]
    qf = q.astype(f32) * (HEAD_DIM ** -0.5)
    c_start = jnp.arange(NC) * CMP_STRIDE
    c_ok = (c_start + CMP_LEN - 1)[None, :] <= q_pos[:, None]
    s_c = jnp.einsum('bqghd,bgnd->bqghn', qf, kc.astype(f32))
    s_c = jnp.where(c_ok[None, :, None, None], s_c, NEG)
    p_c = jax.nn.softmax(s_c, axis=-1) * c_ok.any(-1)[None, :, None, None, None]
    o_c = jnp.einsum('bqghn,bgnd->bqghd', p_c, vc.astype(f32))
    s_start = jnp.arange(NS) * SEL_BLOCK
    cover = ((c_start[:, None] < s_start[None, :] + SEL_BLOCK)
             & (c_start[:, None] + CMP_LEN > s_start[None, :])).astype(f32)
    imp = jnp.einsum('bqghn,ns->bqgs', p_c, cover)
    blk = jnp.arange(NS)
    cur = (q_pos // SEL_BLOCK)[:, None]
    forced = (blk[None] == 0) | (blk[None] == cur) | (blk[None] == cur - 1)
    causal = s_start[None] <= q_pos[:, None]
    score = jnp.where(forced[None, :, None], 1e6, imp)
    score = jnp.where(causal[None, :, None], score, NEG)
    top_s, sel = lax.top_k(score, min(SEL_TOP, NS))
    valid = top_s > 0.5 * NEG
    bi = jnp.arange(B)[:, None, None, None]
    gi = jnp.arange(N_KV)[None, None, :, None]
    kg = kb[bi, gi, sel].astype(f32)
    vg = vb[bi, gi, sel].astype(f32)
    s_s = jnp.einsum('bqghd,bqgnsd->bqghns', qf, kg)
    k_pos = sel[..., None] * SEL_BLOCK + jnp.arange(SEL_BLOCK)
    s_ok = (k_pos <= q_pos[None, :, None, None, None]) & valid[..., None]
    s_s = jnp.where(s_ok[:, :, :, None], s_s, NEG)
    p_s = jax.nn.softmax(s_s.reshape(B, Tq, N_KV, HPG, -1), axis=-1).reshape(s_s.shape)
    o_s = jnp.einsum('bqghns,bqgnsd->bqghd', p_s, vg)
    w_ok = ((w_pos[None] <= q_pos[:, None]) & (w_pos[None] >= q_pos[:, None] - WINDOW)
            & (w_pos[None] >= 0))
    s_w = jnp.einsum('bqghd,bkgd->bqghk', qf, kw.astype(f32))
    s_w = jnp.where(w_ok[None, :, None, None], s_w, NEG)
    o_w = jnp.einsum('bqghk,bkgd->bqghd', jax.nn.softmax(s_w, axis=-1), vw.astype(f32))
    g = gates.astype(f32)
    return g[..., 0:1] * o_c + g[..., 1:2] * o_s + g[..., 2:3] * o_w


def nsa_prompt(q, kvs, gates, pe, w1, b1, w2, b2):
    kc_raw, vc_raw, ks, vs, kw, vw = kvs
    B, T = q.shape[:2]
    kc = compress(kc_raw, pe[0], w1[0], b1[0], w2[0], b2[0])
    vc = compress(vc_raw, pe[1], w1[1], b1[1], w2[1], b2[1])
    ns = T // SEL_BLOCK
    to_blocks = lambda z: jnp.transpose(z.reshape(B, ns, SEL_BLOCK, N_KV, HEAD_DIM), (0, 3, 1, 2, 4))
    kb, vb = to_blocks(ks), to_blocks(vs)
    pad = jnp.zeros((B, WINDOW, N_KV, HEAD_DIM), kw.dtype)
    kw_p = jnp.concatenate([pad, kw], axis=1)
    vw_p = jnp.concatenate([pad, vw], axis=1)
    qg = q.reshape(B, T, N_KV, HPG, HEAD_DIM)
    span = WINDOW + Q_BLOCK

    def query_block(i):
        s = i * Q_BLOCK
        sl = lambda z, n: lax.dynamic_slice_in_dim(z, s, n, axis=1)
        return nsa_attend(sl(qg, Q_BLOCK), s + jnp.arange(Q_BLOCK), kc, vc, kb, vb,
                          sl(kw_p, span), sl(vw_p, span), s - WINDOW + jnp.arange(span),
                          sl(gates, Q_BLOCK))

    o = lax.map(query_block, jnp.arange(T // Q_BLOCK))
    return jnp.moveaxis(o, 0, 1).reshape(B, T, A_WIDTH).astype(q.dtype)


def nsa_sample(q, kvs, gates, cmp_cache, sel_cache, win_cache, page_table, pe, w1, b1, w2, b2):
    kc_new, vc_new, ks_new, vs_new, kw_new, vw_new = kvs
    DB, DS = q.shape[:2]
    past = page_table.shape[1] * PAGE_SIZE
    total = past + DS

    def full_rows(cache, k_new, v_new):
        rows = cache[page_table].reshape(DB, past, 2, N_KV, HEAD_DIM)
        return (jnp.concatenate([rows[:, :, 0], k_new], axis=1),
                jnp.concatenate([rows[:, :, 1], v_new], axis=1))

    kc_all, vc_all = full_rows(cmp_cache, kc_new, vc_new)
    ks_all, vs_all = full_rows(sel_cache, ks_new, vs_new)
    kc = compress(kc_all, pe[0], w1[0], b1[0], w2[0], b2[0])
    vc = compress(vc_all, pe[1], w1[1], b1[1], w2[1], b2[1])
    ns = -(-total // SEL_BLOCK)
    padn = ns * SEL_BLOCK - total
    to_blocks = lambda z: jnp.transpose(
        jnp.pad(z, ((0, 0), (0, padn), (0, 0), (0, 0))).reshape(DB, ns, SEL_BLOCK, N_KV, HEAD_DIM),
        (0, 3, 1, 2, 4))
    wb = win_cache.shape[1]
    kw_all = jnp.concatenate([win_cache[:, :, 0], kw_new], axis=1)
    vw_all = jnp.concatenate([win_cache[:, :, 1], vw_new], axis=1)
    o = nsa_attend(q.reshape(DB, DS, N_KV, HPG, HEAD_DIM), past + jnp.arange(DS), kc, vc,
                   to_blocks(ks_all), to_blocks(vs_all), kw_all, vw_all,
                   past - wb + jnp.arange(wb + DS), gates)
    new_win = jnp.stack([kw_all[:, DS:], vw_all[:, DS:]], axis=2)
    return o.reshape(DB, DS, A_WIDTH).astype(q.dtype), new_win


def moe(x, router_w, router_b, mlp1_w, mlp1_b, mlp2_w, mlp2_b):
    T, D = x.shape
    logits = (x @ router_w + router_b).astype(jnp.float32)
    top_v, top_e = lax.top_k(logits, TOP_K)
    gate = jax.nn.softmax(top_v, axis=-1)
    n_assign = T * TOP_K
    e_flat = top_e.reshape(-1)
    order = jnp.argsort(e_flat)
    e_sorted = e_flat[order]
    tok_sorted = (order // TOP_K).astype(jnp.int32)
    gate_sorted = gate.reshape(-1)[order]
    counts = jnp.bincount(e_flat, length=N_EXPERTS)
    padded = (counts + EXPERT_BLOCK - 1) // EXPERT_BLOCK * EXPERT_BLOCK
    start = jnp.cumsum(counts) - counts
    pend = jnp.cumsum(padded)
    pstart = pend - padded
    dest = pstart[e_sorted] + jnp.arange(n_assign) - start[e_sorted]
    n_blocks = -(-n_assign // EXPERT_BLOCK) + N_EXPERTS
    n_rows = n_blocks * EXPERT_BLOCK
    row_tok = jnp.full((n_rows,), T, jnp.int32).at[dest].set(tok_sorted)
    row_gate = jnp.zeros((n_rows,), jnp.float32).at[dest].set(gate_sorted)
    block_e = jnp.minimum(jnp.searchsorted(pend, jnp.arange(n_blocks) * EXPERT_BLOCK, side='right'),
                          N_EXPERTS - 1)
    xb = jnp.concatenate([x, jnp.zeros((1, D), x.dtype)])[row_tok].reshape(n_blocks, EXPERT_BLOCK, D)

    def expert_block(args):
        xe, e = args
        h = xe @ mlp1_w[e] + mlp1_b[e]
        glu = jnp.minimum(h[:, :D_FF], SWIGLU_LIMIT)
        lin = jnp.clip(h[:, D_FF:], -SWIGLU_LIMIT, SWIGLU_LIMIT)
        return (glu * jax.nn.sigmoid(SWIGLU_ALPHA * glu) * (lin + 1.0)) @ mlp2_w[e] + mlp2_b[e]

    yb = lax.map(expert_block, (xb, block_e)).reshape(n_rows, D)
    y = jnp.zeros((T + 1, D), jnp.float32).at[row_tok].add(yb.astype(jnp.float32) * row_gate[:, None])
    return y[:T].astype(x.dtype)


def merge_and_ffn(x, y_r, y_a, mg, w_pa, w_pb, w_o, ln1_g, ln1_b, router_w, router_b,
                  mlp1_w, mlp1_b, mlp2_w, mlp2_b, ln2_g, ln2_b):
    m = (jax.nn.sigmoid(mg[..., :D_MODEL]) * (y_r @ w_pa)
         + jax.nn.sigmoid(mg[..., D_MODEL:]) * (y_a @ w_pb))
    h = layer_norm(DN_ALPHA * x + m @ w_o, ln1_g, ln1_b)
    B, T, D = h.shape
    f = moe(h.reshape(B * T, D), router_w, router_b, mlp1_w, mlp1_b, mlp2_w, mlp2_b).reshape(B, T, D)
    return layer_norm(DN_ALPHA * h + f, ln2_g, ln2_b)


def setup_inputs(seed: int = 0) -> dict:
    key = jax.random.key(seed)
    keys = iter(jax.random.split(key, 48))

    def nrm(shape, scale):
        return jax.random.normal(next(keys), shape, jnp.float32) * scale

    L = DEPTH
    n_pages = PAST_LEN // PAGE_SIZE
    used = DEC_BATCH * n_pages
    n_pool = used + max(1, used // 4)
    win_buf = min(WINDOW, PAST_LEN)
    page_table = jax.random.permutation(next(keys), n_pool)[:used].reshape(DEC_BATCH, n_pages).astype(jnp.int32)
    kv_shape = (L, n_pool, PAGE_SIZE, 2, N_KV, HEAD_DIM)
    return {
        'x_prompt': nrm((BATCH, SEQ, D_MODEL), 1.0),
        'x_sample': nrm((DEC_BATCH, DEC_SEQ, D_MODEL), 1.0),
        'cache_cmp_kv': nrm(kv_shape, 1.0),
        'cache_sel_kv': nrm(kv_shape, 1.0),
        'cache_win_kv': nrm((L, DEC_BATCH, win_buf, 2, N_KV, HEAD_DIM), 1.0),
        'state_wkv': nrm((L, DEC_BATCH, R_HEADS, R_HEAD, R_HEAD), 0.5),
        'state_shift': nrm((L, DEC_BATCH, R_COLS), 1.0),
        'page_table': page_table,
        'w_in': nrm((L, D_MODEL, IN_COLS), D_MODEL ** -0.5),
        'mu_shift': jax.random.uniform(next(keys), (L, R_COLS), jnp.float32),
        'w0': nrm((L, R_WIDTH), 0.5) - 1.0,
        'w_w2': nrm((L, LORA_W, R_WIDTH), LORA_W ** -0.5),
        'a0': nrm((L, R_WIDTH), 0.5),
        'w_a2': nrm((L, LORA_A, R_WIDTH), LORA_A ** -0.5),
        'g_w2': nrm((L, LORA_G, R_WIDTH), LORA_G ** -0.5),
        'k_k': 0.85 + nrm((L, R_WIDTH), 0.05),
        'k_a': 1.0 + nrm((L, R_WIDTH), 0.05),
        'r_k': nrm((L, R_HEADS, R_HEAD), 0.1),
        'gn_g': 1.0 + nrm((L, R_WIDTH), 0.05),
        'gn_b': nrm((L, R_WIDTH), 0.01),
        'cmp_pe': nrm((L, 2, CMP_LEN, HEAD_DIM), 0.1),
        'cmp_w1': nrm((L, 2, CMP_LEN * HEAD_DIM, CMP_HIDDEN), (CMP_LEN * HEAD_DIM) ** -0.5),
        'cmp_b1': nrm((L, 2, CMP_HIDDEN), 0.01),
        'cmp_w2': nrm((L, 2, CMP_HIDDEN, HEAD_DIM), CMP_HIDDEN ** -0.5),
        'cmp_b2': nrm((L, 2, HEAD_DIM), 0.01),
        'w_pa': nrm((L, R_WIDTH, D_MODEL), R_WIDTH ** -0.5),
        'w_pb': nrm((L, A_WIDTH, D_MODEL), A_WIDTH ** -0.5),
        'w_o': nrm((L, D_MODEL, D_MODEL), DN_BETA * D_MODEL ** -0.5),
        'ln1_g': 1.0 + nrm((L, D_MODEL), 0.05),
        'ln1_b': nrm((L, D_MODEL), 0.01),
        'router_w': nrm((L, D_MODEL, N_EXPERTS), D_MODEL ** -0.5),
        'router_b': nrm((L, N_EXPERTS), 0.01),
        'mlp1_w': nrm((L, N_EXPERTS, D_MODEL, 2 * D_FF), D_MODEL ** -0.5),
        'mlp1_b': nrm((L, N_EXPERTS, 2 * D_FF), 0.01),
        'mlp2_w': nrm((L, N_EXPERTS, D_FF, D_MODEL), DN_BETA * D_FF ** -0.5),
        'mlp2_b': nrm((L, N_EXPERTS, D_MODEL), 0.01),
        'ln2_g': 1.0 + nrm((L, D_MODEL), 0.05),
        'ln2_b': nrm((L, D_MODEL), 0.01),
    }


def reference(x_prompt, x_sample, cache_cmp_kv, cache_sel_kv, cache_win_kv, state_wkv, state_shift,
              page_table, w_in, mu_shift, w0, w_w2, a0, w_a2, g_w2, k_k, k_a, r_k, gn_g, gn_b,
              cmp_pe, cmp_w1, cmp_b1, cmp_w2, cmp_b2, w_pa, w_pb, w_o, ln1_g, ln1_b,
              router_w, router_b, mlp1_w, mlp1_b, mlp2_w, mlp2_b, ln2_g, ln2_b):
    B, T, _ = x_prompt.shape
    DB, DS, _ = x_sample.shape
    past = page_table.shape[1] * PAGE_SIZE
    pos_p = jnp.arange(T)
    pos_s = past + jnp.arange(DS)
    wb_p = min(WINDOW, T)
    hp, hs = x_prompt, x_sample
    cmp_p, sel_p, win_p, wkv_p, shift_p = [], [], [], [], []
    cmp_s, sel_s, win_s, wkv_s, shift_s = [], [], [], [], []
    for l in range(DEPTH):
        rwkv_w = (mu_shift[l], w0[l], w_w2[l], a0[l], w_a2[l], g_w2[l], k_k[l], k_a[l], r_k[l],
                  gn_g[l], gn_b[l])
        cmp_w = (cmp_pe[l], cmp_w1[l], cmp_b1[l], cmp_w2[l], cmp_b2[l])
        out_w = (w_pa[l], w_pb[l], w_o[l], ln1_g[l], ln1_b[l], router_w[l], router_b[l],
                 mlp1_w[l], mlp1_b[l], mlp2_w[l], mlp2_b[l], ln2_g[l], ln2_b[l])
        pr, q, kvs, gates, mg = project(hp, w_in[l], pos_p)
        y_r, wkv, shift = rwkv_mixer(pr, jnp.zeros((B, R_COLS), pr.dtype),
                                     jnp.zeros((B, R_HEADS, R_HEAD, R_HEAD), jnp.float32), *rwkv_w)
        y_a = nsa_prompt(q, kvs, gates, *cmp_w)
        cmp_p.append(jnp.stack([kvs[0], kvs[1]], axis=2))
        sel_p.append(jnp.stack([kvs[2], kvs[3]], axis=2))
        win_p.append(jnp.stack([kvs[4][:, T - wb_p:], kvs[5][:, T - wb_p:]], axis=2))
        wkv_p.append(wkv)
        shift_p.append(shift)
        hp = merge_and_ffn(hp, y_r, y_a, mg, *out_w)
        pr, q, kvs, gates, mg = project(hs, w_in[l], pos_s)
        y_r, wkv, shift = rwkv_mixer(pr, state_shift[l], state_wkv[l], *rwkv_w)
        y_a, new_win = nsa_sample(q, kvs, gates, cache_cmp_kv[l], cache_sel_kv[l], cache_win_kv[l],
                                  page_table, *cmp_w)
        cmp_s.append(jnp.stack([kvs[0], kvs[1]], axis=2))
        sel_s.append(jnp.stack([kvs[2], kvs[3]], axis=2))
        win_s.append(new_win)
        wkv_s.append(wkv)
        shift_s.append(shift)
        hs = merge_and_ffn(hs, y_r, y_a, mg, *out_w)
    cmp_kv_prompt = jnp.stack(cmp_p)
    sel_kv_prompt = jnp.stack(sel_p)
    win_kv_prompt = jnp.stack(win_p)
    wkv_prompt = jnp.stack(wkv_p)
    shift_prompt = jnp.stack(shift_p)
    cmp_kv_sample = jnp.stack(cmp_s)
    sel_kv_sample = jnp.stack(sel_s)
    win_kv_sample = jnp.stack(win_s)
    wkv_sample = jnp.stack(wkv_s)
    shift_sample = jnp.stack(shift_s)
    return (hp, hs, cmp_kv_prompt, sel_kv_prompt, win_kv_prompt, wkv_prompt, shift_prompt,
            cmp_kv_sample, sel_kv_sample, win_kv_sample, wkv_sample, shift_sample)
```

```python
import functools

import jax
import jax.numpy as jnp
from jax import lax
import numpy as np
from jax.experimental import pallas as pl
from jax.experimental.pallas import tpu as pltpu

D_MODEL = 1024
BATCH = 2
SEQ = 8192
DEPTH = 1
DEC_BATCH = 32
DEC_SEQ = 4
PAST_LEN = 8192
PAGE_SIZE = 128

R_HEADS = 8
R_HEAD = 64
R_WIDTH = R_HEADS * R_HEAD
LORA_W = 64
LORA_A = 64
LORA_G = 128
R_COLS = 3 * R_WIDTH + LORA_W + LORA_A + LORA_G
GN_EPS = 64e-5

N_HEADS = 8
N_KV = 2
HPG = N_HEADS // N_KV
HEAD_DIM = 64
A_WIDTH = N_HEADS * HEAD_DIM
KV_WIDTH = N_KV * HEAD_DIM
A_COLS = A_WIDTH + 6 * KV_WIDTH + 3 * N_HEADS
ROT_DIM = HEAD_DIM // 4
ROPE_THETA = 500000.0
CMP_STRIDE = 16
CMP_LEN = 2 * CMP_STRIDE
CMP_HIDDEN = 256
SEL_BLOCK = 64
SEL_TOP = 16
WINDOW = 512
Q_BLOCK = 128

IN_COLS = R_COLS + A_COLS + 2 * D_MODEL

N_EXPERTS = 32
TOP_K = 4
D_FF = 1024
SWIGLU_LIMIT = 7.0
SWIGLU_ALPHA = 1.702
EXPERT_BLOCK = 128

DN_ALPHA = (2 * DEPTH) ** 0.25
DN_BETA = (8 * DEPTH) ** -0.25
LN_EPS = 1e-5
NEG = -1e30


def _mm_kernel(x_ref, w_ref, o_ref):
    o_ref[...] = jnp.dot(x_ref[...].astype(jnp.bfloat16), w_ref[...].astype(jnp.bfloat16),
                         preferred_element_type=jnp.float32)


def pallas_matmul(x, w, tm=512, tn=512):
    M, K = x.shape
    N = w.shape[1]
    Mp = -(-M // tm) * tm
    Np = -(-N // tn) * tn
    xp = jnp.pad(x, ((0, Mp - M), (0, 0))) if Mp != M else x
    wp = jnp.pad(w, ((0, 0), (0, Np - N))) if Np != N else w
    out = pl.pallas_call(
        _mm_kernel,
        grid=(Np // tn, Mp // tm),
        in_specs=[pl.BlockSpec((tm, K), lambda j, i: (i, 0)),
                  pl.BlockSpec((K, tn), lambda j, i: (0, j))],
        out_specs=pl.BlockSpec((tm, tn), lambda j, i: (i, j)),
        out_shape=jax.ShapeDtypeStruct((Mp, Np), jnp.float32),
        name="mm",
    )(xp, wp)
    return out[:M, :N]


def layer_norm(x, g, b):
    xf = x.astype(jnp.float32)
    mu = xf.mean(-1, keepdims=True)
    var = jnp.square(xf - mu).mean(-1, keepdims=True)
    return ((xf - mu) * lax.rsqrt(var + LN_EPS) * g + b).astype(x.dtype)


def rope(x, pos):
    half = ROT_DIM // 2
    inv = ROPE_THETA ** (-jnp.arange(half, dtype=jnp.float32) * 2.0 / ROT_DIM)
    ang = pos.astype(jnp.float32)[:, None] * inv
    cos, sin = jnp.cos(ang)[:, None, :], jnp.sin(ang)[:, None, :]
    xf = x.astype(jnp.float32)
    x1, x2 = xf[..., :half], xf[..., half:ROT_DIM]
    out = jnp.concatenate([x1 * cos - x2 * sin, x2 * cos + x1 * sin, xf[..., ROT_DIM:]], axis=-1)
    return out.astype(x.dtype)


def project(x, w_in, pos):
    B, T, _ = x.shape
    p = pallas_matmul(x.reshape(B * T, -1), w_in).reshape(B, T, -1)
    pr = p[..., :R_COLS]
    pa = p[..., R_COLS:R_COLS + A_COLS]
    mg = p[..., R_COLS + A_COLS:]
    q = rope(pa[..., :A_WIDTH].reshape(B, T, N_HEADS, HEAD_DIM), pos)
    kvs = [pa[..., A_WIDTH + i * KV_WIDTH:A_WIDTH + (i + 1) * KV_WIDTH].reshape(B, T, N_KV, HEAD_DIM)
           for i in range(6)]
    kvs = [rope(z, pos) if i % 2 == 0 else z for i, z in enumerate(kvs)]
    gates = jax.nn.sigmoid(pa[..., A_WIDTH + 6 * KV_WIDTH:]).reshape(B, T, N_KV, HPG, 3)
    return pr, q, kvs, gates, mg


def wkv_scan(s0, r, w, k, v, a, b):
    def step(s, inp):
        r_t, w_t, k_t, v_t, a_t, b_t = inp
        sa = jnp.einsum('bhij,bhj->bhi', s, a_t)
        s = s * w_t[:, :, None, :] + sa[..., None] * b_t[:, :, None, :] + v_t[..., None] * k_t[:, :, None, :]
        return s, jnp.einsum('bhij,bhj->bhi', s, r_t)
    xs = tuple(jnp.swapaxes(z, 0, 1) for z in (r, w, k, v, a, b))
    s, ys = lax.scan(step, s0, xs)
    return jnp.swapaxes(ys, 0, 1), s


def rwkv_mixer(pr, shift_prev, wkv0, mu, w0, w_w2, a0, w_a2, g_w2, k_k, k_a, r_k, gn_g, gn_b):
    B, T, _ = pr.shape
    prev = jnp.concatenate([shift_prev[:, None, :].astype(pr.dtype), pr[:, :-1]], axis=1)
    xm = (pr + (prev - pr) * mu).astype(jnp.float32)
    o1, o2, o3 = R_WIDTH, 2 * R_WIDTH, 3 * R_WIDTH
    o4 = o3 + LORA_W
    o5 = o4 + LORA_A
    r, k, v = xm[..., :o1], xm[..., o1:o2], xm[..., o2:o3]
    xw, xa, xg = xm[..., o3:o4], xm[..., o4:o5], xm[..., o5:]
    w_log = -jax.nn.softplus(-(w0 + jnp.tanh(xw) @ w_w2)) - 0.5
    a = jax.nn.sigmoid(a0 + xa @ w_a2)
    g = jax.nn.sigmoid(xg) @ g_w2
    heads = lambda z: z.reshape(B, T, R_HEADS, R_HEAD)
    kk = heads(k * k_k)
    kk = kk / jnp.maximum(jnp.linalg.norm(kk, axis=-1, keepdims=True), 1e-12)
    k_h = heads(k * (1.0 + (a - 1.0) * k_a))
    r_h, v_h, a_h = heads(r), heads(v), heads(a)
    decay = jnp.exp(-jnp.exp(heads(w_log)))
    y, wkv = wkv_scan(wkv0.astype(jnp.float32), r_h, decay, k_h, v_h, -kk, kk * a_h)
    mean = y.mean(-1, keepdims=True)
    var = jnp.square(y - mean).mean(-1, keepdims=True)
    yn = ((y - mean) * lax.rsqrt(var + GN_EPS)).reshape(B, T, R_WIDTH) * gn_g + gn_b
    bonus = (jnp.sum(r_h * k_h * r_k, axis=-1, keepdims=True) * v_h).reshape(B, T, R_WIDTH)
    out = ((yn + bonus) * g).astype(pr.dtype)
    return out, wkv.astype(pr.dtype), pr[:, -1]


def compress(kv, pe, w1, b1, w2, b2):
    B, L = kv.shape[:2]
    n_chunk = L // CMP_STRIDE
    ch = kv[:, :n_chunk * CMP_STRIDE].reshape(B, n_chunk, CMP_STRIDE, N_KV, HEAD_DIM)
    blk = jnp.concatenate([ch[:, :-1], ch[:, 1:]], axis=2) + pe[:, None, :]
    blk = jnp.transpose(blk, (0, 3, 1, 2, 4)).reshape(B, N_KV, n_chunk - 1, CMP_LEN * HEAD_DIM)
    return jax.nn.gelu(blk @ w1 + b1) @ w2 + b2


def nsa_attend(q, q_pos, kc, vc, kb, vb, kw, vw, w_pos, gates):
    f32 = jnp.float32
    B, Tq = q.shape[:2]
    NC, NS = kc.shape[2], kb.shape[2]
    qf = q.astype(f32) * (HEAD_DIM ** -0.5)
    c_start = jnp.arange(NC) * CMP_STRIDE
    c_ok = (c_start + CMP_LEN - 1)[None, :] <= q_pos[:, None]
    s_c = jnp.einsum('bqghd,bgnd->bqghn', qf, kc.astype(f32))
    s_c = jnp.where(c_ok[None, :, None, None], s_c, NEG)
    p_c = jax.nn.softmax(s_c, axis=-1) * c_ok.any(-1)[None, :, None, None, None]
    o_c = jnp.einsum('bqghn,bgnd->bqghd', p_c, vc.astype(f32))
    s_start = jnp.arange(NS) * SEL_BLOCK
    cover = ((c_start[:, None] < s_start[None, :] + SEL_BLOCK)
             & (c_start[:, None] + CMP_LEN > s_start[None, :])).astype(f32)
    imp = jnp.einsum('bqghn,ns->bqgs', p_c, cover)
    blk = jnp.arange(NS)
    cur = (q_pos // SEL_BLOCK)[:, None]
    forced = (blk[None] == 0) | (blk[None] == cur) | (blk[None] == cur - 1)
    causal = s_start[None] <= q_pos[:, None]
    score = jnp.where(forced[None, :, None], 1e6, imp)
    score = jnp.where(causal[None, :, None], score, NEG)
    top_s, sel = lax.top_k(score, min(SEL_TOP, NS))
    valid = top_s > 0.5 * NEG
    bi = jnp.arange(B)[:, None, None, None]
    gi = jnp.arange(N_KV)[None, None, :, None]
    kg = kb[bi, gi, sel].astype(f32)
    vg = vb[bi, gi, sel].astype(f32)
    s_s = jnp.einsum('bqghd,bqgnsd->bqghns', qf, kg)
    k_pos = sel[..., None] * SEL_BLOCK + jnp.arange(SEL_BLOCK)
    s_ok = (k_pos <= q_pos[None, :, None, None, None]) & valid[..., None]
    s_s = jnp.where(s_ok[:, :, :, None], s_s, NEG)
    p_s = jax.nn.softmax(s_s.reshape(B, Tq, N_KV, HPG, -1), axis=-1).reshape(s_s.shape)
    o_s = jnp.einsum('bqghns,bqgnsd->bqghd', p_s, vg)
    w_ok = ((w_pos[None] <= q_pos[:, None]) & (w_pos[None] >= q_pos[:, None] - WINDOW)
            & (w_pos[None] >= 0))
    s_w = jnp.einsum('bqghd,bkgd->bqghk', qf, kw.astype(f32))
    s_w = jnp.where(w_ok[None, :, None, None], s_w, NEG)
    o_w = jnp.einsum('bqghk,bkgd->bqghd', jax.nn.softmax(s_w, axis=-1), vw.astype(f32))
    g = gates.astype(f32)
    return g[..., 0:1] * o_c + g[..., 1:2] * o_s + g[..., 2:3] * o_w


def nsa_prompt(q, kvs, gates, pe, w1, b1, w2, b2):
    kc_raw, vc_raw, ks, vs, kw, vw = kvs
    B, T = q.shape[:2]
    kc = compress(kc_raw, pe[0], w1[0], b1[0], w2[0], b2[0])
    vc = compress(vc_raw, pe[1], w1[1], b1[1], w2[1], b2[1])
    ns = T // SEL_BLOCK
    to_blocks = lambda z: jnp.transpose(z.reshape(B, ns, SEL_BLOCK, N_KV, HEAD_DIM), (0, 3, 1, 2, 4))
    kb, vb = to_blocks(ks), to_blocks(vs)
    pad = jnp.zeros((B, WINDOW, N_KV, HEAD_DIM), kw.dtype)
    kw_p = jnp.concatenate([pad, kw], axis=1)
    vw_p = jnp.concatenate([pad, vw], axis=1)
    qg = q.reshape(B, T, N_KV, HPG, HEAD_DIM)
    span = WINDOW + Q_BLOCK

    def query_block(i):
        s = i * Q_BLOCK
        sl = lambda z, n: lax.dynamic_slice_in_dim(z, s, n, axis=1)
        return nsa_attend(sl(qg, Q_BLOCK), s + jnp.arange(Q_BLOCK), kc, vc, kb, vb,
                          sl(kw_p, span), sl(vw_p, span), s - WINDOW + jnp.arange(span),
                          sl(gates, Q_BLOCK))

    o = lax.map(query_block, jnp.arange(T // Q_BLOCK))
    return jnp.moveaxis(o, 0, 1).reshape(B, T, A_WIDTH).astype(q.dtype)


def nsa_sample(q, kvs, gates, cmp_cache, sel_cache, win_cache, page_table, pe, w1, b1, w2, b2):
    kc_new, vc_new, ks_new, vs_new, kw_new, vw_new = kvs
    DB, DS = q.shape[:2]
    past = page_table.shape[1] * PAGE_SIZE
    total = past + DS

    def full_rows(cache, k_new, v_new):
        rows = cache[page_table].reshape(DB, past, 2, N_KV, HEAD_DIM)
        return (jnp.concatenate([rows[:, :, 0], k_new], axis=1),
                jnp.concatenate([rows[:, :, 1], v_new], axis=1))

    kc_all, vc_all = full_rows(cmp_cache, kc_new, vc_new)
    ks_all, vs_all = full_rows(sel_cache, ks_new, vs_new)
    kc = compress(kc_all, pe[0], w1[0], b1[0], w2[0], b2[0])
    vc = compress(vc_all, pe[1], w1[1], b1[1], w2[1], b2[1])
    ns = -(-total // SEL_BLOCK)
    padn = ns * SEL_BLOCK - total
    to_blocks = lambda z: jnp.transpose(
        jnp.pad(z, ((0, 0), (0, padn), (0, 0), (0, 0))).reshape(DB, ns, SEL_BLOCK, N_KV, HEAD_DIM),
        (0, 3, 1, 2, 4))
    wb = win_cache.shape[1]
    kw_all = jnp.concatenate([win_cache[:, :, 0], kw_new], axis=1)
    vw_all = jnp.concatenate([win_cache[:, :, 1], vw_new], axis=1)
    o = nsa_attend(q.reshape(DB, DS, N_KV, HPG, HEAD_DIM), past + jnp.arange(DS), kc, vc,
                   to_blocks(ks_all), to_blocks(vs_all), kw_all, vw_all,
                   past - wb + jnp.arange(wb + DS), gates)
    new_win = jnp.stack([kw_all[:, DS:], vw_all[:, DS:]], axis=2)
    return o.reshape(DB, DS, A_WIDTH).astype(q.dtype), new_win


def moe(x, router_w, router_b, mlp1_w, mlp1_b, mlp2_w, mlp2_b):
    T, D = x.shape
    logits = (x @ router_w + router_b).astype(jnp.float32)
    top_v, top_e = lax.top_k(logits, TOP_K)
    gate = jax.nn.softmax(top_v, axis=-1)
    n_assign = T * TOP_K
    e_flat = top_e.reshape(-1)
    order = jnp.argsort(e_flat)
    e_sorted = e_flat[order]
    tok_sorted = (order // TOP_K).astype(jnp.int32)
    gate_sorted = gate.reshape(-1)[order]
    counts = jnp.bincount(e_flat, length=N_EXPERTS)
    padded = (counts + EXPERT_BLOCK - 1) // EXPERT_BLOCK * EXPERT_BLOCK
    start = jnp.cumsum(counts) - counts
    pend = jnp.cumsum(padded)
    pstart = pend - padded
    dest = pstart[e_sorted] + jnp.arange(n_assign) - start[e_sorted]
    n_blocks = -(-n_assign // EXPERT_BLOCK) + N_EXPERTS
    n_rows = n_blocks * EXPERT_BLOCK
    row_tok = jnp.full((n_rows,), T, jnp.int32).at[dest].set(tok_sorted)
    row_gate = jnp.zeros((n_rows,), jnp.float32).at[dest].set(gate_sorted)
    block_e = jnp.minimum(jnp.searchsorted(pend, jnp.arange(n_blocks) * EXPERT_BLOCK, side='right'),
                          N_EXPERTS - 1)
    xb = jnp.concatenate([x, jnp.zeros((1, D), x.dtype)])[row_tok].reshape(n_blocks, EXPERT_BLOCK, D)

    def expert_block(args):
        xe, e = args
        h = xe @ mlp1_w[e] + mlp1_b[e]
        glu = jnp.minimum(h[:, :D_FF], SWIGLU_LIMIT)
        lin = jnp.clip(h[:, D_FF:], -SWIGLU_LIMIT, SWIGLU_LIMIT)
        return (glu * jax.nn.sigmoid(SWIGLU_ALPHA * glu) * (lin + 1.0)) @ mlp2_w[e] + mlp2_b[e]

    yb = lax.map(expert_block, (xb, block_e)).reshape(n_rows, D)
    y = jnp.zeros((T + 1, D), jnp.float32).at[row_tok].add(yb.astype(jnp.float32) * row_gate[:, None])
    return y[:T].astype(x.dtype)


def merge_and_ffn(x, y_r, y_a, mg, w_pa, w_pb, w_o, ln1_g, ln1_b, router_w, router_b,
                  mlp1_w, mlp1_b, mlp2_w, mlp2_b, ln2_g, ln2_b):
    m = (jax.nn.sigmoid(mg[..., :D_MODEL]) * (y_r @ w_pa)
         + jax.nn.sigmoid(mg[..., D_MODEL:]) * (y_a @ w_pb))
    h = layer_norm(DN_ALPHA * x + m @ w_o, ln1_g, ln1_b)
    B, T, D = h.shape
    f = moe(h.reshape(B * T, D), router_w, router_b, mlp1_w, mlp1_b, mlp2_w, mlp2_b).reshape(B, T, D)
    return layer_norm(DN_ALPHA * h + f, ln2_g, ln2_b)


def kernel(x_prompt, x_sample, cache_cmp_kv, cache_sel_kv, cache_win_kv, state_wkv, state_shift,
           page_table, w_in, mu_shift, w0, w_w2, a0, w_a2, g_w2, k_k, k_a, r_k, gn_g, gn_b,
           cmp_pe, cmp_w1, cmp_b1, cmp_w2, cmp_b2, w_pa, w_pb, w_o, ln1_g, ln1_b,
           router_w, router_b, mlp1_w, mlp1_b, mlp2_w, mlp2_b, ln2_g, ln2_b):
    B, T, _ = x_prompt.shape
    DB, DS, _ = x_sample.shape
    past = page_table.shape[1] * PAGE_SIZE
    pos_p = jnp.arange(T)
    pos_s = past + jnp.arange(DS)
    wb_p = min(WINDOW, T)
    hp, hs = x_prompt, x_sample
    cmp_p, sel_p, win_p, wkv_p, shift_p = [], [], [], [], []
    cmp_s, sel_s, win_s, wkv_s, shift_s = [], [], [], [], []
    for l in range(DEPTH):
        rwkv_w = (mu_shift[l], w0[l], w_w2[l], a0[l], w_a2[l], g_w2[l], k_k[l], k_a[l], r_k[l],
                  gn_g[l], gn_b[l])
        cmp_w = (cmp_pe[l], cmp_w1[l], cmp_b1[l], cmp_w2[l], cmp_b2[l])
        out_w = (w_pa[l], w_pb[l], w_o[l], ln1_g[l], ln1_b[l], router_w[l], router_b[l],
                 mlp1_w[l], mlp1_b[l], mlp2_w[l], mlp2_b[l], ln2_g[l], ln2_b[l])
        pr, q, kvs, gates, mg = project(hp, w_in[l], pos_p)
        y_r, wkv, shift = rwkv_mixer(pr, jnp.zeros((B, R_COLS), pr.dtype),
                                     jnp.zeros((B, R_HEADS, R_HEAD, R_HEAD), jnp.float32), *rwkv_w)
        y_a = nsa_prompt(q, kvs, gates, *cmp_w)
        cmp_p.append(jnp.stack([kvs[0], kvs[1]], axis=2))
        sel_p.append(jnp.stack([kvs[2], kvs[3]], axis=2))
        win_p.append(jnp.stack([kvs[4][:, T - wb_p:], kvs[5][:, T - wb_p:]], axis=2))
        wkv_p.append(wkv)
        shift_p.append(shift)
        hp = merge_and_ffn(hp, y_r, y_a, mg, *out_w)
        pr, q, kvs, gates, mg = project(hs, w_in[l], pos_s)
        y_r, wkv, shift = rwkv_mixer(pr, state_shift[l], state_wkv[l], *rwkv_w)
        y_a, new_win = nsa_sample(q, kvs, gates, cache_cmp_kv[l], cache_sel_kv[l], cache_win_kv[l],
                                  page_table, *cmp_w)
        cmp_s.append(jnp.stack([kvs[0], kvs[1]], axis=2))
        sel_s.append(jnp.stack([kvs[2], kvs[3]], axis=2))
        win_s.append(new_win)
        wkv_s.append(wkv)
        shift_s.append(shift)
        hs = merge_and_ffn(hs, y_r, y_a, mg, *out_w)
    return (hp, hs, jnp.stack(cmp_p), jnp.stack(sel_p), jnp.stack(win_p), jnp.stack(wkv_p),
            jnp.stack(shift_p), jnp.stack(cmp_s), jnp.stack(sel_s), jnp.stack(win_s),
            jnp.stack(wkv_s), jnp.stack(shift_s))
```

```python
import functools

import jax
import jax.numpy as jnp
from jax import lax
from jax.experimental import pallas as pl
from jax.experimental.pallas import tpu as pltpu

D_MODEL = 1024
DEPTH = 1
PAGE_SIZE = 128

R_HEADS = 8
R_HEAD = 64
R_WIDTH = R_HEADS * R_HEAD
LORA_W = 64
LORA_A = 64
LORA_G = 128
R_COLS = 3 * R_WIDTH + LORA_W + LORA_A + LORA_G
GN_EPS = 64e-5

N_HEADS = 8
N_KV = 2
HPG = N_HEADS // N_KV
HEAD_DIM = 64
A_WIDTH = N_HEADS * HEAD_DIM
KV_WIDTH = N_KV * HEAD_DIM
N_GATE = 3 * N_HEADS
A_COLS = A_WIDTH + 6 * KV_WIDTH + N_GATE
ROT_DIM = HEAD_DIM // 4
ROPE_THETA = 500000.0
CMP_STRIDE = 16
CMP_LEN = 2 * CMP_STRIDE
SEL_BLOCK = 64
SEL_TOP = 16
WINDOW = 512

N_EXPERTS = 32
TOP_K = 4
D_FF = 1024
SWIGLU_LIMIT = 7.0
SWIGLU_ALPHA = 1.702

DN_ALPHA = (2 * DEPTH) ** 0.25
LN_EPS = 1e-5
NEG = -1e30

LANES = 128
VMEM_LIMIT = 56 * 1024 * 1024


def _bf(x):
    return x.astype(jnp.bfloat16)


def _dot(a, b):
    return jnp.dot(_bf(a), _bf(b), preferred_element_type=jnp.float32)


def _dot_nt(a, b):
    return lax.dot_general(_bf(a), _bf(b), (((1,), (1,)), ((), ())),
                           preferred_element_type=jnp.float32)


def _dot_tn(a, b):
    return lax.dot_general(_bf(a), _bf(b), (((0,), (0,)), ((), ())),
                           preferred_element_type=jnp.float32)


def _mm_kernel(x_ref, w_ref, o_ref):
    o_ref[...] = _dot(x_ref[...], w_ref[...])


def pallas_matmul(x, w):
    M, K = x.shape
    N = w.shape[1]
    tm = 512 if M % 512 == 0 else M
    tn = 512 if N % 512 == 0 else (256 if N % 256 == 0 else LANES)
    assert M % tm == 0 and N % tn == 0
    return pl.pallas_call(
        _mm_kernel,
        grid=(N // tn, M // tm),
        in_specs=[pl.BlockSpec((tm, K), lambda j, i: (i, 0)),
                  pl.BlockSpec((K, tn), lambda j, i: (0, j))],
        out_specs=pl.BlockSpec((tm, tn), lambda j, i: (i, j)),
        out_shape=jax.ShapeDtypeStruct((M, N), jnp.float32),
        name="mm",
    )(x, w)


def _wkv_chunk_kernel(r_ref, lw_ref, k_ref, v_ref, a_ref, b_ref, s0_ref, y_ref, s_out_ref, s_scr,
                      *, C, H):
    c = pl.program_id(1)

    @pl.when(c == 0)
    def _():
        s_scr[...] = s0_ref[0]

    row = lax.broadcasted_iota(jnp.int32, (C, C), 0)
    col = lax.broadcasted_iota(jnp.int32, (C, C), 1)
    incl = row >= col
    strict = row > col
    ltri = jnp.where(incl, 1.0, 0.0).astype(jnp.bfloat16)
    eye = jnp.where(row == col, 1.0, 0.0).astype(jnp.float32)

    for h in range(H):
        lw = lw_ref[0, h]
        r = r_ref[0, h]
        k = k_ref[0, h]
        v = v_ref[0, h]
        a = a_ref[0, h]
        b = b_ref[0, h]
        hi = _bf(lw)
        rem = lw - hi.astype(jnp.float32)
        mid = _bf(rem)
        lo = _bf(rem - mid.astype(jnp.float32))
        lp = (jnp.dot(ltri, hi, preferred_element_type=jnp.float32)
              + jnp.dot(ltri, mid, preferred_element_type=jnp.float32)
              + jnp.dot(ltri, lo, preferred_element_type=jnp.float32))
        lp_end = lp[C - 1:C, :]
        p_end = jnp.exp(lp_end)
        p_inv = jnp.exp(-lp)
        at = a * jnp.exp(lp - lw)
        rt = r * jnp.exp(lp)
        bt = b * p_inv
        kt = k * p_inv
        p_hat = jnp.exp(lp_end - lp)
        bh = b * p_hat
        kh = k * p_hat

        n_ab = jnp.where(strict, _dot_nt(at, bt), 0.0)
        a_ak = jnp.where(strict, _dot_nt(at, kt), 0.0)
        a_rb = jnp.where(incl, _dot_nt(rt, bt), 0.0)
        a_rk = jnp.where(incl, _dot_nt(rt, kt), 0.0)

        t_inv = eye + n_ab
        n_pow = n_ab
        span = 2
        while span < C:
            n_pow = _dot(n_pow, n_pow)
            t_inv = _dot(t_inv, eye + n_pow)
            span *= 2

        s = s_scr[h]
        rhs = _dot_nt(at, s) + _dot(a_ak, v)
        u = _dot(t_inv, rhs)
        y_ref[0, h] = _dot_nt(rt, s) + _dot(a_rb, u) + _dot(a_rk, v)
        s_scr[h] = s * p_end + _dot_tn(u, bh) + _dot_tn(v, kh)

    @pl.when(c == pl.num_programs(1) - 1)
    def _():
        s_out_ref[0] = s_scr[...]


def wkv_chunked(r, lw, k, v, a, b, s0, C):
    B, H, T, D = r.shape
    assert T % C == 0
    seq = pl.BlockSpec((1, H, C, D), lambda bi, ci: (bi, 0, ci, 0))
    st = pl.BlockSpec((1, H, D, D), lambda bi, ci: (bi, 0, 0, 0))
    return pl.pallas_call(
        functools.partial(_wkv_chunk_kernel, C=C, H=H),
        grid=(B, T // C),
        in_specs=[seq] * 6 + [st],
        out_specs=[seq, st],
        out_shape=[jax.ShapeDtypeStruct((B, H, T, D), jnp.float32),
                   jax.ShapeDtypeStruct((B, H, D, D), jnp.float32)],
        scratch_shapes=[pltpu.VMEM((H, D, D), jnp.float32)],
        compiler_params=pltpu.CompilerParams(dimension_semantics=("arbitrary", "arbitrary")),
        name="wkv_chunk",
    )(r, lw, k, v, a, b, s0)


def _nsa_prompt_kernel(q_ref, kc_ref, vc_ref, cover_ref, ks_ref, vs_ref, kw_ref, vw_ref, g_ref, o_ref,
                       *, TQ, TK, NC, NCP):
    f32 = jnp.float32
    bf16 = jnp.bfloat16
    qb = pl.program_id(2)
    R = HPG * TQ
    q = q_ref[0, 0].reshape(R, HEAD_DIM)
    t_pos = qb * TQ + lax.broadcasted_iota(jnp.int32, (TQ, 1), 0)

    n_idx = lax.broadcasted_iota(jnp.int32, (1, NCP), 1)
    c_ok = ((n_idx * CMP_STRIDE + (CMP_LEN - 1)) <= t_pos) & (n_idx < NC)
    s_c = _dot_nt(q, kc_ref[0, 0]).reshape(HPG, TQ, NCP)
    s_c = jnp.where(c_ok[None], s_c, NEG)
    m_c = jnp.max(s_c, axis=-1, keepdims=True)
    p_c = jnp.where(c_ok[None], jnp.exp(s_c - m_c), 0.0)
    l_c = jnp.sum(p_c, axis=-1, keepdims=True)
    p_c = p_c / jnp.where(l_c > 0.0, l_c, 1.0)
    p_cb = p_c.astype(bf16)
    o_c = jnp.dot(p_cb.reshape(R, NCP), vc_ref[0, 0], preferred_element_type=f32)

    cover = cover_ref[...]
    imp = jnp.dot(p_cb[0], cover, preferred_element_type=f32)
    for h in range(1, HPG):
        imp = imp + jnp.dot(p_cb[h], cover, preferred_element_type=f32)
    s_idx = lax.broadcasted_iota(jnp.int32, (1, LANES), 1)
    cur = t_pos // SEL_BLOCK
    forced = (s_idx == 0) | (s_idx == cur) | (s_idx == cur - 1)
    causal = (s_idx * SEL_BLOCK) <= t_pos
    score = jnp.where(forced, 1e6, imp)
    score = jnp.where(causal, score, NEG)
    selm = jnp.zeros((TQ, LANES), f32)
    for _ in range(SEL_TOP):
        top = jnp.max(score, axis=-1, keepdims=True)
        first = jnp.min(jnp.where(score == top, s_idx, LANES), axis=-1, keepdims=True)
        hit = s_idx == first
        selm = jnp.where(hit & (top > 0.5 * NEG), 1.0, selm)
        score = jnp.where(hit, -3e38, score)
    selm_b = selm.astype(bf16)

    blk_row = lax.broadcasted_iota(jnp.int32, (LANES, TK), 0)
    key_blk = lax.broadcasted_iota(jnp.int32, (LANES, TK), 1) // SEL_BLOCK
    key_off = lax.broadcasted_iota(jnp.int32, (1, TK), 1)
    n_tiles = ((qb + 1) * TQ + TK - 1) // TK

    def sel_body(j, carry):
        m, l, acc = carry
        start = pl.multiple_of(j * TK, TK)
        k = ks_ref[0, 0, pl.ds(start, TK), :]
        v = vs_ref[0, 0, pl.ds(start, TK), :]
        s = _dot_nt(q, k).reshape(HPG, TQ, TK)
        expand = jnp.where(blk_row == key_blk + j * (TK // SEL_BLOCK), 1.0, 0.0).astype(bf16)
        picked = jnp.dot(selm_b, expand, preferred_element_type=f32)
        ok = (picked > 0.5) & ((key_off + j * TK) <= t_pos)
        s = jnp.where(ok[None], s, NEG)
        m_new = jnp.maximum(m, jnp.max(s, axis=-1, keepdims=True))
        p = jnp.where(ok[None], jnp.exp(s - m_new), 0.0)
        alpha = jnp.exp(m - m_new)
        l = alpha * l + jnp.sum(p, axis=-1, keepdims=True)
        pv = jnp.dot(p.astype(bf16).reshape(R, TK), v, preferred_element_type=f32)
        acc = alpha * acc + pv.reshape(HPG, TQ, HEAD_DIM)
        return m_new, l, acc

    m0 = jnp.full((HPG, TQ, 1), NEG, f32)
    l0 = jnp.zeros((HPG, TQ, 1), f32)
    a0 = jnp.zeros((HPG, TQ, HEAD_DIM), f32)
    _, l_s, acc_s = lax.fori_loop(0, n_tiles, sel_body, (m0, l0, a0))
    o_s = acc_s / l_s

    n_w = WINDOW // TQ + 1
    lane_q = lax.broadcasted_iota(jnp.int32, (1, TQ), 1)
    s_w, ok_w, v_w = [], [], []
    for i in range(n_w):
        kb = qb - (n_w - 1) + i
        start = pl.multiple_of(jnp.maximum(kb, 0) * TQ, TQ)
        k = kw_ref[0, 0, pl.ds(start, TQ), :]
        v_w.append(vw_ref[0, 0, pl.ds(start, TQ), :])
        kpos = kb * TQ + lane_q
        ok = (kpos <= t_pos) & (kpos >= t_pos - WINDOW) & (kpos >= 0)
        ok_w.append(ok)
        s_w.append(jnp.where(ok[None], _dot_nt(q, k).reshape(HPG, TQ, TQ), NEG))
    m_w = s_w[0].max(axis=-1, keepdims=True)
    for i in range(1, n_w):
        m_w = jnp.maximum(m_w, s_w[i].max(axis=-1, keepdims=True))
    l_w = jnp.zeros((HPG, TQ, 1), f32)
    acc_w = jnp.zeros((R, HEAD_DIM), f32)
    for i in range(n_w):
        p = jnp.where(ok_w[i][None], jnp.exp(s_w[i] - m_w), 0.0)
        l_w = l_w + jnp.sum(p, axis=-1, keepdims=True)
        acc_w = acc_w + jnp.dot(p.astype(bf16).reshape(R, TQ), v_w[i], preferred_element_type=f32)
    o_w = acc_w.reshape(HPG, TQ, HEAD_DIM) / l_w

    g = g_ref[0, 0]
    o_ref[0, 0] = (g[:, :, 0:1] * o_c.reshape(HPG, TQ, HEAD_DIM) + g[:, :, 1:2] * o_s
                   + g[:, :, 2:3] * o_w)


def nsa_prompt_attention(q, kc, vc, ks, vs, kw, vw, gates, TQ=128, TK=512):
    B, G, _, T, D = q.shape
    NC = kc.shape[2]
    NCP = -(-NC // LANES) * LANES
    NS = T // SEL_BLOCK
    assert NS <= LANES and T % TK == 0 and T % TQ == 0 and WINDOW % TQ == 0
    kc = jnp.pad(kc, ((0, 0), (0, 0), (0, NCP - NC), (0, 0))).astype(jnp.bfloat16)
    vc = jnp.pad(vc, ((0, 0), (0, 0), (0, NCP - NC), (0, 0))).astype(jnp.bfloat16)
    c_start = jnp.arange(NCP) * CMP_STRIDE
    s_start = jnp.arange(LANES) * SEL_BLOCK
    cover = ((c_start[:, None] < s_start[None, :] + SEL_BLOCK)
             & (c_start[:, None] + CMP_LEN > s_start[None, :])
             & (jnp.arange(NCP)[:, None] < NC) & (jnp.arange(LANES)[None, :] < NS)).astype(jnp.bfloat16)
    full = lambda n: pl.BlockSpec((1, 1, n, D), lambda b, g, i: (b, g, 0, 0))
    qspec = pl.BlockSpec((1, 1, HPG, TQ, D), lambda b, g, i: (b, g, 0, i, 0))
    return pl.pallas_call(
        functools.partial(_nsa_prompt_kernel, TQ=TQ, TK=TK, NC=NC, NCP=NCP),
        grid=(B, G, T // TQ),
        in_specs=[qspec, full(NCP), full(NCP),
                  pl.BlockSpec((NCP, LANES), lambda b, g, i: (0, 0)),
                  full(T), full(T), full(T), full(T),
                  pl.BlockSpec((1, 1, HPG, TQ, 3), lambda b, g, i: (b, g, 0, i, 0))],
        out_specs=qspec,
        out_shape=jax.ShapeDtypeStruct((B, G, HPG, T, D), jnp.float32),
        compiler_params=pltpu.CompilerParams(
            dimension_semantics=("arbitrary", "arbitrary", "arbitrary"),
            vmem_limit_bytes=VMEM_LIMIT),
        name="nsa_prompt",
    )(q, kc, vc, cover, ks, vs, kw, vw, gates)


def _moe_mlp_kernel(be_ref, nb_ref, x_ref, w1_ref, b1_ref, w2_ref, b2_ref, g_ref, o_ref, w1s, w2s):
    i = pl.program_id(0)
    e = be_ref[i]
    prev = be_ref[jnp.maximum(i - 1, 0)]

    @pl.when((i == 0) | (e != prev))
    def _():
        w1s[...] = w1_ref[0].astype(jnp.bfloat16)
        w2s[...] = w2_ref[0].astype(jnp.bfloat16)

    @pl.when(i < nb_ref[0])
    def _():
        h = jnp.dot(x_ref[...], w1s[...], preferred_element_type=jnp.float32) + b1_ref[0]
        glu = jnp.minimum(h[:, :D_FF], SWIGLU_LIMIT)
        lin = jnp.clip(h[:, D_FF:], -SWIGLU_LIMIT, SWIGLU_LIMIT)
        act = glu * jax.nn.sigmoid(SWIGLU_ALPHA * glu) * (lin + 1.0)
        y = jnp.dot(act.astype(jnp.bfloat16), w2s[...], preferred_element_type=jnp.float32) + b2_ref[0]
        o_ref[...] = y * g_ref[...]

    @pl.when(i >= nb_ref[0])
    def _():
        o_ref[...] = jnp.zeros_like(o_ref)


def moe_mlp(xb, row_gate, block_e, n_used, w1, b1, w2, b2, BM):
    n_rows, D = xb.shape
    n_blocks = n_rows // BM
    E = w1.shape[0]
    grid_spec = pltpu.PrefetchScalarGridSpec(
        num_scalar_prefetch=2,
        grid=(n_blocks,),
        in_specs=[pl.BlockSpec((BM, D), lambda i, be, nb: (i, 0)),
                  pl.BlockSpec((1, D, 2 * D_FF), lambda i, be, nb: (be[i], 0, 0)),
                  pl.BlockSpec((1, 1, 2 * D_FF), lambda i, be, nb: (be[i], 0, 0)),
                  pl.BlockSpec((1, D_FF, D), lambda i, be, nb: (be[i], 0, 0)),
                  pl.BlockSpec((1, 1, D), lambda i, be, nb: (be[i], 0, 0)),
                  pl.BlockSpec((BM, 1), lambda i, be, nb: (i, 0))],
        out_specs=pl.BlockSpec((BM, D), lambda i, be, nb: (i, 0)),
        scratch_shapes=[pltpu.VMEM((D, 2 * D_FF), jnp.bfloat16), pltpu.VMEM((D_FF, D), jnp.bfloat16)],
    )
    return pl.pallas_call(
        _moe_mlp_kernel,
        grid_spec=grid_spec,
        out_shape=jax.ShapeDtypeStruct((n_rows, D), jnp.float32),
        compiler_params=pltpu.CompilerParams(dimension_semantics=("arbitrary",),
                                             vmem_limit_bytes=VMEM_LIMIT),
        name="moe_mlp",
    )(block_e, n_used, xb, w1, b1.reshape(E, 1, -1), w2, b2.reshape(E, 1, -1), row_gate.reshape(n_rows, 1))


def moe(x, router_w, router_b, mlp1_w, mlp1_b, mlp2_w, mlp2_b, BM=128):
    T, D = x.shape
    rw = jnp.pad(router_w, ((0, 0), (0, LANES - N_EXPERTS)))
    logits = pallas_matmul(x, rw)[:, :N_EXPERTS] + router_b
    top_v, top_e = lax.top_k(logits, TOP_K)
    gate = jax.nn.softmax(top_v, axis=-1)
    n_assign = T * TOP_K
    e_flat = top_e.reshape(-1)
    order = jnp.argsort(e_flat)
    e_sorted = e_flat[order]
    tok_sorted = (order // TOP_K).astype(jnp.int32)
    gate_sorted = gate.reshape(-1)[order]
    counts = jnp.bincount(e_flat, length=N_EXPERTS)
    padded = (counts + BM - 1) // BM * BM
    start = jnp.cumsum(counts) - counts
    pend = jnp.cumsum(padded)
    pstart = pend - padded
    dest = pstart[e_sorted] + jnp.arange(n_assign) - start[e_sorted]
    n_blocks = -(-n_assign // BM) + N_EXPERTS
    n_rows = n_blocks * BM
    row_tok = jnp.full((n_rows,), T, jnp.int32).at[dest].set(tok_sorted)
    row_gate = jnp.zeros((n_rows,), jnp.float32).at[dest].set(gate_sorted)
    block_e = jnp.minimum(jnp.searchsorted(pend, jnp.arange(n_blocks) * BM, side='right'),
                          N_EXPERTS - 1).astype(jnp.int32)
    n_used = (pend[-1] // BM).astype(jnp.int32).reshape(1)
    xz = jnp.concatenate([x.astype(jnp.bfloat16), jnp.zeros((1, D), jnp.bfloat16)])
    yb = moe_mlp(xz[row_tok], row_gate, block_e, n_used, mlp1_w, mlp1_b, mlp2_w, mlp2_b, BM)
    y = jnp.zeros((T + 1, D), jnp.float32).at[row_tok].add(yb)
    return y[:T]


def layer_norm(x, g, b):
    mu = x.mean(-1, keepdims=True)
    var = jnp.square(x - mu).mean(-1, keepdims=True)
    return (x - mu) * lax.rsqrt(var + LN_EPS) * g + b


def rope(x, pos):
    half = ROT_DIM // 2
    inv = ROPE_THETA ** (-jnp.arange(half, dtype=jnp.float32) * 2.0 / ROT_DIM)
    ang = pos.astype(jnp.float32)[:, None] * inv
    cos, sin = jnp.cos(ang)[:, None, :], jnp.sin(ang)[:, None, :]
    x1, x2 = x[..., :half], x[..., half:ROT_DIM]
    return jnp.concatenate([x1 * cos - x2 * sin, x2 * cos + x1 * sin, x[..., ROT_DIM:]], axis=-1)


def project(x, w_in, pos):
    B, T, _ = x.shape
    g0 = R_COLS + A_WIDTH + 6 * KV_WIDTH
    w_main = jnp.concatenate([w_in[:, :g0], w_in[:, g0 + N_GATE:]], axis=1)
    w_gate = jnp.pad(w_in[:, g0:g0 + N_GATE], ((0, 0), (0, LANES - N_GATE)))
    x2 = x.reshape(B * T, -1)
    p = pallas_matmul(x2, w_main).reshape(B, T, -1)
    pg = pallas_matmul(x2, w_gate)[:, :N_GATE].reshape(B, T, N_GATE)
    pr = p[..., :R_COLS]
    pa = p[..., R_COLS:g0]
    mg = p[..., g0:]
    q = rope(pa[..., :A_WIDTH].reshape(B, T, N_HEADS, HEAD_DIM), pos)
    kvs = [pa[..., A_WIDTH + i * KV_WIDTH:A_WIDTH + (i + 1) * KV_WIDTH].reshape(B, T, N_KV, HEAD_DIM)
           for i in range(6)]
    kvs = [rope(z, pos) if i % 2 == 0 else z for i, z in enumerate(kvs)]
    gates = jax.nn.sigmoid(pg).reshape(B, T, N_KV, HPG, 3)
    return pr, q, kvs, gates, mg


def rwkv_mixer(pr, shift_prev, wkv0, mu, w0, w_w2, a0, w_a2, g_w2, k_k, k_a, r_k, gn_g, gn_b):
    B, T, _ = pr.shape
    prev = jnp.concatenate([shift_prev[:, None, :], pr[:, :-1]], axis=1)
    xm = pr + (prev - pr) * mu
    o1, o2, o3 = R_WIDTH, 2 * R_WIDTH, 3 * R_WIDTH
    o4 = o3 + LORA_W
    o5 = o4 + LORA_A
    r, k, v = xm[..., :o1], xm[..., o1:o2], xm[..., o2:o3]
    xw, xa, xg = xm[..., o3:o4], xm[..., o4:o5], xm[..., o5:]
    w_log = -jax.nn.softplus(-(w0 + jnp.tanh(xw) @ w_w2)) - 0.5
    a = jax.nn.sigmoid(a0 + xa @ w_a2)
    g = jax.nn.sigmoid(xg) @ g_w2
    heads = lambda z: z.reshape(B, T, R_HEADS, R_HEAD)
    kk = heads(k * k_k)
    kk = kk / jnp.maximum(jnp.linalg.norm(kk, axis=-1, keepdims=True), 1e-12)
    k_h = heads(k * (1.0 + (a - 1.0) * k_a))
    r_h, v_h, a_h = heads(r), heads(v), heads(a)
    log_decay = -jnp.exp(heads(w_log))
    C = 64 if T % 64 == 0 else 8
    Tp = -(-T // C) * C
    hm = lambda z: jnp.pad(jnp.swapaxes(z, 1, 2), ((0, 0), (0, 0), (0, Tp - T), (0, 0)))
    y, wkv = wkv_chunked(hm(r_h), hm(log_decay), hm(k_h), hm(v_h), hm(-kk), hm(kk * a_h), wkv0, C)
    y = jnp.swapaxes(y[:, :, :T], 1, 2)
    mean = y.mean(-1, keepdims=True)
    var = jnp.square(y - mean).mean(-1, keepdims=True)
    yn = ((y - mean) * lax.rsqrt(var + GN_EPS)).reshape(B, T, R_WIDTH) * gn_g + gn_b
    bonus = (jnp.sum(r_h * k_h * r_k, axis=-1, keepdims=True) * v_h).reshape(B, T, R_WIDTH)
    return (yn + bonus) * g, wkv, pr[:, -1]


def compress(kv, pe, w1, b1, w2, b2):
    B, L = kv.shape[:2]
    n_chunk = L // CMP_STRIDE
    ch = kv[:, :n_chunk * CMP_STRIDE].reshape(B, n_chunk, CMP_STRIDE, N_KV, HEAD_DIM)
    blk = jnp.concatenate([ch[:, :-1], ch[:, 1:]], axis=2) + pe[:, None, :]
    blk = jnp.transpose(blk, (0, 3, 1, 2, 4)).reshape(B, N_KV, n_chunk - 1, CMP_LEN * HEAD_DIM)
    return jax.nn.gelu(blk @ w1 + b1) @ w2 + b2


def nsa_attend(q, q_pos, kc, vc, kb, vb, kw, vw, w_pos, gates):
    f32 = jnp.float32
    B, Tq = q.shape[:2]
    NC, NS = kc.shape[2], kb.shape[2]
    qf = q.astype(f32) * (HEAD_DIM ** -0.5)
    c_start = jnp.arange(NC) * CMP_STRIDE
    c_ok = (c_start + CMP_LEN - 1)[None, :] <= q_pos[:, None]
    s_c = jnp.einsum('bqghd,bgnd->bqghn', qf, kc.astype(f32))
    s_c = jnp.where(c_ok[None, :, None, None], s_c, NEG)
    p_c = jax.nn.softmax(s_c, axis=-1) * c_ok.any(-1)[None, :, None, None, None]
    o_c = jnp.einsum('bqghn,bgnd->bqghd', p_c, vc.astype(f32))
    s_start = jnp.arange(NS) * SEL_BLOCK
    cover = ((c_start[:, None] < s_start[None, :] + SEL_BLOCK)
             & (c_start[:, None] + CMP_LEN > s_start[None, :])).astype(f32)
    imp = jnp.einsum('bqghn,ns->bqgs', p_c, cover)
    blk = jnp.arange(NS)
    cur = (q_pos // SEL_BLOCK)[:, None]
    forced = (blk[None] == 0) | (blk[None] == cur) | (blk[None] == cur - 1)
    causal = s_start[None] <= q_pos[:, None]
    score = jnp.where(forced[None, :, None], 1e6, imp)
    score = jnp.where(causal[None, :, None], score, NEG)
    top_s, sel = lax.top_k(score, min(SEL_TOP, NS))
    valid = top_s > 0.5 * NEG
    bi = jnp.arange(B)[:, None, None, None]
    gi = jnp.arange(N_KV)[None, None, :, None]
    kg = kb[bi, gi, sel].astype(f32)
    vg = vb[bi, gi, sel].astype(f32)
    s_s = jnp.einsum('bqghd,bqgnsd->bqghns', qf, kg)
    k_pos = sel[..., None] * SEL_BLOCK + jnp.arange(SEL_BLOCK)
    s_ok = (k_pos <= q_pos[None, :, None, None, None]) & valid[..., None]
    s_s = jnp.where(s_ok[:, :, :, None], s_s, NEG)
    p_s = jax.nn.softmax(s_s.reshape(B, Tq, N_KV, HPG, -1), axis=-1).reshape(s_s.shape)
    o_s = jnp.einsum('bqghns,bqgnsd->bqghd', p_s, vg)
    w_ok = ((w_pos[None] <= q_pos[:, None]) & (w_pos[None] >= q_pos[:, None] - WINDOW)
            & (w_pos[None] >= 0))
    s_w = jnp.einsum('bqghd,bkgd->bqghk', qf, kw.astype(f32))
    s_w = jnp.where(w_ok[None, :, None, None], s_w, NEG)
    o_w = jnp.einsum('bqghk,bkgd->bqghd', jax.nn.softmax(s_w, axis=-1), vw.astype(f32))
    g = gates.astype(f32)
    return g[..., 0:1] * o_c + g[..., 1:2] * o_s + g[..., 2:3] * o_w


def nsa_prompt(q, kvs, gates, pe, w1, b1, w2, b2):
    kc_raw, vc_raw, ks, vs, kw, vw = kvs
    B, T = q.shape[:2]
    kc = compress(kc_raw, pe[0], w1[0], b1[0], w2[0], b2[0])
    vc = compress(vc_raw, pe[1], w1[1], b1[1], w2[1], b2[1])
    qg = _bf(jnp.transpose(q.reshape(B, T, N_KV, HPG, HEAD_DIM), (0, 2, 3, 1, 4)) * (HEAD_DIM ** -0.5))
    tk = lambda z: _bf(jnp.swapaxes(z, 1, 2))
    o = nsa_prompt_attention(qg, kc, vc, tk(ks), tk(vs), tk(kw), tk(vw),
                             jnp.transpose(gates, (0, 2, 3, 1, 4)))
    return jnp.transpose(o, (0, 3, 1, 2, 4)).reshape(B, T, A_WIDTH)


def nsa_sample(q, kvs, gates, cmp_cache, sel_cache, win_cache, page_table, pe, w1, b1, w2, b2):
    kc_new, vc_new, ks_new, vs_new, kw_new, vw_new = kvs
    DB, DS = q.shape[:2]
    past = page_table.shape[1] * PAGE_SIZE
    total = past + DS

    def full_rows(cache, k_new, v_new):
        rows = cache[page_table].reshape(DB, past, 2, N_KV, HEAD_DIM)
        return (jnp.concatenate([rows[:, :, 0], k_new], axis=1),
                jnp.concatenate([rows[:, :, 1], v_new], axis=1))

    kc_all, vc_all = full_rows(cmp_cache, kc_new, vc_new)
    ks_all, vs_all = full_rows(sel_cache, ks_new, vs_new)
    kc = compress(kc_all, pe[0], w1[0], b1[0], w2[0], b2[0])
    vc = compress(vc_all, pe[1], w1[1], b1[1], w2[1], b2[1])
    ns = -(-total // SEL_BLOCK)
    padn = ns * SEL_BLOCK - total
    to_blocks = lambda z: jnp.transpose(
        jnp.pad(z, ((0, 0), (0, padn), (0, 0), (0, 0))).reshape(DB, ns, SEL_BLOCK, N_KV, HEAD_DIM),
        (0, 3, 1, 2, 4))
    wb = win_cache.shape[1]
    kw_all = jnp.concatenate([win_cache[:, :, 0], kw_new], axis=1)
    vw_all = jnp.concatenate([win_cache[:, :, 1], vw_new], axis=1)
    o = nsa_attend(q.reshape(DB, DS, N_KV, HPG, HEAD_DIM), past + jnp.arange(DS), kc, vc,
                   to_blocks(ks_all), to_blocks(vs_all), kw_all, vw_all,
                   past - wb + jnp.arange(wb + DS), gates)
    new_win = jnp.stack([kw_all[:, DS:], vw_all[:, DS:]], axis=2)
    return o.reshape(DB, DS, A_WIDTH), new_win


def merge_and_ffn(x, y_r, y_a, mg, w_pa, w_pb, w_o, ln1_g, ln1_b, router_w, router_b,
                  mlp1_w, mlp1_b, mlp2_w, mlp2_b, ln2_g, ln2_b):
    B, T, D = x.shape
    flat = lambda z: z.reshape(B * T, -1)
    m = (jax.nn.sigmoid(flat(mg[..., :D_MODEL])) * pallas_matmul(flat(y_r), w_pa)
         + jax.nn.sigmoid(flat(mg[..., D_MODEL:])) * pallas_matmul(flat(y_a), w_pb))
    h = layer_norm(DN_ALPHA * flat(x) + pallas_matmul(m, w_o), ln1_g, ln1_b)
    f = moe(h, router_w, router_b, mlp1_w, mlp1_b, mlp2_w, mlp2_b)
    return layer_norm(DN_ALPHA * h + f, ln2_g, ln2_b).reshape(B, T, D)


def kernel(x_prompt, x_sample, cache_cmp_kv, cache_sel_kv, cache_win_kv, state_wkv, state_shift,
           page_table, w_in, mu_shift, w0, w_w2, a0, w_a2, g_w2, k_k, k_a, r_k, gn_g, gn_b,
           cmp_pe, cmp_w1, cmp_b1, cmp_w2, cmp_b2, w_pa, w_pb, w_o, ln1_g, ln1_b,
           router_w, router_b, mlp1_w, mlp1_b, mlp2_w, mlp2_b, ln2_g, ln2_b):
    B, T, _ = x_prompt.shape
    DB, DS, _ = x_sample.shape
    past = page_table.shape[1] * PAGE_SIZE
    pos_p = jnp.arange(T)
    pos_s = past + jnp.arange(DS)
    wb_p = min(WINDOW, T)
    hp, hs = x_prompt, x_sample
    cmp_p, sel_p, win_p, wkv_p, shift_p = [], [], [], [], []
    cmp_s, sel_s, win_s, wkv_s, shift_s = [], [], [], [], []
    for l in range(DEPTH):
        rwkv_w = (mu_shift[l], w0[l], w_w2[l], a0[l], w_a2[l], g_w2[l], k_k[l], k_a[l], r_k[l],
                  gn_g[l], gn_b[l])
        cmp_w = (cmp_pe[l], cmp_w1[l], cmp_b1[l], cmp_w2[l], cmp_b2[l])
        out_w = (w_pa[l], w_pb[l], w_o[l], ln1_g[l], ln1_b[l], router_w[l], router_b[l],
                 mlp1_w[l], mlp1_b[l], mlp2_w[l], mlp2_b[l], ln2_g[l], ln2_b[l])
        pr, q, kvs, gates, mg = project(hp, w_in[l], pos_p)
        y_r, wkv, shift = rwkv_mixer(pr, jnp.zeros((B, R_COLS), pr.dtype),
                                     jnp.zeros((B, R_HEADS, R_HEAD, R_HEAD), jnp.float32), *rwkv_w)
        y_a = nsa_prompt(q, kvs, gates, *cmp_w)
        cmp_p.append(jnp.stack([kvs[0], kvs[1]], axis=2))
        sel_p.append(jnp.stack([kvs[2], kvs[3]], axis=2))
        win_p.append(jnp.stack([kvs[4][:, T - wb_p:], kvs[5][:, T - wb_p:]], axis=2))
        wkv_p.append(wkv)
        shift_p.append(shift)
        hp = merge_and_ffn(hp, y_r, y_a, mg, *out_w)
        pr, q, kvs, gates, mg = project(hs, w_in[l], pos_s)
        y_r, wkv, shift = rwkv_mixer(pr, state_shift[l], state_wkv[l], *rwkv_w)
        y_a, new_win = nsa_sample(q, kvs, gates, cache_cmp_kv[l], cache_sel_kv[l], cache_win_kv[l],
                                  page_table, *cmp_w)
        cmp_s.append(jnp.stack([kvs[0], kvs[1]], axis=2))
        sel_s.append(jnp.stack([kvs[2], kvs[3]], axis=2))
        win_s.append(new_win)
        wkv_s.append(wkv)
        shift_s.append(shift)
        hs = merge_and_ffn(hs, y_r, y_a, mg, *out_w)
    return (hp, hs, jnp.stack(cmp_p), jnp.stack(sel_p), jnp.stack(win_p), jnp.stack(wkv_p),
            jnp.stack(shift_p), jnp.stack(cmp_s), jnp.stack(sel_s), jnp.stack(win_s),
            jnp.stack(wkv_s), jnp.stack(shift_s))
```

```python
import functools

import jax
import jax.numpy as jnp
from jax import lax
from jax.experimental import pallas as pl
from jax.experimental.pallas import tpu as pltpu

D_MODEL = 1024
DEPTH = 1
PAGE_SIZE = 128

R_HEADS = 8
R_HEAD = 64
R_WIDTH = R_HEADS * R_HEAD
LORA_W = 64
LORA_A = 64
LORA_G = 128
R_COLS = 3 * R_WIDTH + LORA_W + LORA_A + LORA_G
GN_EPS = 64e-5

N_HEADS = 8
N_KV = 2
HPG = N_HEADS // N_KV
HEAD_DIM = 64
A_WIDTH = N_HEADS * HEAD_DIM
KV_WIDTH = N_KV * HEAD_DIM
N_GATE = 3 * N_HEADS
A_COLS = A_WIDTH + 6 * KV_WIDTH + N_GATE
ROT_DIM = HEAD_DIM // 4
ROPE_THETA = 500000.0
CMP_STRIDE = 16
CMP_LEN = 2 * CMP_STRIDE
SEL_BLOCK = 64
SEL_TOP = 16
WINDOW = 512

N_EXPERTS = 32
TOP_K = 4
D_FF = 1024
SWIGLU_LIMIT = 7.0
SWIGLU_ALPHA = 1.702

DN_ALPHA = (2 * DEPTH) ** 0.25
LN_EPS = 1e-5
NEG = -1e30

LANES = 128
VMEM_LIMIT = 56 * 1024 * 1024


def _bf(x):
    return x.astype(jnp.bfloat16)


def _dot(a, b):
    return jnp.dot(_bf(a), _bf(b), preferred_element_type=jnp.float32)


def _dot_nt(a, b):
    return lax.dot_general(_bf(a), _bf(b), (((1,), (1,)), ((), ())),
                           preferred_element_type=jnp.float32)


def _mm_kernel(x_ref, w_ref, o_ref):
    o_ref[...] = _dot(x_ref[...], w_ref[...])


def pallas_matmul(x, w):
    x, w = _bf(x), _bf(w)
    M, K = x.shape
    N = w.shape[1]
    tm = 512 if M % 512 == 0 else M
    tn = 512 if N % 512 == 0 else (256 if N % 256 == 0 else LANES)
    assert M % tm == 0 and N % tn == 0
    return pl.pallas_call(
        _mm_kernel,
        grid=(N // tn, M // tm),
        in_specs=[pl.BlockSpec((tm, K), lambda j, i: (i, 0)),
                  pl.BlockSpec((K, tn), lambda j, i: (0, j))],
        out_specs=pl.BlockSpec((tm, tn), lambda j, i: (i, j)),
        out_shape=jax.ShapeDtypeStruct((M, N), jnp.float32),
        name="mm",
    )(x, w)


def _bmm(a, b):
    return lax.dot_general(_bf(a), _bf(b), (((2,), (1,)), ((0,), (0,))), preferred_element_type=jnp.float32)


def _bmm_nt(a, b):
    return lax.dot_general(_bf(a), _bf(b), (((2,), (2,)), ((0,), (0,))), preferred_element_type=jnp.float32)


def _bmm_tn(a, b):
    return lax.dot_general(_bf(a), _bf(b), (((1,), (1,)), ((0,), (0,))), preferred_element_type=jnp.float32)


def _wkv_chunk_kernel(r_ref, lw_ref, k_ref, v_ref, a_ref, b_ref, s0_ref, y_ref, s_out_ref, s_scr,
                      *, C, H):
    c = pl.program_id(1)

    @pl.when(c == 0)
    def _():
        s_scr[...] = s0_ref[0]

    row = lax.broadcasted_iota(jnp.int32, (H, C, C), 1)
    col = lax.broadcasted_iota(jnp.int32, (H, C, C), 2)
    incl = row >= col
    strict = row > col
    ltri = jnp.where(incl, 1.0, 0.0).astype(jnp.bfloat16)
    eye = jnp.where(row == col, 1.0, 0.0).astype(jnp.float32)

    lw = lw_ref[0]
    r = r_ref[0]
    k = k_ref[0]
    v = v_ref[0]
    a = a_ref[0]
    b = b_ref[0]
    hi = _bf(lw)
    rem = lw - hi.astype(jnp.float32)
    mid = _bf(rem)
    lo = _bf(rem - mid.astype(jnp.float32))
    lp = _bmm(ltri, hi) + _bmm(ltri, mid) + _bmm(ltri, lo)
    lp_end = lp[:, C - 1:C, :]
    p_end = jnp.exp(lp_end)
    p_inv = jnp.exp(-lp)
    at = a * jnp.exp(lp - lw)
    rt = r * jnp.exp(lp)
    bt = b * p_inv
    kt = k * p_inv
    p_hat = jnp.exp(lp_end - lp)
    bh = b * p_hat
    kh = k * p_hat

    n_ab = jnp.where(strict, _bmm_nt(at, bt), 0.0)
    a_ak = jnp.where(strict, _bmm_nt(at, kt), 0.0)
    a_rb = jnp.where(incl, _bmm_nt(rt, bt), 0.0)
    a_rk = jnp.where(incl, _bmm_nt(rt, kt), 0.0)

    t_inv = eye + n_ab
    n_pow = n_ab
    span = 2
    while span < C:
        n_pow = _bmm(n_pow, n_pow)
        t_inv = _bmm(t_inv, eye + n_pow)
        span *= 2

    s = s_scr[...]
    rhs = _bmm_nt(at, s) + _bmm(a_ak, v)
    u = _bmm(t_inv, rhs)
    y_ref[0] = _bmm_nt(rt, s) + _bmm(a_rb, u) + _bmm(a_rk, v)
    s_new = s * p_end + _bmm_tn(u, bh) + _bmm_tn(v, kh)
    s_scr[...] = s_new

    @pl.when(c == pl.num_programs(1) - 1)
    def _():
        s_out_ref[0] = s_new


def wkv_chunked(r, lw, k, v, a, b, s0, C):
    B, H, T, D = r.shape
    assert T % C == 0
    seq = pl.BlockSpec((1, H, C, D), lambda bi, ci: (bi, 0, ci, 0))
    st = pl.BlockSpec((1, H, D, D), lambda bi, ci: (bi, 0, 0, 0))
    return pl.pallas_call(
        functools.partial(_wkv_chunk_kernel, C=C, H=H),
        grid=(B, T // C),
        in_specs=[seq] * 6 + [st],
        out_specs=[seq, st],
        out_shape=[jax.ShapeDtypeStruct((B, H, T, D), jnp.float32),
                   jax.ShapeDtypeStruct((B, H, D, D), jnp.float32)],
        scratch_shapes=[pltpu.VMEM((H, D, D), jnp.float32)],
        compiler_params=pltpu.CompilerParams(dimension_semantics=("arbitrary", "arbitrary")),
        name="wkv_chunk",
    )(r, lw, k, v, a, b, s0)


def _nsa_prompt_kernel(q_ref, kc_ref, vc_ref, cover_ref, ks_ref, vs_ref, kw_ref, vw_ref, g_ref, o_ref,
                       *, TQ, TK, NC, NCP):
    f32 = jnp.float32
    bf16 = jnp.bfloat16
    qb = pl.program_id(2)
    R = HPG * TQ
    q = q_ref[0, 0].reshape(R, HEAD_DIM)
    t_pos = qb * TQ + lax.broadcasted_iota(jnp.int32, (TQ, 1), 0)

    n_idx = lax.broadcasted_iota(jnp.int32, (1, NCP), 1)
    c_ok = ((n_idx * CMP_STRIDE + (CMP_LEN - 1)) <= t_pos) & (n_idx < NC)
    s_c = _dot_nt(q, kc_ref[0, 0]).reshape(HPG, TQ, NCP)
    s_c = jnp.where(c_ok[None], s_c, NEG)
    m_c = jnp.max(s_c, axis=-1, keepdims=True)
    p_c = jnp.where(c_ok[None], jnp.exp(s_c - m_c), 0.0)
    l_c = jnp.sum(p_c, axis=-1, keepdims=True)
    p_c = p_c / jnp.where(l_c > 0.0, l_c, 1.0)
    p_cb = p_c.astype(bf16)
    o_c = jnp.dot(p_cb.reshape(R, NCP), vc_ref[0, 0], preferred_element_type=f32)

    cover_t = cover_ref[...]
    imp = _dot_nt(cover_t, p_cb[0])
    for h in range(1, HPG):
        imp = imp + _dot_nt(cover_t, p_cb[h])
    s_col = lax.broadcasted_iota(jnp.int32, (LANES, 1), 0)
    t_row = qb * TQ + lax.broadcasted_iota(jnp.int32, (1, TQ), 1)
    cur = t_row // SEL_BLOCK
    forced = (s_col == 0) | (s_col == cur) | (s_col == cur - 1)
    causal = (s_col * SEL_BLOCK) <= t_row
    score = jnp.where(forced, 1e6, imp)
    score = jnp.where(causal, score, NEG)
    s_col_f = s_col.astype(f32)
    sel_t = jnp.zeros((LANES, TQ), f32)
    for _ in range(SEL_TOP):
        top = jnp.max(score, axis=0, keepdims=True)
        first = jnp.min(jnp.where(score == top, s_col_f, float(LANES)), axis=0, keepdims=True)
        hit = s_col_f == first
        sel_t = jnp.where(hit & (top > 0.5 * NEG), 1.0, sel_t)
        score = jnp.where(hit, -3e38, score)

    sel_neg = ((sel_t.T - 1.0) * (-NEG)).astype(bf16)
    q_aug = jnp.concatenate([jnp.concatenate([sel_neg] * HPG, axis=0), q], axis=1)
    key_off = lax.broadcasted_iota(jnp.int32, (1, TK), 1)

    def sel_tile(j, carry, diagonal):
        m, l, acc = carry
        start = pl.multiple_of(j * TK, TK)
        k = ks_ref[0, 0, pl.ds(start, TK), :]
        v = vs_ref[0, 0, pl.ds(start, TK), :]
        s = _dot_nt(q_aug, k).reshape(HPG, TQ, TK)
        if diagonal:
            s = jnp.where(((key_off + j * TK) <= t_pos)[None], s, NEG)
        m_new = jnp.maximum(m, jnp.max(s, axis=-1, keepdims=True))
        p = jnp.exp(s - m_new)
        alpha = jnp.exp(m - m_new)
        l = alpha * l + jnp.sum(p, axis=-1, keepdims=True)
        pv = jnp.dot(p.astype(bf16).reshape(R, TK), v, preferred_element_type=f32)
        acc = alpha * acc + pv.reshape(HPG, TQ, HEAD_DIM)
        return m_new, l, acc

    m0 = jnp.full((HPG, TQ, 1), NEG, f32)
    l0 = jnp.zeros((HPG, TQ, 1), f32)
    a0 = jnp.zeros((HPG, TQ, HEAD_DIM), f32)
    n_full = (qb * TQ) // TK
    carry = lax.fori_loop(0, n_full, functools.partial(sel_tile, diagonal=False), (m0, l0, a0))
    _, l_s, acc_s = sel_tile(n_full, carry, True)
    o_s = acc_s / l_s

    n_w = WINDOW // TQ + 1
    lane_q = lax.broadcasted_iota(jnp.int32, (1, TQ), 1)
    s_w, ok_w, v_w = [], [], []
    for i in range(n_w):
        kb = qb - (n_w - 1) + i
        start = pl.multiple_of(jnp.maximum(kb, 0) * TQ, TQ)
        k = kw_ref[0, 0, pl.ds(start, TQ), :]
        v_w.append(vw_ref[0, 0, pl.ds(start, TQ), :])
        kpos = kb * TQ + lane_q
        ok = (kpos <= t_pos) & (kpos >= t_pos - WINDOW) & (kpos >= 0)
        ok_w.append(ok)
        s_w.append(jnp.where(ok[None], _dot_nt(q, k).reshape(HPG, TQ, TQ), NEG))
    m_w = s_w[0].max(axis=-1, keepdims=True)
    for i in range(1, n_w):
        m_w = jnp.maximum(m_w, s_w[i].max(axis=-1, keepdims=True))
    l_w = jnp.zeros((HPG, TQ, 1), f32)
    acc_w = jnp.zeros((R, HEAD_DIM), f32)
    for i in range(n_w):
        p = jnp.where(ok_w[i][None], jnp.exp(s_w[i] - m_w), 0.0)
        l_w = l_w + jnp.sum(p, axis=-1, keepdims=True)
        acc_w = acc_w + jnp.dot(p.astype(bf16).reshape(R, TQ), v_w[i], preferred_element_type=f32)
    o_w = acc_w.reshape(HPG, TQ, HEAD_DIM) / l_w

    g = g_ref[0, 0]
    o_ref[0, 0] = (g[:, :, 0:1] * o_c.reshape(HPG, TQ, HEAD_DIM) + g[:, :, 1:2] * o_s
                   + g[:, :, 2:3] * o_w)


def nsa_prompt_attention(q, kc, vc, ks, vs, kw, vw, gates, TQ=256, TK=1024):
    B, G, _, T, D = q.shape
    NC = kc.shape[2]
    NCP = -(-NC // LANES) * LANES
    NS = T // SEL_BLOCK
    assert NS <= LANES and T % TK == 0 and TK % TQ == 0 and WINDOW % TQ == 0
    kc = jnp.pad(kc, ((0, 0), (0, 0), (0, NCP - NC), (0, 0))).astype(jnp.bfloat16)
    vc = jnp.pad(vc, ((0, 0), (0, 0), (0, NCP - NC), (0, 0))).astype(jnp.bfloat16)
    c_start = jnp.arange(NCP) * CMP_STRIDE
    s_start = jnp.arange(LANES) * SEL_BLOCK
    cover = ((c_start[:, None] < s_start[None, :] + SEL_BLOCK)
             & (c_start[:, None] + CMP_LEN > s_start[None, :])
             & (jnp.arange(NCP)[:, None] < NC) & (jnp.arange(LANES)[None, :] < NS)).astype(jnp.bfloat16)
    onehot = (jnp.arange(T)[:, None] // SEL_BLOCK == jnp.arange(LANES)[None, :]).astype(jnp.bfloat16)
    ks = jnp.concatenate([jnp.broadcast_to(onehot, (B, G, T, LANES)), ks], axis=-1)
    full = lambda n, d=D: pl.BlockSpec((1, 1, n, d), lambda b, g, i: (b, g, 0, 0))
    qspec = pl.BlockSpec((1, 1, HPG, TQ, D), lambda b, g, i: (b, g, 0, i, 0))
    return pl.pallas_call(
        functools.partial(_nsa_prompt_kernel, TQ=TQ, TK=TK, NC=NC, NCP=NCP),
        grid=(B, G, T // TQ),
        in_specs=[qspec, full(NCP), full(NCP),
                  pl.BlockSpec((LANES, NCP), lambda b, g, i: (0, 0)),
                  full(T, LANES + D), full(T), full(T), full(T),
                  pl.BlockSpec((1, 1, HPG, TQ, 3), lambda b, g, i: (b, g, 0, i, 0))],
        out_specs=qspec,
        out_shape=jax.ShapeDtypeStruct((B, G, HPG, T, D), jnp.float32),
        compiler_params=pltpu.CompilerParams(
            dimension_semantics=("arbitrary", "arbitrary", "arbitrary"),
            vmem_limit_bytes=VMEM_LIMIT),
        name="nsa_prompt",
    )(q, kc, vc, cover.T, ks, vs, kw, vw, gates)


def _moe_mlp_kernel(be_ref, slot_ref, x_hbm, w1_ref, b1_ref, w2_ref, b2_ref, g_ref, o_hbm,
                    w1s, w2s, xbuf, ybuf, in_sem, out_sem, *, BM, n_assign, n_tok):
    i = pl.program_id(0)
    n = pl.num_programs(0)
    s = i % 2

    def gather(block, sl):
        def body(r, c):
            slot = slot_ref[block * BM + r]
            tok = jnp.where(slot < n_assign, slot // TOP_K, n_tok)
            pltpu.make_async_copy(x_hbm.at[pl.ds(tok, 1)], xbuf.at[sl, pl.ds(r, 1)], in_sem.at[sl]).start()
            return c
        lax.fori_loop(0, BM, body, 0, unroll=8)

    @pl.when(i == 0)
    def _():
        gather(0, 0)

    @pl.when(i + 1 < n)
    def _():
        gather(i + 1, 1 - s)

    e = be_ref[i]
    prev = be_ref[jnp.maximum(i - 1, 0)]

    @pl.when((i == 0) | (e != prev))
    def _():
        w1s[...] = w1_ref[0].astype(jnp.bfloat16)
        w2s[...] = w2_ref[0].astype(jnp.bfloat16)

    def rows_in(sl):
        return pltpu.make_async_copy(x_hbm.at[pl.ds(0, BM)], xbuf.at[sl], in_sem.at[sl])

    def rows_out(sl):
        return pltpu.make_async_copy(ybuf.at[sl], o_hbm.at[pl.ds(0, BM)], out_sem.at[sl])

    rows_in(s).wait()

    @pl.when(i >= 2)
    def _():
        rows_out(s).wait()

    x = xbuf[s].astype(jnp.bfloat16)
    h = jnp.dot(x, w1s[...], preferred_element_type=jnp.float32) + b1_ref[0]
    glu = jnp.minimum(h[:, :D_FF], SWIGLU_LIMIT)
    lin = jnp.clip(h[:, D_FF:], -SWIGLU_LIMIT, SWIGLU_LIMIT)
    act = glu * jax.nn.sigmoid(SWIGLU_ALPHA * glu) * (lin + 1.0)
    y = jnp.dot(act.astype(jnp.bfloat16), w2s[...], preferred_element_type=jnp.float32) + b2_ref[0]
    ybuf[s] = y * g_ref[...]

    def scatter(r, c):
        slot = slot_ref[i * BM + r]
        pltpu.make_async_copy(ybuf.at[s, pl.ds(r, 1)], o_hbm.at[pl.ds(slot, 1)], out_sem.at[s]).start()
        return c
    lax.fori_loop(0, BM, scatter, 0, unroll=8)

    @pl.when(i == n - 1)
    def _():
        rows_out(s).wait()

    @pl.when((i == n - 1) & (n >= 2))
    def _():
        rows_out(1 - s).wait()


def moe_mlp(x_pad, row_slot, row_gate, block_e, w1, b1, w2, b2, BM, n_assign):
    n_rows = row_slot.shape[0]
    D = x_pad.shape[1]
    n_blocks = n_rows // BM
    E = w1.shape[0]
    grid_spec = pltpu.PrefetchScalarGridSpec(
        num_scalar_prefetch=2,
        grid=(n_blocks,),
        in_specs=[pl.BlockSpec(memory_space=pl.ANY),
                  pl.BlockSpec((1, D, 2 * D_FF), lambda i, be, sl: (be[i], 0, 0)),
                  pl.BlockSpec((1, 1, 2 * D_FF), lambda i, be, sl: (be[i], 0, 0)),
                  pl.BlockSpec((1, D_FF, D), lambda i, be, sl: (be[i], 0, 0)),
                  pl.BlockSpec((1, 1, D), lambda i, be, sl: (be[i], 0, 0)),
                  pl.BlockSpec((BM, 1), lambda i, be, sl: (i, 0))],
        out_specs=pl.BlockSpec(memory_space=pl.ANY),
        scratch_shapes=[pltpu.VMEM((D, 2 * D_FF), jnp.bfloat16), pltpu.VMEM((D_FF, D), jnp.bfloat16),
                        pltpu.VMEM((2, BM, D), jnp.float32), pltpu.VMEM((2, BM, D), jnp.float32),
                        pltpu.SemaphoreType.DMA((2,)), pltpu.SemaphoreType.DMA((2,))],
    )
    return pl.pallas_call(
        functools.partial(_moe_mlp_kernel, BM=BM, n_assign=n_assign, n_tok=x_pad.shape[0] - 1),
        grid_spec=grid_spec,
        out_shape=jax.ShapeDtypeStruct((n_rows, D), jnp.float32),
        compiler_params=pltpu.CompilerParams(dimension_semantics=("arbitrary",),
                                             vmem_limit_bytes=VMEM_LIMIT),
        name="moe_mlp",
    )(block_e, row_slot, x_pad, w1, b1.reshape(E, 1, -1), w2, b2.reshape(E, 1, -1), row_gate.reshape(n_rows, 1))


def moe(x, router_w, router_b, mlp1_w, mlp1_b, mlp2_w, mlp2_b, BM=128):
    T, D = x.shape
    rw = jnp.pad(router_w, ((0, 0), (0, LANES - N_EXPERTS)))
    logits = pallas_matmul(x, rw)[:, :N_EXPERTS] + router_b
    top_v, top_e = lax.top_k(logits, TOP_K)
    gate = jax.nn.softmax(top_v, axis=-1)
    n_assign = T * TOP_K
    e_flat = top_e.reshape(-1)
    order = jnp.argsort(e_flat).astype(jnp.int32)
    e_sorted = e_flat[order]
    gate_sorted = gate.reshape(-1)[order]
    counts = jnp.bincount(e_flat, length=N_EXPERTS)
    padded = (counts + BM - 1) // BM * BM
    start = jnp.cumsum(counts) - counts
    pend = jnp.cumsum(padded)
    pstart = pend - padded
    dest = pstart[e_sorted] + jnp.arange(n_assign) - start[e_sorted]
    n_blocks = -(-n_assign // BM) + N_EXPERTS
    n_rows = n_blocks * BM
    is_real = jnp.zeros((n_rows,), jnp.bool_).at[dest].set(True)
    pad_rank = jnp.cumsum(jnp.logical_not(is_real)) - 1
    row_slot = (n_assign + pad_rank).astype(jnp.int32).at[dest].set(order)
    row_gate = jnp.zeros((n_rows,), jnp.float32).at[dest].set(gate_sorted)
    block_e = jnp.minimum(jnp.sum(pend[None, :] <= (jnp.arange(n_blocks) * BM)[:, None], axis=1),
                          N_EXPERTS - 1).astype(jnp.int32)
    x_pad = jnp.concatenate([x, jnp.zeros((1, D), x.dtype)])
    yb = moe_mlp(x_pad, row_slot, row_gate, block_e, mlp1_w, mlp1_b, mlp2_w, mlp2_b, BM, n_assign)
    return yb[:n_assign].reshape(T, TOP_K, D).sum(axis=1)


def layer_norm(x, g, b):
    mu = x.mean(-1, keepdims=True)
    var = jnp.square(x - mu).mean(-1, keepdims=True)
    return (x - mu) * lax.rsqrt(var + LN_EPS) * g + b


def rope(x, pos):
    half = ROT_DIM // 2
    inv = ROPE_THETA ** (-jnp.arange(half, dtype=jnp.float32) * 2.0 / ROT_DIM)
    ang = pos.astype(jnp.float32)[:, None] * inv
    cos, sin = jnp.cos(ang)[:, None, :], jnp.sin(ang)[:, None, :]
    x1, x2 = x[..., :half], x[..., half:ROT_DIM]
    return jnp.concatenate([x1 * cos - x2 * sin, x2 * cos + x1 * sin, x[..., ROT_DIM:]], axis=-1)


def project(x, w_in, pos):
    B, T, _ = x.shape
    g0 = R_COLS + A_WIDTH + 6 * KV_WIDTH
    w_main = jnp.concatenate([w_in[:, :g0], w_in[:, g0 + N_GATE:]], axis=1)
    w_gate = jnp.pad(w_in[:, g0:g0 + N_GATE], ((0, 0), (0, LANES - N_GATE)))
    x2 = x.reshape(B * T, -1)
    p = pallas_matmul(x2, w_main).reshape(B, T, -1)
    pg = pallas_matmul(x2, w_gate)[:, :N_GATE].reshape(B, T, N_GATE)
    pr = p[..., :R_COLS]
    pa = p[..., R_COLS:g0]
    mg = p[..., g0:]
    q = rope(pa[..., :A_WIDTH].reshape(B, T, N_HEADS, HEAD_DIM), pos)
    kvs = [pa[..., A_WIDTH + i * KV_WIDTH:A_WIDTH + (i + 1) * KV_WIDTH].reshape(B, T, N_KV, HEAD_DIM)
           for i in range(6)]
    kvs = [rope(z, pos) if i % 2 == 0 else z for i, z in enumerate(kvs)]
    gates = jax.nn.sigmoid(pg).reshape(B, T, N_KV, HPG, 3)
    return pr, q, kvs, gates, mg


def rwkv_mixer(pr, shift_prev, wkv0, mu, w0, w_w2, a0, w_a2, g_w2, k_k, k_a, r_k, gn_g, gn_b):
    B, T, _ = pr.shape
    prev = jnp.concatenate([shift_prev[:, None, :], pr[:, :-1]], axis=1)
    xm = pr + (prev - pr) * mu
    o1, o2, o3 = R_WIDTH, 2 * R_WIDTH, 3 * R_WIDTH
    o4 = o3 + LORA_W
    o5 = o4 + LORA_A
    r, k, v = xm[..., :o1], xm[..., o1:o2], xm[..., o2:o3]
    xw, xa, xg = xm[..., o3:o4], xm[..., o4:o5], xm[..., o5:]
    w_log = -jax.nn.softplus(-(w0 + jnp.tanh(xw) @ w_w2)) - 0.5
    a = jax.nn.sigmoid(a0 + xa @ w_a2)
    g = jax.nn.sigmoid(xg) @ g_w2
    heads = lambda z: z.reshape(B, T, R_HEADS, R_HEAD)
    kk = heads(k * k_k)
    kk = kk / jnp.maximum(jnp.linalg.norm(kk, axis=-1, keepdims=True), 1e-12)
    k_h = heads(k * (1.0 + (a - 1.0) * k_a))
    r_h, v_h, a_h = heads(r), heads(v), heads(a)
    log_decay = -jnp.exp(heads(w_log))
    C = 64 if T % 64 == 0 else 8
    Tp = -(-T // C) * C
    hm = lambda z: jnp.pad(jnp.swapaxes(z, 1, 2), ((0, 0), (0, 0), (0, Tp - T), (0, 0)))
    y, wkv = wkv_chunked(hm(r_h), hm(log_decay), hm(k_h), hm(v_h), hm(-kk), hm(kk * a_h), wkv0, C)
    y = jnp.swapaxes(y[:, :, :T], 1, 2)
    mean = y.mean(-1, keepdims=True)
    var = jnp.square(y - mean).mean(-1, keepdims=True)
    yn = ((y - mean) * lax.rsqrt(var + GN_EPS)).reshape(B, T, R_WIDTH) * gn_g + gn_b
    bonus = (jnp.sum(r_h * k_h * r_k, axis=-1, keepdims=True) * v_h).reshape(B, T, R_WIDTH)
    return (yn + bonus) * g, wkv, pr[:, -1]


def compress(kv, pe, w1, b1, w2, b2):
    B, L = kv.shape[:2]
    n_chunk = L // CMP_STRIDE
    ch = kv[:, :n_chunk * CMP_STRIDE].reshape(B, n_chunk, CMP_STRIDE, N_KV, HEAD_DIM)
    blk = jnp.concatenate([ch[:, :-1], ch[:, 1:]], axis=2) + pe[:, None, :]
    blk = jnp.transpose(blk, (0, 3, 1, 2, 4)).reshape(B, N_KV, n_chunk - 1, CMP_LEN * HEAD_DIM)
    return jax.nn.gelu(blk @ w1 + b1) @ w2 + b2


def nsa_attend(q, q_pos, kc, vc, kb, vb, kw, vw, w_pos, gates):
    f32 = jnp.float32
    B, Tq = q.shape[:2]
    NC, NS = kc.shape[2], kb.shape[2]
    qf = q.astype(f32) * (HEAD_DIM ** -0.5)
    c_start = jnp.arange(NC) * CMP_STRIDE
    c_ok = (c_start + CMP_LEN - 1)[None, :] <= q_pos[:, None]
    s_c = jnp.einsum('bqghd,bgnd->bqghn', qf, kc.astype(f32))
    s_c = jnp.where(c_ok[None, :, None, None], s_c, NEG)
    p_c = jax.nn.softmax(s_c, axis=-1) * c_ok.any(-1)[None, :, None, None, None]
    o_c = jnp.einsum('bqghn,bgnd->bqghd', p_c, vc.astype(f32))
    s_start = jnp.arange(NS) * SEL_BLOCK
    cover = ((c_start[:, None] < s_start[None, :] + SEL_BLOCK)
             & (c_start[:, None] + CMP_LEN > s_start[None, :])).astype(f32)
    imp = jnp.einsum('bqghn,ns->bqgs', p_c, cover)
    blk = jnp.arange(NS)
    cur = (q_pos // SEL_BLOCK)[:, None]
    forced = (blk[None] == 0) | (blk[None] == cur) | (blk[None] == cur - 1)
    causal = s_start[None] <= q_pos[:, None]
    score = jnp.where(forced[None, :, None], 1e6, imp)
    score = jnp.where(causal[None, :, None], score, NEG)
    top_s, sel = lax.top_k(score, min(SEL_TOP, NS))
    valid = top_s > 0.5 * NEG
    bi = jnp.arange(B)[:, None, None, None]
    gi = jnp.arange(N_KV)[None, None, :, None]
    kg = kb[bi, gi, sel].astype(f32)
    vg = vb[bi, gi, sel].astype(f32)
    s_s = jnp.einsum('bqghd,bqgnsd->bqghns', qf, kg)
    k_pos = sel[..., None] * SEL_BLOCK + jnp.arange(SEL_BLOCK)
    s_ok = (k_pos <= q_pos[None, :, None, None, None]) & valid[..., None]
    s_s = jnp.where(s_ok[:, :, :, None], s_s, NEG)
    p_s = jax.nn.softmax(s_s.reshape(B, Tq, N_KV, HPG, -1), axis=-1).reshape(s_s.shape)
    o_s = jnp.einsum('bqghns,bqgnsd->bqghd', p_s, vg)
    w_ok = ((w_pos[None] <= q_pos[:, None]) & (w_pos[None] >= q_pos[:, None] - WINDOW)
            & (w_pos[None] >= 0))
    s_w = jnp.einsum('bqghd,bkgd->bqghk', qf, kw.astype(f32))
    s_w = jnp.where(w_ok[None, :, None, None], s_w, NEG)
    o_w = jnp.einsum('bqghk,bkgd->bqghd', jax.nn.softmax(s_w, axis=-1), vw.astype(f32))
    g = gates.astype(f32)
    return g[..., 0:1] * o_c + g[..., 1:2] * o_s + g[..., 2:3] * o_w


def nsa_prompt(q, kvs, gates, pe, w1, b1, w2, b2):
    kc_raw, vc_raw, ks, vs, kw, vw = kvs
    B, T = q.shape[:2]
    kc = compress(kc_raw, pe[0], w1[0], b1[0], w2[0], b2[0])
    vc = compress(vc_raw, pe[1], w1[1], b1[1], w2[1], b2[1])
    qg = _bf(jnp.transpose(q.reshape(B, T, N_KV, HPG, HEAD_DIM), (0, 2, 3, 1, 4)) * (HEAD_DIM ** -0.5))
    tk = lambda z: _bf(jnp.swapaxes(z, 1, 2))
    o = nsa_prompt_attention(qg, kc, vc, tk(ks), tk(vs), tk(kw), tk(vw),
                             jnp.transpose(gates, (0, 2, 3, 1, 4)))
    return jnp.transpose(o, (0, 3, 1, 2, 4)).reshape(B, T, A_WIDTH)


def nsa_sample(q, kvs, gates, cmp_cache, sel_cache, win_cache, page_table, pe, w1, b1, w2, b2):
    kc_new, vc_new, ks_new, vs_new, kw_new, vw_new = kvs
    DB, DS = q.shape[:2]
    past = page_table.shape[1] * PAGE_SIZE
    total = past + DS

    def full_rows(cache, k_new, v_new):
        rows = cache[page_table].reshape(DB, past, 2, N_KV, HEAD_DIM)
        return (jnp.concatenate([rows[:, :, 0], k_new], axis=1),
                jnp.concatenate([rows[:, :, 1], v_new], axis=1))

    kc_all, vc_all = full_rows(cmp_cache, kc_new, vc_new)
    ks_all, vs_all = full_rows(sel_cache, ks_new, vs_new)
    kc = compress(kc_all, pe[0], w1[0], b1[0], w2[0], b2[0])
    vc = compress(vc_all, pe[1], w1[1], b1[1], w2[1], b2[1])
    ns = -(-total // SEL_BLOCK)
    padn = ns * SEL_BLOCK - total
    to_blocks = lambda z: jnp.transpose(
        jnp.pad(z, ((0, 0), (0, padn), (0, 0), (0, 0))).reshape(DB, ns, SEL_BLOCK, N_KV, HEAD_DIM),
        (0, 3, 1, 2, 4))
    wb = win_cache.shape[1]
    kw_all = jnp.concatenate([win_cache[:, :, 0], kw_new], axis=1)
    vw_all = jnp.concatenate([win_cache[:, :, 1], vw_new], axis=1)
    o = nsa_attend(q.reshape(DB, DS, N_KV, HPG, HEAD_DIM), past + jnp.arange(DS), kc, vc,
                   to_blocks(ks_all), to_blocks(vs_all), kw_all, vw_all,
                   past - wb + jnp.arange(wb + DS), gates)
    new_win = jnp.stack([kw_all[:, DS:], vw_all[:, DS:]], axis=2)
    return o.reshape(DB, DS, A_WIDTH), new_win


def merge_and_ffn(x, y_r, y_a, mg, w_pa, w_pb, w_o, ln1_g, ln1_b, router_w, router_b,
                  mlp1_w, mlp1_b, mlp2_w, mlp2_b, ln2_g, ln2_b):
    B, T, D = x.shape
    flat = lambda z: z.reshape(B * T, -1)
    m = (jax.nn.sigmoid(flat(mg[..., :D_MODEL])) * pallas_matmul(flat(y_r), w_pa)
         + jax.nn.sigmoid(flat(mg[..., D_MODEL:])) * pallas_matmul(flat(y_a), w_pb))
    h = layer_norm(DN_ALPHA * flat(x) + pallas_matmul(m, w_o), ln1_g, ln1_b)
    f = moe(h, router_w, router_b, mlp1_w, mlp1_b, mlp2_w, mlp2_b)
    return layer_norm(DN_ALPHA * h + f, ln2_g, ln2_b).reshape(B, T, D)


def kernel(x_prompt, x_sample, cache_cmp_kv, cache_sel_kv, cache_win_kv, state_wkv, state_shift,
           page_table, w_in, mu_shift, w0, w_w2, a0, w_a2, g_w2, k_k, k_a, r_k, gn_g, gn_b,
           cmp_pe, cmp_w1, cmp_b1, cmp_w2, cmp_b2, w_pa, w_pb, w_o, ln1_g, ln1_b,
           router_w, router_b, mlp1_w, mlp1_b, mlp2_w, mlp2_b, ln2_g, ln2_b):
    B, T, _ = x_prompt.shape
    DB, DS, _ = x_sample.shape
    past = page_table.shape[1] * PAGE_SIZE
    pos_p = jnp.arange(T)
    pos_s = past + jnp.arange(DS)
    wb_p = min(WINDOW, T)
    hp, hs = x_prompt, x_sample
    cmp_p, sel_p, win_p, wkv_p, shift_p = [], [], [], [], []
    cmp_s, sel_s, win_s, wkv_s, shift_s = [], [], [], [], []
    for l in range(DEPTH):
        rwkv_w = (mu_shift[l], w0[l], w_w2[l], a0[l], w_a2[l], g_w2[l], k_k[l], k_a[l], r_k[l],
                  gn_g[l], gn_b[l])
        cmp_w = (cmp_pe[l], cmp_w1[l], cmp_b1[l], cmp_w2[l], cmp_b2[l])
        out_w = (w_pa[l], w_pb[l], w_o[l], ln1_g[l], ln1_b[l], router_w[l], router_b[l],
                 mlp1_w[l], mlp1_b[l], mlp2_w[l], mlp2_b[l], ln2_g[l], ln2_b[l])
        pr, q, kvs, gates, mg = project(hp, w_in[l], pos_p)
        y_r, wkv, shift = rwkv_mixer(pr, jnp.zeros((B, R_COLS), pr.dtype),
                                     jnp.zeros((B, R_HEADS, R_HEAD, R_HEAD), jnp.float32), *rwkv_w)
        y_a = nsa_prompt(q, kvs, gates, *cmp_w)
        cmp_p.append(jnp.stack([kvs[0], kvs[1]], axis=2))
        sel_p.append(jnp.stack([kvs[2], kvs[3]], axis=2))
        win_p.append(jnp.stack([kvs[4][:, T - wb_p:], kvs[5][:, T - wb_p:]], axis=2))
        wkv_p.append(wkv)
        shift_p.append(shift)
        hp = merge_and_ffn(hp, y_r, y_a, mg, *out_w)
        pr, q, kvs, gates, mg = project(hs, w_in[l], pos_s)
        y_r, wkv, shift = rwkv_mixer(pr, state_shift[l], state_wkv[l], *rwkv_w)
        y_a, new_win = nsa_sample(q, kvs, gates, cache_cmp_kv[l], cache_sel_kv[l], cache_win_kv[l],
                                  page_table, *cmp_w)
        cmp_s.append(jnp.stack([kvs[0], kvs[1]], axis=2))
        sel_s.append(jnp.stack([kvs[2], kvs[3]], axis=2))
        win_s.append(new_win)
        wkv_s.append(wkv)
        shift_s.append(shift)
        hs = merge_and_ffn(hs, y_r, y_a, mg, *out_w)
    return (hp, hs, jnp.stack(cmp_p), jnp.stack(sel_p), jnp.stack(win_p), jnp.stack(wkv_p),
            jnp.stack(shift_p), jnp.stack(cmp_s), jnp.stack(sel_s), jnp.stack(win_s),
            jnp.stack(wkv_s), jnp.stack(shift_s))
```

```python
import functools

import jax
import jax.numpy as jnp
from jax import lax
from jax.experimental import pallas as pl
from jax.experimental.pallas import tpu as pltpu

D_MODEL = 1024
DEPTH = 1
PAGE_SIZE = 128

R_HEADS = 8
R_HEAD = 64
R_WIDTH = R_HEADS * R_HEAD
LORA_W = 64
LORA_A = 64
LORA_G = 128
R_COLS = 3 * R_WIDTH + LORA_W + LORA_A + LORA_G
GN_EPS = 64e-5

N_HEADS = 8
N_KV = 2
HPG = N_HEADS // N_KV
HEAD_DIM = 64
A_WIDTH = N_HEADS * HEAD_DIM
KV_WIDTH = N_KV * HEAD_DIM
N_GATE = 3 * N_HEADS
A_COLS = A_WIDTH + 6 * KV_WIDTH + N_GATE
ROT_DIM = HEAD_DIM // 4
ROPE_THETA = 500000.0
CMP_STRIDE = 16
CMP_LEN = 2 * CMP_STRIDE
CMP_HIDDEN = 256
SEL_BLOCK = 64
SEL_TOP = 16
WINDOW = 512

N_EXPERTS = 32
TOP_K = 4
K_SHIFT = 2
D_FF = 1024
SWIGLU_LIMIT = 7.0
SWIGLU_ALPHA = 1.702

DN_ALPHA = (2 * DEPTH) ** 0.25
LN_EPS = 1e-5
NEG = -1e30

LANES = 128
PAGES_PER_STEP = 8
NEW_PAD = 8
VMEM_LIMIT = 56 * 1024 * 1024


def _bf(x):
    return x.astype(jnp.bfloat16)


def _dot(a, b):
    return jnp.dot(_bf(a), _bf(b), preferred_element_type=jnp.float32)


def _dot_nt(a, b):
    return lax.dot_general(_bf(a), _bf(b), (((1,), (1,)), ((), ())),
                           preferred_element_type=jnp.float32)


def _mm_kernel(x_ref, w_ref, o_ref):
    o_ref[...] = _dot(x_ref[...], w_ref[...])


def pallas_matmul(x, w):
    x, w = _bf(x), _bf(w)
    M, K = x.shape
    N = w.shape[1]
    tm = next(t for t in (512, 384, 256, 128, M) if M % t == 0)
    resident = 2 * (2 * K * N + 4 * tm * N + 2 * tm * K) <= VMEM_LIMIT - (8 << 20)
    tn = N if resident else next(t for t in (512, 256, LANES) if N % t == 0)
    return pl.pallas_call(
        _mm_kernel,
        grid=(N // tn, M // tm),
        in_specs=[pl.BlockSpec((tm, K), lambda j, i: (i, 0)),
                  pl.BlockSpec((K, tn), lambda j, i: (0, j))],
        out_specs=pl.BlockSpec((tm, tn), lambda j, i: (i, j)),
        out_shape=jax.ShapeDtypeStruct((M, N), jnp.float32),
        compiler_params=pltpu.CompilerParams(vmem_limit_bytes=VMEM_LIMIT),
        name="mm",
    )(x, w)


def _bmm(a, b):
    return lax.dot_general(_bf(a), _bf(b), (((2,), (1,)), ((0,), (0,))), preferred_element_type=jnp.float32)


def _bmm_nt(a, b):
    return lax.dot_general(_bf(a), _bf(b), (((2,), (2,)), ((0,), (0,))), preferred_element_type=jnp.float32)


def _bmm_tn(a, b):
    return lax.dot_general(_bf(a), _bf(b), (((1,), (1,)), ((0,), (0,))), preferred_element_type=jnp.float32)


def _wkv_chunk_kernel(r_ref, lw_ref, k_ref, v_ref, a_ref, b_ref, s0_ref, y_ref, s_out_ref, s_scr,
                      *, C, H):
    c = pl.program_id(1)

    @pl.when(c == 0)
    def _():
        s_scr[...] = s0_ref[0]

    row = lax.broadcasted_iota(jnp.int32, (H, C, C), 1)
    col = lax.broadcasted_iota(jnp.int32, (H, C, C), 2)
    incl = row >= col
    strict = row > col
    ltri = jnp.where(incl, 1.0, 0.0).astype(jnp.bfloat16)
    eye = jnp.where(row == col, 1.0, 0.0).astype(jnp.float32)

    lw = lw_ref[0]
    r = r_ref[0]
    k = k_ref[0]
    v = v_ref[0]
    a = a_ref[0]
    b = b_ref[0]
    hi = _bf(lw)
    rem = lw - hi.astype(jnp.float32)
    mid = _bf(rem)
    lo = _bf(rem - mid.astype(jnp.float32))
    lp = _bmm(ltri, hi) + _bmm(ltri, mid) + _bmm(ltri, lo)
    lp_end = lp[:, C - 1:C, :]
    p_end = jnp.exp(lp_end)
    p_inv = jnp.exp(-lp)
    at = a * jnp.exp(lp - lw)
    rt = r * jnp.exp(lp)
    bt = b * p_inv
    kt = k * p_inv
    p_hat = jnp.exp(lp_end - lp)
    bh = b * p_hat
    kh = k * p_hat

    n_ab = jnp.where(strict, _bmm_nt(at, bt), 0.0)
    a_ak = jnp.where(strict, _bmm_nt(at, kt), 0.0)
    a_rb = jnp.where(incl, _bmm_nt(rt, bt), 0.0)
    a_rk = jnp.where(incl, _bmm_nt(rt, kt), 0.0)

    t_inv = eye + n_ab
    n_pow = n_ab
    span = 2
    while span < C:
        n_pow = _bmm(n_pow, n_pow)
        t_inv = _bmm(t_inv, eye + n_pow)
        span *= 2

    s = s_scr[...]
    rhs = _bmm_nt(at, s) + _bmm(a_ak, v)
    u = _bmm(t_inv, rhs)
    y_ref[0] = _bmm_nt(rt, s) + _bmm(a_rb, u) + _bmm(a_rk, v)
    s_new = s * p_end + _bmm_tn(u, bh) + _bmm_tn(v, kh)
    s_scr[...] = s_new

    @pl.when(c == pl.num_programs(1) - 1)
    def _():
        s_out_ref[0] = s_new


def wkv_chunked(r, lw, k, v, a, b, s0, C):
    B, H, T, D = r.shape
    assert T % C == 0
    seq = pl.BlockSpec((1, H, C, D), lambda bi, ci: (bi, 0, ci, 0))
    st = pl.BlockSpec((1, H, D, D), lambda bi, ci: (bi, 0, 0, 0))
    return pl.pallas_call(
        functools.partial(_wkv_chunk_kernel, C=C, H=H),
        grid=(B, T // C),
        in_specs=[seq] * 6 + [st],
        out_specs=[seq, st],
        out_shape=[jax.ShapeDtypeStruct((B, H, T, D), jnp.float32),
                   jax.ShapeDtypeStruct((B, H, D, D), jnp.float32)],
        scratch_shapes=[pltpu.VMEM((H, D, D), jnp.float32)],
        compiler_params=pltpu.CompilerParams(dimension_semantics=("arbitrary", "arbitrary")),
        name="wkv_chunk",
    )(r, lw, k, v, a, b, s0)


def _nsa_prompt_kernel(q_ref, kc_ref, vc_ref, cover_ref, ks_ref, vs_ref, kw_ref, vw_ref, g_ref, o_ref,
                       *, TQ, TK, NC, NCP):
    f32 = jnp.float32
    bf16 = jnp.bfloat16
    qb = pl.program_id(2)
    R = HPG * TQ
    q = q_ref[0, 0].reshape(R, HEAD_DIM)
    t_pos = qb * TQ + lax.broadcasted_iota(jnp.int32, (TQ, 1), 0)

    n_idx = lax.broadcasted_iota(jnp.int32, (1, NCP), 1)
    c_ok = ((n_idx * CMP_STRIDE + (CMP_LEN - 1)) <= t_pos) & (n_idx < NC)
    s_c = _dot_nt(q, kc_ref[0, 0]).reshape(HPG, TQ, NCP)
    s_c = jnp.where(c_ok[None], s_c, NEG)
    m_c = jnp.max(s_c, axis=-1, keepdims=True)
    p_c = jnp.where(c_ok[None], jnp.exp(s_c - m_c), 0.0)
    l_c = jnp.sum(p_c, axis=-1, keepdims=True)
    p_c = p_c / jnp.where(l_c > 0.0, l_c, 1.0)
    p_cb = p_c.astype(bf16)
    o_c = jnp.dot(p_cb.reshape(R, NCP), vc_ref[0, 0], preferred_element_type=f32)

    cover_t = cover_ref[...]
    imp = _dot_nt(cover_t, p_cb[0])
    for h in range(1, HPG):
        imp = imp + _dot_nt(cover_t, p_cb[h])
    s_col = lax.broadcasted_iota(jnp.int32, (LANES, 1), 0)
    t_row = qb * TQ + lax.broadcasted_iota(jnp.int32, (1, TQ), 1)
    cur = t_row // SEL_BLOCK
    forced = (s_col == 0) | (s_col == cur) | (s_col == cur - 1)
    causal = (s_col * SEL_BLOCK) <= t_row
    score = jnp.where(forced, 1e6, imp)
    score = jnp.where(causal, score, NEG)
    s_col_f = s_col.astype(f32)
    sel_t = jnp.zeros((LANES, TQ), f32)
    for _ in range(SEL_TOP):
        top = jnp.max(score, axis=0, keepdims=True)
        first = jnp.min(jnp.where(score == top, s_col_f, float(LANES)), axis=0, keepdims=True)
        hit = s_col_f == first
        sel_t = jnp.where(hit & (top > 0.5 * NEG), 1.0, sel_t)
        score = jnp.where(hit, -3e38, score)

    sel_neg = ((sel_t.T - 1.0) * (-NEG)).astype(bf16)
    q_aug = jnp.concatenate([jnp.concatenate([sel_neg] * HPG, axis=0), q], axis=1)
    key_off = lax.broadcasted_iota(jnp.int32, (1, TK), 1)

    def sel_tile(j, carry, diagonal):
        m, l, acc = carry
        start = pl.multiple_of(j * TK, TK)
        k = ks_ref[0, 0, pl.ds(start, TK), :]
        v = vs_ref[0, 0, pl.ds(start, TK), :]
        s = _dot_nt(q_aug, k).reshape(HPG, TQ, TK)
        if diagonal:
            s = jnp.where(((key_off + j * TK) <= t_pos)[None], s, NEG)
        m_new = jnp.maximum(m, jnp.max(s, axis=-1, keepdims=True))
        p = jnp.exp(s - m_new)
        alpha = jnp.exp(m - m_new)
        l = alpha * l + jnp.sum(p, axis=-1, keepdims=True)
        pv = jnp.dot(p.astype(bf16).reshape(R, TK), v, preferred_element_type=f32)
        acc = alpha * acc + pv.reshape(HPG, TQ, HEAD_DIM)
        return m_new, l, acc

    m0 = jnp.full((HPG, TQ, 1), NEG, f32)
    l0 = jnp.zeros((HPG, TQ, 1), f32)
    a0 = jnp.zeros((HPG, TQ, HEAD_DIM), f32)
    n_full = (qb * TQ) // TK
    carry = lax.fori_loop(0, n_full, functools.partial(sel_tile, diagonal=False), (m0, l0, a0))
    _, l_s, acc_s = sel_tile(n_full, carry, True)
    o_s = acc_s / l_s

    n_w = WINDOW // TQ + 1
    lane_q = lax.broadcasted_iota(jnp.int32, (1, TQ), 1)
    s_w, ok_w, v_w = [], [], []
    for i in range(n_w):
        kb = qb - (n_w - 1) + i
        start = pl.multiple_of(jnp.maximum(kb, 0) * TQ, TQ)
        k = kw_ref[0, 0, pl.ds(start, TQ), :]
        v_w.append(vw_ref[0, 0, pl.ds(start, TQ), :])
        kpos = kb * TQ + lane_q
        ok = (kpos <= t_pos) & (kpos >= t_pos - WINDOW) & (kpos >= 0)
        ok_w.append(ok)
        s_w.append(jnp.where(ok[None], _dot_nt(q, k).reshape(HPG, TQ, TQ), NEG))
    m_w = s_w[0].max(axis=-1, keepdims=True)
    for i in range(1, n_w):
        m_w = jnp.maximum(m_w, s_w[i].max(axis=-1, keepdims=True))
    l_w = jnp.zeros((HPG, TQ, 1), f32)
    acc_w = jnp.zeros((R, HEAD_DIM), f32)
    for i in range(n_w):
        p = jnp.where(ok_w[i][None], jnp.exp(s_w[i] - m_w), 0.0)
        l_w = l_w + jnp.sum(p, axis=-1, keepdims=True)
        acc_w = acc_w + jnp.dot(p.astype(bf16).reshape(R, TQ), v_w[i], preferred_element_type=f32)
    o_w = acc_w.reshape(HPG, TQ, HEAD_DIM) / l_w

    g = g_ref[0, 0]
    o_ref[0, 0] = (g[:, :, 0:1] * o_c.reshape(HPG, TQ, HEAD_DIM) + g[:, :, 1:2] * o_s
                   + g[:, :, 2:3] * o_w)


def nsa_prompt_attention(q, kc, vc, ks, vs, kw, vw, gates, TQ=256, TK=1024):
    B, G, _, T, D = q.shape
    NC = kc.shape[2]
    NCP = -(-NC // LANES) * LANES
    NS = T // SEL_BLOCK
    assert NS <= LANES and T % TK == 0 and TK % TQ == 0 and WINDOW % TQ == 0
    kc = jnp.pad(kc, ((0, 0), (0, 0), (0, NCP - NC), (0, 0))).astype(jnp.bfloat16)
    vc = jnp.pad(vc, ((0, 0), (0, 0), (0, NCP - NC), (0, 0))).astype(jnp.bfloat16)
    c_start = jnp.arange(NCP) * CMP_STRIDE
    s_start = jnp.arange(LANES) * SEL_BLOCK
    cover = ((c_start[:, None] < s_start[None, :] + SEL_BLOCK)
             & (c_start[:, None] + CMP_LEN > s_start[None, :])
             & (jnp.arange(NCP)[:, None] < NC) & (jnp.arange(LANES)[None, :] < NS)).astype(jnp.bfloat16)
    onehot = (jnp.arange(T)[:, None] // SEL_BLOCK == jnp.arange(LANES)[None, :]).astype(jnp.bfloat16)
    ks = jnp.concatenate([jnp.broadcast_to(onehot, (B, G, T, LANES)), ks], axis=-1)
    full = lambda n, d=D: pl.BlockSpec((1, 1, n, d), lambda b, g, i: (b, g, 0, 0))
    qspec = pl.BlockSpec((1, 1, HPG, TQ, D), lambda b, g, i: (b, g, 0, i, 0))
    return pl.pallas_call(
        functools.partial(_nsa_prompt_kernel, TQ=TQ, TK=TK, NC=NC, NCP=NCP),
        grid=(B, G, T // TQ),
        in_specs=[qspec, full(NCP), full(NCP),
                  pl.BlockSpec((LANES, NCP), lambda b, g, i: (0, 0)),
                  full(T, LANES + D), full(T), full(T), full(T),
                  pl.BlockSpec((1, 1, HPG, TQ, 3), lambda b, g, i: (b, g, 0, i, 0))],
        out_specs=qspec,
        out_shape=jax.ShapeDtypeStruct((B, G, HPG, T, D), jnp.float32),
        compiler_params=pltpu.CompilerParams(
            dimension_semantics=("arbitrary", "arbitrary", "arbitrary"),
            vmem_limit_bytes=VMEM_LIMIT),
        name="nsa_prompt",
    )(q, kc, vc, cover.T, ks, vs, kw, vw, gates)


def _cmp_sample_kernel(pt_ref, *refs, n_chunk):
    pages = refs[:PAGES_PER_STEP]
    wcat_ref, c1_ref, w2_ref, b2_ref, o_ref, seq = refs[PAGES_PER_STEP:]
    j = pl.program_id(1)
    rows = PAGE_SIZE // CMP_STRIDE
    for i in range(PAGES_PER_STEP):
        seq[pl.ds(pl.multiple_of((j * PAGES_PER_STEP + i) * rows, rows), rows), :] = pages[i][0]

    @pl.when(j == pl.num_programs(1) - 1)
    def _():
        row_w = 2 * KV_WIDTH
        for kv in range(2):
            for g in range(N_KV):
                q = kv * N_KV + g
                acc = jnp.zeros((n_chunk, 2 * CMP_HIDDEN), jnp.float32)
                for p in range(CMP_STRIDE):
                    lo = p * row_w + q * HEAD_DIM
                    x = _bf(seq[:, lo:lo + HEAD_DIM])
                    acc = acc + jnp.dot(x, wcat_ref[kv, p], preferred_element_type=jnp.float32)
                first = acc[:, :CMP_HIDDEN]
                second = acc[:, CMP_HIDDEN:]
                second = jnp.concatenate([second[1:], second[:1]], axis=0)
                hid = jax.nn.gelu(first + second + c1_ref[kv])
                out = jnp.dot(_bf(hid), w2_ref[kv], preferred_element_type=jnp.float32) + b2_ref[kv]
                o_ref[0, kv, g] = _bf(out)


def compress_sample(cache, page_table, pe, w1, b1, w2, b2):
    n_pool = cache.shape[0]
    DB, n_pages = page_table.shape
    assert n_pages % PAGES_PER_STEP == 0
    rows = PAGE_SIZE // CMP_STRIDE
    width = CMP_STRIDE * 2 * KV_WIDTH
    n_chunk = n_pages * rows
    view = cache.reshape(n_pool, rows, width)
    w1r = w1.reshape(2, CMP_LEN, HEAD_DIM, CMP_HIDDEN)
    wcat = _bf(jnp.concatenate([w1r[:, :CMP_STRIDE], w1r[:, CMP_STRIDE:]], axis=-1))
    c1 = (jnp.einsum('kn,knh->kh', pe.reshape(2, CMP_LEN * HEAD_DIM), w1) + b1).reshape(2, 1, CMP_HIDDEN)
    page_spec = lambda i: pl.BlockSpec((1, rows, width), lambda b, j, pt: (pt[b, j * PAGES_PER_STEP + i], 0, 0))
    const = lambda shape: pl.BlockSpec(shape, lambda b, j, pt: (0,) * len(shape))
    grid_spec = pltpu.PrefetchScalarGridSpec(
        num_scalar_prefetch=1,
        grid=(DB, n_pages // PAGES_PER_STEP),
        in_specs=[page_spec(i) for i in range(PAGES_PER_STEP)]
        + [const((2, CMP_STRIDE, HEAD_DIM, 2 * CMP_HIDDEN)), const((2, 1, CMP_HIDDEN)),
           const((2, CMP_HIDDEN, HEAD_DIM)), const((2, 1, HEAD_DIM))],
        out_specs=pl.BlockSpec((1, 2, N_KV, n_chunk, HEAD_DIM), lambda b, j, pt: (b, 0, 0, 0, 0)),
        scratch_shapes=[pltpu.VMEM((n_chunk, width), jnp.float32)],
    )
    return pl.pallas_call(
        functools.partial(_cmp_sample_kernel, n_chunk=n_chunk),
        grid_spec=grid_spec,
        out_shape=jax.ShapeDtypeStruct((DB, 2, N_KV, n_chunk, HEAD_DIM), jnp.bfloat16),
        compiler_params=pltpu.CompilerParams(dimension_semantics=("arbitrary", "arbitrary"),
                                             vmem_limit_bytes=VMEM_LIMIT),
        name="cmp_sample",
    )(page_table, *([view] * PAGES_PER_STEP), wcat, c1, _bf(w2), b2.reshape(2, 1, HEAD_DIM))


def _nsa_sample_kernel(pt_ref, *refs, DS, NC, past):
    pages = refs[:PAGES_PER_STEP]
    (q_ref, kvc_ref, cover_ref, new_sel_ref, win_ref, new_win_ref, g_ref, o_ref,
     qaug, m_s, l_s, acc_s, oc_s) = refs[PAGES_PER_STEP:]
    f32 = jnp.float32
    j = pl.program_id(1)
    R = HPG * DS
    t_row = lax.broadcasted_iota(jnp.int32, (R, 1), 0) % DS
    n_chunk = kvc_ref.shape[3]

    @pl.when(j == 0)
    def _():
        n_idx = lax.broadcasted_iota(jnp.int32, (1, n_chunk), 1)
        s_col = lax.broadcasted_iota(jnp.int32, (LANES, 1), 0)
        s_col_f = s_col.astype(f32)
        last_blk = past // SEL_BLOCK - 1
        for g in range(N_KV):
            q = q_ref[0, g]
            s_c = jnp.where(n_idx < NC, _dot_nt(q, kvc_ref[0, 0, g]), NEG)
            p_c = jnp.exp(s_c - jnp.max(s_c, axis=-1, keepdims=True))
            p_c = p_c / jnp.sum(p_c, axis=-1, keepdims=True)
            p_cb = _bf(p_c)
            oc_s[g] = jnp.dot(p_cb, kvc_ref[0, 1, g], preferred_element_type=f32)
            imp_rows = _dot_nt(cover_ref[...], p_cb)
            imp = imp_rows[:, 0:DS]
            for h in range(1, HPG):
                imp = imp + imp_rows[:, h * DS:(h + 1) * DS]
            forced = (s_col == 0) | (s_col == last_blk)
            score = jnp.where(forced, 1e6, imp)
            score = jnp.where(s_col <= last_blk, score, NEG)
            sel_t = jnp.zeros((LANES, DS), f32)
            for _ in range(SEL_TOP - 1):
                top = jnp.max(score, axis=0, keepdims=True)
                first = jnp.min(jnp.where(score == top, s_col_f, float(LANES)), axis=0, keepdims=True)
                hit = s_col_f == first
                sel_t = jnp.where(hit & (top > 0.5 * NEG), 1.0, sel_t)
                score = jnp.where(hit, -3e38, score)
            sel_neg = _bf((sel_t.T - 1.0) * (-NEG))
            qaug[g] = jnp.concatenate([jnp.concatenate([sel_neg] * HPG, axis=0), q], axis=1)
        m_s[...] = jnp.full(m_s.shape, NEG, f32)
        l_s[...] = jnp.zeros(l_s.shape, f32)
        acc_s[...] = jnp.zeros(acc_s.shape, f32)

    TK = PAGES_PER_STEP * PAGE_SIZE
    blk = (lax.broadcasted_iota(jnp.int32, (TK, LANES), 0) // SEL_BLOCK
           + j * (TK // SEL_BLOCK))
    onehot = _bf(jnp.where(blk == lax.broadcasted_iota(jnp.int32, (TK, LANES), 1), 1.0, 0.0))
    rows = jnp.concatenate([pages[i][0] for i in range(PAGES_PER_STEP)], axis=0)
    for g in range(N_KV):
        k = _bf(rows[:, g * HEAD_DIM:(g + 1) * HEAD_DIM])
        v = _bf(rows[:, KV_WIDTH + g * HEAD_DIM:KV_WIDTH + (g + 1) * HEAD_DIM])
        s = _dot_nt(qaug[g], jnp.concatenate([onehot, k], axis=1))
        m_new = jnp.maximum(m_s[g], jnp.max(s, axis=-1, keepdims=True))
        p = jnp.exp(s - m_new)
        alpha = jnp.exp(m_s[g] - m_new)
        l_s[g] = alpha * l_s[g] + jnp.sum(p, axis=-1, keepdims=True)
        acc_s[g] = alpha * acc_s[g] + jnp.dot(_bf(p), v, preferred_element_type=f32)
        m_s[g] = m_new

    @pl.when(j == pl.num_programs(1) - 1)
    def _():
        j_new = lax.broadcasted_iota(jnp.int32, (1, NEW_PAD), 1)
        ok_new = (j_new <= t_row) & (j_new < DS)
        w_idx = lax.broadcasted_iota(jnp.int32, (1, WINDOW), 1)
        ok_win = w_idx >= t_row
        win = win_ref[0]
        for g in range(N_KV):
            q = q_ref[0, g]
            s = jnp.where(ok_new, _dot_nt(q, new_sel_ref[0, 0, g]), NEG)
            m_new = jnp.maximum(m_s[g], jnp.max(s, axis=-1, keepdims=True))
            p = jnp.where(ok_new, jnp.exp(s - m_new), 0.0)
            alpha = jnp.exp(m_s[g] - m_new)
            l_fin = alpha * l_s[g] + jnp.sum(p, axis=-1, keepdims=True)
            o_sel = (alpha * acc_s[g]
                     + jnp.dot(_bf(p), new_sel_ref[0, 1, g], preferred_element_type=f32)) / l_fin
            s_a = jnp.where(ok_win, _dot_nt(q, win[:, g * HEAD_DIM:(g + 1) * HEAD_DIM]), NEG)
            s_b = jnp.where(ok_new, _dot_nt(q, new_win_ref[0, 0, g]), NEG)
            m_w = jnp.maximum(jnp.max(s_a, axis=-1, keepdims=True), jnp.max(s_b, axis=-1, keepdims=True))
            p_a = jnp.where(ok_win, jnp.exp(s_a - m_w), 0.0)
            p_b = jnp.where(ok_new, jnp.exp(s_b - m_w), 0.0)
            l_w = jnp.sum(p_a, axis=-1, keepdims=True) + jnp.sum(p_b, axis=-1, keepdims=True)
            v_a = win[:, KV_WIDTH + g * HEAD_DIM:KV_WIDTH + (g + 1) * HEAD_DIM]
            o_win = (jnp.dot(_bf(p_a), _bf(v_a), preferred_element_type=f32)
                     + jnp.dot(_bf(p_b), new_win_ref[0, 1, g], preferred_element_type=f32)) / l_w
            gt = g_ref[0, g]
            o_ref[0, g] = gt[:, 0:1] * oc_s[g] + gt[:, 1:2] * o_sel + gt[:, 2:3] * o_win


def nsa_sample_attention(q, kvc, sel_cache, win_cache, page_table, new_sel, new_win, gates, past):
    DB, G, R, D = q.shape
    DS = R // HPG
    n_pool = sel_cache.shape[0]
    n_pages = page_table.shape[1]
    n_chunk = kvc.shape[3]
    NC = (past + DS) // CMP_STRIDE - 1
    assert past % SEL_BLOCK == 0 and DS < CMP_STRIDE and DS <= NEW_PAD and past // SEL_BLOCK <= LANES
    assert n_pages % PAGES_PER_STEP == 0 and win_cache.shape[1] == WINDOW and NC < n_chunk + 1
    c_start = jnp.arange(n_chunk) * CMP_STRIDE
    s_start = jnp.arange(LANES) * SEL_BLOCK
    cover_t = _bf((c_start[None, :] < s_start[:, None] + SEL_BLOCK)
                  & (c_start[None, :] + CMP_LEN > s_start[:, None])
                  & (jnp.arange(n_chunk)[None, :] < NC) & (jnp.arange(LANES)[:, None] < past // SEL_BLOCK))
    row_w = 2 * KV_WIDTH
    sel_view = sel_cache.reshape(n_pool, PAGE_SIZE, row_w)
    win_view = win_cache.reshape(DB, WINDOW, row_w)
    page_spec = lambda i: pl.BlockSpec((1, PAGE_SIZE, row_w), lambda b, j, pt: (pt[b, j * PAGES_PER_STEP + i], 0, 0))
    per_b = lambda shape: pl.BlockSpec((1,) + shape, lambda b, j, pt: (b,) + (0,) * len(shape))
    grid_spec = pltpu.PrefetchScalarGridSpec(
        num_scalar_prefetch=1,
        grid=(DB, n_pages // PAGES_PER_STEP),
        in_specs=[page_spec(i) for i in range(PAGES_PER_STEP)]
        + [per_b((G, R, D)), per_b((2, G, n_chunk, D)),
           pl.BlockSpec((LANES, n_chunk), lambda b, j, pt: (0, 0)),
           per_b((2, G, NEW_PAD, D)), per_b((WINDOW, row_w)), per_b((2, G, NEW_PAD, D)), per_b((G, R, 3))],
        out_specs=per_b((G, R, D)),
        scratch_shapes=[pltpu.VMEM((G, R, LANES + D), jnp.bfloat16), pltpu.VMEM((G, R, 1), jnp.float32),
                        pltpu.VMEM((G, R, 1), jnp.float32), pltpu.VMEM((G, R, D), jnp.float32),
                        pltpu.VMEM((G, R, D), jnp.float32)],
    )
    return pl.pallas_call(
        functools.partial(_nsa_sample_kernel, DS=DS, NC=NC, past=past),
        grid_spec=grid_spec,
        out_shape=jax.ShapeDtypeStruct((DB, G, R, D), jnp.float32),
        compiler_params=pltpu.CompilerParams(dimension_semantics=("arbitrary", "arbitrary"),
                                             vmem_limit_bytes=VMEM_LIMIT),
        name="nsa_sample",
    )(page_table, *([sel_view] * PAGES_PER_STEP), q, kvc, cover_t, new_sel, win_view, new_win, gates)


def _moe_mlp_kernel(be_ref, base_ref, nval_ref, order_ref, x_hbm, w1_ref, b1_ref, w2_ref, b2_ref, o_hbm,
                    w1s, w2s, xbuf, ybuf, in_sem, out_sem, *, BM, T):
    i = pl.program_id(0)
    n = pl.num_programs(0)
    s = i % 2

    def gather(block, sl):
        base = base_ref[block]

        def body(r, c):
            tok = order_ref[base + r] >> K_SHIFT
            pltpu.make_async_copy(x_hbm.at[pl.ds(tok, 1)], xbuf.at[sl, pl.ds(r, 1)], in_sem.at[sl]).start()
            return c
        lax.fori_loop(0, BM, body, 0, unroll=8)

    @pl.when(i == 0)
    def _():
        gather(0, 0)

    @pl.when(i + 1 < n)
    def _():
        gather(i + 1, 1 - s)

    e = be_ref[i]
    prev = be_ref[jnp.maximum(i - 1, 0)]

    @pl.when((i == 0) | (e != prev))
    def _():
        w1s[...] = w1_ref[0].astype(jnp.bfloat16)
        w2s[...] = w2_ref[0].astype(jnp.bfloat16)

    def rows_in(sl):
        return pltpu.make_async_copy(x_hbm.at[pl.ds(0, BM)], xbuf.at[sl], in_sem.at[sl])

    def rows_out(sl):
        return pltpu.make_async_copy(ybuf.at[sl], o_hbm.at[pl.ds(0, BM)], out_sem.at[sl])

    rows_in(s).wait()

    @pl.when(i >= 2)
    def _():
        rows_out(s).wait()

    x = xbuf[s].astype(jnp.bfloat16)
    h = jnp.dot(x, w1s[...], preferred_element_type=jnp.float32) + b1_ref[0]
    glu = jnp.minimum(h[:, :D_FF], SWIGLU_LIMIT)
    lin = jnp.clip(h[:, D_FF:], -SWIGLU_LIMIT, SWIGLU_LIMIT)
    act = glu * jax.nn.sigmoid(SWIGLU_ALPHA * glu) * (lin + 1.0)
    ybuf[s] = jnp.dot(act.astype(jnp.bfloat16), w2s[...], preferred_element_type=jnp.float32) + b2_ref[0]

    base = base_ref[i]
    nval = nval_ref[i]
    spare = TOP_K * T + i * BM - base - nval

    def scatter(r, c):
        a = order_ref[base + r]
        row = jnp.where(r < nval, (a & (TOP_K - 1)) * T + (a >> K_SHIFT), spare + r)
        pltpu.make_async_copy(ybuf.at[s, pl.ds(r, 1)], o_hbm.at[pl.ds(row, 1)], out_sem.at[s]).start()
        return c
    lax.fori_loop(0, BM, scatter, 0, unroll=8)

    @pl.when(i == n - 1)
    def _():
        rows_out(s).wait()

    @pl.when((i == n - 1) & (n >= 2))
    def _():
        rows_out(1 - s).wait()


def moe_mlp(x, order, block_e, block_base, block_nval, w1, b1, w2, b2, BM):
    T, D = x.shape
    n_blocks = block_e.shape[0]
    E = w1.shape[0]
    grid_spec = pltpu.PrefetchScalarGridSpec(
        num_scalar_prefetch=4,
        grid=(n_blocks,),
        in_specs=[pl.BlockSpec(memory_space=pl.ANY),
                  pl.BlockSpec((1, D, 2 * D_FF), lambda i, be, *_: (be[i], 0, 0)),
                  pl.BlockSpec((1, 1, 2 * D_FF), lambda i, be, *_: (be[i], 0, 0)),
                  pl.BlockSpec((1, D_FF, D), lambda i, be, *_: (be[i], 0, 0)),
                  pl.BlockSpec((1, 1, D), lambda i, be, *_: (be[i], 0, 0))],
        out_specs=pl.BlockSpec(memory_space=pl.ANY),
        scratch_shapes=[pltpu.VMEM((D, 2 * D_FF), jnp.bfloat16), pltpu.VMEM((D_FF, D), jnp.bfloat16),
                        pltpu.VMEM((2, BM, D), jnp.float32), pltpu.VMEM((2, BM, D), jnp.float32),
                        pltpu.SemaphoreType.DMA((2,)), pltpu.SemaphoreType.DMA((2,))],
    )
    return pl.pallas_call(
        functools.partial(_moe_mlp_kernel, BM=BM, T=T),
        grid_spec=grid_spec,
        out_shape=jax.ShapeDtypeStruct((n_blocks * BM, D), jnp.float32),
        compiler_params=pltpu.CompilerParams(dimension_semantics=("arbitrary",),
                                             vmem_limit_bytes=VMEM_LIMIT),
        name="moe_mlp",
    )(block_e, block_base, block_nval, order, x, w1, b1.reshape(E, 1, -1), w2, b2.reshape(E, 1, -1))


def moe(x, router_w, router_b, mlp1_w, mlp1_b, mlp2_w, mlp2_b, BM=128):
    T, D = x.shape
    rw = jnp.pad(router_w, ((0, 0), (0, LANES - N_EXPERTS)))
    logits = pallas_matmul(x, rw)[:, :N_EXPERTS] + router_b
    top_v, top_e = lax.top_k(logits, TOP_K)
    gate = jax.nn.softmax(top_v, axis=-1)
    n_assign = T * TOP_K
    _, order = lax.sort((top_e.reshape(-1).astype(jnp.int32), jnp.arange(n_assign, dtype=jnp.int32)), num_keys=1)
    counts = jnp.sum(top_e.reshape(-1, 1) == jnp.arange(N_EXPERTS)[None, :], axis=0).astype(jnp.int32)
    padded = (counts + BM - 1) // BM * BM
    start = jnp.cumsum(counts) - counts
    pend = jnp.cumsum(padded)
    pstart = pend - padded
    n_blocks = -(-n_assign // BM) + N_EXPERTS
    row0 = jnp.arange(n_blocks, dtype=jnp.int32) * BM
    block_e = jnp.minimum(jnp.sum(pend[None, :] <= row0[:, None], axis=1), N_EXPERTS - 1).astype(jnp.int32)
    off = row0 - pstart[block_e]
    block_nval = jnp.clip(counts[block_e] - off, 0, BM).astype(jnp.int32)
    block_base = jnp.clip(start[block_e] + off, 0, n_assign).astype(jnp.int32)
    order = jnp.concatenate([order, jnp.zeros((BM,), jnp.int32)])
    yb = moe_mlp(x, order, block_e, block_base, block_nval, mlp1_w, mlp1_b, mlp2_w, mlp2_b, BM)
    y = gate[:, 0:1] * yb[:T]
    for k in range(1, TOP_K):
        y = y + gate[:, k:k + 1] * yb[k * T:(k + 1) * T]
    return y


def layer_norm(x, g, b):
    mu = x.mean(-1, keepdims=True)
    var = jnp.square(x - mu).mean(-1, keepdims=True)
    return (x - mu) * lax.rsqrt(var + LN_EPS) * g + b


def rope(x, pos):
    half = ROT_DIM // 2
    inv = ROPE_THETA ** (-jnp.arange(half, dtype=jnp.float32) * 2.0 / ROT_DIM)
    ang = pos.astype(jnp.float32)[:, None] * inv
    cos, sin = jnp.cos(ang)[:, None, :], jnp.sin(ang)[:, None, :]
    x1, x2 = x[..., :half], x[..., half:ROT_DIM]
    return jnp.concatenate([x1 * cos - x2 * sin, x2 * cos + x1 * sin, x[..., ROT_DIM:]], axis=-1)


G0 = R_COLS + A_WIDTH + 6 * KV_WIDTH


def project(x2, w_in):
    w_main = jnp.concatenate([w_in[:, :G0], w_in[:, G0 + N_GATE:]], axis=1)
    w_gate = jnp.pad(w_in[:, G0:G0 + N_GATE], ((0, 0), (0, LANES - N_GATE)))
    return pallas_matmul(x2, w_main), pallas_matmul(x2, w_gate)[:, :N_GATE]


def split_projection(p, pg, B, T, pos):
    pr = p[:, :R_COLS].reshape(B, T, R_COLS)
    pa = p[:, R_COLS:G0].reshape(B, T, G0 - R_COLS)
    q = rope(pa[..., :A_WIDTH].reshape(B, T, N_HEADS, HEAD_DIM), pos)
    kvs = [pa[..., A_WIDTH + i * KV_WIDTH:A_WIDTH + (i + 1) * KV_WIDTH].reshape(B, T, N_KV, HEAD_DIM)
           for i in range(6)]
    kvs = [rope(z, pos) if i % 2 == 0 else z for i, z in enumerate(kvs)]
    gates = jax.nn.sigmoid(pg).reshape(B, T, N_KV, HPG, 3)
    return pr, q, kvs, gates


def rwkv_mixer(pr, shift_prev, wkv0, mu, w0, w_w2, a0, w_a2, g_w2, k_k, k_a, r_k, gn_g, gn_b):
    B, T, _ = pr.shape
    prev = jnp.concatenate([shift_prev[:, None, :], pr[:, :-1]], axis=1)
    xm = pr + (prev - pr) * mu
    o1, o2, o3 = R_WIDTH, 2 * R_WIDTH, 3 * R_WIDTH
    o4 = o3 + LORA_W
    o5 = o4 + LORA_A
    r, k, v = xm[..., :o1], xm[..., o1:o2], xm[..., o2:o3]
    xw, xa, xg = xm[..., o3:o4], xm[..., o4:o5], xm[..., o5:]
    w_log = -jax.nn.softplus(-(w0 + jnp.tanh(xw) @ w_w2)) - 0.5
    a = jax.nn.sigmoid(a0 + xa @ w_a2)
    g = jax.nn.sigmoid(xg) @ g_w2
    heads = lambda z: z.reshape(B, T, R_HEADS, R_HEAD)
    kk = heads(k * k_k)
    kk = kk / jnp.maximum(jnp.linalg.norm(kk, axis=-1, keepdims=True), 1e-12)
    k_h = heads(k * (1.0 + (a - 1.0) * k_a))
    r_h, v_h, a_h = heads(r), heads(v), heads(a)
    log_decay = -jnp.exp(heads(w_log))
    C = 64 if T % 64 == 0 else 8
    Tp = -(-T // C) * C
    hm = lambda z: jnp.pad(jnp.swapaxes(z, 1, 2), ((0, 0), (0, 0), (0, Tp - T), (0, 0)))
    y, wkv = wkv_chunked(hm(r_h), hm(log_decay), hm(k_h), hm(v_h), hm(-kk), hm(kk * a_h), wkv0, C)
    y = jnp.swapaxes(y[:, :, :T], 1, 2)
    mean = y.mean(-1, keepdims=True)
    var = jnp.square(y - mean).mean(-1, keepdims=True)
    yn = ((y - mean) * lax.rsqrt(var + GN_EPS)).reshape(B, T, R_WIDTH) * gn_g + gn_b
    bonus = (jnp.sum(r_h * k_h * r_k, axis=-1, keepdims=True) * v_h).reshape(B, T, R_WIDTH)
    return (yn + bonus) * g, wkv, pr[:, -1]


def compress(kv, pe, w1, b1, w2, b2):
    B, L = kv.shape[:2]
    n_chunk = L // CMP_STRIDE
    ch = kv[:, :n_chunk * CMP_STRIDE].reshape(B, n_chunk, CMP_STRIDE, N_KV, HEAD_DIM)
    blk = jnp.concatenate([ch[:, :-1], ch[:, 1:]], axis=2) + pe[:, None, :]
    blk = jnp.transpose(blk, (0, 3, 1, 2, 4)).reshape(B, N_KV, n_chunk - 1, CMP_LEN * HEAD_DIM)
    return jax.nn.gelu(blk @ w1 + b1) @ w2 + b2


def nsa_prompt(q, kvs, gates, pe, w1, b1, w2, b2):
    kc_raw, vc_raw, ks, vs, kw, vw = kvs
    B, T = q.shape[:2]
    kc = compress(kc_raw, pe[0], w1[0], b1[0], w2[0], b2[0])
    vc = compress(vc_raw, pe[1], w1[1], b1[1], w2[1], b2[1])
    qg = _bf(jnp.transpose(q.reshape(B, T, N_KV, HPG, HEAD_DIM), (0, 2, 3, 1, 4)) * (HEAD_DIM ** -0.5))
    tk = lambda z: _bf(jnp.swapaxes(z, 1, 2))
    o = nsa_prompt_attention(qg, kc, vc, tk(ks), tk(vs), tk(kw), tk(vw),
                             jnp.transpose(gates, (0, 2, 3, 1, 4)))
    return jnp.transpose(o, (0, 3, 1, 2, 4)).reshape(B, T, A_WIDTH)


def nsa_sample(q, kvs, gates, cmp_cache, sel_cache, win_cache, page_table, pe, w1, b1, w2, b2):
    kc_new, vc_new, ks_new, vs_new, kw_new, vw_new = kvs
    DB, DS = q.shape[:2]
    past = page_table.shape[1] * PAGE_SIZE
    kvc = compress_sample(cmp_cache, page_table, pe, w1, b1, w2, b2)
    qg = _bf(jnp.transpose(q.reshape(DB, DS, N_KV, HPG, HEAD_DIM), (0, 2, 3, 1, 4)) * (HEAD_DIM ** -0.5))
    qg = qg.reshape(DB, N_KV, HPG * DS, HEAD_DIM)
    gt = jnp.transpose(gates, (0, 2, 3, 1, 4)).reshape(DB, N_KV, HPG * DS, 3)
    pack = lambda k, v: _bf(jnp.pad(jnp.transpose(jnp.stack([k, v], axis=1), (0, 1, 3, 2, 4)),
                                    ((0, 0), (0, 0), (0, 0), (0, NEW_PAD - DS), (0, 0))))
    o = nsa_sample_attention(qg, kvc, sel_cache, win_cache, page_table, pack(ks_new, vs_new),
                             pack(kw_new, vw_new), gt, past)
    o = jnp.transpose(o.reshape(DB, N_KV, HPG, DS, HEAD_DIM), (0, 3, 1, 2, 4)).reshape(DB, DS, -1)
    new_k = jnp.concatenate([win_cache[:, DS:, 0], kw_new], axis=1)
    new_v = jnp.concatenate([win_cache[:, DS:, 1], vw_new], axis=1)
    return o, jnp.stack([new_k, new_v], axis=2)


def merge_and_ffn(x, y_r, y_a, mg, w_pa, w_pb, w_o, ln1_g, ln1_b, router_w, router_b,
                  mlp1_w, mlp1_b, mlp2_w, mlp2_b, ln2_g, ln2_b):
    m = (jax.nn.sigmoid(mg[:, :D_MODEL]) * pallas_matmul(y_r, w_pa)
         + jax.nn.sigmoid(mg[:, D_MODEL:]) * pallas_matmul(y_a, w_pb))
    h = layer_norm(DN_ALPHA * x + pallas_matmul(m, w_o), ln1_g, ln1_b)
    f = moe(h, router_w, router_b, mlp1_w, mlp1_b, mlp2_w, mlp2_b)
    return layer_norm(DN_ALPHA * h + f, ln2_g, ln2_b)


def kernel(x_prompt, x_sample, cache_cmp_kv, cache_sel_kv, cache_win_kv, state_wkv, state_shift,
           page_table, w_in, mu_shift, w0, w_w2, a0, w_a2, g_w2, k_k, k_a, r_k, gn_g, gn_b,
           cmp_pe, cmp_w1, cmp_b1, cmp_w2, cmp_b2, w_pa, w_pb, w_o, ln1_g, ln1_b,
           router_w, router_b, mlp1_w, mlp1_b, mlp2_w, mlp2_b, ln2_g, ln2_b):
    B, T, D = x_prompt.shape
    DB, DS, _ = x_sample.shape
    n_p = B * T
    past = page_table.shape[1] * PAGE_SIZE
    pos_p = jnp.arange(T)
    pos_s = past + jnp.arange(DS)
    wb_p = min(WINDOW, T)
    h_all = jnp.concatenate([x_prompt.reshape(n_p, D), x_sample.reshape(DB * DS, D)])
    cmp_p, sel_p, win_p, wkv_p, shift_p = [], [], [], [], []
    cmp_s, sel_s, win_s, wkv_s, shift_s = [], [], [], [], []
    for l in range(DEPTH):
        rwkv_w = (mu_shift[l], w0[l], w_w2[l], a0[l], w_a2[l], g_w2[l], k_k[l], k_a[l], r_k[l],
                  gn_g[l], gn_b[l])
        cmp_w = (cmp_pe[l], cmp_w1[l], cmp_b1[l], cmp_w2[l], cmp_b2[l])
        out_w = (w_pa[l], w_pb[l], w_o[l], ln1_g[l], ln1_b[l], router_w[l], router_b[l],
                 mlp1_w[l], mlp1_b[l], mlp2_w[l], mlp2_b[l], ln2_g[l], ln2_b[l])
        p_all, pg_all = project(h_all, w_in[l])
        pr, q, kvs, gates = split_projection(p_all[:n_p], pg_all[:n_p], B, T, pos_p)
        y_r, wkv, shift = rwkv_mixer(pr, jnp.zeros((B, R_COLS), pr.dtype),
                                     jnp.zeros((B, R_HEADS, R_HEAD, R_HEAD), jnp.float32), *rwkv_w)
        y_a = nsa_prompt(q, kvs, gates, *cmp_w)
        cmp_p.append(jnp.stack([kvs[0], kvs[1]], axis=2))
        sel_p.append(jnp.stack([kvs[2], kvs[3]], axis=2))
        win_p.append(jnp.stack([kvs[4][:, T - wb_p:], kvs[5][:, T - wb_p:]], axis=2))
        wkv_p.append(wkv)
        shift_p.append(shift)
        pr, q, kvs, gates = split_projection(p_all[n_p:], pg_all[n_p:], DB, DS, pos_s)
        y_r_s, wkv, shift = rwkv_mixer(pr, state_shift[l], state_wkv[l], *rwkv_w)
        y_a_s, new_win = nsa_sample(q, kvs, gates, cache_cmp_kv[l], cache_sel_kv[l], cache_win_kv[l],
                                    page_table, *cmp_w)
        cmp_s.append(jnp.stack([kvs[0], kvs[1]], axis=2))
        sel_s.append(jnp.stack([kvs[2], kvs[3]], axis=2))
        win_s.append(new_win)
        wkv_s.append(wkv)
        shift_s.append(shift)
        rows = lambda a, b: jnp.concatenate([a.reshape(n_p, -1), b.reshape(DB * DS, -1)])
        h_all = merge_and_ffn(h_all, rows(y_r, y_r_s), rows(y_a, y_a_s), p_all[:, G0:], *out_w)
    hp = h_all[:n_p].reshape(B, T, D)
    hs = h_all[n_p:].reshape(DB, DS, D)
    return (hp, hs, jnp.stack(cmp_p), jnp.stack(sel_p), jnp.stack(win_p), jnp.stack(wkv_p),
            jnp.stack(shift_p), jnp.stack(cmp_s), jnp.stack(sel_s), jnp.stack(win_s),
            jnp.stack(wkv_s), jnp.stack(shift_s))
```

```python
import functools

import jax
import jax.numpy as jnp
from jax import lax
from jax.experimental import pallas as pl
from jax.experimental.pallas import tpu as pltpu

D_MODEL = 1024
DEPTH = 1
PAGE_SIZE = 128

R_HEADS = 8
R_HEAD = 64
R_WIDTH = R_HEADS * R_HEAD
LORA_W = 64
LORA_A = 64
LORA_G = 128
R_COLS = 3 * R_WIDTH + LORA_W + LORA_A + LORA_G
GN_EPS = 64e-5

N_HEADS = 8
N_KV = 2
HPG = N_HEADS // N_KV
HEAD_DIM = 64
A_WIDTH = N_HEADS * HEAD_DIM
KV_WIDTH = N_KV * HEAD_DIM
N_GATE = 3 * N_HEADS
A_COLS = A_WIDTH + 6 * KV_WIDTH + N_GATE
ROT_DIM = HEAD_DIM // 4
ROPE_THETA = 500000.0
CMP_STRIDE = 16
CMP_LEN = 2 * CMP_STRIDE
CMP_HIDDEN = 256
SEL_BLOCK = 64
SEL_TOP = 16
WINDOW = 512

N_EXPERTS = 32
TOP_K = 4
K_SHIFT = 2
D_FF = 1024
SWIGLU_LIMIT = 7.0
SWIGLU_ALPHA = 1.702

DN_ALPHA = (2 * DEPTH) ** 0.25
LN_EPS = 1e-5
NEG = -1e30

LANES = 128
PAGES_PER_STEP = 8
NEW_PAD = 8
VMEM_LIMIT = 56 * 1024 * 1024


def _bf(x):
    return x.astype(jnp.bfloat16)


def _dot(a, b):
    return jnp.dot(_bf(a), _bf(b), preferred_element_type=jnp.float32)


def _dot_nt(a, b):
    return lax.dot_general(_bf(a), _bf(b), (((1,), (1,)), ((), ())),
                           preferred_element_type=jnp.float32)


def _mm_kernel(x_ref, w_ref, o_ref):
    o_ref[...] = _dot(x_ref[...], w_ref[...])


def pallas_matmul(x, w):
    x, w = _bf(x), _bf(w)
    M, K = x.shape
    N = w.shape[1]
    tm = next(t for t in (512, 384, 256, 128, M) if M % t == 0)
    resident = 2 * (2 * K * N + 4 * tm * N + 2 * tm * K) <= VMEM_LIMIT - (8 << 20)
    tn = N if resident else next(t for t in (512, 256, LANES) if N % t == 0)
    return pl.pallas_call(
        _mm_kernel,
        grid=(N // tn, M // tm),
        in_specs=[pl.BlockSpec((tm, K), lambda j, i: (i, 0)),
                  pl.BlockSpec((K, tn), lambda j, i: (0, j))],
        out_specs=pl.BlockSpec((tm, tn), lambda j, i: (i, j)),
        out_shape=jax.ShapeDtypeStruct((M, N), jnp.float32),
        compiler_params=pltpu.CompilerParams(vmem_limit_bytes=VMEM_LIMIT),
        name="mm",
    )(x, w)


def _bmm(a, b):
    return lax.dot_general(_bf(a), _bf(b), (((2,), (1,)), ((0,), (0,))), preferred_element_type=jnp.float32)


def _bmm_nt(a, b):
    return lax.dot_general(_bf(a), _bf(b), (((2,), (2,)), ((0,), (0,))), preferred_element_type=jnp.float32)


def _bmm_tn(a, b):
    return lax.dot_general(_bf(a), _bf(b), (((1,), (1,)), ((0,), (0,))), preferred_element_type=jnp.float32)


def _wkv_chunk_kernel(r_ref, lw_ref, k_ref, v_ref, a_ref, b_ref, s0_ref, y_ref, s_out_ref, s_scr,
                      *, C, H):
    c = pl.program_id(1)

    @pl.when(c == 0)
    def _():
        s_scr[...] = s0_ref[0]

    row = lax.broadcasted_iota(jnp.int32, (H, C, C), 1)
    col = lax.broadcasted_iota(jnp.int32, (H, C, C), 2)
    incl = row >= col
    strict = row > col
    ltri = jnp.where(incl, 1.0, 0.0).astype(jnp.bfloat16)
    eye = jnp.where(row == col, 1.0, 0.0).astype(jnp.float32)

    lw = lw_ref[0]
    r = r_ref[0]
    k = k_ref[0]
    v = v_ref[0]
    a = a_ref[0]
    b = b_ref[0]
    hi = _bf(lw)
    rem = lw - hi.astype(jnp.float32)
    mid = _bf(rem)
    lo = _bf(rem - mid.astype(jnp.float32))
    lp = _bmm(ltri, hi) + _bmm(ltri, mid) + _bmm(ltri, lo)
    lp_end = lp[:, C - 1:C, :]
    p_end = jnp.exp(lp_end)
    p_inv = jnp.exp(-lp)
    at = a * jnp.exp(lp - lw)
    rt = r * jnp.exp(lp)
    bt = b * p_inv
    kt = k * p_inv
    p_hat = jnp.exp(lp_end - lp)
    bh = b * p_hat
    kh = k * p_hat

    n_ab = jnp.where(strict, _bmm_nt(at, bt), 0.0)
    a_ak = jnp.where(strict, _bmm_nt(at, kt), 0.0)
    a_rb = jnp.where(incl, _bmm_nt(rt, bt), 0.0)
    a_rk = jnp.where(incl, _bmm_nt(rt, kt), 0.0)

    t_inv = eye + n_ab
    n_pow = n_ab
    span = 2
    while span < C:
        n_pow = _bmm(n_pow, n_pow)
        t_inv = _bmm(t_inv, eye + n_pow)
        span *= 2

    s = s_scr[...]
    rhs = _bmm_nt(at, s) + _bmm(a_ak, v)
    u = _bmm(t_inv, rhs)
    y_ref[0] = _bmm_nt(rt, s) + _bmm(a_rb, u) + _bmm(a_rk, v)
    s_new = s * p_end + _bmm_tn(u, bh) + _bmm_tn(v, kh)
    s_scr[...] = s_new

    @pl.when(c == pl.num_programs(1) - 1)
    def _():
        s_out_ref[0] = s_new


def wkv_chunked(r, lw, k, v, a, b, s0, C):
    B, H, T, D = r.shape
    assert T % C == 0
    seq = pl.BlockSpec((1, H, C, D), lambda bi, ci: (bi, 0, ci, 0))
    st = pl.BlockSpec((1, H, D, D), lambda bi, ci: (bi, 0, 0, 0))
    return pl.pallas_call(
        functools.partial(_wkv_chunk_kernel, C=C, H=H),
        grid=(B, T // C),
        in_specs=[seq] * 6 + [st],
        out_specs=[seq, st],
        out_shape=[jax.ShapeDtypeStruct((B, H, T, D), jnp.float32),
                   jax.ShapeDtypeStruct((B, H, D, D), jnp.float32)],
        scratch_shapes=[pltpu.VMEM((H, D, D), jnp.float32)],
        compiler_params=pltpu.CompilerParams(dimension_semantics=("arbitrary", "arbitrary")),
        name="wkv_chunk",
    )(r, lw, k, v, a, b, s0)


def _nsa_prompt_kernel(q_ref, kc_ref, vc_ref, cover_ref, ks_ref, vs_ref, kw_ref, vw_ref, g_ref, o_ref,
                       *, TQ, TK, NC, NCP):
    f32 = jnp.float32
    bf16 = jnp.bfloat16
    qb = pl.program_id(2)
    R = HPG * TQ
    q = q_ref[0, 0].reshape(R, HEAD_DIM)
    t_pos = qb * TQ + lax.broadcasted_iota(jnp.int32, (TQ, 1), 0)

    n_idx = lax.broadcasted_iota(jnp.int32, (1, NCP), 1)
    c_ok = ((n_idx * CMP_STRIDE + (CMP_LEN - 1)) <= t_pos) & (n_idx < NC)
    s_c = _dot_nt(q, kc_ref[0, 0]).reshape(HPG, TQ, NCP)
    s_c = jnp.where(c_ok[None], s_c, NEG)
    m_c = jnp.max(s_c, axis=-1, keepdims=True)
    p_c = jnp.where(c_ok[None], jnp.exp(s_c - m_c), 0.0)
    l_c = jnp.sum(p_c, axis=-1, keepdims=True)
    p_c = p_c / jnp.where(l_c > 0.0, l_c, 1.0)
    p_cb = p_c.astype(bf16)
    o_c = jnp.dot(p_cb.reshape(R, NCP), vc_ref[0, 0], preferred_element_type=f32)

    cover_t = cover_ref[...]
    imp = _dot_nt(cover_t, p_cb[0])
    for h in range(1, HPG):
        imp = imp + _dot_nt(cover_t, p_cb[h])
    s_col = lax.broadcasted_iota(jnp.int32, (LANES, 1), 0)
    t_row = qb * TQ + lax.broadcasted_iota(jnp.int32, (1, TQ), 1)
    cur = t_row // SEL_BLOCK
    forced = (s_col == 0) | (s_col == cur) | (s_col == cur - 1)
    causal = (s_col * SEL_BLOCK) <= t_row
    score = jnp.where(forced, 1e6, imp)
    score = jnp.where(causal, score, NEG)
    s_col_f = s_col.astype(f32)
    sel_t = jnp.zeros((LANES, TQ), f32)
    for _ in range(SEL_TOP):
        top = jnp.max(score, axis=0, keepdims=True)
        first = jnp.min(jnp.where(score == top, s_col_f, float(LANES)), axis=0, keepdims=True)
        hit = s_col_f == first
        sel_t = jnp.where(hit & (top > 0.5 * NEG), 1.0, sel_t)
        score = jnp.where(hit, -3e38, score)

    sel_neg = ((sel_t.T - 1.0) * (-NEG)).astype(bf16)
    q_aug = jnp.concatenate([jnp.concatenate([sel_neg] * HPG, axis=0), q], axis=1)
    key_off = lax.broadcasted_iota(jnp.int32, (1, TK), 1)

    def sel_tile(j, carry, diagonal):
        m, l, acc = carry
        start = pl.multiple_of(j * TK, TK)
        k = ks_ref[0, 0, pl.ds(start, TK), :]
        v = vs_ref[0, 0, pl.ds(start, TK), :]
        s = _dot_nt(q_aug, k).reshape(HPG, TQ, TK)
        if diagonal:
            s = jnp.where(((key_off + j * TK) <= t_pos)[None], s, NEG)
        m_new = jnp.maximum(m, jnp.max(s, axis=-1, keepdims=True))
        p = jnp.exp(s - m_new)
        alpha = jnp.exp(m - m_new)
        l = alpha * l + jnp.sum(p, axis=-1, keepdims=True)
        pv = jnp.dot(p.astype(bf16).reshape(R, TK), v, preferred_element_type=f32)
        acc = alpha * acc + pv.reshape(HPG, TQ, HEAD_DIM)
        return m_new, l, acc

    m0 = jnp.full((HPG, TQ, 1), NEG, f32)
    l0 = jnp.zeros((HPG, TQ, 1), f32)
    a0 = jnp.zeros((HPG, TQ, HEAD_DIM), f32)
    n_full = (qb * TQ) // TK
    carry = lax.fori_loop(0, n_full, functools.partial(sel_tile, diagonal=False), (m0, l0, a0))
    _, l_s, acc_s = sel_tile(n_full, carry, True)
    o_s = acc_s / l_s

    n_w = WINDOW // TQ + 1
    lane_q = lax.broadcasted_iota(jnp.int32, (1, TQ), 1)
    s_w, ok_w, v_w = [], [], []
    for i in range(n_w):
        kb = qb - (n_w - 1) + i
        start = pl.multiple_of(jnp.maximum(kb, 0) * TQ, TQ)
        k = kw_ref[0, 0, pl.ds(start, TQ), :]
        v_w.append(vw_ref[0, 0, pl.ds(start, TQ), :])
        kpos = kb * TQ + lane_q
        ok = (kpos <= t_pos) & (kpos >= t_pos - WINDOW) & (kpos >= 0)
        ok_w.append(ok)
        s_w.append(jnp.where(ok[None], _dot_nt(q, k).reshape(HPG, TQ, TQ), NEG))
    m_w = s_w[0].max(axis=-1, keepdims=True)
    for i in range(1, n_w):
        m_w = jnp.maximum(m_w, s_w[i].max(axis=-1, keepdims=True))
    l_w = jnp.zeros((HPG, TQ, 1), f32)
    acc_w = jnp.zeros((R, HEAD_DIM), f32)
    for i in range(n_w):
        p = jnp.where(ok_w[i][None], jnp.exp(s_w[i] - m_w), 0.0)
        l_w = l_w + jnp.sum(p, axis=-1, keepdims=True)
        acc_w = acc_w + jnp.dot(p.astype(bf16).reshape(R, TQ), v_w[i], preferred_element_type=f32)
    o_w = acc_w.reshape(HPG, TQ, HEAD_DIM) / l_w

    g = g_ref[0, 0]
    o_ref[0, 0] = (g[:, :, 0:1] * o_c.reshape(HPG, TQ, HEAD_DIM) + g[:, :, 1:2] * o_s
                   + g[:, :, 2:3] * o_w)


def nsa_prompt_attention(q, kc, vc, ks, vs, kw, vw, gates, TQ=256, TK=1024):
    B, G, _, T, D = q.shape
    NC = kc.shape[2]
    NCP = -(-NC // LANES) * LANES
    NS = T // SEL_BLOCK
    assert NS <= LANES and T % TK == 0 and TK % TQ == 0 and WINDOW % TQ == 0
    kc = jnp.pad(kc, ((0, 0), (0, 0), (0, NCP - NC), (0, 0))).astype(jnp.bfloat16)
    vc = jnp.pad(vc, ((0, 0), (0, 0), (0, NCP - NC), (0, 0))).astype(jnp.bfloat16)
    c_start = jnp.arange(NCP) * CMP_STRIDE
    s_start = jnp.arange(LANES) * SEL_BLOCK
    cover = ((c_start[:, None] < s_start[None, :] + SEL_BLOCK)
             & (c_start[:, None] + CMP_LEN > s_start[None, :])
             & (jnp.arange(NCP)[:, None] < NC) & (jnp.arange(LANES)[None, :] < NS)).astype(jnp.bfloat16)
    onehot = (jnp.arange(T)[:, None] // SEL_BLOCK == jnp.arange(LANES)[None, :]).astype(jnp.bfloat16)
    ks = jnp.concatenate([jnp.broadcast_to(onehot, (B, G, T, LANES)), ks], axis=-1)
    full = lambda n, d=D: pl.BlockSpec((1, 1, n, d), lambda b, g, i: (b, g, 0, 0))
    qspec = pl.BlockSpec((1, 1, HPG, TQ, D), lambda b, g, i: (b, g, 0, i, 0))
    return pl.pallas_call(
        functools.partial(_nsa_prompt_kernel, TQ=TQ, TK=TK, NC=NC, NCP=NCP),
        grid=(B, G, T // TQ),
        in_specs=[qspec, full(NCP), full(NCP),
                  pl.BlockSpec((LANES, NCP), lambda b, g, i: (0, 0)),
                  full(T, LANES + D), full(T), full(T), full(T),
                  pl.BlockSpec((1, 1, HPG, TQ, 3), lambda b, g, i: (b, g, 0, i, 0))],
        out_specs=qspec,
        out_shape=jax.ShapeDtypeStruct((B, G, HPG, T, D), jnp.float32),
        compiler_params=pltpu.CompilerParams(
            dimension_semantics=("arbitrary", "arbitrary", "arbitrary"),
            vmem_limit_bytes=VMEM_LIMIT),
        name="nsa_prompt",
    )(q, kc, vc, cover.T, ks, vs, kw, vw, gates)


def _cmp_sample_kernel(pt_ref, *refs, n_chunk):
    pages = refs[:PAGES_PER_STEP]
    wcat_ref, c1_ref, w2_ref, b2_ref, o_ref, seq, tile = refs[PAGES_PER_STEP:]
    j = pl.program_id(1)
    rows = PAGE_SIZE // CMP_STRIDE
    for i in range(PAGES_PER_STEP):
        dst = pl.multiple_of((j * PAGES_PER_STEP + i) * rows, rows)
        for kv in range(2):
            for g in range(N_KV):
                q = kv * N_KV + g
                tile[...] = pages[i][0, kv, g].T
                for p in range(CMP_STRIDE):
                    seq[p * 2 * N_KV + q, pl.ds(dst, rows), :] = tile[pl.ds(p, rows, stride=CMP_STRIDE), :]

    @pl.when(j == pl.num_programs(1) - 1)
    def _():
        for kv in range(2):
            for g in range(N_KV):
                q = kv * N_KV + g
                acc = jnp.zeros((n_chunk, 2 * CMP_HIDDEN), jnp.float32)
                for p in range(CMP_STRIDE):
                    acc = acc + jnp.dot(_bf(seq[p * 2 * N_KV + q]), wcat_ref[kv, p],
                                        preferred_element_type=jnp.float32)
                first = acc[:, :CMP_HIDDEN]
                second = acc[:, CMP_HIDDEN:]
                second = jnp.concatenate([second[1:], second[:1]], axis=0)
                hid = jax.nn.gelu(first + second + c1_ref[kv])
                out = jnp.dot(_bf(hid), w2_ref[kv], preferred_element_type=jnp.float32) + b2_ref[kv]
                o_ref[0, kv, g] = _bf(out)


def compress_sample(cache, page_table, pe, w1, b1, w2, b2):
    n_pool = cache.shape[0]
    DB, n_pages = page_table.shape
    assert n_pages % PAGES_PER_STEP == 0
    rows = PAGE_SIZE // CMP_STRIDE
    n_chunk = n_pages * rows
    view = jnp.transpose(cache, (0, 2, 3, 4, 1))
    w1r = w1.reshape(2, CMP_LEN, HEAD_DIM, CMP_HIDDEN)
    wcat = _bf(jnp.concatenate([w1r[:, :CMP_STRIDE], w1r[:, CMP_STRIDE:]], axis=-1))
    c1 = (jnp.einsum('kn,knh->kh', pe.reshape(2, CMP_LEN * HEAD_DIM), w1) + b1).reshape(2, 1, CMP_HIDDEN)
    page_spec = lambda i: pl.BlockSpec((1, 2, N_KV, HEAD_DIM, PAGE_SIZE),
                                       lambda b, j, pt: (pt[b, j * PAGES_PER_STEP + i], 0, 0, 0, 0))
    const = lambda shape: pl.BlockSpec(shape, lambda b, j, pt: (0,) * len(shape))
    grid_spec = pltpu.PrefetchScalarGridSpec(
        num_scalar_prefetch=1,
        grid=(DB, n_pages // PAGES_PER_STEP),
        in_specs=[page_spec(i) for i in range(PAGES_PER_STEP)]
        + [const((2, CMP_STRIDE, HEAD_DIM, 2 * CMP_HIDDEN)), const((2, 1, CMP_HIDDEN)),
           const((2, CMP_HIDDEN, HEAD_DIM)), const((2, 1, HEAD_DIM))],
        out_specs=pl.BlockSpec((1, 2, N_KV, n_chunk, HEAD_DIM), lambda b, j, pt: (b, 0, 0, 0, 0)),
        scratch_shapes=[pltpu.VMEM((CMP_STRIDE * 2 * N_KV, n_chunk, HEAD_DIM), jnp.float32),
                        pltpu.VMEM((PAGE_SIZE, HEAD_DIM), jnp.float32)],
    )
    return pl.pallas_call(
        functools.partial(_cmp_sample_kernel, n_chunk=n_chunk),
        grid_spec=grid_spec,
        out_shape=jax.ShapeDtypeStruct((DB, 2, N_KV, n_chunk, HEAD_DIM), jnp.bfloat16),
        compiler_params=pltpu.CompilerParams(dimension_semantics=("arbitrary", "arbitrary"),
                                             vmem_limit_bytes=VMEM_LIMIT),
        name="cmp_sample",
    )(page_table, *([view] * PAGES_PER_STEP), wcat, c1, _bf(w2), b2.reshape(2, 1, HEAD_DIM))


def _nsa_sample_kernel(pt_ref, *refs, DS, NC, past):
    pages = refs[:PAGES_PER_STEP]
    (q_ref, kvc_ref, cover_ref, new_sel_ref, win_ref, new_win_ref, g_ref, o_ref,
     qaug, m_s, l_s, acc_s, oc_s) = refs[PAGES_PER_STEP:]
    f32 = jnp.float32
    j = pl.program_id(1)
    R = HPG * DS
    t_row = lax.broadcasted_iota(jnp.int32, (R, 1), 0) % DS
    n_chunk = kvc_ref.shape[3]

    @pl.when(j == 0)
    def _():
        n_idx = lax.broadcasted_iota(jnp.int32, (1, n_chunk), 1)
        s_col = lax.broadcasted_iota(jnp.int32, (LANES, 1), 0)
        s_col_f = s_col.astype(f32)
        last_blk = past // SEL_BLOCK - 1
        for g in range(N_KV):
            q = q_ref[0, g]
            s_c = jnp.where(n_idx < NC, _dot_nt(q, kvc_ref[0, 0, g]), NEG)
            p_c = jnp.exp(s_c - jnp.max(s_c, axis=-1, keepdims=True))
            p_c = p_c / jnp.sum(p_c, axis=-1, keepdims=True)
            p_cb = _bf(p_c)
            oc_s[g] = jnp.dot(p_cb, kvc_ref[0, 1, g], preferred_element_type=f32)
            imp_rows = _dot_nt(cover_ref[...], p_cb)
            imp = imp_rows[:, 0:DS]
            for h in range(1, HPG):
                imp = imp + imp_rows[:, h * DS:(h + 1) * DS]
            forced = (s_col == 0) | (s_col == last_blk)
            score = jnp.where(forced, 1e6, imp)
            score = jnp.where(s_col <= last_blk, score, NEG)
            sel_t = jnp.zeros((LANES, DS), f32)
            for _ in range(SEL_TOP - 1):
                top = jnp.max(score, axis=0, keepdims=True)
                first = jnp.min(jnp.where(score == top, s_col_f, float(LANES)), axis=0, keepdims=True)
                hit = s_col_f == first
                sel_t = jnp.where(hit & (top > 0.5 * NEG), 1.0, sel_t)
                score = jnp.where(hit, -3e38, score)
            sel_neg = _bf((sel_t.T - 1.0) * (-NEG))
            qaug[g] = jnp.concatenate([jnp.concatenate([sel_neg] * HPG, axis=0), q], axis=1)
        m_s[...] = jnp.full(m_s.shape, NEG, f32)
        l_s[...] = jnp.zeros(l_s.shape, f32)
        acc_s[...] = jnp.zeros(acc_s.shape, f32)

    TK = PAGES_PER_STEP * PAGE_SIZE
    blk = (lax.broadcasted_iota(jnp.int32, (LANES, TK), 1) // SEL_BLOCK
           + j * (TK // SEL_BLOCK))
    onehot = _bf(jnp.where(blk == lax.broadcasted_iota(jnp.int32, (LANES, TK), 0), 1.0, 0.0))
    for g in range(N_KV):
        k_t = _bf(jnp.concatenate([pages[i][0, 0, g] for i in range(PAGES_PER_STEP)], axis=1))
        v_t = _bf(jnp.concatenate([pages[i][0, 1, g] for i in range(PAGES_PER_STEP)], axis=1))
        s = jnp.dot(qaug[g], jnp.concatenate([onehot, k_t], axis=0), preferred_element_type=f32)
        m_new = jnp.maximum(m_s[g], jnp.max(s, axis=-1, keepdims=True))
        p = jnp.exp(s - m_new)
        alpha = jnp.exp(m_s[g] - m_new)
        l_s[g] = alpha * l_s[g] + jnp.sum(p, axis=-1, keepdims=True)
        acc_s[g] = alpha * acc_s[g] + _dot_nt(p, v_t)
        m_s[g] = m_new

    @pl.when(j == pl.num_programs(1) - 1)
    def _():
        j_new = lax.broadcasted_iota(jnp.int32, (1, NEW_PAD), 1)
        ok_new = (j_new <= t_row) & (j_new < DS)
        w_idx = lax.broadcasted_iota(jnp.int32, (1, WINDOW), 1)
        ok_win = w_idx >= t_row
        for g in range(N_KV):
            q = q_ref[0, g]
            s = jnp.where(ok_new, _dot_nt(q, new_sel_ref[0, 0, g]), NEG)
            m_new = jnp.maximum(m_s[g], jnp.max(s, axis=-1, keepdims=True))
            p = jnp.where(ok_new, jnp.exp(s - m_new), 0.0)
            alpha = jnp.exp(m_s[g] - m_new)
            l_fin = alpha * l_s[g] + jnp.sum(p, axis=-1, keepdims=True)
            o_sel = (alpha * acc_s[g]
                     + jnp.dot(_bf(p), new_sel_ref[0, 1, g], preferred_element_type=f32)) / l_fin
            s_a = jnp.where(ok_win, jnp.dot(q, _bf(win_ref[0, 0, g]), preferred_element_type=f32), NEG)
            s_b = jnp.where(ok_new, _dot_nt(q, new_win_ref[0, 0, g]), NEG)
            m_w = jnp.maximum(jnp.max(s_a, axis=-1, keepdims=True), jnp.max(s_b, axis=-1, keepdims=True))
            p_a = jnp.where(ok_win, jnp.exp(s_a - m_w), 0.0)
            p_b = jnp.where(ok_new, jnp.exp(s_b - m_w), 0.0)
            l_w = jnp.sum(p_a, axis=-1, keepdims=True) + jnp.sum(p_b, axis=-1, keepdims=True)
            o_win = (_dot_nt(p_a, win_ref[0, 1, g])
                     + jnp.dot(_bf(p_b), new_win_ref[0, 1, g], preferred_element_type=f32)) / l_w
            gt = g_ref[0, g]
            o_ref[0, g] = gt[:, 0:1] * oc_s[g] + gt[:, 1:2] * o_sel + gt[:, 2:3] * o_win


def nsa_sample_attention(q, kvc, sel_cache, win_cache, page_table, new_sel, new_win, gates, past):
    DB, G, R, D = q.shape
    DS = R // HPG
    n_pool = sel_cache.shape[0]
    n_pages = page_table.shape[1]
    n_chunk = kvc.shape[3]
    NC = (past + DS) // CMP_STRIDE - 1
    assert past % SEL_BLOCK == 0 and DS < CMP_STRIDE and DS <= NEW_PAD and past // SEL_BLOCK <= LANES
    assert n_pages % PAGES_PER_STEP == 0 and win_cache.shape[1] == WINDOW and NC < n_chunk + 1
    c_start = jnp.arange(n_chunk) * CMP_STRIDE
    s_start = jnp.arange(LANES) * SEL_BLOCK
    cover_t = _bf((c_start[None, :] < s_start[:, None] + SEL_BLOCK)
                  & (c_start[None, :] + CMP_LEN > s_start[:, None])
                  & (jnp.arange(n_chunk)[None, :] < NC) & (jnp.arange(LANES)[:, None] < past // SEL_BLOCK))
    sel_view = jnp.transpose(sel_cache, (0, 2, 3, 4, 1))
    win_view = jnp.transpose(win_cache, (0, 2, 3, 4, 1))
    page_spec = lambda i: pl.BlockSpec((1, 2, G, D, PAGE_SIZE),
                                       lambda b, j, pt: (pt[b, j * PAGES_PER_STEP + i], 0, 0, 0, 0))
    per_b = lambda shape: pl.BlockSpec((1,) + shape, lambda b, j, pt: (b,) + (0,) * len(shape))
    grid_spec = pltpu.PrefetchScalarGridSpec(
        num_scalar_prefetch=1,
        grid=(DB, n_pages // PAGES_PER_STEP),
        in_specs=[page_spec(i) for i in range(PAGES_PER_STEP)]
        + [per_b((G, R, D)), per_b((2, G, n_chunk, D)),
           pl.BlockSpec((LANES, n_chunk), lambda b, j, pt: (0, 0)),
           per_b((2, G, NEW_PAD, D)), per_b((2, G, D, WINDOW)), per_b((2, G, NEW_PAD, D)), per_b((G, R, 3))],
        out_specs=per_b((G, R, D)),
        scratch_shapes=[pltpu.VMEM((G, R, LANES + D), jnp.bfloat16), pltpu.VMEM((G, R, 1), jnp.float32),
                        pltpu.VMEM((G, R, 1), jnp.float32), pltpu.VMEM((G, R, D), jnp.float32),
                        pltpu.VMEM((G, R, D), jnp.float32)],
    )
    return pl.pallas_call(
        functools.partial(_nsa_sample_kernel, DS=DS, NC=NC, past=past),
        grid_spec=grid_spec,
        out_shape=jax.ShapeDtypeStruct((DB, G, R, D), jnp.float32),
        compiler_params=pltpu.CompilerParams(dimension_semantics=("arbitrary", "arbitrary"),
                                             vmem_limit_bytes=VMEM_LIMIT),
        name="nsa_sample",
    )(page_table, *([sel_view] * PAGES_PER_STEP), q, kvc, cover_t, new_sel, win_view, new_win, gates)


def _moe_mlp_kernel(be_ref, base_ref, nval_ref, order_ref, x_hbm, w1_ref, b1_ref, w2_ref, b2_ref, o_hbm,
                    w1s, w2s, xbuf, ybuf, in_sem, out_sem, *, BM, T):
    i = pl.program_id(0)
    n = pl.num_programs(0)
    s = i % 2

    def gather(block, sl):
        base = base_ref[block]

        def body(r, c):
            tok = order_ref[base + r] >> K_SHIFT
            pltpu.make_async_copy(x_hbm.at[pl.ds(tok, 1)], xbuf.at[sl, pl.ds(r, 1)], in_sem.at[sl]).start()
            return c
        lax.fori_loop(0, BM, body, 0, unroll=8)

    @pl.when(i == 0)
    def _():
        gather(0, 0)

    @pl.when(i + 1 < n)
    def _():
        gather(i + 1, 1 - s)

    e = be_ref[i]
    prev = be_ref[jnp.maximum(i - 1, 0)]

    @pl.when((i == 0) | (e != prev))
    def _():
        w1s[...] = w1_ref[0].astype(jnp.bfloat16)
        w2s[...] = w2_ref[0].astype(jnp.bfloat16)

    def rows_in(sl):
        return pltpu.make_async_copy(x_hbm.at[pl.ds(0, BM)], xbuf.at[sl], in_sem.at[sl])

    def rows_out(sl):
        return pltpu.make_async_copy(ybuf.at[sl], o_hbm.at[pl.ds(0, BM)], out_sem.at[sl])

    rows_in(s).wait()

    @pl.when(i >= 2)
    def _():
        rows_out(s).wait()

    x = xbuf[s].astype(jnp.bfloat16)
    h = jnp.dot(x, w1s[...], preferred_element_type=jnp.float32) + b1_ref[0]
    glu = jnp.minimum(h[:, :D_FF], SWIGLU_LIMIT)
    lin = jnp.clip(h[:, D_FF:], -SWIGLU_LIMIT, SWIGLU_LIMIT)
    act = glu * jax.nn.sigmoid(SWIGLU_ALPHA * glu) * (lin + 1.0)
    ybuf[s] = jnp.dot(act.astype(jnp.bfloat16), w2s[...], preferred_element_type=jnp.float32) + b2_ref[0]

    base = base_ref[i]
    nval = nval_ref[i]
    spare = TOP_K * T + i * BM - base - nval

    def scatter(r, c):
        a = order_ref[base + r]
        row = jnp.where(r < nval, (a & (TOP_K - 1)) * T + (a >> K_SHIFT), spare + r)
        pltpu.make_async_copy(ybuf.at[s, pl.ds(r, 1)], o_hbm.at[pl.ds(row, 1)], out_sem.at[s]).start()
        return c
    lax.fori_loop(0, BM, scatter, 0, unroll=8)

    @pl.when(i == n - 1)
    def _():
        rows_out(s).wait()

    @pl.when((i == n - 1) & (n >= 2))
    def _():
        rows_out(1 - s).wait()


def moe_mlp(x, order, block_e, block_base, block_nval, w1, b1, w2, b2, BM):
    T, D = x.shape
    n_blocks = block_e.shape[0]
    E = w1.shape[0]
    grid_spec = pltpu.PrefetchScalarGridSpec(
        num_scalar_prefetch=4,
        grid=(n_blocks,),
        in_specs=[pl.BlockSpec(memory_space=pl.ANY),
                  pl.BlockSpec((1, D, 2 * D_FF), lambda i, be, *_: (be[i], 0, 0)),
                  pl.BlockSpec((1, 1, 2 * D_FF), lambda i, be, *_: (be[i], 0, 0)),
                  pl.BlockSpec((1, D_FF, D), lambda i, be, *_: (be[i], 0, 0)),
                  pl.BlockSpec((1, 1, D), lambda i, be, *_: (be[i], 0, 0))],
        out_specs=pl.BlockSpec(memory_space=pl.ANY),
        scratch_shapes=[pltpu.VMEM((D, 2 * D_FF), jnp.bfloat16), pltpu.VMEM((D_FF, D), jnp.bfloat16),
                        pltpu.VMEM((2, BM, D), jnp.float32), pltpu.VMEM((2, BM, D), jnp.float32),
                        pltpu.SemaphoreType.DMA((2,)), pltpu.SemaphoreType.DMA((2,))],
    )
    return pl.pallas_call(
        functools.partial(_moe_mlp_kernel, BM=BM, T=T),
        grid_spec=grid_spec,
        out_shape=jax.ShapeDtypeStruct((n_blocks * BM, D), jnp.float32),
        compiler_params=pltpu.CompilerParams(dimension_semantics=("arbitrary",),
                                             vmem_limit_bytes=VMEM_LIMIT),
        name="moe_mlp",
    )(block_e, block_base, block_nval, order, x, w1, b1.reshape(E, 1, -1), w2, b2.reshape(E, 1, -1))


def moe(x, router_w, router_b, mlp1_w, mlp1_b, mlp2_w, mlp2_b, BM=256):
    T, D = x.shape
    rw = jnp.pad(router_w, ((0, 0), (0, LANES - N_EXPERTS)))
    logits = pallas_matmul(x, rw)[:, :N_EXPERTS] + router_b
    top_v, top_e = lax.top_k(logits, TOP_K)
    gate = jax.nn.softmax(top_v, axis=-1)
    n_assign = T * TOP_K
    _, order = lax.sort((top_e.reshape(-1).astype(jnp.int32), jnp.arange(n_assign, dtype=jnp.int32)), num_keys=1)
    counts = jnp.sum(top_e.reshape(-1, 1) == jnp.arange(N_EXPERTS)[None, :], axis=0).astype(jnp.int32)
    padded = (counts + BM - 1) // BM * BM
    start = jnp.cumsum(counts) - counts
    pend = jnp.cumsum(padded)
    pstart = pend - padded
    n_blocks = -(-n_assign // BM) + N_EXPERTS
    row0 = jnp.arange(n_blocks, dtype=jnp.int32) * BM
    block_e = jnp.minimum(jnp.sum(pend[None, :] <= row0[:, None], axis=1), N_EXPERTS - 1).astype(jnp.int32)
    off = row0 - pstart[block_e]
    block_nval = jnp.clip(counts[block_e] - off, 0, BM).astype(jnp.int32)
    block_base = jnp.clip(start[block_e] + off, 0, n_assign).astype(jnp.int32)
    order = jnp.concatenate([order, jnp.zeros((BM,), jnp.int32)])
    yb = moe_mlp(x, order, block_e, block_base, block_nval, mlp1_w, mlp1_b, mlp2_w, mlp2_b, BM)
    y = gate[:, 0:1] * yb[:T]
    for k in range(1, TOP_K):
        y = y + gate[:, k:k + 1] * yb[k * T:(k + 1) * T]
    return y


def layer_norm(x, g, b):
    mu = x.mean(-1, keepdims=True)
    var = jnp.square(x - mu).mean(-1, keepdims=True)
    return (x - mu) * lax.rsqrt(var + LN_EPS) * g + b


def rope(x, pos):
    half = ROT_DIM // 2
    inv = ROPE_THETA ** (-jnp.arange(half, dtype=jnp.float32) * 2.0 / ROT_DIM)
    ang = pos.astype(jnp.float32)[:, None] * inv
    cos, sin = jnp.cos(ang)[:, None, :], jnp.sin(ang)[:, None, :]
    x1, x2 = x[..., :half], x[..., half:ROT_DIM]
    return jnp.concatenate([x1 * cos - x2 * sin, x2 * cos + x1 * sin, x[..., ROT_DIM:]], axis=-1)


G0 = R_COLS + A_WIDTH + 6 * KV_WIDTH


def project(x2, w_in):
    w_main = jnp.concatenate([w_in[:, :G0], w_in[:, G0 + N_GATE:]], axis=1)
    w_gate = jnp.pad(w_in[:, G0:G0 + N_GATE], ((0, 0), (0, LANES - N_GATE)))
    return pallas_matmul(x2, w_main), pallas_matmul(x2, w_gate)[:, :N_GATE]


def split_projection(p, pg, B, T, pos):
    pr = p[:, :R_COLS].reshape(B, T, R_COLS)
    pa = p[:, R_COLS:G0].reshape(B, T, G0 - R_COLS)
    q = rope(pa[..., :A_WIDTH].reshape(B, T, N_HEADS, HEAD_DIM), pos)
    kvs = [pa[..., A_WIDTH + i * KV_WIDTH:A_WIDTH + (i + 1) * KV_WIDTH].reshape(B, T, N_KV, HEAD_DIM)
           for i in range(6)]
    kvs = [rope(z, pos) if i % 2 == 0 else z for i, z in enumerate(kvs)]
    gates = jax.nn.sigmoid(pg).reshape(B, T, N_KV, HPG, 3)
    return pr, q, kvs, gates


def rwkv_mixer(pr, shift_prev, wkv0, mu, w0, w_w2, a0, w_a2, g_w2, k_k, k_a, r_k, gn_g, gn_b):
    B, T, _ = pr.shape
    prev = jnp.concatenate([shift_prev[:, None, :], pr[:, :-1]], axis=1)
    xm = pr + (prev - pr) * mu
    o1, o2, o3 = R_WIDTH, 2 * R_WIDTH, 3 * R_WIDTH
    o4 = o3 + LORA_W
    o5 = o4 + LORA_A
    r, k, v = xm[..., :o1], xm[..., o1:o2], xm[..., o2:o3]
    xw, xa, xg = xm[..., o3:o4], xm[..., o4:o5], xm[..., o5:]
    w_log = -jax.nn.softplus(-(w0 + jnp.tanh(xw) @ w_w2)) - 0.5
    a = jax.nn.sigmoid(a0 + xa @ w_a2)
    g = jax.nn.sigmoid(xg) @ g_w2
    heads = lambda z: z.reshape(B, T, R_HEADS, R_HEAD)
    kk = heads(k * k_k)
    kk = kk / jnp.maximum(jnp.linalg.norm(kk, axis=-1, keepdims=True), 1e-12)
    k_h = heads(k * (1.0 + (a - 1.0) * k_a))
    r_h, v_h, a_h = heads(r), heads(v), heads(a)
    log_decay = -jnp.exp(heads(w_log))
    C = 64 if T % 64 == 0 else 8
    Tp = -(-T // C) * C
    hm = lambda z: jnp.pad(jnp.swapaxes(z, 1, 2), ((0, 0), (0, 0), (0, Tp - T), (0, 0)))
    y, wkv = wkv_chunked(hm(r_h), hm(log_decay), hm(k_h), hm(v_h), hm(-kk), hm(kk * a_h), wkv0, C)
    y = jnp.swapaxes(y[:, :, :T], 1, 2)
    mean = y.mean(-1, keepdims=True)
    var = jnp.square(y - mean).mean(-1, keepdims=True)
    yn = ((y - mean) * lax.rsqrt(var + GN_EPS)).reshape(B, T, R_WIDTH) * gn_g + gn_b
    bonus = (jnp.sum(r_h * k_h * r_k, axis=-1, keepdims=True) * v_h).reshape(B, T, R_WIDTH)
    return (yn + bonus) * g, wkv, pr[:, -1]


def compress(kv, pe, w1, b1, w2, b2):
    B, L = kv.shape[:2]
    n_chunk = L // CMP_STRIDE
    ch = kv[:, :n_chunk * CMP_STRIDE].reshape(B, n_chunk, CMP_STRIDE, N_KV, HEAD_DIM)
    blk = jnp.concatenate([ch[:, :-1], ch[:, 1:]], axis=2) + pe[:, None, :]
    blk = jnp.transpose(blk, (0, 3, 1, 2, 4)).reshape(B, N_KV, n_chunk - 1, CMP_LEN * HEAD_DIM)
    return jax.nn.gelu(blk @ w1 + b1) @ w2 + b2


def nsa_prompt(q, kvs, gates, pe, w1, b1, w2, b2):
    kc_raw, vc_raw, ks, vs, kw, vw = kvs
    B, T = q.shape[:2]
    kc = compress(kc_raw, pe[0], w1[0], b1[0], w2[0], b2[0])
    vc = compress(vc_raw, pe[1], w1[1], b1[1], w2[1], b2[1])
    qg = _bf(jnp.transpose(q.reshape(B, T, N_KV, HPG, HEAD_DIM), (0, 2, 3, 1, 4)) * (HEAD_DIM ** -0.5))
    tk = lambda z: _bf(jnp.swapaxes(z, 1, 2))
    o = nsa_prompt_attention(qg, kc, vc, tk(ks), tk(vs), tk(kw), tk(vw),
                             jnp.transpose(gates, (0, 2, 3, 1, 4)))
    return jnp.transpose(o, (0, 3, 1, 2, 4)).reshape(B, T, A_WIDTH)


def nsa_sample(q, kvs, gates, cmp_cache, sel_cache, win_cache, page_table, pe, w1, b1, w2, b2):
    kc_new, vc_new, ks_new, vs_new, kw_new, vw_new = kvs
    DB, DS = q.shape[:2]
    past = page_table.shape[1] * PAGE_SIZE
    kvc = compress_sample(cmp_cache, page_table, pe, w1, b1, w2, b2)
    qg = _bf(jnp.transpose(q.reshape(DB, DS, N_KV, HPG, HEAD_DIM), (0, 2, 3, 1, 4)) * (HEAD_DIM ** -0.5))
    qg = qg.reshape(DB, N_KV, HPG * DS, HEAD_DIM)
    gt = jnp.transpose(gates, (0, 2, 3, 1, 4)).reshape(DB, N_KV, HPG * DS, 3)
    pack = lambda k, v: _bf(jnp.pad(jnp.transpose(jnp.stack([k, v], axis=1), (0, 1, 3, 2, 4)),
                                    ((0, 0), (0, 0), (0, 0), (0, NEW_PAD - DS), (0, 0))))
    o = nsa_sample_attention(qg, kvc, sel_cache, win_cache, page_table, pack(ks_new, vs_new),
                             pack(kw_new, vw_new), gt, past)
    o = jnp.transpose(o.reshape(DB, N_KV, HPG, DS, HEAD_DIM), (0, 3, 1, 2, 4)).reshape(DB, DS, -1)
    new_k = jnp.concatenate([win_cache[:, DS:, 0], kw_new], axis=1)
    new_v = jnp.concatenate([win_cache[:, DS:, 1], vw_new], axis=1)
    return o, jnp.stack([new_k, new_v], axis=2)


def merge_and_ffn(x, y_r, y_a, mg, w_pa, w_pb, w_o, ln1_g, ln1_b, router_w, router_b,
                  mlp1_w, mlp1_b, mlp2_w, mlp2_b, ln2_g, ln2_b):
    m = (jax.nn.sigmoid(mg[:, :D_MODEL]) * pallas_matmul(y_r, w_pa)
         + jax.nn.sigmoid(mg[:, D_MODEL:]) * pallas_matmul(y_a, w_pb))
    h = layer_norm(DN_ALPHA * x + pallas_matmul(m, w_o), ln1_g, ln1_b)
    f = moe(h, router_w, router_b, mlp1_w, mlp1_b, mlp2_w, mlp2_b)
    return layer_norm(DN_ALPHA * h + f, ln2_g, ln2_b)


def kernel(x_prompt, x_sample, cache_cmp_kv, cache_sel_kv, cache_win_kv, state_wkv, state_shift,
           page_table, w_in, mu_shift, w0, w_w2, a0, w_a2, g_w2, k_k, k_a, r_k, gn_g, gn_b,
           cmp_pe, cmp_w1, cmp_b1, cmp_w2, cmp_b2, w_pa, w_pb, w_o, ln1_g, ln1_b,
           router_w, router_b, mlp1_w, mlp1_b, mlp2_w, mlp2_b, ln2_g, ln2_b):
    B, T, D = x_prompt.shape
    DB, DS, _ = x_sample.shape
    n_p = B * T
    past = page_table.shape[1] * PAGE_SIZE
    pos_p = jnp.arange(T)
    pos_s = past + jnp.arange(DS)
    wb_p = min(WINDOW, T)
    h_all = jnp.concatenate([x_prompt.reshape(n_p, D), x_sample.reshape(DB * DS, D)])
    cmp_p, sel_p, win_p, wkv_p, shift_p = [], [], [], [], []
    cmp_s, sel_s, win_s, wkv_s, shift_s = [], [], [], [], []
    for l in range(DEPTH):
        rwkv_w = (mu_shift[l], w0[l], w_w2[l], a0[l], w_a2[l], g_w2[l], k_k[l], k_a[l], r_k[l],
                  gn_g[l], gn_b[l])
        cmp_w = (cmp_pe[l], cmp_w1[l], cmp_b1[l], cmp_w2[l], cmp_b2[l])
        out_w = (w_pa[l], w_pb[l], w_o[l], ln1_g[l], ln1_b[l], router_w[l], router_b[l],
                 mlp1_w[l], mlp1_b[l], mlp2_w[l], mlp2_b[l], ln2_g[l], ln2_b[l])
        p_all, pg_all = project(h_all, w_in[l])
        pr, q, kvs, gates = split_projection(p_all[:n_p], pg_all[:n_p], B, T, pos_p)
        y_r, wkv, shift = rwkv_mixer(pr, jnp.zeros((B, R_COLS), pr.dtype),
                                     jnp.zeros((B, R_HEADS, R_HEAD, R_HEAD), jnp.float32), *rwkv_w)
        y_a = nsa_prompt(q, kvs, gates, *cmp_w)
        cmp_p.append(jnp.stack([kvs[0], kvs[1]], axis=2))
        sel_p.append(jnp.stack([kvs[2], kvs[3]], axis=2))
        win_p.append(jnp.stack([kvs[4][:, T - wb_p:], kvs[5][:, T - wb_p:]], axis=2))
        wkv_p.append(wkv)
        shift_p.append(shift)
        pr, q, kvs, gates = split_projection(p_all[n_p:], pg_all[n_p:], DB, DS, pos_s)
        y_r_s, wkv, shift = rwkv_mixer(pr, state_shift[l], state_wkv[l], *rwkv_w)
        y_a_s, new_win = nsa_sample(q, kvs, gates, cache_cmp_kv[l], cache_sel_kv[l], cache_win_kv[l],
                                    page_table, *cmp_w)
        cmp_s.append(jnp.stack([kvs[0], kvs[1]], axis=2))
        sel_s.append(jnp.stack([kvs[2], kvs[3]], axis=2))
        win_s.append(new_win)
        wkv_s.append(wkv)
        shift_s.append(shift)
        rows = lambda a, b: jnp.concatenate([a.reshape(n_p, -1), b.reshape(DB * DS, -1)])
        h_all = merge_and_ffn(h_all, rows(y_r, y_r_s), rows(y_a, y_a_s), p_all[:, G0:], *out_w)
    hp = h_all[:n_p].reshape(B, T, D)
    hs = h_all[n_p:].reshape(DB, DS, D)
    return (hp, hs, jnp.stack(cmp_p), jnp.stack(sel_p), jnp.stack(win_p), jnp.stack(wkv_p),
            jnp.stack(shift_p), jnp.stack(cmp_s), jnp.stack(sel_s), jnp.stack(win_s),
            jnp.stack(wkv_s), jnp.stack(shift_s))
```

```python
import functools

import jax
import jax.numpy as jnp
from jax import lax
from jax.experimental import pallas as pl
from jax.experimental.pallas import tpu as pltpu

D_MODEL = 1024
DEPTH = 1
PAGE_SIZE = 128

R_HEADS = 8
R_HEAD = 64
R_WIDTH = R_HEADS * R_HEAD
LORA_W = 64
LORA_A = 64
LORA_G = 128
R_COLS = 3 * R_WIDTH + LORA_W + LORA_A + LORA_G
GN_EPS = 64e-5

N_HEADS = 8
N_KV = 2
HPG = N_HEADS // N_KV
HEAD_DIM = 64
A_WIDTH = N_HEADS * HEAD_DIM
KV_WIDTH = N_KV * HEAD_DIM
N_GATE = 3 * N_HEADS
A_COLS = A_WIDTH + 6 * KV_WIDTH + N_GATE
ROT_DIM = HEAD_DIM // 4
ROPE_THETA = 500000.0
CMP_STRIDE = 16
CMP_LEN = 2 * CMP_STRIDE
CMP_HIDDEN = 256
SEL_BLOCK = 64
SEL_TOP = 16
WINDOW = 512

N_EXPERTS = 32
TOP_K = 4
K_SHIFT = 2
D_FF = 1024
SWIGLU_LIMIT = 7.0
SWIGLU_ALPHA = 1.702

DN_ALPHA = (2 * DEPTH) ** 0.25
LN_EPS = 1e-5
NEG = -1e30

LANES = 128
SUBLANES = 8
PAGES_PER_STEP = 8
NEW_PAD = 8
VMEM_LIMIT = 56 * 1024 * 1024


def _bf(x):
    return x.astype(jnp.bfloat16)


def _dot(a, b):
    return jnp.dot(_bf(a), _bf(b), preferred_element_type=jnp.float32)


def _dot_nt(a, b):
    return lax.dot_general(_bf(a), _bf(b), (((1,), (1,)), ((), ())),
                           preferred_element_type=jnp.float32)


def _mm_kernel(x_ref, w_ref, o_ref):
    o_ref[...] = _dot(x_ref[...], w_ref[...])


def pallas_matmul(x, w):
    x, w = _bf(x), _bf(w)
    M, K = x.shape
    N = w.shape[1]
    tm = next(t for t in (512, 384, 256, 128, M) if M % t == 0)
    resident = 2 * (2 * K * N + 4 * tm * N + 2 * tm * K) <= VMEM_LIMIT - (8 << 20)
    tn = N if resident else next(t for t in (512, 256, LANES) if N % t == 0)
    return pl.pallas_call(
        _mm_kernel,
        grid=(N // tn, M // tm),
        in_specs=[pl.BlockSpec((tm, K), lambda j, i: (i, 0)),
                  pl.BlockSpec((K, tn), lambda j, i: (0, j))],
        out_specs=pl.BlockSpec((tm, tn), lambda j, i: (i, j)),
        out_shape=jax.ShapeDtypeStruct((M, N), jnp.float32),
        compiler_params=pltpu.CompilerParams(vmem_limit_bytes=VMEM_LIMIT),
        name="mm",
    )(x, w)


def _bmm(a, b):
    return lax.dot_general(_bf(a), _bf(b), (((2,), (1,)), ((0,), (0,))), preferred_element_type=jnp.float32)


def _bmm_nt(a, b):
    return lax.dot_general(_bf(a), _bf(b), (((2,), (2,)), ((0,), (0,))), preferred_element_type=jnp.float32)


def _bmm_tn(a, b):
    return lax.dot_general(_bf(a), _bf(b), (((1,), (1,)), ((0,), (0,))), preferred_element_type=jnp.float32)


def _wkv_chunk_kernel(r_ref, lw_ref, k_ref, v_ref, a_ref, b_ref, s0_ref, y_ref, s_out_ref, s_scr,
                      *, C, H):
    c = pl.program_id(1)

    @pl.when(c == 0)
    def _():
        s_scr[...] = s0_ref[0]

    row = lax.broadcasted_iota(jnp.int32, (H, C, C), 1)
    col = lax.broadcasted_iota(jnp.int32, (H, C, C), 2)
    incl = row >= col
    strict = row > col
    ltri = jnp.where(incl, 1.0, 0.0).astype(jnp.bfloat16)
    eye = jnp.where(row == col, 1.0, 0.0).astype(jnp.float32)

    lw = lw_ref[0]
    r = r_ref[0]
    k = k_ref[0]
    v = v_ref[0]
    a = a_ref[0]
    b = b_ref[0]
    hi = _bf(lw)
    rem = lw - hi.astype(jnp.float32)
    mid = _bf(rem)
    lo = _bf(rem - mid.astype(jnp.float32))
    lp = _bmm(ltri, hi) + _bmm(ltri, mid) + _bmm(ltri, lo)
    lp_end = lp[:, C - 1:C, :]
    p_end = jnp.exp(lp_end)
    p_inv = jnp.exp(-lp)
    at = a * jnp.exp(lp - lw)
    rt = r * jnp.exp(lp)
    bt = b * p_inv
    kt = k * p_inv
    p_hat = jnp.exp(lp_end - lp)
    bh = b * p_hat
    kh = k * p_hat

    n_ab = jnp.where(strict, _bmm_nt(at, bt), 0.0)
    a_ak = jnp.where(strict, _bmm_nt(at, kt), 0.0)
    a_rb = jnp.where(incl, _bmm_nt(rt, bt), 0.0)
    a_rk = jnp.where(incl, _bmm_nt(rt, kt), 0.0)

    t_inv = eye + n_ab
    n_pow = n_ab
    span = 2
    while span < C:
        n_pow = _bmm(n_pow, n_pow)
        t_inv = _bmm(t_inv, eye + n_pow)
        span *= 2

    s = s_scr[...]
    rhs = _bmm_nt(at, s) + _bmm(a_ak, v)
    u = _bmm(t_inv, rhs)
    y_ref[0] = _bmm_nt(rt, s) + _bmm(a_rb, u) + _bmm(a_rk, v)
    s_new = s * p_end + _bmm_tn(u, bh) + _bmm_tn(v, kh)
    s_scr[...] = s_new

    @pl.when(c == pl.num_programs(1) - 1)
    def _():
        s_out_ref[0] = s_new


def wkv_chunked(r, lw, k, v, a, b, s0, C):
    B, H, T, D = r.shape
    assert T % C == 0
    seq = pl.BlockSpec((1, H, C, D), lambda bi, ci: (bi, 0, ci, 0))
    st = pl.BlockSpec((1, H, D, D), lambda bi, ci: (bi, 0, 0, 0))
    return pl.pallas_call(
        functools.partial(_wkv_chunk_kernel, C=C, H=H),
        grid=(B, T // C),
        in_specs=[seq] * 6 + [st],
        out_specs=[seq, st],
        out_shape=[jax.ShapeDtypeStruct((B, H, T, D), jnp.float32),
                   jax.ShapeDtypeStruct((B, H, D, D), jnp.float32)],
        scratch_shapes=[pltpu.VMEM((H, D, D), jnp.float32)],
        compiler_params=pltpu.CompilerParams(dimension_semantics=("arbitrary", "arbitrary")),
        name="wkv_chunk",
    )(r, lw, k, v, a, b, s0)


def _nsa_prompt_kernel(q_ref, kc_ref, vc_ref, cover_ref, ks_ref, vs_ref, kw_ref, vw_ref, g_ref, o_ref,
                       *, TQ, TK, NC, NCP):
    f32 = jnp.float32
    bf16 = jnp.bfloat16
    qb = pl.program_id(2)
    R = HPG * TQ
    q = q_ref[0, 0].reshape(R, HEAD_DIM)
    t_pos = qb * TQ + lax.broadcasted_iota(jnp.int32, (TQ, 1), 0)

    n_idx = lax.broadcasted_iota(jnp.int32, (1, NCP), 1)
    c_ok = ((n_idx * CMP_STRIDE + (CMP_LEN - 1)) <= t_pos) & (n_idx < NC)
    s_c = _dot_nt(q, kc_ref[0, 0]).reshape(HPG, TQ, NCP)
    s_c = jnp.where(c_ok[None], s_c, NEG)
    m_c = jnp.max(s_c, axis=-1, keepdims=True)
    p_c = jnp.where(c_ok[None], jnp.exp(s_c - m_c), 0.0)
    l_c = jnp.sum(p_c, axis=-1, keepdims=True)
    p_c = p_c / jnp.where(l_c > 0.0, l_c, 1.0)
    p_cb = p_c.astype(bf16)
    o_c = jnp.dot(p_cb.reshape(R, NCP), vc_ref[0, 0], preferred_element_type=f32)

    cover_t = cover_ref[...]
    imp = _dot_nt(cover_t, p_cb[0])
    for h in range(1, HPG):
        imp = imp + _dot_nt(cover_t, p_cb[h])
    s_col = lax.broadcasted_iota(jnp.int32, (LANES, 1), 0)
    t_row = qb * TQ + lax.broadcasted_iota(jnp.int32, (1, TQ), 1)
    cur = t_row // SEL_BLOCK
    forced = (s_col == 0) | (s_col == cur) | (s_col == cur - 1)
    causal = (s_col * SEL_BLOCK) <= t_row
    score = jnp.where(forced, 1e6, imp)
    score = jnp.where(causal, score, NEG)
    s_col_f = s_col.astype(f32)
    sel_t = jnp.zeros((LANES, TQ), f32)
    for _ in range(SEL_TOP):
        top = jnp.max(score, axis=0, keepdims=True)
        first = jnp.min(jnp.where(score == top, s_col_f, float(LANES)), axis=0, keepdims=True)
        hit = s_col_f == first
        sel_t = jnp.where(hit & (top > 0.5 * NEG), 1.0, sel_t)
        score = jnp.where(hit, -3e38, score)

    sel_neg = ((sel_t.T - 1.0) * (-NEG)).astype(bf16)
    q_aug = jnp.concatenate([jnp.concatenate([sel_neg] * HPG, axis=0), q], axis=1)
    key_off = lax.broadcasted_iota(jnp.int32, (1, TK), 1)

    def sel_tile(j, carry, diagonal):
        m, l, acc = carry
        start = pl.multiple_of(j * TK, TK)
        k = ks_ref[0, 0, pl.ds(start, TK), :]
        v = vs_ref[0, 0, pl.ds(start, TK), :]
        s = _dot_nt(q_aug, k).reshape(HPG, TQ, TK)
        if diagonal:
            s = jnp.where(((key_off + j * TK) <= t_pos)[None], s, NEG)
        m_new = jnp.maximum(m, jnp.max(s, axis=-1, keepdims=True))
        p = jnp.exp(s - m_new)
        alpha = jnp.exp(m - m_new)
        l = alpha * l + jnp.sum(p, axis=-1, keepdims=True)
        pv = jnp.dot(p.astype(bf16).reshape(R, TK), v, preferred_element_type=f32)
        acc = alpha * acc + pv.reshape(HPG, TQ, HEAD_DIM)
        return m_new, l, acc

    m0 = jnp.full((HPG, TQ, 1), NEG, f32)
    l0 = jnp.zeros((HPG, TQ, 1), f32)
    a0 = jnp.zeros((HPG, TQ, HEAD_DIM), f32)
    n_full = (qb * TQ) // TK
    carry = lax.fori_loop(0, n_full, functools.partial(sel_tile, diagonal=False), (m0, l0, a0))
    _, l_s, acc_s = sel_tile(n_full, carry, True)
    o_s = acc_s / l_s

    n_w = WINDOW // TQ + 1
    lane_q = lax.broadcasted_iota(jnp.int32, (1, TQ), 1)
    s_w, ok_w, v_w = [], [], []
    for i in range(n_w):
        kb = qb - (n_w - 1) + i
        start = pl.multiple_of(jnp.maximum(kb, 0) * TQ, TQ)
        k = kw_ref[0, 0, pl.ds(start, TQ), :]
        v_w.append(vw_ref[0, 0, pl.ds(start, TQ), :])
        kpos = kb * TQ + lane_q
        ok = (kpos <= t_pos) & (kpos >= t_pos - WINDOW) & (kpos >= 0)
        ok_w.append(ok)
        s_w.append(jnp.where(ok[None], _dot_nt(q, k).reshape(HPG, TQ, TQ), NEG))
    m_w = s_w[0].max(axis=-1, keepdims=True)
    for i in range(1, n_w):
        m_w = jnp.maximum(m_w, s_w[i].max(axis=-1, keepdims=True))
    l_w = jnp.zeros((HPG, TQ, 1), f32)
    acc_w = jnp.zeros((R, HEAD_DIM), f32)
    for i in range(n_w):
        p = jnp.where(ok_w[i][None], jnp.exp(s_w[i] - m_w), 0.0)
        l_w = l_w + jnp.sum(p, axis=-1, keepdims=True)
        acc_w = acc_w + jnp.dot(p.astype(bf16).reshape(R, TQ), v_w[i], preferred_element_type=f32)
    o_w = acc_w.reshape(HPG, TQ, HEAD_DIM) / l_w

    g = g_ref[0, 0]
    o_ref[0, 0] = (g[:, :, 0:1] * o_c.reshape(HPG, TQ, HEAD_DIM) + g[:, :, 1:2] * o_s
                   + g[:, :, 2:3] * o_w)


def nsa_prompt_attention(q, kc, vc, ks, vs, kw, vw, gates, TQ=256, TK=1024):
    B, G, _, T, D = q.shape
    NC = kc.shape[2]
    NCP = -(-NC // LANES) * LANES
    NS = T // SEL_BLOCK
    assert NS <= LANES and T % TK == 0 and TK % TQ == 0 and WINDOW % TQ == 0
    kc = jnp.pad(kc, ((0, 0), (0, 0), (0, NCP - NC), (0, 0))).astype(jnp.bfloat16)
    vc = jnp.pad(vc, ((0, 0), (0, 0), (0, NCP - NC), (0, 0))).astype(jnp.bfloat16)
    c_start = jnp.arange(NCP) * CMP_STRIDE
    s_start = jnp.arange(LANES) * SEL_BLOCK
    cover = ((c_start[:, None] < s_start[None, :] + SEL_BLOCK)
             & (c_start[:, None] + CMP_LEN > s_start[None, :])
             & (jnp.arange(NCP)[:, None] < NC) & (jnp.arange(LANES)[None, :] < NS)).astype(jnp.bfloat16)
    onehot = (jnp.arange(T)[:, None] // SEL_BLOCK == jnp.arange(LANES)[None, :]).astype(jnp.bfloat16)
    ks = jnp.concatenate([jnp.broadcast_to(onehot, (B, G, T, LANES)), ks], axis=-1)
    full = lambda n, d=D: pl.BlockSpec((1, 1, n, d), lambda b, g, i: (b, g, 0, 0))
    qspec = pl.BlockSpec((1, 1, HPG, TQ, D), lambda b, g, i: (b, g, 0, i, 0))
    return pl.pallas_call(
        functools.partial(_nsa_prompt_kernel, TQ=TQ, TK=TK, NC=NC, NCP=NCP),
        grid=(B, G, T // TQ),
        in_specs=[qspec, full(NCP), full(NCP),
                  pl.BlockSpec((LANES, NCP), lambda b, g, i: (0, 0)),
                  full(T, LANES + D), full(T), full(T), full(T),
                  pl.BlockSpec((1, 1, HPG, TQ, 3), lambda b, g, i: (b, g, 0, i, 0))],
        out_specs=qspec,
        out_shape=jax.ShapeDtypeStruct((B, G, HPG, T, D), jnp.float32),
        compiler_params=pltpu.CompilerParams(
            dimension_semantics=("arbitrary", "arbitrary", "arbitrary"),
            vmem_limit_bytes=VMEM_LIMIT),
        name="nsa_prompt",
    )(q, kc, vc, cover.T, ks, vs, kw, vw, gates)


def _cmp_sample_kernel(pt_ref, *refs, n_chunk):
    pages = refs[:PAGES_PER_STEP]
    wcat_ref, c1_ref, w2_ref, b2_ref, o_ref, seq, tile = refs[PAGES_PER_STEP:]
    j = pl.program_id(1)
    rows = PAGE_SIZE // CMP_STRIDE
    for i in range(PAGES_PER_STEP):
        dst = pl.multiple_of((j * PAGES_PER_STEP + i) * rows, rows)
        for kv in range(2):
            for g in range(N_KV):
                q = kv * N_KV + g
                tile[...] = pages[i][0, kv, g].T
                for p in range(CMP_STRIDE):
                    seq[p * 2 * N_KV + q, pl.ds(dst, rows), :] = tile[pl.ds(p, rows, stride=CMP_STRIDE), :]

    @pl.when(j == pl.num_programs(1) - 1)
    def _():
        for kv in range(2):
            for g in range(N_KV):
                q = kv * N_KV + g
                acc = jnp.zeros((n_chunk, 2 * CMP_HIDDEN), jnp.float32)
                for p in range(CMP_STRIDE):
                    acc = acc + jnp.dot(_bf(seq[p * 2 * N_KV + q]), wcat_ref[kv, p],
                                        preferred_element_type=jnp.float32)
                first = acc[:, :CMP_HIDDEN]
                second = acc[:, CMP_HIDDEN:]
                second = jnp.concatenate([second[1:], second[:1]], axis=0)
                hid = jax.nn.gelu(first + second + c1_ref[kv])
                out = jnp.dot(_bf(hid), w2_ref[kv], preferred_element_type=jnp.float32) + b2_ref[kv]
                o_ref[0, kv, g] = _bf(out)


def compress_sample(cache, page_table, pe, w1, b1, w2, b2):
    n_pool = cache.shape[0]
    DB, n_pages = page_table.shape
    assert n_pages % PAGES_PER_STEP == 0
    rows = PAGE_SIZE // CMP_STRIDE
    n_chunk = n_pages * rows
    view = jnp.transpose(cache, (0, 2, 3, 4, 1))
    w1r = w1.reshape(2, CMP_LEN, HEAD_DIM, CMP_HIDDEN)
    wcat = _bf(jnp.concatenate([w1r[:, :CMP_STRIDE], w1r[:, CMP_STRIDE:]], axis=-1))
    c1 = (jnp.einsum('kn,knh->kh', pe.reshape(2, CMP_LEN * HEAD_DIM), w1) + b1).reshape(2, 1, CMP_HIDDEN)
    page_spec = lambda i: pl.BlockSpec((1, 2, N_KV, HEAD_DIM, PAGE_SIZE),
                                       lambda b, j, pt: (pt[b, j * PAGES_PER_STEP + i], 0, 0, 0, 0))
    const = lambda shape: pl.BlockSpec(shape, lambda b, j, pt: (0,) * len(shape))
    grid_spec = pltpu.PrefetchScalarGridSpec(
        num_scalar_prefetch=1,
        grid=(DB, n_pages // PAGES_PER_STEP),
        in_specs=[page_spec(i) for i in range(PAGES_PER_STEP)]
        + [const((2, CMP_STRIDE, HEAD_DIM, 2 * CMP_HIDDEN)), const((2, 1, CMP_HIDDEN)),
           const((2, CMP_HIDDEN, HEAD_DIM)), const((2, 1, HEAD_DIM))],
        out_specs=pl.BlockSpec((1, 2, N_KV, n_chunk, HEAD_DIM), lambda b, j, pt: (b, 0, 0, 0, 0)),
        scratch_shapes=[pltpu.VMEM((CMP_STRIDE * 2 * N_KV, n_chunk, HEAD_DIM), jnp.float32),
                        pltpu.VMEM((PAGE_SIZE, HEAD_DIM), jnp.float32)],
    )
    return pl.pallas_call(
        functools.partial(_cmp_sample_kernel, n_chunk=n_chunk),
        grid_spec=grid_spec,
        out_shape=jax.ShapeDtypeStruct((DB, 2, N_KV, n_chunk, HEAD_DIM), jnp.bfloat16),
        compiler_params=pltpu.CompilerParams(dimension_semantics=("arbitrary", "arbitrary"),
                                             vmem_limit_bytes=VMEM_LIMIT),
        name="cmp_sample",
    )(page_table, *([view] * PAGES_PER_STEP), wcat, c1, _bf(w2), b2.reshape(2, 1, HEAD_DIM))


def _nsa_sample_kernel(pt_ref, *refs, DS, NC, past):
    pages = refs[:PAGES_PER_STEP]
    (q_ref, kvc_ref, cover_ref, new_sel_ref, win_ref, new_win_ref, g_ref, o_ref,
     qaug, m_s, l_s, acc_s, oc_s) = refs[PAGES_PER_STEP:]
    f32 = jnp.float32
    j = pl.program_id(1)
    R = HPG * DS
    t_row = lax.broadcasted_iota(jnp.int32, (R, 1), 0) % DS
    n_chunk = kvc_ref.shape[3]

    @pl.when(j == 0)
    def _():
        n_idx = lax.broadcasted_iota(jnp.int32, (1, n_chunk), 1)
        s_col = lax.broadcasted_iota(jnp.int32, (LANES, 1), 0)
        s_col_f = s_col.astype(f32)
        last_blk = past // SEL_BLOCK - 1
        for g in range(N_KV):
            q = q_ref[0, g]
            s_c = jnp.where(n_idx < NC, _dot_nt(q, kvc_ref[0, 0, g]), NEG)
            p_c = jnp.exp(s_c - jnp.max(s_c, axis=-1, keepdims=True))
            p_c = p_c / jnp.sum(p_c, axis=-1, keepdims=True)
            p_cb = _bf(p_c)
            oc_s[g] = jnp.dot(p_cb, kvc_ref[0, 1, g], preferred_element_type=f32)
            imp_rows = _dot_nt(cover_ref[...], p_cb)
            imp = imp_rows[:, 0:DS]
            for h in range(1, HPG):
                imp = imp + imp_rows[:, h * DS:(h + 1) * DS]
            forced = (s_col == 0) | (s_col == last_blk)
            score = jnp.where(forced, 1e6, imp)
            score = jnp.where(s_col <= last_blk, score, NEG)
            sel_t = jnp.zeros((LANES, DS), f32)
            for _ in range(SEL_TOP - 1):
                top = jnp.max(score, axis=0, keepdims=True)
                first = jnp.min(jnp.where(score == top, s_col_f, float(LANES)), axis=0, keepdims=True)
                hit = s_col_f == first
                sel_t = jnp.where(hit & (top > 0.5 * NEG), 1.0, sel_t)
                score = jnp.where(hit, -3e38, score)
            sel_neg = _bf((sel_t.T - 1.0) * (-NEG))
            qaug[g] = jnp.concatenate([jnp.concatenate([sel_neg] * HPG, axis=0), q], axis=1)
        m_s[...] = jnp.full(m_s.shape, NEG, f32)
        l_s[...] = jnp.zeros(l_s.shape, f32)
        acc_s[...] = jnp.zeros(acc_s.shape, f32)

    TK = PAGES_PER_STEP * PAGE_SIZE
    blk = (lax.broadcasted_iota(jnp.int32, (LANES, TK), 1) // SEL_BLOCK
           + j * (TK // SEL_BLOCK))
    onehot = _bf(jnp.where(blk == lax.broadcasted_iota(jnp.int32, (LANES, TK), 0), 1.0, 0.0))
    for g in range(N_KV):
        k_t = _bf(jnp.concatenate([pages[i][0, 0, g] for i in range(PAGES_PER_STEP)], axis=1))
        v_t = _bf(jnp.concatenate([pages[i][0, 1, g] for i in range(PAGES_PER_STEP)], axis=1))
        s = jnp.dot(qaug[g], jnp.concatenate([onehot, k_t], axis=0), preferred_element_type=f32)
        m_new = jnp.maximum(m_s[g], jnp.max(s, axis=-1, keepdims=True))
        p = jnp.exp(s - m_new)
        alpha = jnp.exp(m_s[g] - m_new)
        l_s[g] = alpha * l_s[g] + jnp.sum(p, axis=-1, keepdims=True)
        acc_s[g] = alpha * acc_s[g] + _dot_nt(p, v_t)
        m_s[g] = m_new

    @pl.when(j == pl.num_programs(1) - 1)
    def _():
        j_new = lax.broadcasted_iota(jnp.int32, (1, NEW_PAD), 1)
        ok_new = (j_new <= t_row) & (j_new < DS)
        w_idx = lax.broadcasted_iota(jnp.int32, (1, WINDOW), 1)
        ok_win = w_idx >= t_row
        for g in range(N_KV):
            q = q_ref[0, g]
            s = jnp.where(ok_new, _dot_nt(q, new_sel_ref[0, 0, g]), NEG)
            m_new = jnp.maximum(m_s[g], jnp.max(s, axis=-1, keepdims=True))
            p = jnp.where(ok_new, jnp.exp(s - m_new), 0.0)
            alpha = jnp.exp(m_s[g] - m_new)
            l_fin = alpha * l_s[g] + jnp.sum(p, axis=-1, keepdims=True)
            o_sel = (alpha * acc_s[g]
                     + jnp.dot(_bf(p), new_sel_ref[0, 1, g], preferred_element_type=f32)) / l_fin
            s_a = jnp.where(ok_win, jnp.dot(q, _bf(win_ref[0, 0, g]), preferred_element_type=f32), NEG)
            s_b = jnp.where(ok_new, _dot_nt(q, new_win_ref[0, 0, g]), NEG)
            m_w = jnp.maximum(jnp.max(s_a, axis=-1, keepdims=True), jnp.max(s_b, axis=-1, keepdims=True))
            p_a = jnp.where(ok_win, jnp.exp(s_a - m_w), 0.0)
            p_b = jnp.where(ok_new, jnp.exp(s_b - m_w), 0.0)
            l_w = jnp.sum(p_a, axis=-1, keepdims=True) + jnp.sum(p_b, axis=-1, keepdims=True)
            o_win = (_dot_nt(p_a, win_ref[0, 1, g])
                     + jnp.dot(_bf(p_b), new_win_ref[0, 1, g], preferred_element_type=f32)) / l_w
            gt = g_ref[0, g]
            o_ref[0, g] = gt[:, 0:1] * oc_s[g] + gt[:, 1:2] * o_sel + gt[:, 2:3] * o_win


def nsa_sample_attention(q, kvc, sel_cache, win_cache, page_table, new_sel, new_win, gates, past):
    DB, G, R, D = q.shape
    DS = R // HPG
    n_pool = sel_cache.shape[0]
    n_pages = page_table.shape[1]
    n_chunk = kvc.shape[3]
    NC = (past + DS) // CMP_STRIDE - 1
    assert past % SEL_BLOCK == 0 and DS < CMP_STRIDE and DS <= NEW_PAD and past // SEL_BLOCK <= LANES
    assert n_pages % PAGES_PER_STEP == 0 and win_cache.shape[1] == WINDOW and NC < n_chunk + 1
    c_start = jnp.arange(n_chunk) * CMP_STRIDE
    s_start = jnp.arange(LANES) * SEL_BLOCK
    cover_t = _bf((c_start[None, :] < s_start[:, None] + SEL_BLOCK)
                  & (c_start[None, :] + CMP_LEN > s_start[:, None])
                  & (jnp.arange(n_chunk)[None, :] < NC) & (jnp.arange(LANES)[:, None] < past // SEL_BLOCK))
    sel_view = jnp.transpose(sel_cache, (0, 2, 3, 4, 1))
    win_view = jnp.transpose(win_cache, (0, 2, 3, 4, 1))
    page_spec = lambda i: pl.BlockSpec((1, 2, G, D, PAGE_SIZE),
                                       lambda b, j, pt: (pt[b, j * PAGES_PER_STEP + i], 0, 0, 0, 0))
    per_b = lambda shape: pl.BlockSpec((1,) + shape, lambda b, j, pt: (b,) + (0,) * len(shape))
    grid_spec = pltpu.PrefetchScalarGridSpec(
        num_scalar_prefetch=1,
        grid=(DB, n_pages // PAGES_PER_STEP),
        in_specs=[page_spec(i) for i in range(PAGES_PER_STEP)]
        + [per_b((G, R, D)), per_b((2, G, n_chunk, D)),
           pl.BlockSpec((LANES, n_chunk), lambda b, j, pt: (0, 0)),
           per_b((2, G, NEW_PAD, D)), per_b((2, G, D, WINDOW)), per_b((2, G, NEW_PAD, D)), per_b((G, R, 3))],
        out_specs=per_b((G, R, D)),
        scratch_shapes=[pltpu.VMEM((G, R, LANES + D), jnp.bfloat16), pltpu.VMEM((G, R, 1), jnp.float32),
                        pltpu.VMEM((G, R, 1), jnp.float32), pltpu.VMEM((G, R, D), jnp.float32),
                        pltpu.VMEM((G, R, D), jnp.float32)],
    )
    return pl.pallas_call(
        functools.partial(_nsa_sample_kernel, DS=DS, NC=NC, past=past),
        grid_spec=grid_spec,
        out_shape=jax.ShapeDtypeStruct((DB, G, R, D), jnp.float32),
        compiler_params=pltpu.CompilerParams(dimension_semantics=("arbitrary", "arbitrary"),
                                             vmem_limit_bytes=VMEM_LIMIT),
        name="nsa_sample",
    )(page_table, *([sel_view] * PAGES_PER_STEP), q, kvc, cover_t, new_sel, win_view, new_win, gates)


def _moe_mlp_kernel(be_ref, base_ref, nval_ref, order_ref, x_hbm, w1_ref, b1_ref, w2_ref, b2_ref, o_hbm,
                    w1s, w2s, xbuf, ybuf, in_sem, out_sem, *, BM, T):
    i = pl.program_id(0)
    n = pl.num_programs(0)
    s = i % 2

    def gather(block, sl):
        base = base_ref[block]

        def body(r, c):
            tok = order_ref[base + r] >> K_SHIFT
            pltpu.make_async_copy(x_hbm.at[tok], xbuf.at[sl, r], in_sem.at[sl]).start()
            return c
        lax.fori_loop(0, BM, body, 0, unroll=8)

    @pl.when(i == 0)
    def _():
        gather(0, 0)

    @pl.when(i + 1 < n)
    def _():
        gather(i + 1, 1 - s)

    e = be_ref[i]
    prev = be_ref[jnp.maximum(i - 1, 0)]

    @pl.when((i == 0) | (e != prev))
    def _():
        w1s[...] = w1_ref[0].astype(jnp.bfloat16)
        w2s[...] = w2_ref[0].astype(jnp.bfloat16)

    def rows_in(sl):
        return pltpu.make_async_copy(x_hbm.at[pl.ds(0, BM)], xbuf.at[sl], in_sem.at[sl])

    def rows_out(sl):
        return pltpu.make_async_copy(ybuf.at[sl], o_hbm.at[pl.ds(0, BM)], out_sem.at[sl])

    rows_in(s).wait()

    @pl.when(i >= 2)
    def _():
        rows_out(s).wait()

    x = jnp.concatenate([xbuf[s, :, c, :] for c in range(SUBLANES)], axis=1).astype(jnp.bfloat16)
    h = jnp.dot(x, w1s[...], preferred_element_type=jnp.float32) + b1_ref[0]
    glu = jnp.minimum(h[:, :D_FF], SWIGLU_LIMIT)
    lin = jnp.clip(h[:, D_FF:], -SWIGLU_LIMIT, SWIGLU_LIMIT)
    act = glu * jax.nn.sigmoid(SWIGLU_ALPHA * glu) * (lin + 1.0)
    y = jnp.dot(act.astype(jnp.bfloat16), w2s[...], preferred_element_type=jnp.float32) + b2_ref[0]
    for c in range(SUBLANES):
        ybuf[s, :, c, :] = y[:, c * LANES:(c + 1) * LANES]

    base = base_ref[i]
    nval = nval_ref[i]
    spare = TOP_K * T + i * BM - base - nval

    def scatter(r, c):
        a = order_ref[base + r]
        row = jnp.where(r < nval, (a & (TOP_K - 1)) * T + (a >> K_SHIFT), spare + r)
        pltpu.make_async_copy(ybuf.at[s, r], o_hbm.at[row], out_sem.at[s]).start()
        return c
    lax.fori_loop(0, BM, scatter, 0, unroll=8)

    @pl.when(i == n - 1)
    def _():
        rows_out(s).wait()

    @pl.when((i == n - 1) & (n >= 2))
    def _():
        rows_out(1 - s).wait()


def moe_mlp(x, order, block_e, block_base, block_nval, w1, b1, w2, b2, BM):
    T, D = x.shape
    assert D == SUBLANES * LANES
    n_blocks = block_e.shape[0]
    E = w1.shape[0]
    grid_spec = pltpu.PrefetchScalarGridSpec(
        num_scalar_prefetch=4,
        grid=(n_blocks,),
        in_specs=[pl.BlockSpec(memory_space=pl.ANY),
                  pl.BlockSpec((1, D, 2 * D_FF), lambda i, be, *_: (be[i], 0, 0)),
                  pl.BlockSpec((1, 1, 2 * D_FF), lambda i, be, *_: (be[i], 0, 0)),
                  pl.BlockSpec((1, D_FF, D), lambda i, be, *_: (be[i], 0, 0)),
                  pl.BlockSpec((1, 1, D), lambda i, be, *_: (be[i], 0, 0))],
        out_specs=pl.BlockSpec(memory_space=pl.ANY),
        scratch_shapes=[pltpu.VMEM((D, 2 * D_FF), jnp.bfloat16), pltpu.VMEM((D_FF, D), jnp.bfloat16),
                        pltpu.VMEM((2, BM, SUBLANES, LANES), jnp.float32),
                        pltpu.VMEM((2, BM, SUBLANES, LANES), jnp.float32),
                        pltpu.SemaphoreType.DMA((2,)), pltpu.SemaphoreType.DMA((2,))],
    )
    return pl.pallas_call(
        functools.partial(_moe_mlp_kernel, BM=BM, T=T),
        grid_spec=grid_spec,
        out_shape=jax.ShapeDtypeStruct((n_blocks * BM, SUBLANES, LANES), jnp.float32),
        compiler_params=pltpu.CompilerParams(dimension_semantics=("arbitrary",),
                                             vmem_limit_bytes=VMEM_LIMIT),
        name="moe_mlp",
    )(block_e, block_base, block_nval, order, x.reshape(T, SUBLANES, LANES), w1, b1.reshape(E, 1, -1), w2, b2.reshape(E, 1, -1))


def moe(x, router_w, router_b, mlp1_w, mlp1_b, mlp2_w, mlp2_b, BM=256):
    T, D = x.shape
    rw = jnp.pad(router_w, ((0, 0), (0, LANES - N_EXPERTS)))
    logits = pallas_matmul(x, rw)[:, :N_EXPERTS] + router_b
    top_v, top_e = lax.top_k(logits, TOP_K)
    gate = jax.nn.softmax(top_v, axis=-1)
    n_assign = T * TOP_K
    _, order = lax.sort((top_e.reshape(-1).astype(jnp.int32), jnp.arange(n_assign, dtype=jnp.int32)), num_keys=1)
    counts = jnp.sum(top_e.reshape(-1, 1) == jnp.arange(N_EXPERTS)[None, :], axis=0).astype(jnp.int32)
    padded = (counts + BM - 1) // BM * BM
    start = jnp.cumsum(counts) - counts
    pend = jnp.cumsum(padded)
    pstart = pend - padded
    n_blocks = -(-n_assign // BM) + N_EXPERTS
    row0 = jnp.arange(n_blocks, dtype=jnp.int32) * BM
    block_e = jnp.minimum(jnp.sum(pend[None, :] <= row0[:, None], axis=1), N_EXPERTS - 1).astype(jnp.int32)
    off = row0 - pstart[block_e]
    block_nval = jnp.clip(counts[block_e] - off, 0, BM).astype(jnp.int32)
    block_base = jnp.clip(start[block_e] + off, 0, n_assign).astype(jnp.int32)
    order = jnp.concatenate([order, jnp.zeros((BM,), jnp.int32)])
    yb = moe_mlp(x, order, block_e, block_base, block_nval, mlp1_w, mlp1_b, mlp2_w, mlp2_b, BM)
    yb = yb.reshape(-1, D)
    y = gate[:, 0:1] * yb[:T]
    for k in range(1, TOP_K):
        y = y + gate[:, k:k + 1] * yb[k * T:(k + 1) * T]
    return y


def layer_norm(x, g, b):
    mu = x.mean(-1, keepdims=True)
    var = jnp.square(x - mu).mean(-1, keepdims=True)
    return (x - mu) * lax.rsqrt(var + LN_EPS) * g + b


def rope(x, pos):
    half = ROT_DIM // 2
    inv = ROPE_THETA ** (-jnp.arange(half, dtype=jnp.float32) * 2.0 / ROT_DIM)
    ang = pos.astype(jnp.float32)[:, None] * inv
    cos, sin = jnp.cos(ang)[:, None, :], jnp.sin(ang)[:, None, :]
    x1, x2 = x[..., :half], x[..., half:ROT_DIM]
    return jnp.concatenate([x1 * cos - x2 * sin, x2 * cos + x1 * sin, x[..., ROT_DIM:]], axis=-1)


G0 = R_COLS + A_WIDTH + 6 * KV_WIDTH


def project(x2, w_in):
    w_main = jnp.concatenate([w_in[:, :G0], w_in[:, G0 + N_GATE:]], axis=1)
    w_gate = jnp.pad(w_in[:, G0:G0 + N_GATE], ((0, 0), (0, LANES - N_GATE)))
    return pallas_matmul(x2, w_main), pallas_matmul(x2, w_gate)[:, :N_GATE]


def split_projection(p, pg, B, T, pos):
    pr = p[:, :R_COLS].reshape(B, T, R_COLS)
    pa = p[:, R_COLS:G0].reshape(B, T, G0 - R_COLS)
    q = rope(pa[..., :A_WIDTH].reshape(B, T, N_HEADS, HEAD_DIM), pos)
    kvs = [pa[..., A_WIDTH + i * KV_WIDTH:A_WIDTH + (i + 1) * KV_WIDTH].reshape(B, T, N_KV, HEAD_DIM)
           for i in range(6)]
    kvs = [rope(z, pos) if i % 2 == 0 else z for i, z in enumerate(kvs)]
    gates = jax.nn.sigmoid(pg).reshape(B, T, N_KV, HPG, 3)
    return pr, q, kvs, gates


def rwkv_mixer(pr, shift_prev, wkv0, mu, w0, w_w2, a0, w_a2, g_w2, k_k, k_a, r_k, gn_g, gn_b):
    B, T, _ = pr.shape
    prev = jnp.concatenate([shift_prev[:, None, :], pr[:, :-1]], axis=1)
    xm = pr + (prev - pr) * mu
    o1, o2, o3 = R_WIDTH, 2 * R_WIDTH, 3 * R_WIDTH
    o4 = o3 + LORA_W
    o5 = o4 + LORA_A
    r, k, v = xm[..., :o1], xm[..., o1:o2], xm[..., o2:o3]
    xw, xa, xg = xm[..., o3:o4], xm[..., o4:o5], xm[..., o5:]
    w_log = -jax.nn.softplus(-(w0 + jnp.tanh(xw) @ w_w2)) - 0.5
    a = jax.nn.sigmoid(a0 + xa @ w_a2)
    g = jax.nn.sigmoid(xg) @ g_w2
    heads = lambda z: z.reshape(B, T, R_HEADS, R_HEAD)
    kk = heads(k * k_k)
    kk = kk / jnp.maximum(jnp.linalg.norm(kk, axis=-1, keepdims=True), 1e-12)
    k_h = heads(k * (1.0 + (a - 1.0) * k_a))
    r_h, v_h, a_h = heads(r), heads(v), heads(a)
    log_decay = -jnp.exp(heads(w_log))
    C = 64 if T % 64 == 0 else 8
    Tp = -(-T // C) * C
    hm = lambda z: jnp.pad(jnp.swapaxes(z, 1, 2), ((0, 0), (0, 0), (0, Tp - T), (0, 0)))
    y, wkv = wkv_chunked(hm(r_h), hm(log_decay), hm(k_h), hm(v_h), hm(-kk), hm(kk * a_h), wkv0, C)
    y = jnp.swapaxes(y[:, :, :T], 1, 2)
    mean = y.mean(-1, keepdims=True)
    var = jnp.square(y - mean).mean(-1, keepdims=True)
    yn = ((y - mean) * lax.rsqrt(var + GN_EPS)).reshape(B, T, R_WIDTH) * gn_g + gn_b
    bonus = (jnp.sum(r_h * k_h * r_k, axis=-1, keepdims=True) * v_h).reshape(B, T, R_WIDTH)
    return (yn + bonus) * g, wkv, pr[:, -1]


def compress(kv, pe, w1, b1, w2, b2):
    B, L = kv.shape[:2]
    n_chunk = L // CMP_STRIDE
    ch = kv[:, :n_chunk * CMP_STRIDE].reshape(B, n_chunk, CMP_STRIDE, N_KV, HEAD_DIM)
    blk = jnp.concatenate([ch[:, :-1], ch[:, 1:]], axis=2) + pe[:, None, :]
    blk = jnp.transpose(blk, (0, 3, 1, 2, 4)).reshape(B, N_KV, n_chunk - 1, CMP_LEN * HEAD_DIM)
    return jax.nn.gelu(blk @ w1 + b1) @ w2 + b2


def nsa_prompt(q, kvs, gates, pe, w1, b1, w2, b2):
    kc_raw, vc_raw, ks, vs, kw, vw = kvs
    B, T = q.shape[:2]
    kc = compress(kc_raw, pe[0], w1[0], b1[0], w2[0], b2[0])
    vc = compress(vc_raw, pe[1], w1[1], b1[1], w2[1], b2[1])
    qg = _bf(jnp.transpose(q.reshape(B, T, N_KV, HPG, HEAD_DIM), (0, 2, 3, 1, 4)) * (HEAD_DIM ** -0.5))
    tk = lambda z: _bf(jnp.swapaxes(z, 1, 2))
    o = nsa_prompt_attention(qg, kc, vc, tk(ks), tk(vs), tk(kw), tk(vw),
                             jnp.transpose(gates, (0, 2, 3, 1, 4)))
    return jnp.transpose(o, (0, 3, 1, 2, 4)).reshape(B, T, A_WIDTH)


def nsa_sample(q, kvs, gates, cmp_cache, sel_cache, win_cache, page_table, pe, w1, b1, w2, b2):
    kc_new, vc_new, ks_new, vs_new, kw_new, vw_new = kvs
    DB, DS = q.shape[:2]
    past = page_table.shape[1] * PAGE_SIZE
    kvc = compress_sample(cmp_cache, page_table, pe, w1, b1, w2, b2)
    qg = _bf(jnp.transpose(q.reshape(DB, DS, N_KV, HPG, HEAD_DIM), (0, 2, 3, 1, 4)) * (HEAD_DIM ** -0.5))
    qg = qg.reshape(DB, N_KV, HPG * DS, HEAD_DIM)
    gt = jnp.transpose(gates, (0, 2, 3, 1, 4)).reshape(DB, N_KV, HPG * DS, 3)
    pack = lambda k, v: _bf(jnp.pad(jnp.transpose(jnp.stack([k, v], axis=1), (0, 1, 3, 2, 4)),
                                    ((0, 0), (0, 0), (0, 0), (0, NEW_PAD - DS), (0, 0))))
    o = nsa_sample_attention(qg, kvc, sel_cache, win_cache, page_table, pack(ks_new, vs_new),
                             pack(kw_new, vw_new), gt, past)
    o = jnp.transpose(o.reshape(DB, N_KV, HPG, DS, HEAD_DIM), (0, 3, 1, 2, 4)).reshape(DB, DS, -1)
    new_k = jnp.concatenate([win_cache[:, DS:, 0], kw_new], axis=1)
    new_v = jnp.concatenate([win_cache[:, DS:, 1], vw_new], axis=1)
    return o, jnp.stack([new_k, new_v], axis=2)


def merge_and_ffn(x, y_r, y_a, mg, w_pa, w_pb, w_o, ln1_g, ln1_b, router_w, router_b,
                  mlp1_w, mlp1_b, mlp2_w, mlp2_b, ln2_g, ln2_b):
    m = (jax.nn.sigmoid(mg[:, :D_MODEL]) * pallas_matmul(y_r, w_pa)
         + jax.nn.sigmoid(mg[:, D_MODEL:]) * pallas_matmul(y_a, w_pb))
    h = layer_norm(DN_ALPHA * x + pallas_matmul(m, w_o), ln1_g, ln1_b)
    f = moe(h, router_w, router_b, mlp1_w, mlp1_b, mlp2_w, mlp2_b)
    return layer_norm(DN_ALPHA * h + f, ln2_g, ln2_b)


def kernel(x_prompt, x_sample, cache_cmp_kv, cache_sel_kv, cache_win_kv, state_wkv, state_shift,
           page_table, w_in, mu_shift, w0, w_w2, a0, w_a2, g_w2, k_k, k_a, r_k, gn_g, gn_b,
           cmp_pe, cmp_w1, cmp_b1, cmp_w2, cmp_b2, w_pa, w_pb, w_o, ln1_g, ln1_b,
           router_w, router_b, mlp1_w, mlp1_b, mlp2_w, mlp2_b, ln2_g, ln2_b):
    B, T, D = x_prompt.shape
    DB, DS, _ = x_sample.shape
    n_p = B * T
    past = page_table.shape[1] * PAGE_SIZE
    pos_p = jnp.arange(T)
    pos_s = past + jnp.arange(DS)
    wb_p = min(WINDOW, T)
    h_all = jnp.concatenate([x_prompt.reshape(n_p, D), x_sample.reshape(DB * DS, D)])
    cmp_p, sel_p, win_p, wkv_p, shift_p = [], [], [], [], []
    cmp_s, sel_s, win_s, wkv_s, shift_s = [], [], [], [], []
    for l in range(DEPTH):
        rwkv_w = (mu_shift[l], w0[l], w_w2[l], a0[l], w_a2[l], g_w2[l], k_k[l], k_a[l], r_k[l],
                  gn_g[l], gn_b[l])
        cmp_w = (cmp_pe[l], cmp_w1[l], cmp_b1[l], cmp_w2[l], cmp_b2[l])
        out_w = (w_pa[l], w_pb[l], w_o[l], ln1_g[l], ln1_b[l], router_w[l], router_b[l],
                 mlp1_w[l], mlp1_b[l], mlp2_w[l], mlp2_b[l], ln2_g[l], ln2_b[l])
        p_all, pg_all = project(h_all, w_in[l])
        pr, q, kvs, gates = split_projection(p_all[:n_p], pg_all[:n_p], B, T, pos_p)
        y_r, wkv, shift = rwkv_mixer(pr, jnp.zeros((B, R_COLS), pr.dtype),
                                     jnp.zeros((B, R_HEADS, R_HEAD, R_HEAD), jnp.float32), *rwkv_w)
        y_a = nsa_prompt(q, kvs, gates, *cmp_w)
        cmp_p.append(jnp.stack([kvs[0], kvs[1]], axis=2))
        sel_p.append(jnp.stack([kvs[2], kvs[3]], axis=2))
        win_p.append(jnp.stack([kvs[4][:, T - wb_p:], kvs[5][:, T - wb_p:]], axis=2))
        wkv_p.append(wkv)
        shift_p.append(shift)
        pr, q, kvs, gates = split_projection(p_all[n_p:], pg_all[n_p:], DB, DS, pos_s)
        y_r_s, wkv, shift = rwkv_mixer(pr, state_shift[l], state_wkv[l], *rwkv_w)
        y_a_s, new_win = nsa_sample(q, kvs, gates, cache_cmp_kv[l], cache_sel_kv[l], cache_win_kv[l],
                                    page_table, *cmp_w)
        cmp_s.append(jnp.stack([kvs[0], kvs[1]], axis=2))
        sel_s.append(jnp.stack([kvs[2], kvs[3]], axis=2))
        win_s.append(new_win)
        wkv_s.append(wkv)
        shift_s.append(shift)
        rows = lambda a, b: jnp.concatenate([a.reshape(n_p, -1), b.reshape(DB * DS, -1)])
        h_all = merge_and_ffn(h_all, rows(y_r, y_r_s), rows(y_a, y_a_s), p_all[:, G0:], *out_w)
    hp = h_all[:n_p].reshape(B, T, D)
    hs = h_all[n_p:].reshape(DB, DS, D)
    return (hp, hs, jnp.stack(cmp_p), jnp.stack(sel_p), jnp.stack(win_p), jnp.stack(wkv_p),
            jnp.stack(shift_p), jnp.stack(cmp_s), jnp.stack(sel_s), jnp.stack(win_s),
            jnp.stack(wkv_s), jnp.stack(shift_s))
```

```python
import functools

import jax
import jax.numpy as jnp
from jax import lax
from jax.experimental import pallas as pl
from jax.experimental.pallas import tpu as pltpu

D_MODEL = 1024
DEPTH = 1
PAGE_SIZE = 128

R_HEADS = 8
R_HEAD = 64
R_WIDTH = R_HEADS * R_HEAD
LORA_W = 64
LORA_A = 64
LORA_G = 128
R_COLS = 3 * R_WIDTH + LORA_W + LORA_A + LORA_G
GN_EPS = 64e-5

N_HEADS = 8
N_KV = 2
HPG = N_HEADS // N_KV
HEAD_DIM = 64
A_WIDTH = N_HEADS * HEAD_DIM
KV_WIDTH = N_KV * HEAD_DIM
N_GATE = 3 * N_HEADS
A_COLS = A_WIDTH + 6 * KV_WIDTH + N_GATE
ROT_DIM = HEAD_DIM // 4
ROPE_THETA = 500000.0
CMP_STRIDE = 16
CMP_LEN = 2 * CMP_STRIDE
CMP_HIDDEN = 256
SEL_BLOCK = 64
SEL_TOP = 16
WINDOW = 512

N_EXPERTS = 32
TOP_K = 4
K_SHIFT = 2
D_FF = 1024
SWIGLU_LIMIT = 7.0
SWIGLU_ALPHA = 1.702

DN_ALPHA = (2 * DEPTH) ** 0.25
LN_EPS = 1e-5
NEG = -1e30

LANES = 128
SUBLANES = 8
PAGES_PER_STEP = 8
NEW_PAD = 8
VMEM_LIMIT = 56 * 1024 * 1024


def _bf(x):
    return x.astype(jnp.bfloat16)


def _dot(a, b):
    return jnp.dot(_bf(a), _bf(b), preferred_element_type=jnp.float32)


def _dot_nt(a, b):
    return lax.dot_general(_bf(a), _bf(b), (((1,), (1,)), ((), ())),
                           preferred_element_type=jnp.float32)


def _mm_kernel(x_ref, w_ref, o_ref):
    o_ref[...] = _dot(x_ref[...], w_ref[...])


def pallas_matmul(x, w):
    x, w = _bf(x), _bf(w)
    M, K = x.shape
    N = w.shape[1]
    tm = next(t for t in (512, 384, 256, 128, M) if M % t == 0)
    resident = 2 * (2 * K * N + 4 * tm * N + 2 * tm * K) <= VMEM_LIMIT - (8 << 20)
    tn = N if resident else next(t for t in (512, 256, LANES) if N % t == 0)
    return pl.pallas_call(
        _mm_kernel,
        grid=(N // tn, M // tm),
        in_specs=[pl.BlockSpec((tm, K), lambda j, i: (i, 0)),
                  pl.BlockSpec((K, tn), lambda j, i: (0, j))],
        out_specs=pl.BlockSpec((tm, tn), lambda j, i: (i, j)),
        out_shape=jax.ShapeDtypeStruct((M, N), jnp.float32),
        compiler_params=pltpu.CompilerParams(vmem_limit_bytes=VMEM_LIMIT),
        name="mm",
    )(x, w)


def _bmm(a, b):
    return lax.dot_general(_bf(a), _bf(b), (((2,), (1,)), ((0,), (0,))), preferred_element_type=jnp.float32)


def _bmm_nt(a, b):
    return lax.dot_general(_bf(a), _bf(b), (((2,), (2,)), ((0,), (0,))), preferred_element_type=jnp.float32)


def _bmm_tn(a, b):
    return lax.dot_general(_bf(a), _bf(b), (((1,), (1,)), ((0,), (0,))), preferred_element_type=jnp.float32)


def _head_sum(x, ones_bd):
    hi = _bf(x)
    rem = x - hi.astype(jnp.float32)
    mid = _bf(rem)
    lo = _bf(rem - mid.astype(jnp.float32))
    dot = lambda t: jnp.dot(t, ones_bd, preferred_element_type=jnp.float32)
    return dot(hi) + dot(mid) + dot(lo)


def head_ones():
    h = jnp.arange(R_WIDTH) // R_HEAD
    return _bf(h[:, None] == h[None, :])


def _rwkv_prep_kernel(p_ref, pb_ref, sp_ref, mu_ref, w0_ref, ww2_ref, a0_ref, wa2_ref, gw2_ref,
                      kk_ref, ka_ref, rk_ref, bd_ref,
                      r_ref, lw_ref, k_ref, v_ref, a_ref, b_ref, bonus_ref, g_ref, *, tiles_per_seq):
    i = pl.program_id(0)
    pr = p_ref[...]
    first = jnp.where(i % tiles_per_seq == 0, sp_ref[0], pb_ref[SUBLANES - 1:SUBLANES, :])
    rolled = pltpu.roll(pr, shift=1, axis=0)
    prev = jnp.where(lax.broadcasted_iota(jnp.int32, (pr.shape[0], 1), 0) == 0, first, rolled)
    xm = pr + (prev - pr) * mu_ref[...]
    o1, o2, o3 = R_WIDTH, 2 * R_WIDTH, 3 * R_WIDTH
    o4 = o3 + LORA_W
    o5 = o4 + LORA_A
    r, k, v = xm[:, :o1], xm[:, o1:o2], xm[:, o2:o3]
    xw, xa, xg = xm[:, o3:o4], xm[:, o4:o5], xm[:, o5:]
    dot = lambda x, w_ref: jnp.dot(_bf(x), _bf(w_ref[...]), preferred_element_type=jnp.float32)
    w_log = -jax.nn.softplus(-(w0_ref[...] + dot(jnp.tanh(xw), ww2_ref))) - 0.5
    a = jax.nn.sigmoid(a0_ref[...] + dot(xa, wa2_ref))
    g_ref[...] = dot(jax.nn.sigmoid(xg), gw2_ref)
    ones_bd = bd_ref[...]
    kk = k * kk_ref[...]
    kk = kk / jnp.maximum(jnp.sqrt(_head_sum(kk * kk, ones_bd)), 1e-12)
    k_h = k * (1.0 + (a - 1.0) * ka_ref[...])
    r_ref[...] = r
    lw_ref[...] = -jnp.exp(w_log)
    k_ref[...] = k_h
    v_ref[...] = v
    a_ref[...] = -kk
    b_ref[...] = kk * a
    bonus_ref[...] = _head_sum(r * k_h * rk_ref[...], ones_bd) * v


def rwkv_prep(p_all, n_rows, seq_len, shift_prev, mu, w0, w_w2, a0, w_a2, g_w2, k_k, k_a, r_k, tm=256):
    assert seq_len % tm == 0 and n_rows % seq_len == 0
    n_seq = n_rows // seq_len
    tiles_per_seq = seq_len // tm
    row = lambda z: z.reshape(1, -1)
    const = lambda shape: pl.BlockSpec(shape, lambda i: (0,) * len(shape))
    out = pl.BlockSpec((tm, R_WIDTH), lambda i: (i, 0))
    outs = pl.pallas_call(
        functools.partial(_rwkv_prep_kernel, tiles_per_seq=tiles_per_seq),
        grid=(n_rows // tm,),
        in_specs=[pl.BlockSpec((tm, R_COLS), lambda i: (i, 0)),
                  pl.BlockSpec((SUBLANES, R_COLS), lambda i: (jnp.maximum(i * (tm // SUBLANES) - 1, 0), 0)),
                  pl.BlockSpec((1, 1, R_COLS), lambda i: (i // tiles_per_seq, 0, 0)),
                  const((1, R_COLS)), const((1, R_WIDTH)), const((LORA_W, R_WIDTH)), const((1, R_WIDTH)),
                  const((LORA_A, R_WIDTH)), const((LORA_G, R_WIDTH)), const((1, R_WIDTH)), const((1, R_WIDTH)),
                  const((1, R_WIDTH)), const((R_WIDTH, R_WIDTH))],
        out_specs=[out] * 8,
        out_shape=[jax.ShapeDtypeStruct((n_rows, R_WIDTH), jnp.float32)] * 8,
        compiler_params=pltpu.CompilerParams(dimension_semantics=("arbitrary",), vmem_limit_bytes=VMEM_LIMIT),
        name="rwkv_prep",
    )(p_all, p_all, shift_prev.reshape(n_seq, 1, R_COLS), row(mu), row(w0), w_w2, row(a0), w_a2, g_w2,
      row(k_k), row(k_a), row(r_k), head_ones())
    return outs


def _wkv_chunk_kernel(r_ref, lw_ref, k_ref, v_ref, a_ref, b_ref, bonus_ref, g_ref, gng_ref, gnb_ref, bd_ref,
                      s0_ref, y_ref, s_out_ref, s_scr, *, C, H):
    c = pl.program_id(1)
    D = R_HEAD

    @pl.when(c == 0)
    def _():
        s_scr[...] = s0_ref[0]

    row = lax.broadcasted_iota(jnp.int32, (H, C, C), 1)
    col = lax.broadcasted_iota(jnp.int32, (H, C, C), 2)
    incl = row >= col
    strict = row > col
    ltri = jnp.where(incl, 1.0, 0.0).astype(jnp.bfloat16)
    eye = jnp.where(row == col, 1.0, 0.0).astype(jnp.float32)

    heads = lambda ref: jnp.stack([ref[0, :, h * D:(h + 1) * D] for h in range(H)])
    lw = heads(lw_ref)
    r = heads(r_ref)
    k = heads(k_ref)
    v = heads(v_ref)
    a = heads(a_ref)
    b = heads(b_ref)
    hi = _bf(lw)
    rem = lw - hi.astype(jnp.float32)
    mid = _bf(rem)
    lo = _bf(rem - mid.astype(jnp.float32))
    lp = _bmm(ltri, hi) + _bmm(ltri, mid) + _bmm(ltri, lo)
    lp_end = lp[:, C - 1:C, :]
    p_end = jnp.exp(lp_end)
    p_inv = jnp.exp(-lp)
    at = a * jnp.exp(lp - lw)
    rt = r * jnp.exp(lp)
    bt = b * p_inv
    kt = k * p_inv
    p_hat = jnp.exp(lp_end - lp)
    bh = b * p_hat
    kh = k * p_hat

    n_ab = jnp.where(strict, _bmm_nt(at, bt), 0.0)
    a_ak = jnp.where(strict, _bmm_nt(at, kt), 0.0)
    a_rb = jnp.where(incl, _bmm_nt(rt, bt), 0.0)
    a_rk = jnp.where(incl, _bmm_nt(rt, kt), 0.0)

    t_inv = eye + n_ab
    n_pow = n_ab
    span = 2
    while span < C:
        n_pow = _bmm(n_pow, n_pow)
        t_inv = _bmm(t_inv, eye + n_pow)
        span *= 2

    s = s_scr[...]
    rhs = _bmm_nt(at, s) + _bmm(a_ak, v)
    u = _bmm(t_inv, rhs)
    y = _bmm_nt(rt, s) + _bmm(a_rb, u) + _bmm(a_rk, v)
    s_new = s * p_end + _bmm_tn(u, bh) + _bmm_tn(v, kh)
    s_scr[...] = s_new

    yt = jnp.concatenate([y[h] for h in range(H)], axis=1)
    ones_bd = bd_ref[...]
    dev = yt - _head_sum(yt, ones_bd) * (1.0 / D)
    var = _head_sum(dev * dev, ones_bd) * (1.0 / D)
    yn = dev * lax.rsqrt(var + GN_EPS) * gng_ref[...] + gnb_ref[...]
    y_ref[0] = (yn + bonus_ref[0]) * g_ref[0]

    @pl.when(c == pl.num_programs(1) - 1)
    def _():
        s_out_ref[0] = s_new


def wkv_chunked(r, lw, k, v, a, b, bonus, g, gn_g, gn_b, s0, C):
    B, T, W = r.shape
    H, D = R_HEADS, R_HEAD
    assert T % C == 0
    seq = pl.BlockSpec((1, C, W), lambda bi, ci: (bi, ci, 0))
    st = pl.BlockSpec((1, H, D, D), lambda bi, ci: (bi, 0, 0, 0))
    const = lambda shape: pl.BlockSpec(shape, lambda bi, ci: (0,) * len(shape))
    return pl.pallas_call(
        functools.partial(_wkv_chunk_kernel, C=C, H=H),
        grid=(B, T // C),
        in_specs=[seq] * 8 + [const((1, W)), const((1, W)), const((W, W)), st],
        out_specs=[seq, st],
        out_shape=[jax.ShapeDtypeStruct((B, T, W), jnp.float32),
                   jax.ShapeDtypeStruct((B, H, D, D), jnp.float32)],
        scratch_shapes=[pltpu.VMEM((H, D, D), jnp.float32)],
        compiler_params=pltpu.CompilerParams(dimension_semantics=("arbitrary", "arbitrary")),
        name="wkv_chunk",
    )(r, lw, k, v, a, b, bonus, g, gn_g.reshape(1, W), gn_b.reshape(1, W), head_ones(), s0)


def _nsa_prompt_kernel(q_ref, kc_ref, vc_ref, cover_ref, ks_ref, vs_ref, kw_ref, vw_ref, g_ref, o_ref,
                       *, TQ, TK, NC, NCP):
    f32 = jnp.float32
    bf16 = jnp.bfloat16
    qb = pl.program_id(2)
    R = HPG * TQ
    q = q_ref[0, 0].reshape(R, HEAD_DIM)
    t_pos = qb * TQ + lax.broadcasted_iota(jnp.int32, (TQ, 1), 0)

    n_idx = lax.broadcasted_iota(jnp.int32, (1, NCP), 1)
    c_ok = ((n_idx * CMP_STRIDE + (CMP_LEN - 1)) <= t_pos) & (n_idx < NC)
    s_c = _dot_nt(q, kc_ref[0, 0]).reshape(HPG, TQ, NCP)
    s_c = jnp.where(c_ok[None], s_c, NEG)
    m_c = jnp.max(s_c, axis=-1, keepdims=True)
    p_c = jnp.where(c_ok[None], jnp.exp(s_c - m_c), 0.0)
    l_c = jnp.sum(p_c, axis=-1, keepdims=True)
    p_c = p_c / jnp.where(l_c > 0.0, l_c, 1.0)
    p_cb = p_c.astype(bf16)
    o_c = jnp.dot(p_cb.reshape(R, NCP), vc_ref[0, 0], preferred_element_type=f32)

    cover_t = cover_ref[...]
    imp = _dot_nt(cover_t, p_cb[0])
    for h in range(1, HPG):
        imp = imp + _dot_nt(cover_t, p_cb[h])
    s_col = lax.broadcasted_iota(jnp.int32, (LANES, 1), 0)
    t_row = qb * TQ + lax.broadcasted_iota(jnp.int32, (1, TQ), 1)
    cur = t_row // SEL_BLOCK
    forced = (s_col == 0) | (s_col == cur) | (s_col == cur - 1)
    causal = (s_col * SEL_BLOCK) <= t_row
    score = jnp.where(forced, 1e6, imp)
    score = jnp.where(causal, score, NEG)
    s_col_f = s_col.astype(f32)
    sel_t = jnp.zeros((LANES, TQ), f32)
    for _ in range(SEL_TOP):
        top = jnp.max(score, axis=0, keepdims=True)
        first = jnp.min(jnp.where(score == top, s_col_f, float(LANES)), axis=0, keepdims=True)
        hit = s_col_f == first
        sel_t = jnp.where(hit & (top > 0.5 * NEG), 1.0, sel_t)
        score = jnp.where(hit, -3e38, score)

    sel_neg = ((sel_t.T - 1.0) * (-NEG)).astype(bf16)
    q_aug = jnp.concatenate([jnp.concatenate([sel_neg] * HPG, axis=0), q], axis=1)
    key_off = lax.broadcasted_iota(jnp.int32, (1, TK), 1)

    def sel_tile(j, carry, diagonal):
        m, l, acc = carry
        start = pl.multiple_of(j * TK, TK)
        k = ks_ref[0, 0, pl.ds(start, TK), :]
        v = vs_ref[0, 0, pl.ds(start, TK), :]
        s = _dot_nt(q_aug, k).reshape(HPG, TQ, TK)
        if diagonal:
            s = jnp.where(((key_off + j * TK) <= t_pos)[None], s, NEG)
        m_new = jnp.maximum(m, jnp.max(s, axis=-1, keepdims=True))
        p = jnp.exp(s - m_new)
        alpha = jnp.exp(m - m_new)
        l = alpha * l + jnp.sum(p, axis=-1, keepdims=True)
        pv = jnp.dot(p.astype(bf16).reshape(R, TK), v, preferred_element_type=f32)
        acc = alpha * acc + pv.reshape(HPG, TQ, HEAD_DIM)
        return m_new, l, acc

    m0 = jnp.full((HPG, TQ, 1), NEG, f32)
    l0 = jnp.zeros((HPG, TQ, 1), f32)
    a0 = jnp.zeros((HPG, TQ, HEAD_DIM), f32)
    n_full = (qb * TQ) // TK
    carry = lax.fori_loop(0, n_full, functools.partial(sel_tile, diagonal=False), (m0, l0, a0))
    _, l_s, acc_s = sel_tile(n_full, carry, True)
    o_s = acc_s / l_s

    n_w = WINDOW // TQ + 1
    lane_q = lax.broadcasted_iota(jnp.int32, (1, TQ), 1)
    s_w, ok_w, v_w = [], [], []
    for i in range(n_w):
        kb = qb - (n_w - 1) + i
        start = pl.multiple_of(jnp.maximum(kb, 0) * TQ, TQ)
        k = kw_ref[0, 0, pl.ds(start, TQ), :]
        v_w.append(vw_ref[0, 0, pl.ds(start, TQ), :])
        kpos = kb * TQ + lane_q
        ok = (kpos <= t_pos) & (kpos >= t_pos - WINDOW) & (kpos >= 0)
        ok_w.append(ok)
        s_w.append(jnp.where(ok[None], _dot_nt(q, k).reshape(HPG, TQ, TQ), NEG))
    m_w = s_w[0].max(axis=-1, keepdims=True)
    for i in range(1, n_w):
        m_w = jnp.maximum(m_w, s_w[i].max(axis=-1, keepdims=True))
    l_w = jnp.zeros((HPG, TQ, 1), f32)
    acc_w = jnp.zeros((R, HEAD_DIM), f32)
    for i in range(n_w):
        p = jnp.where(ok_w[i][None], jnp.exp(s_w[i] - m_w), 0.0)
        l_w = l_w + jnp.sum(p, axis=-1, keepdims=True)
        acc_w = acc_w + jnp.dot(p.astype(bf16).reshape(R, TQ), v_w[i], preferred_element_type=f32)
    o_w = acc_w.reshape(HPG, TQ, HEAD_DIM) / l_w

    g = g_ref[0, 0]
    o_ref[0, 0] = (g[:, :, 0:1] * o_c.reshape(HPG, TQ, HEAD_DIM) + g[:, :, 1:2] * o_s
                   + g[:, :, 2:3] * o_w)


def nsa_prompt_attention(q, kc, vc, ks, vs, kw, vw, gates, TQ=256, TK=1024):
    B, G, _, T, D = q.shape
    NC = kc.shape[2]
    NCP = -(-NC // LANES) * LANES
    NS = T // SEL_BLOCK
    assert NS <= LANES and T % TK == 0 and TK % TQ == 0 and WINDOW % TQ == 0
    kc = jnp.pad(kc, ((0, 0), (0, 0), (0, NCP - NC), (0, 0))).astype(jnp.bfloat16)
    vc = jnp.pad(vc, ((0, 0), (0, 0), (0, NCP - NC), (0, 0))).astype(jnp.bfloat16)
    c_start = jnp.arange(NCP) * CMP_STRIDE
    s_start = jnp.arange(LANES) * SEL_BLOCK
    cover = ((c_start[:, None] < s_start[None, :] + SEL_BLOCK)
             & (c_start[:, None] + CMP_LEN > s_start[None, :])
             & (jnp.arange(NCP)[:, None] < NC) & (jnp.arange(LANES)[None, :] < NS)).astype(jnp.bfloat16)
    onehot = (jnp.arange(T)[:, None] // SEL_BLOCK == jnp.arange(LANES)[None, :]).astype(jnp.bfloat16)
    ks = jnp.concatenate([jnp.broadcast_to(onehot, (B, G, T, LANES)), ks], axis=-1)
    full = lambda n, d=D: pl.BlockSpec((1, 1, n, d), lambda b, g, i: (b, g, 0, 0))
    qspec = pl.BlockSpec((1, 1, HPG, TQ, D), lambda b, g, i: (b, g, 0, i, 0))
    return pl.pallas_call(
        functools.partial(_nsa_prompt_kernel, TQ=TQ, TK=TK, NC=NC, NCP=NCP),
        grid=(B, G, T // TQ),
        in_specs=[qspec, full(NCP), full(NCP),
                  pl.BlockSpec((LANES, NCP), lambda b, g, i: (0, 0)),
                  full(T, LANES + D), full(T), full(T), full(T),
                  pl.BlockSpec((1, 1, HPG, TQ, 3), lambda b, g, i: (b, g, 0, i, 0))],
        out_specs=qspec,
        out_shape=jax.ShapeDtypeStruct((B, G, HPG, T, D), jnp.float32),
        compiler_params=pltpu.CompilerParams(
            dimension_semantics=("arbitrary", "arbitrary", "arbitrary"),
            vmem_limit_bytes=VMEM_LIMIT),
        name="nsa_prompt",
    )(q, kc, vc, cover.T, ks, vs, kw, vw, gates)


def _cmp_sample_kernel(pt_ref, *refs, n_chunk):
    pages = refs[:PAGES_PER_STEP]
    wcat_ref, c1_ref, w2_ref, b2_ref, o_ref, seq, tile = refs[PAGES_PER_STEP:]
    j = pl.program_id(1)
    rows = PAGE_SIZE // CMP_STRIDE
    for i in range(PAGES_PER_STEP):
        dst = pl.multiple_of((j * PAGES_PER_STEP + i) * rows, rows)
        for kv in range(2):
            for g in range(N_KV):
                q = kv * N_KV + g
                tile[...] = pages[i][0, kv, g].T
                for p in range(CMP_STRIDE):
                    seq[p * 2 * N_KV + q, pl.ds(dst, rows), :] = tile[pl.ds(p, rows, stride=CMP_STRIDE), :]

    @pl.when(j == pl.num_programs(1) - 1)
    def _():
        for kv in range(2):
            for g in range(N_KV):
                q = kv * N_KV + g
                acc = jnp.zeros((n_chunk, 2 * CMP_HIDDEN), jnp.float32)
                for p in range(CMP_STRIDE):
                    acc = acc + jnp.dot(_bf(seq[p * 2 * N_KV + q]), wcat_ref[kv, p],
                                        preferred_element_type=jnp.float32)
                first = acc[:, :CMP_HIDDEN]
                second = acc[:, CMP_HIDDEN:]
                second = jnp.concatenate([second[1:], second[:1]], axis=0)
                hid = jax.nn.gelu(first + second + c1_ref[kv])
                out = jnp.dot(_bf(hid), w2_ref[kv], preferred_element_type=jnp.float32) + b2_ref[kv]
                o_ref[0, kv, g] = _bf(out)


def compress_sample(cache, page_table, pe, w1, b1, w2, b2):
    n_pool = cache.shape[0]
    DB, n_pages = page_table.shape
    assert n_pages % PAGES_PER_STEP == 0
    rows = PAGE_SIZE // CMP_STRIDE
    n_chunk = n_pages * rows
    view = jnp.transpose(cache, (0, 2, 3, 4, 1))
    w1r = w1.reshape(2, CMP_LEN, HEAD_DIM, CMP_HIDDEN)
    wcat = _bf(jnp.concatenate([w1r[:, :CMP_STRIDE], w1r[:, CMP_STRIDE:]], axis=-1))
    c1 = (jnp.einsum('kn,knh->kh', pe.reshape(2, CMP_LEN * HEAD_DIM), w1) + b1).reshape(2, 1, CMP_HIDDEN)
    page_spec = lambda i: pl.BlockSpec((1, 2, N_KV, HEAD_DIM, PAGE_SIZE),
                                       lambda b, j, pt: (pt[b, j * PAGES_PER_STEP + i], 0, 0, 0, 0))
    const = lambda shape: pl.BlockSpec(shape, lambda b, j, pt: (0,) * len(shape))
    grid_spec = pltpu.PrefetchScalarGridSpec(
        num_scalar_prefetch=1,
        grid=(DB, n_pages // PAGES_PER_STEP),
        in_specs=[page_spec(i) for i in range(PAGES_PER_STEP)]
        + [const((2, CMP_STRIDE, HEAD_DIM, 2 * CMP_HIDDEN)), const((2, 1, CMP_HIDDEN)),
           const((2, CMP_HIDDEN, HEAD_DIM)), const((2, 1, HEAD_DIM))],
        out_specs=pl.BlockSpec((1, 2, N_KV, n_chunk, HEAD_DIM), lambda b, j, pt: (b, 0, 0, 0, 0)),
        scratch_shapes=[pltpu.VMEM((CMP_STRIDE * 2 * N_KV, n_chunk, HEAD_DIM), jnp.float32),
                        pltpu.VMEM((PAGE_SIZE, HEAD_DIM), jnp.float32)],
    )
    return pl.pallas_call(
        functools.partial(_cmp_sample_kernel, n_chunk=n_chunk),
        grid_spec=grid_spec,
        out_shape=jax.ShapeDtypeStruct((DB, 2, N_KV, n_chunk, HEAD_DIM), jnp.bfloat16),
        compiler_params=pltpu.CompilerParams(dimension_semantics=("arbitrary", "arbitrary"),
                                             vmem_limit_bytes=VMEM_LIMIT),
        name="cmp_sample",
    )(page_table, *([view] * PAGES_PER_STEP), wcat, c1, _bf(w2), b2.reshape(2, 1, HEAD_DIM))


def _nsa_sample_kernel(pt_ref, *refs, DS, NC, past):
    pages = refs[:PAGES_PER_STEP]
    (q_ref, kvc_ref, cover_ref, new_sel_ref, win_ref, new_win_ref, g_ref, o_ref,
     qaug, m_s, l_s, acc_s, oc_s) = refs[PAGES_PER_STEP:]
    f32 = jnp.float32
    j = pl.program_id(1)
    R = HPG * DS
    t_row = lax.broadcasted_iota(jnp.int32, (R, 1), 0) % DS
    n_chunk = kvc_ref.shape[3]

    @pl.when(j == 0)
    def _():
        n_idx = lax.broadcasted_iota(jnp.int32, (1, n_chunk), 1)
        s_col = lax.broadcasted_iota(jnp.int32, (LANES, 1), 0)
        s_col_f = s_col.astype(f32)
        last_blk = past // SEL_BLOCK - 1
        for g in range(N_KV):
            q = q_ref[0, g]
            s_c = jnp.where(n_idx < NC, _dot_nt(q, kvc_ref[0, 0, g]), NEG)
            p_c = jnp.exp(s_c - jnp.max(s_c, axis=-1, keepdims=True))
            p_c = p_c / jnp.sum(p_c, axis=-1, keepdims=True)
            p_cb = _bf(p_c)
            oc_s[g] = jnp.dot(p_cb, kvc_ref[0, 1, g], preferred_element_type=f32)
            imp_rows = _dot_nt(cover_ref[...], p_cb)
            imp = imp_rows[:, 0:DS]
            for h in range(1, HPG):
                imp = imp + imp_rows[:, h * DS:(h + 1) * DS]
            forced = (s_col == 0) | (s_col == last_blk)
            score = jnp.where(forced, 1e6, imp)
            score = jnp.where(s_col <= last_blk, score, NEG)
            sel_t = jnp.zeros((LANES, DS), f32)
            for _ in range(SEL_TOP - 1):
                top = jnp.max(score, axis=0, keepdims=True)
                first = jnp.min(jnp.where(score == top, s_col_f, float(LANES)), axis=0, keepdims=True)
                hit = s_col_f == first
                sel_t = jnp.where(hit & (top > 0.5 * NEG), 1.0, sel_t)
                score = jnp.where(hit, -3e38, score)
            sel_neg = _bf((sel_t.T - 1.0) * (-NEG))
            qaug[g] = jnp.concatenate([jnp.concatenate([sel_neg] * HPG, axis=0), q], axis=1)
        m_s[...] = jnp.full(m_s.shape, NEG, f32)
        l_s[...] = jnp.zeros(l_s.shape, f32)
        acc_s[...] = jnp.zeros(acc_s.shape, f32)

    TK = PAGES_PER_STEP * PAGE_SIZE
    blk = (lax.broadcasted_iota(jnp.int32, (LANES, TK), 1) // SEL_BLOCK
           + j * (TK // SEL_BLOCK))
    onehot = _bf(jnp.where(blk == lax.broadcasted_iota(jnp.int32, (LANES, TK), 0), 1.0, 0.0))
    for g in range(N_KV):
        k_t = _bf(jnp.concatenate([pages[i][0, 0, g] for i in range(PAGES_PER_STEP)], axis=1))
        v_t = _bf(jnp.concatenate([pages[i][0, 1, g] for i in range(PAGES_PER_STEP)], axis=1))
        s = jnp.dot(qaug[g], jnp.concatenate([onehot, k_t], axis=0), preferred_element_type=f32)
        m_new = jnp.maximum(m_s[g], jnp.max(s, axis=-1, keepdims=True))
        p = jnp.exp(s - m_new)
        alpha = jnp.exp(m_s[g] - m_new)
        l_s[g] = alpha * l_s[g] + jnp.sum(p, axis=-1, keepdims=True)
        acc_s[g] = alpha * acc_s[g] + _dot_nt(p, v_t)
        m_s[g] = m_new

    @pl.when(j == pl.num_programs(1) - 1)
    def _():
        j_new = lax.broadcasted_iota(jnp.int32, (1, NEW_PAD), 1)
        ok_new = (j_new <= t_row) & (j_new < DS)
        w_idx = lax.broadcasted_iota(jnp.int32, (1, WINDOW), 1)
        ok_win = w_idx >= t_row
        for g in range(N_KV):
            q = q_ref[0, g]
            s = jnp.where(ok_new, _dot_nt(q, new_sel_ref[0, 0, g]), NEG)
            m_new = jnp.maximum(m_s[g], jnp.max(s, axis=-1, keepdims=True))
            p = jnp.where(ok_new, jnp.exp(s - m_new), 0.0)
            alpha = jnp.exp(m_s[g] - m_new)
            l_fin = alpha * l_s[g] + jnp.sum(p, axis=-1, keepdims=True)
            o_sel = (alpha * acc_s[g]
                     + jnp.dot(_bf(p), new_sel_ref[0, 1, g], preferred_element_type=f32)) / l_fin
            s_a = jnp.where(ok_win, jnp.dot(q, _bf(win_ref[0, 0, g]), preferred_element_type=f32), NEG)
            s_b = jnp.where(ok_new, _dot_nt(q, new_win_ref[0, 0, g]), NEG)
            m_w = jnp.maximum(jnp.max(s_a, axis=-1, keepdims=True), jnp.max(s_b, axis=-1, keepdims=True))
            p_a = jnp.where(ok_win, jnp.exp(s_a - m_w), 0.0)
            p_b = jnp.where(ok_new, jnp.exp(s_b - m_w), 0.0)
            l_w = jnp.sum(p_a, axis=-1, keepdims=True) + jnp.sum(p_b, axis=-1, keepdims=True)
            o_win = (_dot_nt(p_a, win_ref[0, 1, g])
                     + jnp.dot(_bf(p_b), new_win_ref[0, 1, g], preferred_element_type=f32)) / l_w
            gt = g_ref[0, g]
            o_ref[0, g] = gt[:, 0:1] * oc_s[g] + gt[:, 1:2] * o_sel + gt[:, 2:3] * o_win


def nsa_sample_attention(q, kvc, sel_cache, win_cache, page_table, new_sel, new_win, gates, past):
    DB, G, R, D = q.shape
    DS = R // HPG
    n_pool = sel_cache.shape[0]
    n_pages = page_table.shape[1]
    n_chunk = kvc.shape[3]
    NC = (past + DS) // CMP_STRIDE - 1
    assert past % SEL_BLOCK == 0 and DS < CMP_STRIDE and DS <= NEW_PAD and past // SEL_BLOCK <= LANES
    assert n_pages % PAGES_PER_STEP == 0 and win_cache.shape[1] == WINDOW and NC < n_chunk + 1
    c_start = jnp.arange(n_chunk) * CMP_STRIDE
    s_start = jnp.arange(LANES) * SEL_BLOCK
    cover_t = _bf((c_start[None, :] < s_start[:, None] + SEL_BLOCK)
                  & (c_start[None, :] + CMP_LEN > s_start[:, None])
                  & (jnp.arange(n_chunk)[None, :] < NC) & (jnp.arange(LANES)[:, None] < past // SEL_BLOCK))
    sel_view = jnp.transpose(sel_cache, (0, 2, 3, 4, 1))
    win_view = jnp.transpose(win_cache, (0, 2, 3, 4, 1))
    page_spec = lambda i: pl.BlockSpec((1, 2, G, D, PAGE_SIZE),
                                       lambda b, j, pt: (pt[b, j * PAGES_PER_STEP + i], 0, 0, 0, 0))
    per_b = lambda shape: pl.BlockSpec((1,) + shape, lambda b, j, pt: (b,) + (0,) * len(shape))
    grid_spec = pltpu.PrefetchScalarGridSpec(
        num_scalar_prefetch=1,
        grid=(DB, n_pages // PAGES_PER_STEP),
        in_specs=[page_spec(i) for i in range(PAGES_PER_STEP)]
        + [per_b((G, R, D)), per_b((2, G, n_chunk, D)),
           pl.BlockSpec((LANES, n_chunk), lambda b, j, pt: (0, 0)),
           per_b((2, G, NEW_PAD, D)), per_b((2, G, D, WINDOW)), per_b((2, G, NEW_PAD, D)), per_b((G, R, 3))],
        out_specs=per_b((G, R, D)),
        scratch_shapes=[pltpu.VMEM((G, R, LANES + D), jnp.bfloat16), pltpu.VMEM((G, R, 1), jnp.float32),
                        pltpu.VMEM((G, R, 1), jnp.float32), pltpu.VMEM((G, R, D), jnp.float32),
                        pltpu.VMEM((G, R, D), jnp.float32)],
    )
    return pl.pallas_call(
        functools.partial(_nsa_sample_kernel, DS=DS, NC=NC, past=past),
        grid_spec=grid_spec,
        out_shape=jax.ShapeDtypeStruct((DB, G, R, D), jnp.float32),
        compiler_params=pltpu.CompilerParams(dimension_semantics=("arbitrary", "arbitrary"),
                                             vmem_limit_bytes=VMEM_LIMIT),
        name="nsa_sample",
    )(page_table, *([sel_view] * PAGES_PER_STEP), q, kvc, cover_t, new_sel, win_view, new_win, gates)


def _moe_mlp_kernel(be_ref, base_ref, nval_ref, order_ref, x_hbm, w1_ref, b1_ref, w2_ref, b2_ref, o_hbm,
                    w1s, w2s, xbuf, ybuf, in_sem, out_sem, *, BM, T):
    i = pl.program_id(0)
    n = pl.num_programs(0)
    s = i % 2

    def gather(block, sl):
        base = base_ref[block]

        def body(r, c):
            tok = order_ref[base + r] >> K_SHIFT
            pltpu.make_async_copy(x_hbm.at[tok], xbuf.at[sl, r], in_sem.at[sl]).start()
            return c
        lax.fori_loop(0, BM, body, 0, unroll=8)

    @pl.when(i == 0)
    def _():
        gather(0, 0)

    @pl.when(i + 1 < n)
    def _():
        gather(i + 1, 1 - s)

    e = be_ref[i]
    prev = be_ref[jnp.maximum(i - 1, 0)]

    @pl.when((i == 0) | (e != prev))
    def _():
        w1s[...] = w1_ref[0].astype(jnp.bfloat16)
        w2s[...] = w2_ref[0].astype(jnp.bfloat16)

    def rows_in(sl):
        return pltpu.make_async_copy(x_hbm.at[pl.ds(0, BM)], xbuf.at[sl], in_sem.at[sl])

    def rows_out(sl):
        return pltpu.make_async_copy(ybuf.at[sl], o_hbm.at[pl.ds(0, BM)], out_sem.at[sl])

    rows_in(s).wait()

    @pl.when(i >= 2)
    def _():
        rows_out(s).wait()

    x = jnp.concatenate([xbuf[s, :, c, :] for c in range(SUBLANES)], axis=1).astype(jnp.bfloat16)
    h = jnp.dot(x, w1s[...], preferred_element_type=jnp.float32) + b1_ref[0]
    glu = jnp.minimum(h[:, :D_FF], SWIGLU_LIMIT)
    lin = jnp.clip(h[:, D_FF:], -SWIGLU_LIMIT, SWIGLU_LIMIT)
    act = glu * jax.nn.sigmoid(SWIGLU_ALPHA * glu) * (lin + 1.0)
    y = jnp.dot(act.astype(jnp.bfloat16), w2s[...], preferred_element_type=jnp.float32) + b2_ref[0]
    for c in range(SUBLANES):
        ybuf[s, :, c, :] = y[:, c * LANES:(c + 1) * LANES]

    base = base_ref[i]
    nval = nval_ref[i]
    spare = TOP_K * T + i * BM - base - nval

    def scatter(r, c):
        a = order_ref[base + r]
        row = jnp.where(r < nval, (a & (TOP_K - 1)) * T + (a >> K_SHIFT), spare + r)
        pltpu.make_async_copy(ybuf.at[s, r], o_hbm.at[row], out_sem.at[s]).start()
        return c
    lax.fori_loop(0, BM, scatter, 0, unroll=8)

    @pl.when(i == n - 1)
    def _():
        rows_out(s).wait()

    @pl.when((i == n - 1) & (n >= 2))
    def _():
        rows_out(1 - s).wait()


def moe_mlp(x, order, block_e, block_base, block_nval, w1, b1, w2, b2, BM):
    T, D = x.shape
    assert D == SUBLANES * LANES
    n_blocks = block_e.shape[0]
    E = w1.shape[0]
    grid_spec = pltpu.PrefetchScalarGridSpec(
        num_scalar_prefetch=4,
        grid=(n_blocks,),
        in_specs=[pl.BlockSpec(memory_space=pl.ANY),
                  pl.BlockSpec((1, D, 2 * D_FF), lambda i, be, *_: (be[i], 0, 0)),
                  pl.BlockSpec((1, 1, 2 * D_FF), lambda i, be, *_: (be[i], 0, 0)),
                  pl.BlockSpec((1, D_FF, D), lambda i, be, *_: (be[i], 0, 0)),
                  pl.BlockSpec((1, 1, D), lambda i, be, *_: (be[i], 0, 0))],
        out_specs=pl.BlockSpec(memory_space=pl.ANY),
        scratch_shapes=[pltpu.VMEM((D, 2 * D_FF), jnp.bfloat16), pltpu.VMEM((D_FF, D), jnp.bfloat16),
                        pltpu.VMEM((2, BM, SUBLANES, LANES), jnp.float32),
                        pltpu.VMEM((2, BM, SUBLANES, LANES), jnp.float32),
                        pltpu.SemaphoreType.DMA((2,)), pltpu.SemaphoreType.DMA((2,))],
    )
    return pl.pallas_call(
        functools.partial(_moe_mlp_kernel, BM=BM, T=T),
        grid_spec=grid_spec,
        out_shape=jax.ShapeDtypeStruct((n_blocks * BM, SUBLANES, LANES), jnp.float32),
        compiler_params=pltpu.CompilerParams(dimension_semantics=("arbitrary",),
                                             vmem_limit_bytes=VMEM_LIMIT),
        name="moe_mlp",
    )(block_e, block_base, block_nval, order, x.reshape(T, SUBLANES, LANES), w1, b1.reshape(E, 1, -1), w2, b2.reshape(E, 1, -1))


def moe(x, router_w, router_b, mlp1_w, mlp1_b, mlp2_w, mlp2_b, BM=256):
    T, D = x.shape
    rw = jnp.pad(router_w, ((0, 0), (0, LANES - N_EXPERTS)))
    logits = pallas_matmul(x, rw)[:, :N_EXPERTS] + router_b
    top_v, top_e = lax.top_k(logits, TOP_K)
    gate = jax.nn.softmax(top_v, axis=-1)
    n_assign = T * TOP_K
    _, order = lax.sort((top_e.reshape(-1).astype(jnp.int32), jnp.arange(n_assign, dtype=jnp.int32)), num_keys=1)
    counts = jnp.sum(top_e.reshape(-1, 1) == jnp.arange(N_EXPERTS)[None, :], axis=0).astype(jnp.int32)
    padded = (counts + BM - 1) // BM * BM
    start = jnp.cumsum(counts) - counts
    pend = jnp.cumsum(padded)
    pstart = pend - padded
    n_blocks = -(-n_assign // BM) + N_EXPERTS
    row0 = jnp.arange(n_blocks, dtype=jnp.int32) * BM
    block_e = jnp.minimum(jnp.sum(pend[None, :] <= row0[:, None], axis=1), N_EXPERTS - 1).astype(jnp.int32)
    off = row0 - pstart[block_e]
    block_nval = jnp.clip(counts[block_e] - off, 0, BM).astype(jnp.int32)
    block_base = jnp.clip(start[block_e] + off, 0, n_assign).astype(jnp.int32)
    order = jnp.concatenate([order, jnp.zeros((BM,), jnp.int32)])
    yb = moe_mlp(x, order, block_e, block_base, block_nval, mlp1_w, mlp1_b, mlp2_w, mlp2_b, BM)
    y = gate[:, 0, None, None] * yb[:T]
    for k in range(1, TOP_K):
        y = y + gate[:, k, None, None] * yb[k * T:(k + 1) * T]
    return y.reshape(T, D)


def layer_norm(x, g, b):
    mu = x.mean(-1, keepdims=True)
    var = jnp.square(x - mu).mean(-1, keepdims=True)
    return (x - mu) * lax.rsqrt(var + LN_EPS) * g + b


def rope(x, pos):
    half = ROT_DIM // 2
    inv = ROPE_THETA ** (-jnp.arange(half, dtype=jnp.float32) * 2.0 / ROT_DIM)
    ang = pos.astype(jnp.float32)[:, None] * inv
    cos, sin = jnp.cos(ang)[:, None, :], jnp.sin(ang)[:, None, :]
    x1, x2 = x[..., :half], x[..., half:ROT_DIM]
    return jnp.concatenate([x1 * cos - x2 * sin, x2 * cos + x1 * sin, x[..., ROT_DIM:]], axis=-1)


G0 = R_COLS + A_WIDTH + 6 * KV_WIDTH


def project(x2, w_in):
    w_main = jnp.concatenate([w_in[:, :G0], w_in[:, G0 + N_GATE:]], axis=1)
    w_gate = jnp.pad(w_in[:, G0:G0 + N_GATE], ((0, 0), (0, LANES - N_GATE)))
    return pallas_matmul(x2, w_main), pallas_matmul(x2, w_gate)[:, :N_GATE]


def split_projection(p, pg, B, T, pos):
    pr = p[:, :R_COLS].reshape(B, T, R_COLS)
    pa = p[:, R_COLS:G0].reshape(B, T, G0 - R_COLS)
    q = rope(pa[..., :A_WIDTH].reshape(B, T, N_HEADS, HEAD_DIM), pos)
    kvs = [pa[..., A_WIDTH + i * KV_WIDTH:A_WIDTH + (i + 1) * KV_WIDTH].reshape(B, T, N_KV, HEAD_DIM)
           for i in range(6)]
    kvs = [rope(z, pos) if i % 2 == 0 else z for i, z in enumerate(kvs)]
    gates = jax.nn.sigmoid(pg).reshape(B, T, N_KV, HPG, 3)
    return pr, q, kvs, gates


def rwkv_prep_rows(pr, shift_prev, mu, w0, w_w2, a0, w_a2, g_w2, k_k, k_a, r_k):
    B, T, _ = pr.shape
    prev = jnp.concatenate([shift_prev[:, None, :], pr[:, :-1]], axis=1)
    xm = pr + (prev - pr) * mu
    o1, o2, o3 = R_WIDTH, 2 * R_WIDTH, 3 * R_WIDTH
    o4 = o3 + LORA_W
    o5 = o4 + LORA_A
    r, k, v = xm[..., :o1], xm[..., o1:o2], xm[..., o2:o3]
    xw, xa, xg = xm[..., o3:o4], xm[..., o4:o5], xm[..., o5:]
    w_log = -jax.nn.softplus(-(w0 + jnp.tanh(xw) @ w_w2)) - 0.5
    a = jax.nn.sigmoid(a0 + xa @ w_a2)
    g = jax.nn.sigmoid(xg) @ g_w2
    heads = lambda z: z.reshape(B, T, R_HEADS, R_HEAD)
    flat = lambda z: z.reshape(B, T, R_WIDTH)
    kk = heads(k * k_k)
    kk = flat(kk / jnp.maximum(jnp.linalg.norm(kk, axis=-1, keepdims=True), 1e-12))
    k_h = k * (1.0 + (a - 1.0) * k_a)
    bonus = flat(jnp.sum(heads(r * k_h * r_k.reshape(-1)), axis=-1, keepdims=True) * heads(v))
    return r, -jnp.exp(w_log), k_h, v, -kk, kk * a, bonus, g


def rwkv_mixer(parts, wkv0, gn_g, gn_b):
    B, T, _ = parts[0].shape
    C = 64 if T % 64 == 0 else 8
    Tp = -(-T // C) * C
    padded = [jnp.pad(z, ((0, 0), (0, Tp - T), (0, 0))) for z in parts]
    y, wkv = wkv_chunked(*padded, gn_g, gn_b, wkv0, C)
    return y[:, :T], wkv


def compress(kv, pe, w1, b1, w2, b2):
    B, L = kv.shape[:2]
    n_chunk = L // CMP_STRIDE
    ch = kv[:, :n_chunk * CMP_STRIDE].reshape(B, n_chunk, CMP_STRIDE, N_KV, HEAD_DIM)
    blk = jnp.concatenate([ch[:, :-1], ch[:, 1:]], axis=2) + pe[:, None, :]
    blk = jnp.transpose(blk, (0, 3, 1, 2, 4)).reshape(B, N_KV, n_chunk - 1, CMP_LEN * HEAD_DIM)
    return jax.nn.gelu(blk @ w1 + b1) @ w2 + b2


def nsa_prompt(q, kvs, gates, pe, w1, b1, w2, b2):
    kc_raw, vc_raw, ks, vs, kw, vw = kvs
    B, T = q.shape[:2]
    kc = compress(kc_raw, pe[0], w1[0], b1[0], w2[0], b2[0])
    vc = compress(vc_raw, pe[1], w1[1], b1[1], w2[1], b2[1])
    qg = _bf(jnp.transpose(q.reshape(B, T, N_KV, HPG, HEAD_DIM), (0, 2, 3, 1, 4)) * (HEAD_DIM ** -0.5))
    tk = lambda z: _bf(jnp.swapaxes(z, 1, 2))
    o = nsa_prompt_attention(qg, kc, vc, tk(ks), tk(vs), tk(kw), tk(vw),
                             jnp.transpose(gates, (0, 2, 3, 1, 4)))
    return jnp.transpose(o, (0, 3, 1, 2, 4)).reshape(B, T, A_WIDTH)


def nsa_sample(q, kvs, gates, cmp_cache, sel_cache, win_cache, page_table, pe, w1, b1, w2, b2):
    kc_new, vc_new, ks_new, vs_new, kw_new, vw_new = kvs
    DB, DS = q.shape[:2]
    past = page_table.shape[1] * PAGE_SIZE
    kvc = compress_sample(cmp_cache, page_table, pe, w1, b1, w2, b2)
    qg = _bf(jnp.transpose(q.reshape(DB, DS, N_KV, HPG, HEAD_DIM), (0, 2, 3, 1, 4)) * (HEAD_DIM ** -0.5))
    qg = qg.reshape(DB, N_KV, HPG * DS, HEAD_DIM)
    gt = jnp.transpose(gates, (0, 2, 3, 1, 4)).reshape(DB, N_KV, HPG * DS, 3)
    pack = lambda k, v: _bf(jnp.pad(jnp.transpose(jnp.stack([k, v], axis=1), (0, 1, 3, 2, 4)),
                                    ((0, 0), (0, 0), (0, 0), (0, NEW_PAD - DS), (0, 0))))
    o = nsa_sample_attention(qg, kvc, sel_cache, win_cache, page_table, pack(ks_new, vs_new),
                             pack(kw_new, vw_new), gt, past)
    o = jnp.transpose(o.reshape(DB, N_KV, HPG, DS, HEAD_DIM), (0, 3, 1, 2, 4)).reshape(DB, DS, -1)
    new_k = jnp.concatenate([win_cache[:, DS:, 0], kw_new], axis=1)
    new_v = jnp.concatenate([win_cache[:, DS:, 1], vw_new], axis=1)
    return o, jnp.stack([new_k, new_v], axis=2)


def merge_and_ffn(x, y_r, y_a, mg, w_pa, w_pb, w_o, ln1_g, ln1_b, router_w, router_b,
                  mlp1_w, mlp1_b, mlp2_w, mlp2_b, ln2_g, ln2_b):
    m = (jax.nn.sigmoid(mg[:, :D_MODEL]) * pallas_matmul(y_r, w_pa)
         + jax.nn.sigmoid(mg[:, D_MODEL:]) * pallas_matmul(y_a, w_pb))
    h = layer_norm(DN_ALPHA * x + pallas_matmul(m, w_o), ln1_g, ln1_b)
    f = moe(h, router_w, router_b, mlp1_w, mlp1_b, mlp2_w, mlp2_b)
    return layer_norm(DN_ALPHA * h + f, ln2_g, ln2_b)


def kernel(x_prompt, x_sample, cache_cmp_kv, cache_sel_kv, cache_win_kv, state_wkv, state_shift,
           page_table, w_in, mu_shift, w0, w_w2, a0, w_a2, g_w2, k_k, k_a, r_k, gn_g, gn_b,
           cmp_pe, cmp_w1, cmp_b1, cmp_w2, cmp_b2, w_pa, w_pb, w_o, ln1_g, ln1_b,
           router_w, router_b, mlp1_w, mlp1_b, mlp2_w, mlp2_b, ln2_g, ln2_b):
    B, T, D = x_prompt.shape
    DB, DS, _ = x_sample.shape
    n_p = B * T
    past = page_table.shape[1] * PAGE_SIZE
    pos_p = jnp.arange(T)
    pos_s = past + jnp.arange(DS)
    wb_p = min(WINDOW, T)
    h_all = jnp.concatenate([x_prompt.reshape(n_p, D), x_sample.reshape(DB * DS, D)])
    cmp_p, sel_p, win_p, wkv_p, shift_p = [], [], [], [], []
    cmp_s, sel_s, win_s, wkv_s, shift_s = [], [], [], [], []
    for l in range(DEPTH):
        rwkv_w = (mu_shift[l], w0[l], w_w2[l], a0[l], w_a2[l], g_w2[l], k_k[l], k_a[l], r_k[l],
                  gn_g[l], gn_b[l])
        cmp_w = (cmp_pe[l], cmp_w1[l], cmp_b1[l], cmp_w2[l], cmp_b2[l])
        out_w = (w_pa[l], w_pb[l], w_o[l], ln1_g[l], ln1_b[l], router_w[l], router_b[l],
                 mlp1_w[l], mlp1_b[l], mlp2_w[l], mlp2_b[l], ln2_g[l], ln2_b[l])
        p_all, pg_all = project(h_all, w_in[l])
        pr, q, kvs, gates = split_projection(p_all[:n_p], pg_all[:n_p], B, T, pos_p)
        parts = rwkv_prep(p_all, n_p, T, jnp.zeros((B, R_COLS), jnp.float32), *rwkv_w[:9])
        y_r, wkv = rwkv_mixer([z.reshape(B, T, R_WIDTH) for z in parts],
                              jnp.zeros((B, R_HEADS, R_HEAD, R_HEAD), jnp.float32), *rwkv_w[9:])
        shift = pr[:, -1]
        y_a = nsa_prompt(q, kvs, gates, *cmp_w)
        cmp_p.append(jnp.stack([kvs[0], kvs[1]], axis=2))
        sel_p.append(jnp.stack([kvs[2], kvs[3]], axis=2))
        win_p.append(jnp.stack([kvs[4][:, T - wb_p:], kvs[5][:, T - wb_p:]], axis=2))
        wkv_p.append(wkv)
        shift_p.append(shift)
        pr, q, kvs, gates = split_projection(p_all[n_p:], pg_all[n_p:], DB, DS, pos_s)
        y_r_s, wkv = rwkv_mixer(rwkv_prep_rows(pr, state_shift[l], *rwkv_w[:9]), state_wkv[l], *rwkv_w[9:])
        shift = pr[:, -1]
        y_a_s, new_win = nsa_sample(q, kvs, gates, cache_cmp_kv[l], cache_sel_kv[l], cache_win_kv[l],
                                    page_table, *cmp_w)
        cmp_s.append(jnp.stack([kvs[0], kvs[1]], axis=2))
        sel_s.append(jnp.stack([kvs[2], kvs[3]], axis=2))
        win_s.append(new_win)
        wkv_s.append(wkv)
        shift_s.append(shift)
        rows = lambda a, b: jnp.concatenate([a.reshape(n_p, -1), b.reshape(DB * DS, -1)])
        h_all = merge_and_ffn(h_all, rows(y_r, y_r_s), rows(y_a, y_a_s), p_all[:, G0:], *out_w)
    hp = h_all[:n_p].reshape(B, T, D)
    hs = h_all[n_p:].reshape(DB, DS, D)
    return (hp, hs, jnp.stack(cmp_p), jnp.stack(sel_p), jnp.stack(win_p), jnp.stack(wkv_p),
            jnp.stack(shift_p), jnp.stack(cmp_s), jnp.stack(sel_s), jnp.stack(win_s),
            jnp.stack(wkv_s), jnp.stack(shift_s))
```

```python
import functools

import jax
import jax.numpy as jnp
from jax import lax
from jax.experimental import pallas as pl
from jax.experimental.pallas import tpu as pltpu

D_MODEL = 1024
DEPTH = 1
PAGE_SIZE = 128

R_HEADS = 8
R_HEAD = 64
R_WIDTH = R_HEADS * R_HEAD
LORA_W = 64
LORA_A = 64
LORA_G = 128
R_COLS = 3 * R_WIDTH + LORA_W + LORA_A + LORA_G
GN_EPS = 64e-5

N_HEADS = 8
N_KV = 2
HPG = N_HEADS // N_KV
HEAD_DIM = 64
A_WIDTH = N_HEADS * HEAD_DIM
KV_WIDTH = N_KV * HEAD_DIM
N_GATE = 3 * N_HEADS
A_COLS = A_WIDTH + 6 * KV_WIDTH + N_GATE
ROT_DIM = HEAD_DIM // 4
ROPE_THETA = 500000.0
CMP_STRIDE = 16
CMP_LEN = 2 * CMP_STRIDE
CMP_HIDDEN = 256
SEL_BLOCK = 64
SEL_TOP = 16
WINDOW = 512

N_EXPERTS = 32
TOP_K = 4
K_SHIFT = 2
D_FF = 1024
SWIGLU_LIMIT = 7.0
SWIGLU_ALPHA = 1.702

DN_ALPHA = (2 * DEPTH) ** 0.25
LN_EPS = 1e-5
NEG = -1e30

LANES = 128
SUBLANES = 8
PAGES_PER_STEP = 8
NEW_PAD = 8
VMEM_LIMIT = 56 * 1024 * 1024


def _bf(x):
    return x.astype(jnp.bfloat16)


def _dot(a, b):
    return jnp.dot(_bf(a), _bf(b), preferred_element_type=jnp.float32)


def _dot_nt(a, b):
    return lax.dot_general(_bf(a), _bf(b), (((1,), (1,)), ((), ())),
                           preferred_element_type=jnp.float32)


def _mm_kernel(x_ref, w_ref, o_ref):
    o_ref[...] = _dot(x_ref[...], w_ref[...])


def pallas_matmul(x, w):
    x, w = _bf(x), _bf(w)
    M, K = x.shape
    N = w.shape[1]
    tm = next(t for t in (512, 384, 256, 128, M) if M % t == 0)
    resident = 2 * (2 * K * N + 4 * tm * N + 2 * tm * K) <= VMEM_LIMIT - (8 << 20)
    tn = N if resident else next(t for t in (512, 256, LANES) if N % t == 0)
    return pl.pallas_call(
        _mm_kernel,
        grid=(N // tn, M // tm),
        in_specs=[pl.BlockSpec((tm, K), lambda j, i: (i, 0)),
                  pl.BlockSpec((K, tn), lambda j, i: (0, j))],
        out_specs=pl.BlockSpec((tm, tn), lambda j, i: (i, j)),
        out_shape=jax.ShapeDtypeStruct((M, N), jnp.float32),
        compiler_params=pltpu.CompilerParams(vmem_limit_bytes=VMEM_LIMIT),
        name="mm",
    )(x, w)


def _bmm(a, b):
    return lax.dot_general(_bf(a), _bf(b), (((2,), (1,)), ((0,), (0,))), preferred_element_type=jnp.float32)


def _bmm_nt(a, b):
    return lax.dot_general(_bf(a), _bf(b), (((2,), (2,)), ((0,), (0,))), preferred_element_type=jnp.float32)


def _bmm_tn(a, b):
    return lax.dot_general(_bf(a), _bf(b), (((1,), (1,)), ((0,), (0,))), preferred_element_type=jnp.float32)


def _head_sum(x, ones_bd):
    hi = _bf(x)
    rem = x - hi.astype(jnp.float32)
    mid = _bf(rem)
    lo = _bf(rem - mid.astype(jnp.float32))
    dot = lambda t: jnp.dot(t, ones_bd, preferred_element_type=jnp.float32)
    return dot(hi) + dot(mid) + dot(lo)


def head_ones():
    h = jnp.arange(R_WIDTH) // R_HEAD
    return _bf(h[:, None] == h[None, :])


def _rwkv_prep_kernel(p_ref, pb_ref, sp_ref, mu_ref, w0_ref, ww2_ref, a0_ref, wa2_ref, gw2_ref,
                      kk_ref, ka_ref, rk_ref, bd_ref,
                      r_ref, lw_ref, k_ref, v_ref, a_ref, b_ref, bonus_ref, g_ref, *, tiles_per_seq):
    i = pl.program_id(0)
    pr = p_ref[...]
    first = jnp.where(i % tiles_per_seq == 0, sp_ref[0], pb_ref[SUBLANES - 1:SUBLANES, :])
    rolled = pltpu.roll(pr, shift=1, axis=0)
    prev = jnp.where(lax.broadcasted_iota(jnp.int32, (pr.shape[0], 1), 0) == 0, first, rolled)
    xm = pr + (prev - pr) * mu_ref[...]
    o1, o2, o3 = R_WIDTH, 2 * R_WIDTH, 3 * R_WIDTH
    o4 = o3 + LORA_W
    o5 = o4 + LORA_A
    r, k, v = xm[:, :o1], xm[:, o1:o2], xm[:, o2:o3]
    xw, xa, xg = xm[:, o3:o4], xm[:, o4:o5], xm[:, o5:]
    dot = lambda x, w_ref: jnp.dot(_bf(x), _bf(w_ref[...]), preferred_element_type=jnp.float32)
    w_log = -jax.nn.softplus(-(w0_ref[...] + dot(jnp.tanh(xw), ww2_ref))) - 0.5
    a = jax.nn.sigmoid(a0_ref[...] + dot(xa, wa2_ref))
    g_ref[...] = dot(jax.nn.sigmoid(xg), gw2_ref)
    ones_bd = bd_ref[...]
    kk = k * kk_ref[...]
    kk = kk / jnp.maximum(jnp.sqrt(_head_sum(kk * kk, ones_bd)), 1e-12)
    k_h = k * (1.0 + (a - 1.0) * ka_ref[...])
    r_ref[...] = r
    lw_ref[...] = -jnp.exp(w_log)
    k_ref[...] = k_h
    v_ref[...] = v
    a_ref[...] = -kk
    b_ref[...] = kk * a
    bonus_ref[...] = _head_sum(r * k_h * rk_ref[...], ones_bd) * v


def rwkv_prep(p_all, n_rows, seq_len, shift_prev, mu, w0, w_w2, a0, w_a2, g_w2, k_k, k_a, r_k, tm=256):
    assert seq_len % tm == 0 and n_rows % seq_len == 0
    n_seq = n_rows // seq_len
    tiles_per_seq = seq_len // tm
    row = lambda z: z.reshape(1, -1)
    const = lambda shape: pl.BlockSpec(shape, lambda i: (0,) * len(shape))
    out = pl.BlockSpec((tm, R_WIDTH), lambda i: (i, 0))
    outs = pl.pallas_call(
        functools.partial(_rwkv_prep_kernel, tiles_per_seq=tiles_per_seq),
        grid=(n_rows // tm,),
        in_specs=[pl.BlockSpec((tm, R_COLS), lambda i: (i, 0)),
                  pl.BlockSpec((SUBLANES, R_COLS), lambda i: (jnp.maximum(i * (tm // SUBLANES) - 1, 0), 0)),
                  pl.BlockSpec((1, 1, R_COLS), lambda i: (i // tiles_per_seq, 0, 0)),
                  const((1, R_COLS)), const((1, R_WIDTH)), const((LORA_W, R_WIDTH)), const((1, R_WIDTH)),
                  const((LORA_A, R_WIDTH)), const((LORA_G, R_WIDTH)), const((1, R_WIDTH)), const((1, R_WIDTH)),
                  const((1, R_WIDTH)), const((R_WIDTH, R_WIDTH))],
        out_specs=[out] * 8,
        out_shape=[jax.ShapeDtypeStruct((n_rows, R_WIDTH), jnp.float32)] * 8,
        compiler_params=pltpu.CompilerParams(dimension_semantics=("arbitrary",), vmem_limit_bytes=VMEM_LIMIT),
        name="rwkv_prep",
    )(p_all, p_all, shift_prev.reshape(n_seq, 1, R_COLS), row(mu), row(w0), w_w2, row(a0), w_a2, g_w2,
      row(k_k), row(k_a), row(r_k), head_ones())
    return outs


def _wkv_chunk_kernel(r_ref, lw_ref, k_ref, v_ref, a_ref, b_ref, bonus_ref, g_ref, gng_ref, gnb_ref, bd_ref,
                      s0_ref, y_ref, s_out_ref, s_scr, *, C, H):
    c = pl.program_id(1)
    D = R_HEAD

    @pl.when(c == 0)
    def _():
        s_scr[...] = s0_ref[0]

    row = lax.broadcasted_iota(jnp.int32, (H, C, C), 1)
    col = lax.broadcasted_iota(jnp.int32, (H, C, C), 2)
    incl = row >= col
    strict = row > col
    ltri = jnp.where(incl, 1.0, 0.0).astype(jnp.bfloat16)
    eye = jnp.where(row == col, 1.0, 0.0).astype(jnp.float32)

    heads = lambda ref: jnp.stack([ref[0, :, h * D:(h + 1) * D] for h in range(H)])
    lw = heads(lw_ref)
    r = heads(r_ref)
    k = heads(k_ref)
    v = heads(v_ref)
    a = heads(a_ref)
    b = heads(b_ref)
    hi = _bf(lw)
    rem = lw - hi.astype(jnp.float32)
    mid = _bf(rem)
    lo = _bf(rem - mid.astype(jnp.float32))
    lp = _bmm(ltri, hi) + _bmm(ltri, mid) + _bmm(ltri, lo)
    lp_end = lp[:, C - 1:C, :]
    p_end = jnp.exp(lp_end)
    p_inv = jnp.exp(-lp)
    at = a * jnp.exp(lp - lw)
    rt = r * jnp.exp(lp)
    bt = b * p_inv
    kt = k * p_inv
    p_hat = jnp.exp(lp_end - lp)
    bh = b * p_hat
    kh = k * p_hat

    n_ab = jnp.where(strict, _bmm_nt(at, bt), 0.0)
    a_ak = jnp.where(strict, _bmm_nt(at, kt), 0.0)
    a_rb = jnp.where(incl, _bmm_nt(rt, bt), 0.0)
    a_rk = jnp.where(incl, _bmm_nt(rt, kt), 0.0)

    t_inv = eye + n_ab
    n_pow = n_ab
    span = 2
    while span < C:
        n_pow = _bmm(n_pow, n_pow)
        t_inv = _bmm(t_inv, eye + n_pow)
        span *= 2

    s = s_scr[...]
    rhs = _bmm_nt(at, s) + _bmm(a_ak, v)
    u = _bmm(t_inv, rhs)
    y = _bmm_nt(rt, s) + _bmm(a_rb, u) + _bmm(a_rk, v)
    s_new = s * p_end + _bmm_tn(u, bh) + _bmm_tn(v, kh)
    s_scr[...] = s_new

    yt = jnp.concatenate([y[h] for h in range(H)], axis=1)
    ones_bd = bd_ref[...]
    dev = yt - _head_sum(yt, ones_bd) * (1.0 / D)
    var = _head_sum(dev * dev, ones_bd) * (1.0 / D)
    yn = dev * lax.rsqrt(var + GN_EPS) * gng_ref[...] + gnb_ref[...]
    y_ref[0] = (yn + bonus_ref[0]) * g_ref[0]

    @pl.when(c == pl.num_programs(1) - 1)
    def _():
        s_out_ref[0] = s_new


def wkv_chunked(r, lw, k, v, a, b, bonus, g, gn_g, gn_b, s0, C):
    B, T, W = r.shape
    H, D = R_HEADS, R_HEAD
    assert T % C == 0
    seq = pl.BlockSpec((1, C, W), lambda bi, ci: (bi, ci, 0))
    st = pl.BlockSpec((1, H, D, D), lambda bi, ci: (bi, 0, 0, 0))
    const = lambda shape: pl.BlockSpec(shape, lambda bi, ci: (0,) * len(shape))
    return pl.pallas_call(
        functools.partial(_wkv_chunk_kernel, C=C, H=H),
        grid=(B, T // C),
        in_specs=[seq] * 8 + [const((1, W)), const((1, W)), const((W, W)), st],
        out_specs=[seq, st],
        out_shape=[jax.ShapeDtypeStruct((B, T, W), jnp.float32),
                   jax.ShapeDtypeStruct((B, H, D, D), jnp.float32)],
        scratch_shapes=[pltpu.VMEM((H, D, D), jnp.float32)],
        compiler_params=pltpu.CompilerParams(dimension_semantics=("arbitrary", "arbitrary")),
        name="wkv_chunk",
    )(r, lw, k, v, a, b, bonus, g, gn_g.reshape(1, W), gn_b.reshape(1, W), head_ones(), s0)


def _nsa_prompt_kernel(q_ref, kc_ref, vc_ref, cover_ref, ks_ref, vs_ref, kw_ref, vw_ref, g_ref, o_ref,
                       *, TQ, TK, NC, NCP):
    f32 = jnp.float32
    bf16 = jnp.bfloat16
    qb = pl.program_id(2)
    R = HPG * TQ
    q = q_ref[0, 0].reshape(R, HEAD_DIM)
    t_pos = qb * TQ + lax.broadcasted_iota(jnp.int32, (TQ, 1), 0)

    n_idx = lax.broadcasted_iota(jnp.int32, (1, NCP), 1)
    c_ok = ((n_idx * CMP_STRIDE + (CMP_LEN - 1)) <= t_pos) & (n_idx < NC)
    s_c = _dot_nt(q, kc_ref[0, 0]).reshape(HPG, TQ, NCP)
    s_c = jnp.where(c_ok[None], s_c, NEG)
    m_c = jnp.max(s_c, axis=-1, keepdims=True)
    p_c = jnp.where(c_ok[None], jnp.exp(s_c - m_c), 0.0)
    l_c = jnp.sum(p_c, axis=-1, keepdims=True)
    p_c = p_c / jnp.where(l_c > 0.0, l_c, 1.0)
    p_cb = p_c.astype(bf16)
    o_c = jnp.dot(p_cb.reshape(R, NCP), vc_ref[0, 0], preferred_element_type=f32)

    cover_t = cover_ref[...]
    imp = _dot_nt(cover_t, p_cb[0])
    for h in range(1, HPG):
        imp = imp + _dot_nt(cover_t, p_cb[h])
    s_col = lax.broadcasted_iota(jnp.int32, (LANES, 1), 0)
    t_row = qb * TQ + lax.broadcasted_iota(jnp.int32, (1, TQ), 1)
    cur = t_row // SEL_BLOCK
    forced = (s_col == 0) | (s_col == cur) | (s_col == cur - 1)
    causal = (s_col * SEL_BLOCK) <= t_row
    score = jnp.where(forced, 1e6, imp)
    score = jnp.where(causal, score, NEG)
    s_col_f = s_col.astype(f32)
    sel_t = jnp.zeros((LANES, TQ), f32)
    for _ in range(SEL_TOP):
        top = jnp.max(score, axis=0, keepdims=True)
        first = jnp.min(jnp.where(score == top, s_col_f, float(LANES)), axis=0, keepdims=True)
        hit = s_col_f == first
        sel_t = jnp.where(hit & (top > 0.5 * NEG), 1.0, sel_t)
        score = jnp.where(hit, -3e38, score)

    sel_neg = ((sel_t.T - 1.0) * (-NEG)).astype(bf16)
    q_aug = jnp.concatenate([jnp.concatenate([sel_neg] * HPG, axis=0), q], axis=1)
    key_off = lax.broadcasted_iota(jnp.int32, (1, TK), 1)

    def sel_tile(j, carry, diagonal):
        m, l, acc = carry
        start = pl.multiple_of(j * TK, TK)
        k = ks_ref[0, 0, pl.ds(start, TK), :]
        v = vs_ref[0, 0, pl.ds(start, TK), :]
        s = _dot_nt(q_aug, k).reshape(HPG, TQ, TK)
        if diagonal:
            s = jnp.where(((key_off + j * TK) <= t_pos)[None], s, NEG)
        m_new = jnp.maximum(m, jnp.max(s, axis=-1, keepdims=True))
        p = jnp.exp(s - m_new)
        alpha = jnp.exp(m - m_new)
        l = alpha * l + jnp.sum(p, axis=-1, keepdims=True)
        pv = jnp.dot(p.astype(bf16).reshape(R, TK), v, preferred_element_type=f32)
        acc = alpha * acc + pv.reshape(HPG, TQ, HEAD_DIM)
        return m_new, l, acc

    m0 = jnp.full((HPG, TQ, 1), NEG, f32)
    l0 = jnp.zeros((HPG, TQ, 1), f32)
    a0 = jnp.zeros((HPG, TQ, HEAD_DIM), f32)
    n_full = (qb * TQ) // TK
    carry = lax.fori_loop(0, n_full, functools.partial(sel_tile, diagonal=False), (m0, l0, a0))
    _, l_s, acc_s = sel_tile(n_full, carry, True)
    o_s = acc_s / l_s

    n_w = WINDOW // TQ + 1
    lane_q = lax.broadcasted_iota(jnp.int32, (1, TQ), 1)
    s_w, ok_w, v_w = [], [], []
    for i in range(n_w):
        kb = qb - (n_w - 1) + i
        start = pl.multiple_of(jnp.maximum(kb, 0) * TQ, TQ)
        k = kw_ref[0, 0, pl.ds(start, TQ), :]
        v_w.append(vw_ref[0, 0, pl.ds(start, TQ), :])
        kpos = kb * TQ + lane_q
        ok = (kpos <= t_pos) & (kpos >= t_pos - WINDOW) & (kpos >= 0)
        ok_w.append(ok)
        s_w.append(jnp.where(ok[None], _dot_nt(q, k).reshape(HPG, TQ, TQ), NEG))
    m_w = s_w[0].max(axis=-1, keepdims=True)
    for i in range(1, n_w):
        m_w = jnp.maximum(m_w, s_w[i].max(axis=-1, keepdims=True))
    l_w = jnp.zeros((HPG, TQ, 1), f32)
    acc_w = jnp.zeros((R, HEAD_DIM), f32)
    for i in range(n_w):
        p = jnp.where(ok_w[i][None], jnp.exp(s_w[i] - m_w), 0.0)
        l_w = l_w + jnp.sum(p, axis=-1, keepdims=True)
        acc_w = acc_w + jnp.dot(p.astype(bf16).reshape(R, TQ), v_w[i], preferred_element_type=f32)
    o_w = acc_w.reshape(HPG, TQ, HEAD_DIM) / l_w

    g = g_ref[0, 0]
    o_ref[0, 0] = (g[:, :, 0:1] * o_c.reshape(HPG, TQ, HEAD_DIM) + g[:, :, 1:2] * o_s
                   + g[:, :, 2:3] * o_w)


def nsa_prompt_attention(q, kc, vc, ks, vs, kw, vw, gates, TQ=256, TK=1024):
    B, G, _, T, D = q.shape
    NC = kc.shape[2]
    NCP = -(-NC // LANES) * LANES
    NS = T // SEL_BLOCK
    assert NS <= LANES and T % TK == 0 and TK % TQ == 0 and WINDOW % TQ == 0
    kc = jnp.pad(kc, ((0, 0), (0, 0), (0, NCP - NC), (0, 0))).astype(jnp.bfloat16)
    vc = jnp.pad(vc, ((0, 0), (0, 0), (0, NCP - NC), (0, 0))).astype(jnp.bfloat16)
    c_start = jnp.arange(NCP) * CMP_STRIDE
    s_start = jnp.arange(LANES) * SEL_BLOCK
    cover = ((c_start[:, None] < s_start[None, :] + SEL_BLOCK)
             & (c_start[:, None] + CMP_LEN > s_start[None, :])
             & (jnp.arange(NCP)[:, None] < NC) & (jnp.arange(LANES)[None, :] < NS)).astype(jnp.bfloat16)
    onehot = (jnp.arange(T)[:, None] // SEL_BLOCK == jnp.arange(LANES)[None, :]).astype(jnp.bfloat16)
    ks = jnp.concatenate([jnp.broadcast_to(onehot, (B, G, T, LANES)), ks], axis=-1)
    full = lambda n, d=D: pl.BlockSpec((1, 1, n, d), lambda b, g, i: (b, g, 0, 0))
    qspec = pl.BlockSpec((1, 1, HPG, TQ, D), lambda b, g, i: (b, g, 0, i, 0))
    return pl.pallas_call(
        functools.partial(_nsa_prompt_kernel, TQ=TQ, TK=TK, NC=NC, NCP=NCP),
        grid=(B, G, T // TQ),
        in_specs=[qspec, full(NCP), full(NCP),
                  pl.BlockSpec((LANES, NCP), lambda b, g, i: (0, 0)),
                  full(T, LANES + D), full(T), full(T), full(T),
                  pl.BlockSpec((1, 1, HPG, TQ, 3), lambda b, g, i: (b, g, 0, i, 0))],
        out_specs=qspec,
        out_shape=jax.ShapeDtypeStruct((B, G, HPG, T, D), jnp.float32),
        compiler_params=pltpu.CompilerParams(
            dimension_semantics=("arbitrary", "arbitrary", "arbitrary"),
            vmem_limit_bytes=VMEM_LIMIT),
        name="nsa_prompt",
    )(q, kc, vc, cover.T, ks, vs, kw, vw, gates)


def _cmp_sample_kernel(pt_ref, *refs, n_chunk):
    pages = refs[:PAGES_PER_STEP]
    wcat_ref, c1_ref, w2_ref, b2_ref, o_ref, seq = refs[PAGES_PER_STEP:]
    j = pl.program_id(1)
    rows = PAGE_SIZE // CMP_STRIDE
    src = lax.broadcasted_iota(jnp.int32, (PAGE_SIZE, PAGE_SIZE), 0)
    src = (src % rows) * CMP_STRIDE + src // rows
    perm = _bf(jnp.where(src == lax.broadcasted_iota(jnp.int32, (PAGE_SIZE, PAGE_SIZE), 1), 1.0, 0.0))
    for i in range(PAGES_PER_STEP):
        dst = pl.multiple_of((j * PAGES_PER_STEP + i) * rows, rows)
        for kv in range(2):
            for g in range(N_KV):
                q = kv * N_KV + g
                by_pos = _dot_nt(perm, pages[i][0, kv, g])
                for p in range(CMP_STRIDE):
                    seq[q, pl.ds(dst, rows), p * HEAD_DIM:(p + 1) * HEAD_DIM] = by_pos[p * rows:(p + 1) * rows]

    @pl.when(j == pl.num_programs(1) - 1)
    def _():
        for kv in range(2):
            for g in range(N_KV):
                q = kv * N_KV + g
                acc = jnp.dot(_bf(seq[q]), wcat_ref[kv], preferred_element_type=jnp.float32)
                first = acc[:, :CMP_HIDDEN]
                second = acc[:, CMP_HIDDEN:]
                second = jnp.concatenate([second[1:], second[:1]], axis=0)
                hid = jax.nn.gelu(first + second + c1_ref[kv])
                out = jnp.dot(_bf(hid), w2_ref[kv], preferred_element_type=jnp.float32) + b2_ref[kv]
                o_ref[0, kv, g] = _bf(out)


def compress_sample(cache, page_table, pe, w1, b1, w2, b2):
    n_pool = cache.shape[0]
    DB, n_pages = page_table.shape
    assert n_pages % PAGES_PER_STEP == 0
    rows = PAGE_SIZE // CMP_STRIDE
    n_chunk = n_pages * rows
    view = jnp.transpose(cache, (0, 2, 3, 4, 1))
    w1r = w1.reshape(2, CMP_LEN, HEAD_DIM, CMP_HIDDEN)
    wcat = _bf(jnp.concatenate([w1r[:, :CMP_STRIDE], w1r[:, CMP_STRIDE:]], axis=-1))
    wcat = wcat.reshape(2, CMP_STRIDE * HEAD_DIM, 2 * CMP_HIDDEN)
    c1 = (jnp.einsum('kn,knh->kh', pe.reshape(2, CMP_LEN * HEAD_DIM), w1) + b1).reshape(2, 1, CMP_HIDDEN)
    page_spec = lambda i: pl.BlockSpec((1, 2, N_KV, HEAD_DIM, PAGE_SIZE),
                                       lambda b, j, pt: (pt[b, j * PAGES_PER_STEP + i], 0, 0, 0, 0))
    const = lambda shape: pl.BlockSpec(shape, lambda b, j, pt: (0,) * len(shape))
    grid_spec = pltpu.PrefetchScalarGridSpec(
        num_scalar_prefetch=1,
        grid=(DB, n_pages // PAGES_PER_STEP),
        in_specs=[page_spec(i) for i in range(PAGES_PER_STEP)]
        + [const((2, CMP_STRIDE * HEAD_DIM, 2 * CMP_HIDDEN)), const((2, 1, CMP_HIDDEN)),
           const((2, CMP_HIDDEN, HEAD_DIM)), const((2, 1, HEAD_DIM))],
        out_specs=pl.BlockSpec((1, 2, N_KV, n_chunk, HEAD_DIM), lambda b, j, pt: (b, 0, 0, 0, 0)),
        scratch_shapes=[pltpu.VMEM((2 * N_KV, n_chunk, CMP_STRIDE * HEAD_DIM), jnp.float32)],
    )
    return pl.pallas_call(
        functools.partial(_cmp_sample_kernel, n_chunk=n_chunk),
        grid_spec=grid_spec,
        out_shape=jax.ShapeDtypeStruct((DB, 2, N_KV, n_chunk, HEAD_DIM), jnp.bfloat16),
        compiler_params=pltpu.CompilerParams(dimension_semantics=("arbitrary", "arbitrary"),
                                             vmem_limit_bytes=VMEM_LIMIT),
        name="cmp_sample",
    )(page_table, *([view] * PAGES_PER_STEP), wcat, c1, _bf(w2), b2.reshape(2, 1, HEAD_DIM))


def _nsa_sample_kernel(pt_ref, *refs, DS, NC, past):
    pages = refs[:PAGES_PER_STEP]
    (q_ref, kvc_ref, cover_ref, new_sel_ref, win_ref, new_win_ref, g_ref, o_ref,
     qaug, m_s, l_s, acc_s, oc_s) = refs[PAGES_PER_STEP:]
    f32 = jnp.float32
    j = pl.program_id(1)
    R = HPG * DS
    t_row = lax.broadcasted_iota(jnp.int32, (R, 1), 0) % DS
    n_chunk = kvc_ref.shape[3]

    @pl.when(j == 0)
    def _():
        n_idx = lax.broadcasted_iota(jnp.int32, (1, n_chunk), 1)
        s_col = lax.broadcasted_iota(jnp.int32, (LANES, 1), 0)
        s_col_f = s_col.astype(f32)
        last_blk = past // SEL_BLOCK - 1
        for g in range(N_KV):
            q = q_ref[0, g]
            s_c = jnp.where(n_idx < NC, _dot_nt(q, kvc_ref[0, 0, g]), NEG)
            p_c = jnp.exp(s_c - jnp.max(s_c, axis=-1, keepdims=True))
            p_c = p_c / jnp.sum(p_c, axis=-1, keepdims=True)
            p_cb = _bf(p_c)
            oc_s[g] = jnp.dot(p_cb, kvc_ref[0, 1, g], preferred_element_type=f32)
            imp_rows = _dot_nt(cover_ref[...], p_cb)
            imp = imp_rows[:, 0:DS]
            for h in range(1, HPG):
                imp = imp + imp_rows[:, h * DS:(h + 1) * DS]
            forced = (s_col == 0) | (s_col == last_blk)
            score = jnp.where(forced, 1e6, imp)
            score = jnp.where(s_col <= last_blk, score, NEG)
            sel_t = jnp.zeros((LANES, DS), f32)
            for _ in range(SEL_TOP - 1):
                top = jnp.max(score, axis=0, keepdims=True)
                first = jnp.min(jnp.where(score == top, s_col_f, float(LANES)), axis=0, keepdims=True)
                hit = s_col_f == first
                sel_t = jnp.where(hit & (top > 0.5 * NEG), 1.0, sel_t)
                score = jnp.where(hit, -3e38, score)
            sel_neg = _bf((sel_t.T - 1.0) * (-NEG))
            qaug[g] = jnp.concatenate([jnp.concatenate([sel_neg] * HPG, axis=0), q], axis=1)
        m_s[...] = jnp.full(m_s.shape, NEG, f32)
        l_s[...] = jnp.zeros(l_s.shape, f32)
        acc_s[...] = jnp.zeros(acc_s.shape, f32)

    TK = PAGES_PER_STEP * PAGE_SIZE
    blk = (lax.broadcasted_iota(jnp.int32, (LANES, TK), 1) // SEL_BLOCK
           + j * (TK // SEL_BLOCK))
    onehot = _bf(jnp.where(blk == lax.broadcasted_iota(jnp.int32, (LANES, TK), 0), 1.0, 0.0))
    for g in range(N_KV):
        k_t = _bf(jnp.concatenate([pages[i][0, 0, g] for i in range(PAGES_PER_STEP)], axis=1))
        v_t = _bf(jnp.concatenate([pages[i][0, 1, g] for i in range(PAGES_PER_STEP)], axis=1))
        s = jnp.dot(qaug[g], jnp.concatenate([onehot, k_t], axis=0), preferred_element_type=f32)
        m_new = jnp.maximum(m_s[g], jnp.max(s, axis=-1, keepdims=True))
        p = jnp.exp(s - m_new)
        alpha = jnp.exp(m_s[g] - m_new)
        l_s[g] = alpha * l_s[g] + jnp.sum(p, axis=-1, keepdims=True)
        acc_s[g] = alpha * acc_s[g] + _dot_nt(p, v_t)
        m_s[g] = m_new

    @pl.when(j == pl.num_programs(1) - 1)
    def _():
        j_new = lax.broadcasted_iota(jnp.int32, (1, NEW_PAD), 1)
        ok_new = (j_new <= t_row) & (j_new < DS)
        w_idx = lax.broadcasted_iota(jnp.int32, (1, WINDOW), 1)
        ok_win = w_idx >= t_row
        for g in range(N_KV):
            q = q_ref[0, g]
            s = jnp.where(ok_new, _dot_nt(q, new_sel_ref[0, 0, g]), NEG)
            m_new = jnp.maximum(m_s[g], jnp.max(s, axis=-1, keepdims=True))
            p = jnp.where(ok_new, jnp.exp(s - m_new), 0.0)
            alpha = jnp.exp(m_s[g] - m_new)
            l_fin = alpha * l_s[g] + jnp.sum(p, axis=-1, keepdims=True)
            o_sel = (alpha * acc_s[g]
                     + jnp.dot(_bf(p), new_sel_ref[0, 1, g], preferred_element_type=f32)) / l_fin
            s_a = jnp.where(ok_win, jnp.dot(q, _bf(win_ref[0, 0, g]), preferred_element_type=f32), NEG)
            s_b = jnp.where(ok_new, _dot_nt(q, new_win_ref[0, 0, g]), NEG)
            m_w = jnp.maximum(jnp.max(s_a, axis=-1, keepdims=True), jnp.max(s_b, axis=-1, keepdims=True))
            p_a = jnp.where(ok_win, jnp.exp(s_a - m_w), 0.0)
            p_b = jnp.where(ok_new, jnp.exp(s_b - m_w), 0.0)
            l_w = jnp.sum(p_a, axis=-1, keepdims=True) + jnp.sum(p_b, axis=-1, keepdims=True)
            o_win = (_dot_nt(p_a, win_ref[0, 1, g])
                     + jnp.dot(_bf(p_b), new_win_ref[0, 1, g], preferred_element_type=f32)) / l_w
            gt = g_ref[0, g]
            o_ref[0, g] = gt[:, 0:1] * oc_s[g] + gt[:, 1:2] * o_sel + gt[:, 2:3] * o_win


def nsa_sample_attention(q, kvc, sel_cache, win_cache, page_table, new_sel, new_win, gates, past):
    DB, G, R, D = q.shape
    DS = R // HPG
    n_pool = sel_cache.shape[0]
    n_pages = page_table.shape[1]
    n_chunk = kvc.shape[3]
    NC = (past + DS) // CMP_STRIDE - 1
    assert past % SEL_BLOCK == 0 and DS < CMP_STRIDE and DS <= NEW_PAD and past // SEL_BLOCK <= LANES
    assert n_pages % PAGES_PER_STEP == 0 and win_cache.shape[1] == WINDOW and NC < n_chunk + 1
    c_start = jnp.arange(n_chunk) * CMP_STRIDE
    s_start = jnp.arange(LANES) * SEL_BLOCK
    cover_t = _bf((c_start[None, :] < s_start[:, None] + SEL_BLOCK)
                  & (c_start[None, :] + CMP_LEN > s_start[:, None])
                  & (jnp.arange(n_chunk)[None, :] < NC) & (jnp.arange(LANES)[:, None] < past // SEL_BLOCK))
    sel_view = jnp.transpose(sel_cache, (0, 2, 3, 4, 1))
    win_view = jnp.transpose(win_cache, (0, 2, 3, 4, 1))
    page_spec = lambda i: pl.BlockSpec((1, 2, G, D, PAGE_SIZE),
                                       lambda b, j, pt: (pt[b, j * PAGES_PER_STEP + i], 0, 0, 0, 0))
    per_b = lambda shape: pl.BlockSpec((1,) + shape, lambda b, j, pt: (b,) + (0,) * len(shape))
    grid_spec = pltpu.PrefetchScalarGridSpec(
        num_scalar_prefetch=1,
        grid=(DB, n_pages // PAGES_PER_STEP),
        in_specs=[page_spec(i) for i in range(PAGES_PER_STEP)]
        + [per_b((G, R, D)), per_b((2, G, n_chunk, D)),
           pl.BlockSpec((LANES, n_chunk), lambda b, j, pt: (0, 0)),
           per_b((2, G, NEW_PAD, D)), per_b((2, G, D, WINDOW)), per_b((2, G, NEW_PAD, D)), per_b((G, R, 3))],
        out_specs=per_b((G, R, D)),
        scratch_shapes=[pltpu.VMEM((G, R, LANES + D), jnp.bfloat16), pltpu.VMEM((G, R, 1), jnp.float32),
                        pltpu.VMEM((G, R, 1), jnp.float32), pltpu.VMEM((G, R, D), jnp.float32),
                        pltpu.VMEM((G, R, D), jnp.float32)],
    )
    return pl.pallas_call(
        functools.partial(_nsa_sample_kernel, DS=DS, NC=NC, past=past),
        grid_spec=grid_spec,
        out_shape=jax.ShapeDtypeStruct((DB, G, R, D), jnp.float32),
        compiler_params=pltpu.CompilerParams(dimension_semantics=("arbitrary", "arbitrary"),
                                             vmem_limit_bytes=VMEM_LIMIT),
        name="nsa_sample",
    )(page_table, *([sel_view] * PAGES_PER_STEP), q, kvc, cover_t, new_sel, win_view, new_win, gates)


def _merge_kernel(x_ref, yr_ref, ya_ref, gr_ref, ga_ref, wpa_ref, wpb_ref, wo_ref, g1_ref, b1_ref, rw_ref,
                  h_ref, lg_ref):
    m = (jax.nn.sigmoid(gr_ref[...]) * _dot(yr_ref[...], wpa_ref[...])
         + jax.nn.sigmoid(ga_ref[...]) * _dot(ya_ref[...], wpb_ref[...]))
    z = DN_ALPHA * x_ref[...] + _dot(m, wo_ref[...])
    mu = jnp.mean(z, axis=-1, keepdims=True)
    dev = z - mu
    var = jnp.mean(dev * dev, axis=-1, keepdims=True)
    h = dev * lax.rsqrt(var + LN_EPS) * g1_ref[...] + b1_ref[...]
    h_ref[...] = h
    lg_ref[...] = _dot(h, rw_ref[...])


def merge_rows(x, y_r, y_a, p_all, g0, w_pa, w_pb, w_o, ln1_g, ln1_b, router_w):
    N, D = x.shape
    W = y_r.shape[1]
    assert g0 % D == 0 and D == D_MODEL
    tm = next(t for t in (384, 256, 128, 64, 8) if N % t == 0)
    rw = jnp.pad(router_w, ((0, 0), (0, LANES - router_w.shape[1])))
    rows = lambda w, c=0: pl.BlockSpec((tm, w), lambda i, c=c: (i, c))
    const = lambda shape: pl.BlockSpec(shape, lambda i: (0,) * len(shape))
    return pl.pallas_call(
        _merge_kernel,
        grid=(N // tm,),
        in_specs=[rows(D), rows(W), rows(W), rows(D, g0 // D), rows(D, g0 // D + 1),
                  const((W, D)), const((W, D)), const((D, D)), const((1, D)), const((1, D)), const((D, LANES))],
        out_specs=[rows(D), rows(LANES)],
        out_shape=[jax.ShapeDtypeStruct((N, D), jnp.float32), jax.ShapeDtypeStruct((N, LANES), jnp.float32)],
        compiler_params=pltpu.CompilerParams(dimension_semantics=("arbitrary",), vmem_limit_bytes=VMEM_LIMIT),
        name="merge_rows",
    )(x, y_r, y_a, p_all, p_all, _bf(w_pa), _bf(w_pb), _bf(w_o), ln1_g.reshape(1, D), ln1_b.reshape(1, D), _bf(rw))


def _moe_mlp_kernel(be_ref, base_ref, nval_ref, order_ref, x_hbm, w1_ref, b1_ref, w2_ref, b2_ref, o_hbm,
                    w1s, w2s, xbuf, ybuf, in_sem, out_sem, *, BM, T):
    i = pl.program_id(0)
    n = pl.num_programs(0)
    s = i % 2

    def gather(block, sl):
        base = base_ref[block]

        def body(r, c):
            tok = order_ref[base + r] >> K_SHIFT
            pltpu.make_async_copy(x_hbm.at[tok], xbuf.at[sl, r], in_sem.at[sl]).start()
            return c
        lax.fori_loop(0, BM, body, 0, unroll=8)

    @pl.when(i == 0)
    def _():
        gather(0, 0)

    @pl.when(i + 1 < n)
    def _():
        gather(i + 1, 1 - s)

    e = be_ref[i]
    prev = be_ref[jnp.maximum(i - 1, 0)]

    @pl.when((i == 0) | (e != prev))
    def _():
        w1s[...] = w1_ref[0].astype(jnp.bfloat16)
        w2s[...] = w2_ref[0].astype(jnp.bfloat16)

    def rows_in(sl):
        return pltpu.make_async_copy(x_hbm.at[pl.ds(0, BM)], xbuf.at[sl], in_sem.at[sl])

    def rows_out(sl):
        return pltpu.make_async_copy(ybuf.at[sl], o_hbm.at[pl.ds(0, BM)], out_sem.at[sl])

    rows_in(s).wait()

    @pl.when(i >= 2)
    def _():
        rows_out(s).wait()

    x = jnp.concatenate([xbuf[s, :, c, :] for c in range(SUBLANES)], axis=1).astype(jnp.bfloat16)
    h = jnp.dot(x, w1s[...], preferred_element_type=jnp.float32) + b1_ref[0]
    glu = jnp.minimum(h[:, :D_FF], SWIGLU_LIMIT)
    lin = jnp.clip(h[:, D_FF:], -SWIGLU_LIMIT, SWIGLU_LIMIT)
    act = glu * jax.nn.sigmoid(SWIGLU_ALPHA * glu) * (lin + 1.0)
    y = jnp.dot(act.astype(jnp.bfloat16), w2s[...], preferred_element_type=jnp.float32) + b2_ref[0]
    for c in range(SUBLANES):
        ybuf[s, :, c, :] = y[:, c * LANES:(c + 1) * LANES]

    base = base_ref[i]
    nval = nval_ref[i]
    spare = TOP_K * T + i * BM - base - nval

    def scatter(r, c):
        a = order_ref[base + r]
        row = jnp.where(r < nval, (a & (TOP_K - 1)) * T + (a >> K_SHIFT), spare + r)
        pltpu.make_async_copy(ybuf.at[s, r], o_hbm.at[row], out_sem.at[s]).start()
        return c
    lax.fori_loop(0, BM, scatter, 0, unroll=8)

    @pl.when(i == n - 1)
    def _():
        rows_out(s).wait()

    @pl.when((i == n - 1) & (n >= 2))
    def _():
        rows_out(1 - s).wait()


def moe_mlp(x, order, block_e, block_base, block_nval, w1, b1, w2, b2, BM):
    T, D = x.shape
    assert D == SUBLANES * LANES
    n_blocks = block_e.shape[0]
    E = w1.shape[0]
    grid_spec = pltpu.PrefetchScalarGridSpec(
        num_scalar_prefetch=4,
        grid=(n_blocks,),
        in_specs=[pl.BlockSpec(memory_space=pl.ANY),
                  pl.BlockSpec((1, D, 2 * D_FF), lambda i, be, *_: (be[i], 0, 0)),
                  pl.BlockSpec((1, 1, 2 * D_FF), lambda i, be, *_: (be[i], 0, 0)),
                  pl.BlockSpec((1, D_FF, D), lambda i, be, *_: (be[i], 0, 0)),
                  pl.BlockSpec((1, 1, D), lambda i, be, *_: (be[i], 0, 0))],
        out_specs=pl.BlockSpec(memory_space=pl.ANY),
        scratch_shapes=[pltpu.VMEM((D, 2 * D_FF), jnp.bfloat16), pltpu.VMEM((D_FF, D), jnp.bfloat16),
                        pltpu.VMEM((2, BM, SUBLANES, LANES), jnp.float32),
                        pltpu.VMEM((2, BM, SUBLANES, LANES), jnp.float32),
                        pltpu.SemaphoreType.DMA((2,)), pltpu.SemaphoreType.DMA((2,))],
    )
    return pl.pallas_call(
        functools.partial(_moe_mlp_kernel, BM=BM, T=T),
        grid_spec=grid_spec,
        out_shape=jax.ShapeDtypeStruct((n_blocks * BM, SUBLANES, LANES), jnp.float32),
        compiler_params=pltpu.CompilerParams(dimension_semantics=("arbitrary",),
                                             vmem_limit_bytes=VMEM_LIMIT),
        name="moe_mlp",
    )(block_e, block_base, block_nval, order, x.reshape(T, SUBLANES, LANES), w1, b1.reshape(E, 1, -1), w2, b2.reshape(E, 1, -1))


def _combine_kernel(h_ref, gate_ref, *refs):
    slabs = refs[:TOP_K]
    g2_ref, b2_ref, o_ref = refs[TOP_K:]
    gate = gate_ref[...]
    f = None
    for k in range(TOP_K):
        rows = jnp.concatenate([slabs[k][:, c, :] for c in range(SUBLANES)], axis=1)
        term = gate[:, k:k + 1] * rows
        f = term if f is None else f + term
    z = DN_ALPHA * h_ref[...] + f
    mu = jnp.mean(z, axis=-1, keepdims=True)
    dev = z - mu
    var = jnp.mean(dev * dev, axis=-1, keepdims=True)
    o_ref[...] = dev * lax.rsqrt(var + LN_EPS) * g2_ref[...] + b2_ref[...]


def combine_rows(h, gate, yb, ln2_g, ln2_b):
    T, D = h.shape
    tm = next(t for t in (384, 256, 128, 64, 8) if T % t == 0)
    nb = T // tm
    slab = lambda k: pl.BlockSpec((tm, SUBLANES, LANES), lambda i, k=k: (k * nb + i, 0, 0))
    rows = lambda w: pl.BlockSpec((tm, w), lambda i: (i, 0))
    const = lambda shape: pl.BlockSpec(shape, lambda i: (0,) * len(shape))
    return pl.pallas_call(
        _combine_kernel,
        grid=(nb,),
        in_specs=[rows(D), rows(TOP_K)] + [slab(k) for k in range(TOP_K)] + [const((1, D)), const((1, D))],
        out_specs=rows(D),
        out_shape=jax.ShapeDtypeStruct((T, D), jnp.float32),
        compiler_params=pltpu.CompilerParams(dimension_semantics=("arbitrary",), vmem_limit_bytes=VMEM_LIMIT),
        name="combine_rows",
    )(h, gate, *([yb] * TOP_K), ln2_g.reshape(1, D), ln2_b.reshape(1, D))


def moe(x, logits, mlp1_w, mlp1_b, mlp2_w, mlp2_b, BM=256):
    T, D = x.shape
    top_v, top_e = lax.top_k(logits, TOP_K)
    gate = jax.nn.softmax(top_v, axis=-1)
    n_assign = T * TOP_K
    _, order = lax.sort((top_e.reshape(-1).astype(jnp.int32), jnp.arange(n_assign, dtype=jnp.int32)), num_keys=1)
    counts = jnp.sum(top_e.reshape(-1, 1) == jnp.arange(N_EXPERTS)[None, :], axis=0).astype(jnp.int32)
    padded = (counts + BM - 1) // BM * BM
    start = jnp.cumsum(counts) - counts
    pend = jnp.cumsum(padded)
    pstart = pend - padded
    n_blocks = -(-n_assign // BM) + N_EXPERTS
    row0 = jnp.arange(n_blocks, dtype=jnp.int32) * BM
    block_e = jnp.minimum(jnp.sum(pend[None, :] <= row0[:, None], axis=1), N_EXPERTS - 1).astype(jnp.int32)
    off = row0 - pstart[block_e]
    block_nval = jnp.clip(counts[block_e] - off, 0, BM).astype(jnp.int32)
    block_base = jnp.clip(start[block_e] + off, 0, n_assign).astype(jnp.int32)
    order = jnp.concatenate([order, jnp.zeros((BM,), jnp.int32)])
    yb = moe_mlp(x, order, block_e, block_base, block_nval, mlp1_w, mlp1_b, mlp2_w, mlp2_b, BM)
    return gate, yb


def rope(x, pos):
    half = ROT_DIM // 2
    inv = ROPE_THETA ** (-jnp.arange(half, dtype=jnp.float32) * 2.0 / ROT_DIM)
    ang = pos.astype(jnp.float32)[:, None] * inv
    cos, sin = jnp.cos(ang)[:, None, :], jnp.sin(ang)[:, None, :]
    x1, x2 = x[..., :half], x[..., half:ROT_DIM]
    return jnp.concatenate([x1 * cos - x2 * sin, x2 * cos + x1 * sin, x[..., ROT_DIM:]], axis=-1)


G0 = R_COLS + A_WIDTH + 6 * KV_WIDTH


def project(x2, w_in):
    w_main = jnp.concatenate([w_in[:, :G0], w_in[:, G0 + N_GATE:]], axis=1)
    w_gate = jnp.pad(w_in[:, G0:G0 + N_GATE], ((0, 0), (0, LANES - N_GATE)))
    return pallas_matmul(x2, w_main), pallas_matmul(x2, w_gate)[:, :N_GATE]


def split_projection(p, pg, B, T, pos):
    pr = p[:, :R_COLS].reshape(B, T, R_COLS)
    pa = p[:, R_COLS:G0].reshape(B, T, G0 - R_COLS)
    q = rope(pa[..., :A_WIDTH].reshape(B, T, N_HEADS, HEAD_DIM), pos)
    kvs = [pa[..., A_WIDTH + i * KV_WIDTH:A_WIDTH + (i + 1) * KV_WIDTH].reshape(B, T, N_KV, HEAD_DIM)
           for i in range(6)]
    kvs = [rope(z, pos) if i % 2 == 0 else z for i, z in enumerate(kvs)]
    gates = jax.nn.sigmoid(pg).reshape(B, T, N_KV, HPG, 3)
    return pr, q, kvs, gates


def rwkv_prep_rows(pr, shift_prev, mu, w0, w_w2, a0, w_a2, g_w2, k_k, k_a, r_k):
    B, T, _ = pr.shape
    prev = jnp.concatenate([shift_prev[:, None, :], pr[:, :-1]], axis=1)
    xm = pr + (prev - pr) * mu
    o1, o2, o3 = R_WIDTH, 2 * R_WIDTH, 3 * R_WIDTH
    o4 = o3 + LORA_W
    o5 = o4 + LORA_A
    r, k, v = xm[..., :o1], xm[..., o1:o2], xm[..., o2:o3]
    xw, xa, xg = xm[..., o3:o4], xm[..., o4:o5], xm[..., o5:]
    w_log = -jax.nn.softplus(-(w0 + jnp.tanh(xw) @ w_w2)) - 0.5
    a = jax.nn.sigmoid(a0 + xa @ w_a2)
    g = jax.nn.sigmoid(xg) @ g_w2
    heads = lambda z: z.reshape(B, T, R_HEADS, R_HEAD)
    flat = lambda z: z.reshape(B, T, R_WIDTH)
    kk = heads(k * k_k)
    kk = flat(kk / jnp.maximum(jnp.linalg.norm(kk, axis=-1, keepdims=True), 1e-12))
    k_h = k * (1.0 + (a - 1.0) * k_a)
    bonus = flat(jnp.sum(heads(r * k_h * r_k.reshape(-1)), axis=-1, keepdims=True) * heads(v))
    return r, -jnp.exp(w_log), k_h, v, -kk, kk * a, bonus, g


def rwkv_mixer(parts, wkv0, gn_g, gn_b):
    B, T, _ = parts[0].shape
    C = 64 if T % 64 == 0 else 8
    Tp = -(-T // C) * C
    padded = [jnp.pad(z, ((0, 0), (0, Tp - T), (0, 0))) for z in parts]
    y, wkv = wkv_chunked(*padded, gn_g, gn_b, wkv0, C)
    return y[:, :T], wkv


def compress(kv, pe, w1, b1, w2, b2):
    B, L = kv.shape[:2]
    n_chunk = L // CMP_STRIDE
    ch = kv[:, :n_chunk * CMP_STRIDE].reshape(B, n_chunk, CMP_STRIDE, N_KV, HEAD_DIM)
    blk = jnp.concatenate([ch[:, :-1], ch[:, 1:]], axis=2) + pe[:, None, :]
    blk = jnp.transpose(blk, (0, 3, 1, 2, 4)).reshape(B, N_KV, n_chunk - 1, CMP_LEN * HEAD_DIM)
    return jax.nn.gelu(blk @ w1 + b1) @ w2 + b2


def nsa_prompt(q, kvs, gates, pe, w1, b1, w2, b2):
    kc_raw, vc_raw, ks, vs, kw, vw = kvs
    B, T = q.shape[:2]
    kc = compress(kc_raw, pe[0], w1[0], b1[0], w2[0], b2[0])
    vc = compress(vc_raw, pe[1], w1[1], b1[1], w2[1], b2[1])
    qg = _bf(jnp.transpose(q.reshape(B, T, N_KV, HPG, HEAD_DIM), (0, 2, 3, 1, 4)) * (HEAD_DIM ** -0.5))
    tk = lambda z: _bf(jnp.swapaxes(z, 1, 2))
    o = nsa_prompt_attention(qg, kc, vc, tk(ks), tk(vs), tk(kw), tk(vw),
                             jnp.transpose(gates, (0, 2, 3, 1, 4)))
    return jnp.transpose(o, (0, 3, 1, 2, 4)).reshape(B, T, A_WIDTH)


def nsa_sample(q, kvs, gates, cmp_cache, sel_cache, win_cache, page_table, pe, w1, b1, w2, b2):
    kc_new, vc_new, ks_new, vs_new, kw_new, vw_new = kvs
    DB, DS = q.shape[:2]
    past = page_table.shape[1] * PAGE_SIZE
    kvc = compress_sample(cmp_cache, page_table, pe, w1, b1, w2, b2)
    qg = _bf(jnp.transpose(q.reshape(DB, DS, N_KV, HPG, HEAD_DIM), (0, 2, 3, 1, 4)) * (HEAD_DIM ** -0.5))
    qg = qg.reshape(DB, N_KV, HPG * DS, HEAD_DIM)
    gt = jnp.transpose(gates, (0, 2, 3, 1, 4)).reshape(DB, N_KV, HPG * DS, 3)
    pack = lambda k, v: _bf(jnp.pad(jnp.transpose(jnp.stack([k, v], axis=1), (0, 1, 3, 2, 4)),
                                    ((0, 0), (0, 0), (0, 0), (0, NEW_PAD - DS), (0, 0))))
    o = nsa_sample_attention(qg, kvc, sel_cache, win_cache, page_table, pack(ks_new, vs_new),
                             pack(kw_new, vw_new), gt, past)
    o = jnp.transpose(o.reshape(DB, N_KV, HPG, DS, HEAD_DIM), (0, 3, 1, 2, 4)).reshape(DB, DS, -1)
    new_k = jnp.concatenate([win_cache[:, DS:, 0], kw_new], axis=1)
    new_v = jnp.concatenate([win_cache[:, DS:, 1], vw_new], axis=1)
    return o, jnp.stack([new_k, new_v], axis=2)


def merge_and_ffn(x, y_r, y_a, p_all, w_pa, w_pb, w_o, ln1_g, ln1_b, router_w, router_b,
                  mlp1_w, mlp1_b, mlp2_w, mlp2_b, ln2_g, ln2_b):
    h, logits = merge_rows(x, y_r, y_a, p_all, G0, w_pa, w_pb, w_o, ln1_g, ln1_b, router_w)
    gate, yb = moe(h, logits[:, :N_EXPERTS] + router_b, mlp1_w, mlp1_b, mlp2_w, mlp2_b)
    return combine_rows(h, gate, yb, ln2_g, ln2_b)


def kernel(x_prompt, x_sample, cache_cmp_kv, cache_sel_kv, cache_win_kv, state_wkv, state_shift,
           page_table, w_in, mu_shift, w0, w_w2, a0, w_a2, g_w2, k_k, k_a, r_k, gn_g, gn_b,
           cmp_pe, cmp_w1, cmp_b1, cmp_w2, cmp_b2, w_pa, w_pb, w_o, ln1_g, ln1_b,
           router_w, router_b, mlp1_w, mlp1_b, mlp2_w, mlp2_b, ln2_g, ln2_b):
    B, T, D = x_prompt.shape
    DB, DS, _ = x_sample.shape
    n_p = B * T
    past = page_table.shape[1] * PAGE_SIZE
    pos_p = jnp.arange(T)
    pos_s = past + jnp.arange(DS)
    wb_p = min(WINDOW, T)
    h_all = jnp.concatenate([x_prompt.reshape(n_p, D), x_sample.reshape(DB * DS, D)])
    cmp_p, sel_p, win_p, wkv_p, shift_p = [], [], [], [], []
    cmp_s, sel_s, win_s, wkv_s, shift_s = [], [], [], [], []
    for l in range(DEPTH):
        rwkv_w = (mu_shift[l], w0[l], w_w2[l], a0[l], w_a2[l], g_w2[l], k_k[l], k_a[l], r_k[l],
                  gn_g[l], gn_b[l])
        cmp_w = (cmp_pe[l], cmp_w1[l], cmp_b1[l], cmp_w2[l], cmp_b2[l])
        out_w = (w_pa[l], w_pb[l], w_o[l], ln1_g[l], ln1_b[l], router_w[l], router_b[l],
                 mlp1_w[l], mlp1_b[l], mlp2_w[l], mlp2_b[l], ln2_g[l], ln2_b[l])
        p_all, pg_all = project(h_all, w_in[l])
        pr, q, kvs, gates = split_projection(p_all[:n_p], pg_all[:n_p], B, T, pos_p)
        parts = rwkv_prep(p_all, n_p, T, jnp.zeros((B, R_COLS), jnp.float32), *rwkv_w[:9])
        y_r, wkv = rwkv_mixer([z.reshape(B, T, R_WIDTH) for z in parts],
                              jnp.zeros((B, R_HEADS, R_HEAD, R_HEAD), jnp.float32), *rwkv_w[9:])
        shift = p_all[T - 1:n_p:T, :R_COLS]
        y_a = nsa_prompt(q, kvs, gates, *cmp_w)
        cmp_p.append(jnp.stack([kvs[0], kvs[1]], axis=2))
        sel_p.append(jnp.stack([kvs[2], kvs[3]], axis=2))
        win_p.append(jnp.stack([kvs[4][:, T - wb_p:], kvs[5][:, T - wb_p:]], axis=2))
        wkv_p.append(wkv)
        shift_p.append(shift)
        pr, q, kvs, gates = split_projection(p_all[n_p:], pg_all[n_p:], DB, DS, pos_s)
        y_r_s, wkv = rwkv_mixer(rwkv_prep_rows(pr, state_shift[l], *rwkv_w[:9]), state_wkv[l], *rwkv_w[9:])
        shift = pr[:, -1]
        y_a_s, new_win = nsa_sample(q, kvs, gates, cache_cmp_kv[l], cache_sel_kv[l], cache_win_kv[l],
                                    page_table, *cmp_w)
        cmp_s.append(jnp.stack([kvs[0], kvs[1]], axis=2))
        sel_s.append(jnp.stack([kvs[2], kvs[3]], axis=2))
        win_s.append(new_win)
        wkv_s.append(wkv)
        shift_s.append(shift)
        rows = lambda a, b: jnp.concatenate([a.reshape(n_p, -1), b.reshape(DB * DS, -1)])
        h_all = merge_and_ffn(h_all, rows(y_r, y_r_s), rows(y_a, y_a_s), p_all, *out_w)
    hp = h_all[:n_p].reshape(B, T, D)
    hs = h_all[n_p:].reshape(DB, DS, D)
    return (hp, hs, jnp.stack(cmp_p), jnp.stack(sel_p), jnp.stack(win_p), jnp.stack(wkv_p),
            jnp.stack(shift_p), jnp.stack(cmp_s), jnp.stack(sel_s), jnp.stack(win_s),
            jnp.stack(wkv_s), jnp.stack(shift_s))
```

```python
import functools

import jax
import jax.numpy as jnp
from jax import lax
from jax.experimental import pallas as pl
from jax.experimental.pallas import tpu as pltpu

D_MODEL = 1024
DEPTH = 1
PAGE_SIZE = 128

R_HEADS = 8
R_HEAD = 64
R_WIDTH = R_HEADS * R_HEAD
LORA_W = 64
LORA_A = 64
LORA_G = 128
R_COLS = 3 * R_WIDTH + LORA_W + LORA_A + LORA_G
GN_EPS = 64e-5

N_HEADS = 8
N_KV = 2
HPG = N_HEADS // N_KV
HEAD_DIM = 64
A_WIDTH = N_HEADS * HEAD_DIM
KV_WIDTH = N_KV * HEAD_DIM
N_GATE = 3 * N_HEADS
A_COLS = A_WIDTH + 6 * KV_WIDTH + N_GATE
ROT_DIM = HEAD_DIM // 4
ROPE_THETA = 500000.0
CMP_STRIDE = 16
CMP_LEN = 2 * CMP_STRIDE
CMP_HIDDEN = 256
SEL_BLOCK = 64
SEL_TOP = 16
WINDOW = 512

N_EXPERTS = 32
TOP_K = 4
K_SHIFT = 2
D_FF = 1024
SWIGLU_LIMIT = 7.0
SWIGLU_ALPHA = 1.702

DN_ALPHA = (2 * DEPTH) ** 0.25
LN_EPS = 1e-5
NEG = -1e30
LOG2E = 1.4426950408889634

LANES = 128
SUBLANES = 8
PAGES_PER_STEP = 8
NEW_PAD = 8
VMEM_LIMIT = 56 * 1024 * 1024


def _bf(x):
    return x.astype(jnp.bfloat16)


def _dot(a, b):
    return jnp.dot(_bf(a), _bf(b), preferred_element_type=jnp.float32)


def _dot_nt(a, b):
    return lax.dot_general(_bf(a), _bf(b), (((1,), (1,)), ((), ())),
                           preferred_element_type=jnp.float32)


def _mm_kernel(x_ref, w_ref, o_ref):
    o_ref[...] = _dot(x_ref[...], w_ref[...])


def pallas_matmul(x, w):
    x, w = _bf(x), _bf(w)
    M, K = x.shape
    N = w.shape[1]
    tm = next(t for t in (512, 384, 256, 128, M) if M % t == 0)
    resident = 2 * (2 * K * N + 4 * tm * N + 2 * tm * K) <= VMEM_LIMIT - (8 << 20)
    tn = N if resident else next(t for t in (512, 256, LANES) if N % t == 0)
    return pl.pallas_call(
        _mm_kernel,
        grid=(N // tn, M // tm),
        in_specs=[pl.BlockSpec((tm, K), lambda j, i: (i, 0)),
                  pl.BlockSpec((K, tn), lambda j, i: (0, j))],
        out_specs=pl.BlockSpec((tm, tn), lambda j, i: (i, j)),
        out_shape=jax.ShapeDtypeStruct((M, N), jnp.float32),
        compiler_params=pltpu.CompilerParams(vmem_limit_bytes=VMEM_LIMIT),
        name="mm",
    )(x, w)


def _bmm(a, b):
    return lax.dot_general(_bf(a), _bf(b), (((2,), (1,)), ((0,), (0,))), preferred_element_type=jnp.float32)


def _bmm_nt(a, b):
    return lax.dot_general(_bf(a), _bf(b), (((2,), (2,)), ((0,), (0,))), preferred_element_type=jnp.float32)


def _bmm_tn(a, b):
    return lax.dot_general(_bf(a), _bf(b), (((1,), (1,)), ((0,), (0,))), preferred_element_type=jnp.float32)


def _head_sum(x, ones_bd):
    hi = _bf(x)
    rem = x - hi.astype(jnp.float32)
    mid = _bf(rem)
    lo = _bf(rem - mid.astype(jnp.float32))
    dot = lambda t: jnp.dot(t, ones_bd, preferred_element_type=jnp.float32)
    return dot(hi) + dot(mid) + dot(lo)


def head_ones():
    h = jnp.arange(R_WIDTH) // R_HEAD
    return _bf(h[:, None] == h[None, :])


def _rwkv_prep_kernel(p_ref, pb_ref, sp_ref, mu_ref, w0_ref, ww2_ref, a0_ref, wa2_ref, gw2_ref,
                      kk_ref, ka_ref, rk_ref, bd_ref,
                      r_ref, lw_ref, k_ref, v_ref, a_ref, b_ref, bonus_ref, g_ref, *, tiles_per_seq):
    i = pl.program_id(0)
    pr = p_ref[...]
    first = jnp.where(i % tiles_per_seq == 0, sp_ref[0], pb_ref[SUBLANES - 1:SUBLANES, :])
    rolled = pltpu.roll(pr, shift=1, axis=0)
    prev = jnp.where(lax.broadcasted_iota(jnp.int32, (pr.shape[0], 1), 0) == 0, first, rolled)
    xm = pr + (prev - pr) * mu_ref[...]
    o1, o2, o3 = R_WIDTH, 2 * R_WIDTH, 3 * R_WIDTH
    o4 = o3 + LORA_W
    o5 = o4 + LORA_A
    r, k, v = xm[:, :o1], xm[:, o1:o2], xm[:, o2:o3]
    xw, xa, xg = xm[:, o3:o4], xm[:, o4:o5], xm[:, o5:]
    dot = lambda x, w_ref: jnp.dot(_bf(x), _bf(w_ref[...]), preferred_element_type=jnp.float32)
    w_log = -jax.nn.softplus(-(w0_ref[...] + dot(jnp.tanh(xw), ww2_ref))) - 0.5
    a = jax.nn.sigmoid(a0_ref[...] + dot(xa, wa2_ref))
    g_ref[...] = dot(jax.nn.sigmoid(xg), gw2_ref)
    ones_bd = bd_ref[...]
    kk = k * kk_ref[...]
    kk = kk / jnp.maximum(jnp.sqrt(_head_sum(kk * kk, ones_bd)), 1e-12)
    k_h = k * (1.0 + (a - 1.0) * ka_ref[...])
    r_ref[...] = r
    lw_ref[...] = -jnp.exp(w_log)
    k_ref[...] = k_h
    v_ref[...] = v
    a_ref[...] = -kk
    b_ref[...] = kk * a
    bonus_ref[...] = _head_sum(r * k_h * rk_ref[...], ones_bd) * v


def rwkv_prep(p_all, n_rows, seq_len, shift_prev, mu, w0, w_w2, a0, w_a2, g_w2, k_k, k_a, r_k, tm=256):
    assert seq_len % tm == 0 and n_rows % seq_len == 0
    n_seq = n_rows // seq_len
    tiles_per_seq = seq_len // tm
    row = lambda z: z.reshape(1, -1)
    const = lambda shape: pl.BlockSpec(shape, lambda i: (0,) * len(shape))
    out = pl.BlockSpec((tm, R_WIDTH), lambda i: (i, 0))
    outs = pl.pallas_call(
        functools.partial(_rwkv_prep_kernel, tiles_per_seq=tiles_per_seq),
        grid=(n_rows // tm,),
        in_specs=[pl.BlockSpec((tm, R_COLS), lambda i: (i, 0)),
                  pl.BlockSpec((SUBLANES, R_COLS), lambda i: (jnp.maximum(i * (tm // SUBLANES) - 1, 0), 0)),
                  pl.BlockSpec((1, 1, R_COLS), lambda i: (i // tiles_per_seq, 0, 0)),
                  const((1, R_COLS)), const((1, R_WIDTH)), const((LORA_W, R_WIDTH)), const((1, R_WIDTH)),
                  const((LORA_A, R_WIDTH)), const((LORA_G, R_WIDTH)), const((1, R_WIDTH)), const((1, R_WIDTH)),
                  const((1, R_WIDTH)), const((R_WIDTH, R_WIDTH))],
        out_specs=[out] * 8,
        out_shape=[jax.ShapeDtypeStruct((n_rows, R_WIDTH), jnp.float32)] * 8,
        compiler_params=pltpu.CompilerParams(dimension_semantics=("arbitrary",), vmem_limit_bytes=VMEM_LIMIT),
        name="rwkv_prep",
    )(p_all, p_all, shift_prev.reshape(n_seq, 1, R_COLS), row(mu), row(w0), w_w2, row(a0), w_a2, g_w2,
      row(k_k), row(k_a), row(r_k), head_ones())
    return outs


def _wkv_chunk_kernel(r_ref, lw_ref, k_ref, v_ref, a_ref, b_ref, bonus_ref, g_ref, gng_ref, gnb_ref, bd_ref,
                      s0_ref, y_ref, s_out_ref, s_scr, *, C, H):
    c = pl.program_id(1)
    D = R_HEAD

    @pl.when(c == 0)
    def _():
        s_scr[...] = s0_ref[0]

    row = lax.broadcasted_iota(jnp.int32, (H, C, C), 1)
    col = lax.broadcasted_iota(jnp.int32, (H, C, C), 2)
    incl = row >= col
    strict = row > col
    ltri = jnp.where(incl, 1.0, 0.0).astype(jnp.bfloat16)
    eye = jnp.where(row == col, 1.0, 0.0).astype(jnp.float32)

    heads = lambda ref: jnp.stack([ref[0, :, h * D:(h + 1) * D] for h in range(H)])
    lw = heads(lw_ref)
    r = heads(r_ref)
    k = heads(k_ref)
    v = heads(v_ref)
    a = heads(a_ref)
    b = heads(b_ref)
    hi = _bf(lw)
    rem = lw - hi.astype(jnp.float32)
    mid = _bf(rem)
    lo = _bf(rem - mid.astype(jnp.float32))
    lp = _bmm(ltri, hi) + _bmm(ltri, mid) + _bmm(ltri, lo)
    lp_end = lp[:, C - 1:C, :]
    p_end = jnp.exp(lp_end)
    p_inv = jnp.exp(-lp)
    at = a * jnp.exp(lp - lw)
    rt = r * jnp.exp(lp)
    bt = b * p_inv
    kt = k * p_inv
    p_hat = jnp.exp(lp_end - lp)
    bh = b * p_hat
    kh = k * p_hat

    n_ab = jnp.where(strict, _bmm_nt(at, bt), 0.0)
    a_ak = jnp.where(strict, _bmm_nt(at, kt), 0.0)
    a_rb = jnp.where(incl, _bmm_nt(rt, bt), 0.0)
    a_rk = jnp.where(incl, _bmm_nt(rt, kt), 0.0)

    t_inv = eye + n_ab
    n_pow = n_ab
    span = 2
    while span < C:
        n_pow = _bmm(n_pow, n_pow)
        t_inv = _bmm(t_inv, eye + n_pow)
        span *= 2

    s = s_scr[...]
    rhs = _bmm_nt(at, s) + _bmm(a_ak, v)
    u = _bmm(t_inv, rhs)
    y = _bmm_nt(rt, s) + _bmm(a_rb, u) + _bmm(a_rk, v)
    s_new = s * p_end + _bmm_tn(u, bh) + _bmm_tn(v, kh)
    s_scr[...] = s_new

    yt = jnp.concatenate([y[h] for h in range(H)], axis=1)
    ones_bd = bd_ref[...]
    dev = yt - _head_sum(yt, ones_bd) * (1.0 / D)
    var = _head_sum(dev * dev, ones_bd) * (1.0 / D)
    yn = dev * lax.rsqrt(var + GN_EPS) * gng_ref[...] + gnb_ref[...]
    y_ref[0] = (yn + bonus_ref[0]) * g_ref[0]

    @pl.when(c == pl.num_programs(1) - 1)
    def _():
        s_out_ref[0] = s_new


def wkv_chunked(r, lw, k, v, a, b, bonus, g, gn_g, gn_b, s0, C):
    B, T, W = r.shape
    H, D = R_HEADS, R_HEAD
    assert T % C == 0
    seq = pl.BlockSpec((1, C, W), lambda bi, ci: (bi, ci, 0))
    st = pl.BlockSpec((1, H, D, D), lambda bi, ci: (bi, 0, 0, 0))
    const = lambda shape: pl.BlockSpec(shape, lambda bi, ci: (0,) * len(shape))
    return pl.pallas_call(
        functools.partial(_wkv_chunk_kernel, C=C, H=H),
        grid=(B, T // C),
        in_specs=[seq] * 8 + [const((1, W)), const((1, W)), const((W, W)), st],
        out_specs=[seq, st],
        out_shape=[jax.ShapeDtypeStruct((B, T, W), jnp.float32),
                   jax.ShapeDtypeStruct((B, H, D, D), jnp.float32)],
        scratch_shapes=[pltpu.VMEM((H, D, D), jnp.float32)],
        compiler_params=pltpu.CompilerParams(dimension_semantics=("arbitrary", "arbitrary")),
        name="wkv_chunk",
    )(r, lw, k, v, a, b, bonus, g, gn_g.reshape(1, W), gn_b.reshape(1, W), head_ones(), s0)


def _nsa_prompt_kernel(q_ref, kc_ref, vc_ref, cover_ref, ks_ref, vs_ref, kw_ref, vw_ref, g_ref, o_ref,
                       *, TQ, TK, NC, NCP):
    f32 = jnp.float32
    bf16 = jnp.bfloat16
    qb = pl.program_id(2)
    R = HPG * TQ
    q = q_ref[0, 0].reshape(R, HEAD_DIM)
    t_pos = qb * TQ + lax.broadcasted_iota(jnp.int32, (TQ, 1), 0)

    n_idx = lax.broadcasted_iota(jnp.int32, (1, NCP), 1)
    c_ok = ((n_idx * CMP_STRIDE + (CMP_LEN - 1)) <= t_pos) & (n_idx < NC)
    s_c = _dot_nt(q, kc_ref[0, 0]).reshape(HPG, TQ, NCP)
    s_c = jnp.where(c_ok[None], s_c, NEG)
    m_c = jnp.max(s_c, axis=-1, keepdims=True)
    p_c = jnp.where(c_ok[None], jnp.exp2(s_c - m_c), 0.0)
    l_c = jnp.sum(p_c, axis=-1, keepdims=True)
    p_c = p_c / jnp.where(l_c > 0.0, l_c, 1.0)
    p_cb = p_c.astype(bf16)
    o_c = jnp.dot(p_cb.reshape(R, NCP), vc_ref[0, 0], preferred_element_type=f32)

    cover_t = cover_ref[...]
    imp = _dot_nt(cover_t, p_cb[0])
    for h in range(1, HPG):
        imp = imp + _dot_nt(cover_t, p_cb[h])
    s_col = lax.broadcasted_iota(jnp.int32, (LANES, 1), 0)
    t_row = qb * TQ + lax.broadcasted_iota(jnp.int32, (1, TQ), 1)
    cur = t_row // SEL_BLOCK
    forced = (s_col == 0) | (s_col == cur) | (s_col == cur - 1)
    causal = (s_col * SEL_BLOCK) <= t_row
    score = jnp.where(forced, 1e6, imp)
    score = jnp.where(causal, score, NEG)
    s_col_f = s_col.astype(f32)
    sel_t = jnp.zeros((LANES, TQ), f32)
    for _ in range(SEL_TOP):
        top = jnp.max(score, axis=0, keepdims=True)
        first = jnp.min(jnp.where(score == top, s_col_f, float(LANES)), axis=0, keepdims=True)
        hit = s_col_f == first
        sel_t = jnp.where(hit & (top > 0.5 * NEG), 1.0, sel_t)
        score = jnp.where(hit, -3e38, score)
    sel_neg = ((sel_t.T - 1.0) * (-NEG)).astype(bf16)
    q_aug = jnp.concatenate([jnp.concatenate([sel_neg] * HPG, axis=0), q], axis=1)
    key_off = lax.broadcasted_iota(jnp.int32, (1, TK), 1)

    def sel_tile(j, carry, diagonal):
        m, acc = carry
        start = pl.multiple_of(j * TK, TK)
        k = ks_ref[0, 0, pl.ds(start, TK), :]
        v = vs_ref[0, 0, pl.ds(start, TK), :]
        s = _dot_nt(q_aug, k).reshape(HPG, TQ, TK)
        if diagonal:
            s = jnp.where(((key_off + j * TK) <= t_pos)[None], s, NEG)
        m_new = jnp.maximum(m, jnp.max(s, axis=-1, keepdims=True))
        p = jnp.exp2((s - m_new).astype(bf16))
        alpha = jnp.exp2(m - m_new)
        pv = jnp.dot(p.reshape(R, TK), v, preferred_element_type=f32)
        acc = alpha * acc + pv.reshape(HPG, TQ, LANES)
        return m_new, acc

    m0 = jnp.full((HPG, TQ, 1), NEG, f32)
    a0 = jnp.zeros((HPG, TQ, LANES), f32)
    n_full = (qb * TQ) // TK
    carry = lax.fori_loop(0, n_full, functools.partial(sel_tile, diagonal=False), (m0, a0))
    _, acc_s = sel_tile(n_full, carry, True)
    o_s = acc_s[:, :, :HEAD_DIM] / acc_s[:, :, HEAD_DIM:HEAD_DIM + 1]

    n_w = WINDOW // TQ + 1
    lane_q = lax.broadcasted_iota(jnp.int32, (1, TQ), 1)
    s_w, ok_w, v_w = [], [], []
    for i in range(n_w):
        kb = qb - (n_w - 1) + i
        start = pl.multiple_of(jnp.maximum(kb, 0) * TQ, TQ)
        k = kw_ref[0, 0, pl.ds(start, TQ), :]
        v_w.append(vw_ref[0, 0, pl.ds(start, TQ), :])
        kpos = kb * TQ + lane_q
        ok = (kpos <= t_pos) & (kpos >= t_pos - WINDOW) & (kpos >= 0)
        ok_w.append(ok)
        s_w.append(jnp.where(ok[None], _dot_nt(q, k).reshape(HPG, TQ, TQ), NEG))
    m_w = s_w[0].max(axis=-1, keepdims=True)
    for i in range(1, n_w):
        m_w = jnp.maximum(m_w, s_w[i].max(axis=-1, keepdims=True))
    l_w = jnp.zeros((HPG, TQ, 1), f32)
    acc_w = jnp.zeros((R, HEAD_DIM), f32)
    for i in range(n_w):
        p = jnp.where(ok_w[i][None], jnp.exp2(s_w[i] - m_w), 0.0)
        l_w = l_w + jnp.sum(p, axis=-1, keepdims=True)
        acc_w = acc_w + jnp.dot(p.astype(bf16).reshape(R, TQ), v_w[i], preferred_element_type=f32)
    o_w = acc_w.reshape(HPG, TQ, HEAD_DIM) / l_w

    g = g_ref[0, 0]
    o_ref[0, 0] = (g[:, :, 0:1] * o_c.reshape(HPG, TQ, HEAD_DIM) + g[:, :, 1:2] * o_s
                   + g[:, :, 2:3] * o_w)


def nsa_prompt_attention(q, kc, vc, ks, vs, kw, vw, gates, TQ=256, TK=1024):
    B, G, _, T, D = q.shape
    NC = kc.shape[2]
    NCP = -(-NC // LANES) * LANES
    NS = T // SEL_BLOCK
    assert NS <= LANES and T % TK == 0 and TK % TQ == 0 and WINDOW % TQ == 0
    kc = jnp.pad(kc, ((0, 0), (0, 0), (0, NCP - NC), (0, 0))).astype(jnp.bfloat16)
    vc = jnp.pad(vc, ((0, 0), (0, 0), (0, NCP - NC), (0, 0))).astype(jnp.bfloat16)
    c_start = jnp.arange(NCP) * CMP_STRIDE
    s_start = jnp.arange(LANES) * SEL_BLOCK
    cover = ((c_start[:, None] < s_start[None, :] + SEL_BLOCK)
             & (c_start[:, None] + CMP_LEN > s_start[None, :])
             & (jnp.arange(NCP)[:, None] < NC) & (jnp.arange(LANES)[None, :] < NS)).astype(jnp.bfloat16)
    onehot = (jnp.arange(T)[:, None] // SEL_BLOCK == jnp.arange(LANES)[None, :]).astype(jnp.bfloat16)
    ks = jnp.concatenate([jnp.broadcast_to(onehot, (B, G, T, LANES)), ks], axis=-1)
    ones_col = (jnp.arange(LANES - D) == 0).astype(jnp.bfloat16)
    vs = jnp.concatenate([vs, jnp.broadcast_to(ones_col, (B, G, T, LANES - D))], axis=-1)
    full = lambda n, d=D: pl.BlockSpec((1, 1, n, d), lambda b, g, i: (b, g, 0, 0))
    qspec = pl.BlockSpec((1, 1, HPG, TQ, D), lambda b, g, i: (b, g, 0, i, 0))
    return pl.pallas_call(
        functools.partial(_nsa_prompt_kernel, TQ=TQ, TK=TK, NC=NC, NCP=NCP),
        grid=(B, G, T // TQ),
        in_specs=[qspec, full(NCP), full(NCP),
                  pl.BlockSpec((LANES, NCP), lambda b, g, i: (0, 0)),
                  full(T, LANES + D), full(T, LANES), full(T), full(T),
                  pl.BlockSpec((1, 1, HPG, TQ, 3), lambda b, g, i: (b, g, 0, i, 0))],
        out_specs=qspec,
        out_shape=jax.ShapeDtypeStruct((B, G, HPG, T, D), jnp.float32),
        compiler_params=pltpu.CompilerParams(
            dimension_semantics=("arbitrary", "arbitrary", "arbitrary"),
            vmem_limit_bytes=VMEM_LIMIT),
        name="nsa_prompt",
    )(q, kc, vc, cover.T, ks, vs, kw, vw, gates)


def _cmp_sample_kernel(pt_ref, *refs, n_chunk):
    pages = refs[:PAGES_PER_STEP]
    wcat_ref, c1_ref, w2_ref, b2_ref, o_ref, seq = refs[PAGES_PER_STEP:]
    j = pl.program_id(1)
    rows = PAGE_SIZE // CMP_STRIDE
    src = lax.broadcasted_iota(jnp.int32, (PAGE_SIZE, PAGE_SIZE), 0)
    src = (src % rows) * CMP_STRIDE + src // rows
    perm = _bf(jnp.where(src == lax.broadcasted_iota(jnp.int32, (PAGE_SIZE, PAGE_SIZE), 1), 1.0, 0.0))
    for i in range(PAGES_PER_STEP):
        dst = pl.multiple_of((j * PAGES_PER_STEP + i) * rows, rows)
        for kv in range(2):
            for g in range(N_KV):
                q = kv * N_KV + g
                by_pos = _dot_nt(perm, pages[i][0, kv, g])
                for p in range(CMP_STRIDE):
                    seq[q, pl.ds(dst, rows), p * HEAD_DIM:(p + 1) * HEAD_DIM] = by_pos[p * rows:(p + 1) * rows]

    @pl.when(j == pl.num_programs(1) - 1)
    def _():
        for kv in range(2):
            for g in range(N_KV):
                q = kv * N_KV + g
                acc = jnp.dot(_bf(seq[q]), wcat_ref[kv], preferred_element_type=jnp.float32)
                first = acc[:, :CMP_HIDDEN]
                second = acc[:, CMP_HIDDEN:]
                second = jnp.concatenate([second[1:], second[:1]], axis=0)
                hid = jax.nn.gelu(first + second + c1_ref[kv])
                out = jnp.dot(_bf(hid), w2_ref[kv], preferred_element_type=jnp.float32) + b2_ref[kv]
                o_ref[0, kv, g] = _bf(out)


def compress_sample(cache, page_table, pe, w1, b1, w2, b2):
    n_pool = cache.shape[0]
    DB, n_pages = page_table.shape
    assert n_pages % PAGES_PER_STEP == 0
    rows = PAGE_SIZE // CMP_STRIDE
    n_chunk = n_pages * rows
    view = jnp.transpose(cache, (0, 2, 3, 4, 1))
    w1r = w1.reshape(2, CMP_LEN, HEAD_DIM, CMP_HIDDEN)
    wcat = _bf(jnp.concatenate([w1r[:, :CMP_STRIDE], w1r[:, CMP_STRIDE:]], axis=-1))
    wcat = wcat.reshape(2, CMP_STRIDE * HEAD_DIM, 2 * CMP_HIDDEN)
    c1 = (jnp.einsum('kn,knh->kh', pe.reshape(2, CMP_LEN * HEAD_DIM), w1) + b1).reshape(2, 1, CMP_HIDDEN)
    page_spec = lambda i: pl.BlockSpec((1, 2, N_KV, HEAD_DIM, PAGE_SIZE),
                                       lambda b, j, pt: (pt[b, j * PAGES_PER_STEP + i], 0, 0, 0, 0))
    const = lambda shape: pl.BlockSpec(shape, lambda b, j, pt: (0,) * len(shape))
    grid_spec = pltpu.PrefetchScalarGridSpec(
        num_scalar_prefetch=1,
        grid=(DB, n_pages // PAGES_PER_STEP),
        in_specs=[page_spec(i) for i in range(PAGES_PER_STEP)]
        + [const((2, CMP_STRIDE * HEAD_DIM, 2 * CMP_HIDDEN)), const((2, 1, CMP_HIDDEN)),
           const((2, CMP_HIDDEN, HEAD_DIM)), const((2, 1, HEAD_DIM))],
        out_specs=pl.BlockSpec((1, 2, N_KV, n_chunk, HEAD_DIM), lambda b, j, pt: (b, 0, 0, 0, 0)),
        scratch_shapes=[pltpu.VMEM((2 * N_KV, n_chunk, CMP_STRIDE * HEAD_DIM), jnp.float32)],
    )
    return pl.pallas_call(
        functools.partial(_cmp_sample_kernel, n_chunk=n_chunk),
        grid_spec=grid_spec,
        out_shape=jax.ShapeDtypeStruct((DB, 2, N_KV, n_chunk, HEAD_DIM), jnp.bfloat16),
        compiler_params=pltpu.CompilerParams(dimension_semantics=("arbitrary", "arbitrary"),
                                             vmem_limit_bytes=VMEM_LIMIT),
        name="cmp_sample",
    )(page_table, *([view] * PAGES_PER_STEP), wcat, c1, _bf(w2), b2.reshape(2, 1, HEAD_DIM))


def _nsa_sample_kernel(pt_ref, *refs, DS, NC, past):
    pages = refs[:PAGES_PER_STEP]
    (q_ref, kvc_ref, cover_ref, new_sel_ref, win_ref, new_win_ref, g_ref, o_ref,
     qaug, m_s, l_s, acc_s, oc_s) = refs[PAGES_PER_STEP:]
    f32 = jnp.float32
    j = pl.program_id(1)
    R = HPG * DS
    t_row = lax.broadcasted_iota(jnp.int32, (R, 1), 0) % DS
    n_chunk = kvc_ref.shape[3]

    @pl.when(j == 0)
    def _():
        n_idx = lax.broadcasted_iota(jnp.int32, (1, n_chunk), 1)
        s_col = lax.broadcasted_iota(jnp.int32, (LANES, 1), 0)
        s_col_f = s_col.astype(f32)
        last_blk = past // SEL_BLOCK - 1
        for g in range(N_KV):
            q = q_ref[0, g]
            s_c = jnp.where(n_idx < NC, _dot_nt(q, kvc_ref[0, 0, g]), NEG)
            p_c = jnp.exp(s_c - jnp.max(s_c, axis=-1, keepdims=True))
            p_c = p_c / jnp.sum(p_c, axis=-1, keepdims=True)
            p_cb = _bf(p_c)
            oc_s[g] = jnp.dot(p_cb, kvc_ref[0, 1, g], preferred_element_type=f32)
            imp_rows = _dot_nt(cover_ref[...], p_cb)
            imp = imp_rows[:, 0:DS]
            for h in range(1, HPG):
                imp = imp + imp_rows[:, h * DS:(h + 1) * DS]
            forced = (s_col == 0) | (s_col == last_blk)
            score = jnp.where(forced, 1e6, imp)
            score = jnp.where(s_col <= last_blk, score, NEG)
            sel_t = jnp.zeros((LANES, DS), f32)
            for _ in range(SEL_TOP - 1):
                top = jnp.max(score, axis=0, keepdims=True)
                first = jnp.min(jnp.where(score == top, s_col_f, float(LANES)), axis=0, keepdims=True)
                hit = s_col_f == first
                sel_t = jnp.where(hit & (top > 0.5 * NEG), 1.0, sel_t)
                score = jnp.where(hit, -3e38, score)
            sel_neg = _bf((sel_t.T - 1.0) * (-NEG))
            qaug[g] = jnp.concatenate([jnp.concatenate([sel_neg] * HPG, axis=0), q], axis=1)
        m_s[...] = jnp.full(m_s.shape, NEG, f32)
        l_s[...] = jnp.zeros(l_s.shape, f32)
        acc_s[...] = jnp.zeros(acc_s.shape, f32)

    TK = PAGES_PER_STEP * PAGE_SIZE
    blk = (lax.broadcasted_iota(jnp.int32, (LANES, TK), 1) // SEL_BLOCK
           + j * (TK // SEL_BLOCK))
    onehot = _bf(jnp.where(blk == lax.broadcasted_iota(jnp.int32, (LANES, TK), 0), 1.0, 0.0))
    for g in range(N_KV):
        k_t = _bf(jnp.concatenate([pages[i][0, 0, g] for i in range(PAGES_PER_STEP)], axis=1))
        v_t = _bf(jnp.concatenate([pages[i][0, 1, g] for i in range(PAGES_PER_STEP)], axis=1))
        s = jnp.dot(qaug[g], jnp.concatenate([onehot, k_t], axis=0), preferred_element_type=f32)
        m_new = jnp.maximum(m_s[g], jnp.max(s, axis=-1, keepdims=True))
        p = jnp.exp(s - m_new)
        alpha = jnp.exp(m_s[g] - m_new)
        l_s[g] = alpha * l_s[g] + jnp.sum(p, axis=-1, keepdims=True)
        acc_s[g] = alpha * acc_s[g] + _dot_nt(p, v_t)
        m_s[g] = m_new

    @pl.when(j == pl.num_programs(1) - 1)
    def _():
        j_new = lax.broadcasted_iota(jnp.int32, (1, NEW_PAD), 1)
        ok_new = (j_new <= t_row) & (j_new < DS)
        w_idx = lax.broadcasted_iota(jnp.int32, (1, WINDOW), 1)
        ok_win = w_idx >= t_row
        for g in range(N_KV):
            q = q_ref[0, g]
            s = jnp.where(ok_new, _dot_nt(q, new_sel_ref[0, 0, g]), NEG)
            m_new = jnp.maximum(m_s[g], jnp.max(s, axis=-1, keepdims=True))
            p = jnp.where(ok_new, jnp.exp(s - m_new), 0.0)
            alpha = jnp.exp(m_s[g] - m_new)
            l_fin = alpha * l_s[g] + jnp.sum(p, axis=-1, keepdims=True)
            o_sel = (alpha * acc_s[g]
                     + jnp.dot(_bf(p), new_sel_ref[0, 1, g], preferred_element_type=f32)) / l_fin
            s_a = jnp.where(ok_win, jnp.dot(q, _bf(win_ref[0, 0, g]), preferred_element_type=f32), NEG)
            s_b = jnp.where(ok_new, _dot_nt(q, new_win_ref[0, 0, g]), NEG)
            m_w = jnp.maximum(jnp.max(s_a, axis=-1, keepdims=True), jnp.max(s_b, axis=-1, keepdims=True))
            p_a = jnp.where(ok_win, jnp.exp(s_a - m_w), 0.0)
            p_b = jnp.where(ok_new, jnp.exp(s_b - m_w), 0.0)
            l_w = jnp.sum(p_a, axis=-1, keepdims=True) + jnp.sum(p_b, axis=-1, keepdims=True)
            o_win = (_dot_nt(p_a, win_ref[0, 1, g])
                     + jnp.dot(_bf(p_b), new_win_ref[0, 1, g], preferred_element_type=f32)) / l_w
            gt = g_ref[0, g]
            o_ref[0, g] = gt[:, 0:1] * oc_s[g] + gt[:, 1:2] * o_sel + gt[:, 2:3] * o_win


def nsa_sample_attention(q, kvc, sel_cache, win_cache, page_table, new_sel, new_win, gates, past):
    DB, G, R, D = q.shape
    DS = R // HPG
    n_pool = sel_cache.shape[0]
    n_pages = page_table.shape[1]
    n_chunk = kvc.shape[3]
    NC = (past + DS) // CMP_STRIDE - 1
    assert past % SEL_BLOCK == 0 and DS < CMP_STRIDE and DS <= NEW_PAD and past // SEL_BLOCK <= LANES
    assert n_pages % PAGES_PER_STEP == 0 and win_cache.shape[1] == WINDOW and NC < n_chunk + 1
    c_start = jnp.arange(n_chunk) * CMP_STRIDE
    s_start = jnp.arange(LANES) * SEL_BLOCK
    cover_t = _bf((c_start[None, :] < s_start[:, None] + SEL_BLOCK)
                  & (c_start[None, :] + CMP_LEN > s_start[:, None])
                  & (jnp.arange(n_chunk)[None, :] < NC) & (jnp.arange(LANES)[:, None] < past // SEL_BLOCK))
    sel_view = jnp.transpose(sel_cache, (0, 2, 3, 4, 1))
    win_view = jnp.transpose(win_cache, (0, 2, 3, 4, 1))
    page_spec = lambda i: pl.BlockSpec((1, 2, G, D, PAGE_SIZE),
                                       lambda b, j, pt: (pt[b, j * PAGES_PER_STEP + i], 0, 0, 0, 0))
    per_b = lambda shape: pl.BlockSpec((1,) + shape, lambda b, j, pt: (b,) + (0,) * len(shape))
    grid_spec = pltpu.PrefetchScalarGridSpec(
        num_scalar_prefetch=1,
        grid=(DB, n_pages // PAGES_PER_STEP),
        in_specs=[page_spec(i) for i in range(PAGES_PER_STEP)]
        + [per_b((G, R, D)), per_b((2, G, n_chunk, D)),
           pl.BlockSpec((LANES, n_chunk), lambda b, j, pt: (0, 0)),
           per_b((2, G, NEW_PAD, D)), per_b((2, G, D, WINDOW)), per_b((2, G, NEW_PAD, D)), per_b((G, R, 3))],
        out_specs=per_b((G, R, D)),
        scratch_shapes=[pltpu.VMEM((G, R, LANES + D), jnp.bfloat16), pltpu.VMEM((G, R, 1), jnp.float32),
                        pltpu.VMEM((G, R, 1), jnp.float32), pltpu.VMEM((G, R, D), jnp.float32),
                        pltpu.VMEM((G, R, D), jnp.float32)],
    )
    return pl.pallas_call(
        functools.partial(_nsa_sample_kernel, DS=DS, NC=NC, past=past),
        grid_spec=grid_spec,
        out_shape=jax.ShapeDtypeStruct((DB, G, R, D), jnp.float32),
        compiler_params=pltpu.CompilerParams(dimension_semantics=("arbitrary", "arbitrary"),
                                             vmem_limit_bytes=VMEM_LIMIT),
        name="nsa_sample",
    )(page_table, *([sel_view] * PAGES_PER_STEP), q, kvc, cover_t, new_sel, win_view, new_win, gates)


def _merge_kernel(x_ref, yr_ref, ya_ref, gr_ref, ga_ref, wpa_ref, wpb_ref, wo_ref, g1_ref, b1_ref, rw_ref,
                  h_ref, lg_ref):
    m = (jax.nn.sigmoid(gr_ref[...]) * _dot(yr_ref[...], wpa_ref[...])
         + jax.nn.sigmoid(ga_ref[...]) * _dot(ya_ref[...], wpb_ref[...]))
    z = DN_ALPHA * x_ref[...] + _dot(m, wo_ref[...])
    mu = jnp.mean(z, axis=-1, keepdims=True)
    dev = z - mu
    var = jnp.mean(dev * dev, axis=-1, keepdims=True)
    h = dev * lax.rsqrt(var + LN_EPS) * g1_ref[...] + b1_ref[...]
    h_ref[...] = h
    lg_ref[...] = _dot(h, rw_ref[...])


def merge_rows(x, y_r, y_a, p_all, g0, w_pa, w_pb, w_o, ln1_g, ln1_b, router_w):
    N, D = x.shape
    W = y_r.shape[1]
    assert g0 % D == 0 and D == D_MODEL
    tm = next(t for t in (384, 256, 128, 64, 8) if N % t == 0)
    rw = jnp.pad(router_w, ((0, 0), (0, LANES - router_w.shape[1])))
    rows = lambda w, c=0: pl.BlockSpec((tm, w), lambda i, c=c: (i, c))
    const = lambda shape: pl.BlockSpec(shape, lambda i: (0,) * len(shape))
    return pl.pallas_call(
        _merge_kernel,
        grid=(N // tm,),
        in_specs=[rows(D), rows(W), rows(W), rows(D, g0 // D), rows(D, g0 // D + 1),
                  const((W, D)), const((W, D)), const((D, D)), const((1, D)), const((1, D)), const((D, LANES))],
        out_specs=[rows(D), rows(LANES)],
        out_shape=[jax.ShapeDtypeStruct((N, D), jnp.float32), jax.ShapeDtypeStruct((N, LANES), jnp.float32)],
        compiler_params=pltpu.CompilerParams(dimension_semantics=("arbitrary",), vmem_limit_bytes=VMEM_LIMIT),
        name="merge_rows",
    )(x, y_r, y_a, p_all, p_all, _bf(w_pa), _bf(w_pb), _bf(w_o), ln1_g.reshape(1, D), ln1_b.reshape(1, D), _bf(rw))


def _moe_mlp_kernel(be_ref, base_ref, nval_ref, order_ref, x_hbm, w1_ref, b1_ref, w2_ref, b2_ref, o_hbm,
                    w1s, w2s, xbuf, ybuf, in_sem, out_sem, *, BM, T):
    i = pl.program_id(0)
    n = pl.num_programs(0)
    s = i % 2

    def gather(block, sl):
        base = base_ref[block]

        def body(r, c):
            tok = order_ref[base + r] >> K_SHIFT
            pltpu.make_async_copy(x_hbm.at[tok], xbuf.at[sl, r], in_sem.at[sl]).start()
            return c
        lax.fori_loop(0, BM, body, 0, unroll=8)

    @pl.when(i == 0)
    def _():
        gather(0, 0)

    @pl.when(i + 1 < n)
    def _():
        gather(i + 1, 1 - s)

    e = be_ref[i]
    prev = be_ref[jnp.maximum(i - 1, 0)]

    @pl.when((i == 0) | (e != prev))
    def _():
        w1s[...] = w1_ref[0].astype(jnp.bfloat16)
        w2s[...] = w2_ref[0].astype(jnp.bfloat16)

    def rows_in(sl):
        return pltpu.make_async_copy(x_hbm.at[pl.ds(0, BM)], xbuf.at[sl], in_sem.at[sl])

    def rows_out(sl):
        return pltpu.make_async_copy(ybuf.at[sl], o_hbm.at[pl.ds(0, BM)], out_sem.at[sl])

    rows_in(s).wait()

    @pl.when(i >= 2)
    def _():
        rows_out(s).wait()

    x = jnp.concatenate([xbuf[s, :, c, :] for c in range(SUBLANES)], axis=1).astype(jnp.bfloat16)
    h = jnp.dot(x, w1s[...], preferred_element_type=jnp.float32) + b1_ref[0]
    glu = jnp.minimum(h[:, :D_FF], SWIGLU_LIMIT)
    lin = jnp.clip(h[:, D_FF:], -SWIGLU_LIMIT, SWIGLU_LIMIT)
    act = glu * jax.nn.sigmoid(SWIGLU_ALPHA * glu) * (lin + 1.0)
    y = jnp.dot(act.astype(jnp.bfloat16), w2s[...], preferred_element_type=jnp.float32) + b2_ref[0]
    for c in range(SUBLANES):
        ybuf[s, :, c, :] = y[:, c * LANES:(c + 1) * LANES]

    base = base_ref[i]
    nval = nval_ref[i]
    spare = TOP_K * T + i * BM - base - nval

    def scatter(r, c):
        a = order_ref[base + r]
        row = jnp.where(r < nval, (a & (TOP_K - 1)) * T + (a >> K_SHIFT), spare + r)
        pltpu.make_async_copy(ybuf.at[s, r], o_hbm.at[row], out_sem.at[s]).start()
        return c
    lax.fori_loop(0, BM, scatter, 0, unroll=8)

    @pl.when(i == n - 1)
    def _():
        rows_out(s).wait()

    @pl.when((i == n - 1) & (n >= 2))
    def _():
        rows_out(1 - s).wait()


def moe_mlp(x, order, block_e, block_base, block_nval, w1, b1, w2, b2, BM):
    T, D = x.shape
    assert D == SUBLANES * LANES
    n_blocks = block_e.shape[0]
    E = w1.shape[0]
    grid_spec = pltpu.PrefetchScalarGridSpec(
        num_scalar_prefetch=4,
        grid=(n_blocks,),
        in_specs=[pl.BlockSpec(memory_space=pl.ANY),
                  pl.BlockSpec((1, D, 2 * D_FF), lambda i, be, *_: (be[i], 0, 0)),
                  pl.BlockSpec((1, 1, 2 * D_FF), lambda i, be, *_: (be[i], 0, 0)),
                  pl.BlockSpec((1, D_FF, D), lambda i, be, *_: (be[i], 0, 0)),
                  pl.BlockSpec((1, 1, D), lambda i, be, *_: (be[i], 0, 0))],
        out_specs=pl.BlockSpec(memory_space=pl.ANY),
        scratch_shapes=[pltpu.VMEM((D, 2 * D_FF), jnp.bfloat16), pltpu.VMEM((D_FF, D), jnp.bfloat16),
                        pltpu.VMEM((2, BM, SUBLANES, LANES), jnp.float32),
                        pltpu.VMEM((2, BM, SUBLANES, LANES), jnp.float32),
                        pltpu.SemaphoreType.DMA((2,)), pltpu.SemaphoreType.DMA((2,))],
    )
    return pl.pallas_call(
        functools.partial(_moe_mlp_kernel, BM=BM, T=T),
        grid_spec=grid_spec,
        out_shape=jax.ShapeDtypeStruct((n_blocks * BM, SUBLANES, LANES), jnp.float32),
        compiler_params=pltpu.CompilerParams(dimension_semantics=("arbitrary",),
                                             vmem_limit_bytes=VMEM_LIMIT),
        name="moe_mlp",
    )(block_e, block_base, block_nval, order, x.reshape(T, SUBLANES, LANES), w1, b1.reshape(E, 1, -1), w2, b2.reshape(E, 1, -1))


def _combine_kernel(h_ref, gate_ref, *refs):
    slabs = refs[:TOP_K]
    g2_ref, b2_ref, o_ref = refs[TOP_K:]
    gate = gate_ref[...]
    f = None
    for k in range(TOP_K):
        rows = jnp.concatenate([slabs[k][:, c, :] for c in range(SUBLANES)], axis=1)
        term = gate[:, k:k + 1] * rows
        f = term if f is None else f + term
    z = DN_ALPHA * h_ref[...] + f
    mu = jnp.mean(z, axis=-1, keepdims=True)
    dev = z - mu
    var = jnp.mean(dev * dev, axis=-1, keepdims=True)
    o_ref[...] = dev * lax.rsqrt(var + LN_EPS) * g2_ref[...] + b2_ref[...]


def combine_rows(h, gate, yb, ln2_g, ln2_b):
    T, D = h.shape
    tm = next(t for t in (384, 256, 128, 64, 8) if T % t == 0)
    nb = T // tm
    slab = lambda k: pl.BlockSpec((tm, SUBLANES, LANES), lambda i, k=k: (k * nb + i, 0, 0))
    rows = lambda w: pl.BlockSpec((tm, w), lambda i: (i, 0))
    const = lambda shape: pl.BlockSpec(shape, lambda i: (0,) * len(shape))
    return pl.pallas_call(
        _combine_kernel,
        grid=(nb,),
        in_specs=[rows(D), rows(TOP_K)] + [slab(k) for k in range(TOP_K)] + [const((1, D)), const((1, D))],
        out_specs=rows(D),
        out_shape=jax.ShapeDtypeStruct((T, D), jnp.float32),
        compiler_params=pltpu.CompilerParams(dimension_semantics=("arbitrary",), vmem_limit_bytes=VMEM_LIMIT),
        name="combine_rows",
    )(h, gate, *([yb] * TOP_K), ln2_g.reshape(1, D), ln2_b.reshape(1, D))


def moe(x, logits, mlp1_w, mlp1_b, mlp2_w, mlp2_b, BM=256):
    T, D = x.shape
    top_v, top_e = lax.top_k(logits, TOP_K)
    gate = jax.nn.softmax(top_v, axis=-1)
    n_assign = T * TOP_K
    _, order = lax.sort((top_e.reshape(-1).astype(jnp.int32), jnp.arange(n_assign, dtype=jnp.int32)), num_keys=1)
    counts = jnp.sum(top_e.reshape(-1, 1) == jnp.arange(N_EXPERTS)[None, :], axis=0).astype(jnp.int32)
    padded = (counts + BM - 1) // BM * BM
    start = jnp.cumsum(counts) - counts
    pend = jnp.cumsum(padded)
    pstart = pend - padded
    n_blocks = -(-n_assign // BM) + N_EXPERTS
    row0 = jnp.arange(n_blocks, dtype=jnp.int32) * BM
    block_e = jnp.minimum(jnp.sum(pend[None, :] <= row0[:, None], axis=1), N_EXPERTS - 1).astype(jnp.int32)
    off = row0 - pstart[block_e]
    block_nval = jnp.clip(counts[block_e] - off, 0, BM).astype(jnp.int32)
    block_base = jnp.clip(start[block_e] + off, 0, n_assign).astype(jnp.int32)
    order = jnp.concatenate([order, jnp.zeros((BM,), jnp.int32)])
    yb = moe_mlp(x, order, block_e, block_base, block_nval, mlp1_w, mlp1_b, mlp2_w, mlp2_b, BM)
    return gate, yb


def rope(x, pos):
    half = ROT_DIM // 2
    inv = ROPE_THETA ** (-jnp.arange(half, dtype=jnp.float32) * 2.0 / ROT_DIM)
    ang = pos.astype(jnp.float32)[:, None] * inv
    cos, sin = jnp.cos(ang)[:, None, :], jnp.sin(ang)[:, None, :]
    x1, x2 = x[..., :half], x[..., half:ROT_DIM]
    return jnp.concatenate([x1 * cos - x2 * sin, x2 * cos + x1 * sin, x[..., ROT_DIM:]], axis=-1)


G0 = R_COLS + A_WIDTH + 6 * KV_WIDTH


def project(x2, w_in):
    w_main = jnp.concatenate([w_in[:, :G0], w_in[:, G0 + N_GATE:]], axis=1)
    w_gate = jnp.pad(w_in[:, G0:G0 + N_GATE], ((0, 0), (0, LANES - N_GATE)))
    return pallas_matmul(x2, w_main), pallas_matmul(x2, w_gate)[:, :N_GATE]


def split_projection(p, pg, B, T, pos):
    pr = p[:, :R_COLS].reshape(B, T, R_COLS)
    pa = p[:, R_COLS:G0].reshape(B, T, G0 - R_COLS)
    q = rope(pa[..., :A_WIDTH].reshape(B, T, N_HEADS, HEAD_DIM), pos)
    kvs = [pa[..., A_WIDTH + i * KV_WIDTH:A_WIDTH + (i + 1) * KV_WIDTH].reshape(B, T, N_KV, HEAD_DIM)
           for i in range(6)]
    kvs = [rope(z, pos) if i % 2 == 0 else z for i, z in enumerate(kvs)]
    gates = jax.nn.sigmoid(pg).reshape(B, T, N_KV, HPG, 3)
    return pr, q, kvs, gates


def rwkv_prep_rows(pr, shift_prev, mu, w0, w_w2, a0, w_a2, g_w2, k_k, k_a, r_k):
    B, T, _ = pr.shape
    prev = jnp.concatenate([shift_prev[:, None, :], pr[:, :-1]], axis=1)
    xm = pr + (prev - pr) * mu
    o1, o2, o3 = R_WIDTH, 2 * R_WIDTH, 3 * R_WIDTH
    o4 = o3 + LORA_W
    o5 = o4 + LORA_A
    r, k, v = xm[..., :o1], xm[..., o1:o2], xm[..., o2:o3]
    xw, xa, xg = xm[..., o3:o4], xm[..., o4:o5], xm[..., o5:]
    w_log = -jax.nn.softplus(-(w0 + jnp.tanh(xw) @ w_w2)) - 0.5
    a = jax.nn.sigmoid(a0 + xa @ w_a2)
    g = jax.nn.sigmoid(xg) @ g_w2
    heads = lambda z: z.reshape(B, T, R_HEADS, R_HEAD)
    flat = lambda z: z.reshape(B, T, R_WIDTH)
    kk = heads(k * k_k)
    kk = flat(kk / jnp.maximum(jnp.linalg.norm(kk, axis=-1, keepdims=True), 1e-12))
    k_h = k * (1.0 + (a - 1.0) * k_a)
    bonus = flat(jnp.sum(heads(r * k_h * r_k.reshape(-1)), axis=-1, keepdims=True) * heads(v))
    return r, -jnp.exp(w_log), k_h, v, -kk, kk * a, bonus, g


def rwkv_mixer(parts, wkv0, gn_g, gn_b):
    B, T, _ = parts[0].shape
    C = 64 if T % 64 == 0 else 8
    Tp = -(-T // C) * C
    padded = [jnp.pad(z, ((0, 0), (0, Tp - T), (0, 0))) for z in parts]
    y, wkv = wkv_chunked(*padded, gn_g, gn_b, wkv0, C)
    return y[:, :T], wkv


def compress(kv, pe, w1, b1, w2, b2):
    B, L = kv.shape[:2]
    n_chunk = L // CMP_STRIDE
    ch = kv[:, :n_chunk * CMP_STRIDE].reshape(B, n_chunk, CMP_STRIDE, N_KV, HEAD_DIM)
    blk = jnp.concatenate([ch[:, :-1], ch[:, 1:]], axis=2) + pe[:, None, :]
    blk = jnp.transpose(blk, (0, 3, 1, 2, 4)).reshape(B, N_KV, n_chunk - 1, CMP_LEN * HEAD_DIM)
    return jax.nn.gelu(blk @ w1 + b1) @ w2 + b2


def nsa_prompt(q, kvs, gates, pe, w1, b1, w2, b2):
    kc_raw, vc_raw, ks, vs, kw, vw = kvs
    B, T = q.shape[:2]
    kc = compress(kc_raw, pe[0], w1[0], b1[0], w2[0], b2[0])
    vc = compress(vc_raw, pe[1], w1[1], b1[1], w2[1], b2[1])
    qg = _bf(jnp.transpose(q.reshape(B, T, N_KV, HPG, HEAD_DIM), (0, 2, 3, 1, 4)) * (HEAD_DIM ** -0.5 * LOG2E))
    tk = lambda z: _bf(jnp.swapaxes(z, 1, 2))
    o = nsa_prompt_attention(qg, kc, vc, tk(ks), tk(vs), tk(kw), tk(vw),
                             jnp.transpose(gates, (0, 2, 3, 1, 4)))
    return jnp.transpose(o, (0, 3, 1, 2, 4)).reshape(B, T, A_WIDTH)


def nsa_sample(q, kvs, gates, cmp_cache, sel_cache, win_cache, page_table, pe, w1, b1, w2, b2):
    kc_new, vc_new, ks_new, vs_new, kw_new, vw_new = kvs
    DB, DS = q.shape[:2]
    past = page_table.shape[1] * PAGE_SIZE
    kvc = compress_sample(cmp_cache, page_table, pe, w1, b1, w2, b2)
    qg = _bf(jnp.transpose(q.reshape(DB, DS, N_KV, HPG, HEAD_DIM), (0, 2, 3, 1, 4)) * (HEAD_DIM ** -0.5))
    qg = qg.reshape(DB, N_KV, HPG * DS, HEAD_DIM)
    gt = jnp.transpose(gates, (0, 2, 3, 1, 4)).reshape(DB, N_KV, HPG * DS, 3)
    pack = lambda k, v: _bf(jnp.pad(jnp.transpose(jnp.stack([k, v], axis=1), (0, 1, 3, 2, 4)),
                                    ((0, 0), (0, 0), (0, 0), (0, NEW_PAD - DS), (0, 0))))
    o = nsa_sample_attention(qg, kvc, sel_cache, win_cache, page_table, pack(ks_new, vs_new),
                             pack(kw_new, vw_new), gt, past)
    o = jnp.transpose(o.reshape(DB, N_KV, HPG, DS, HEAD_DIM), (0, 3, 1, 2, 4)).reshape(DB, DS, -1)
    new_k = jnp.concatenate([win_cache[:, DS:, 0], kw_new], axis=1)
    new_v = jnp.concatenate([win_cache[:, DS:, 1], vw_new], axis=1)
    return o, jnp.stack([new_k, new_v], axis=2)


def merge_and_ffn(x, y_r, y_a, p_all, w_pa, w_pb, w_o, ln1_g, ln1_b, router_w, router_b,
                  mlp1_w, mlp1_b, mlp2_w, mlp2_b, ln2_g, ln2_b):
    h, logits = merge_rows(x, y_r, y_a, p_all, G0, w_pa, w_pb, w_o, ln1_g, ln1_b, router_w)
    gate, yb = moe(h, logits[:, :N_EXPERTS] + router_b, mlp1_w, mlp1_b, mlp2_w, mlp2_b)
    return combine_rows(h, gate, yb, ln2_g, ln2_b)


def kernel(x_prompt, x_sample, cache_cmp_kv, cache_sel_kv, cache_win_kv, state_wkv, state_shift,
           page_table, w_in, mu_shift, w0, w_w2, a0, w_a2, g_w2, k_k, k_a, r_k, gn_g, gn_b,
           cmp_pe, cmp_w1, cmp_b1, cmp_w2, cmp_b2, w_pa, w_pb, w_o, ln1_g, ln1_b,
           router_w, router_b, mlp1_w, mlp1_b, mlp2_w, mlp2_b, ln2_g, ln2_b):
    B, T, D = x_prompt.shape
    DB, DS, _ = x_sample.shape
    n_p = B * T
    past = page_table.shape[1] * PAGE_SIZE
    pos_p = jnp.arange(T)
    pos_s = past + jnp.arange(DS)
    wb_p = min(WINDOW, T)
    h_all = jnp.concatenate([x_prompt.reshape(n_p, D), x_sample.reshape(DB * DS, D)])
    cmp_p, sel_p, win_p, wkv_p, shift_p = [], [], [], [], []
    cmp_s, sel_s, win_s, wkv_s, shift_s = [], [], [], [], []
    for l in range(DEPTH):
        rwkv_w = (mu_shift[l], w0[l], w_w2[l], a0[l], w_a2[l], g_w2[l], k_k[l], k_a[l], r_k[l],
                  gn_g[l], gn_b[l])
        cmp_w = (cmp_pe[l], cmp_w1[l], cmp_b1[l], cmp_w2[l], cmp_b2[l])
        out_w = (w_pa[l], w_pb[l], w_o[l], ln1_g[l], ln1_b[l], router_w[l], router_b[l],
                 mlp1_w[l], mlp1_b[l], mlp2_w[l], mlp2_b[l], ln2_g[l], ln2_b[l])
        p_all, pg_all = project(h_all, w_in[l])
        pr, q, kvs, gates = split_projection(p_all[:n_p], pg_all[:n_p], B, T, pos_p)
        parts = rwkv_prep(p_all, n_p, T, jnp.zeros((B, R_COLS), jnp.float32), *rwkv_w[:9])
        y_r, wkv = rwkv_mixer([z.reshape(B, T, R_WIDTH) for z in parts],
                              jnp.zeros((B, R_HEADS, R_HEAD, R_HEAD), jnp.float32), *rwkv_w[9:])
        shift = p_all[T - 1:n_p:T, :R_COLS]
        y_a = nsa_prompt(q, kvs, gates, *cmp_w)
        cmp_p.append(jnp.stack([kvs[0], kvs[1]], axis=2))
        sel_p.append(jnp.stack([kvs[2], kvs[3]], axis=2))
        win_p.append(jnp.stack([kvs[4][:, T - wb_p:], kvs[5][:, T - wb_p:]], axis=2))
        wkv_p.append(wkv)
        shift_p.append(shift)
        pr, q, kvs, gates = split_projection(p_all[n_p:], pg_all[n_p:], DB, DS, pos_s)
        y_r_s, wkv = rwkv_mixer(rwkv_prep_rows(pr, state_shift[l], *rwkv_w[:9]), state_wkv[l], *rwkv_w[9:])
        shift = pr[:, -1]
        y_a_s, new_win = nsa_sample(q, kvs, gates, cache_cmp_kv[l], cache_sel_kv[l], cache_win_kv[l],
                                    page_table, *cmp_w)
        cmp_s.append(jnp.stack([kvs[0], kvs[1]], axis=2))
        sel_s.append(jnp.stack([kvs[2], kvs[3]], axis=2))
        win_s.append(new_win)
        wkv_s.append(wkv)
        shift_s.append(shift)
        rows = lambda a, b: jnp.concatenate([a.reshape(n_p, -1), b.reshape(DB * DS, -1)])
        h_all = merge_and_ffn(h_all, rows(y_r, y_r_s), rows(y_a, y_a_s), p_all, *out_w)
    hp = h_all[:n_p].reshape(B, T, D)
    hs = h_all[n_p:].reshape(DB, DS, D)
    return (hp, hs, jnp.stack(cmp_p), jnp.stack(sel_p), jnp.stack(win_p), jnp.stack(wkv_p),
            jnp.stack(shift_p), jnp.stack(cmp_s), jnp.stack(sel_s), jnp.stack(win_s),
            jnp.stack(wkv_s), jnp.stack(shift_s))
```

```python
import functools

import jax
import jax.numpy as jnp
from jax import lax
from jax.experimental import pallas as pl
from jax.experimental.pallas import tpu as pltpu

D_MODEL = 1024
DEPTH = 1
PAGE_SIZE = 128

R_HEADS = 8
R_HEAD = 64
R_WIDTH = R_HEADS * R_HEAD
LORA_W = 64
LORA_A = 64
LORA_G = 128
R_COLS = 3 * R_WIDTH + LORA_W + LORA_A + LORA_G
GN_EPS = 64e-5

N_HEADS = 8
N_KV = 2
HPG = N_HEADS // N_KV
HEAD_DIM = 64
A_WIDTH = N_HEADS * HEAD_DIM
KV_WIDTH = N_KV * HEAD_DIM
N_GATE = 3 * N_HEADS
A_COLS = A_WIDTH + 6 * KV_WIDTH + N_GATE
ROT_DIM = HEAD_DIM // 4
ROPE_THETA = 500000.0
CMP_STRIDE = 16
CMP_LEN = 2 * CMP_STRIDE
CMP_HIDDEN = 256
SEL_BLOCK = 64
SEL_TOP = 16
WINDOW = 512

N_EXPERTS = 32
TOP_K = 4
K_SHIFT = 2
D_FF = 1024
SWIGLU_LIMIT = 7.0
SWIGLU_ALPHA = 1.702

DN_ALPHA = (2 * DEPTH) ** 0.25
LN_EPS = 1e-5
NEG = -1e30
LOG2E = 1.4426950408889634

LANES = 128
SUBLANES = 8
PAGES_PER_STEP = 16
NEW_PAD = 8
VMEM_LIMIT = 56 * 1024 * 1024


def _bf(x):
    return x.astype(jnp.bfloat16)


def _dot(a, b):
    return jnp.dot(_bf(a), _bf(b), preferred_element_type=jnp.float32)


def _dot_nt(a, b):
    return lax.dot_general(_bf(a), _bf(b), (((1,), (1,)), ((), ())),
                           preferred_element_type=jnp.float32)


def _mm_kernel(x_ref, w_ref, o_ref):
    o_ref[...] = _dot(x_ref[...], w_ref[...])


def pallas_matmul(x, w):
    x, w = _bf(x), _bf(w)
    M, K = x.shape
    N = w.shape[1]
    tm = next(t for t in (512, 384, 256, 128, M) if M % t == 0)
    resident = 2 * (2 * K * N + 4 * tm * N + 2 * tm * K) <= VMEM_LIMIT - (8 << 20)
    tn = N if resident else next(t for t in (512, 256, LANES) if N % t == 0)
    return pl.pallas_call(
        _mm_kernel,
        grid=(N // tn, M // tm),
        in_specs=[pl.BlockSpec((tm, K), lambda j, i: (i, 0)),
                  pl.BlockSpec((K, tn), lambda j, i: (0, j))],
        out_specs=pl.BlockSpec((tm, tn), lambda j, i: (i, j)),
        out_shape=jax.ShapeDtypeStruct((M, N), jnp.float32),
        compiler_params=pltpu.CompilerParams(vmem_limit_bytes=VMEM_LIMIT),
        name="mm",
    )(x, w)


def _bmm(a, b):
    return lax.dot_general(_bf(a), _bf(b), (((2,), (1,)), ((0,), (0,))), preferred_element_type=jnp.float32)


def _bmm_nt(a, b):
    return lax.dot_general(_bf(a), _bf(b), (((2,), (2,)), ((0,), (0,))), preferred_element_type=jnp.float32)


def _bmm_tn(a, b):
    return lax.dot_general(_bf(a), _bf(b), (((1,), (1,)), ((0,), (0,))), preferred_element_type=jnp.float32)


def _head_sum(x, ones_bd):
    hi = _bf(x)
    rem = x - hi.astype(jnp.float32)
    mid = _bf(rem)
    lo = _bf(rem - mid.astype(jnp.float32))
    dot = lambda t: jnp.dot(t, ones_bd, preferred_element_type=jnp.float32)
    return dot(hi) + dot(mid) + dot(lo)


def head_ones():
    h = jnp.arange(R_WIDTH) // R_HEAD
    return _bf(h[:, None] == h[None, :])


def _rwkv_prep_kernel(p_ref, pb_ref, sp_ref, mu_ref, w0_ref, ww2_ref, a0_ref, wa2_ref, gw2_ref,
                      kk_ref, ka_ref, rk_ref, bd_ref,
                      r_ref, lw_ref, k_ref, v_ref, a_ref, b_ref, bonus_ref, g_ref, *, tiles_per_seq):
    i = pl.program_id(0)
    pr = p_ref[...]
    first = jnp.where(i % tiles_per_seq == 0, sp_ref[0], pb_ref[SUBLANES - 1:SUBLANES, :])
    rolled = pltpu.roll(pr, shift=1, axis=0)
    prev = jnp.where(lax.broadcasted_iota(jnp.int32, (pr.shape[0], 1), 0) == 0, first, rolled)
    xm = pr + (prev - pr) * mu_ref[...]
    o1, o2, o3 = R_WIDTH, 2 * R_WIDTH, 3 * R_WIDTH
    o4 = o3 + LORA_W
    o5 = o4 + LORA_A
    r, k, v = xm[:, :o1], xm[:, o1:o2], xm[:, o2:o3]
    xw, xa, xg = xm[:, o3:o4], xm[:, o4:o5], xm[:, o5:]
    dot = lambda x, w_ref: jnp.dot(_bf(x), _bf(w_ref[...]), preferred_element_type=jnp.float32)
    w_log = -jax.nn.softplus(-(w0_ref[...] + dot(jnp.tanh(xw), ww2_ref))) - 0.5
    a = jax.nn.sigmoid(a0_ref[...] + dot(xa, wa2_ref))
    g_ref[...] = dot(jax.nn.sigmoid(xg), gw2_ref)
    ones_bd = bd_ref[...]
    kk = k * kk_ref[...]
    kk = kk / jnp.maximum(jnp.sqrt(_head_sum(kk * kk, ones_bd)), 1e-12)
    k_h = k * (1.0 + (a - 1.0) * ka_ref[...])
    r_ref[...] = r
    lw_ref[...] = -jnp.exp(w_log)
    k_ref[...] = k_h
    v_ref[...] = v
    a_ref[...] = -kk
    b_ref[...] = kk * a
    bonus_ref[...] = _head_sum(r * k_h * rk_ref[...], ones_bd) * v


def rwkv_prep(p_all, n_rows, seq_len, shift_prev, mu, w0, w_w2, a0, w_a2, g_w2, k_k, k_a, r_k, tm=256):
    assert seq_len % tm == 0 and n_rows % seq_len == 0
    n_seq = n_rows // seq_len
    tiles_per_seq = seq_len // tm
    row = lambda z: z.reshape(1, -1)
    const = lambda shape: pl.BlockSpec(shape, lambda i: (0,) * len(shape))
    out = pl.BlockSpec((tm, R_WIDTH), lambda i: (i, 0))
    outs = pl.pallas_call(
        functools.partial(_rwkv_prep_kernel, tiles_per_seq=tiles_per_seq),
        grid=(n_rows // tm,),
        in_specs=[pl.BlockSpec((tm, R_COLS), lambda i: (i, 0)),
                  pl.BlockSpec((SUBLANES, R_COLS), lambda i: (jnp.maximum(i * (tm // SUBLANES) - 1, 0), 0)),
                  pl.BlockSpec((1, 1, R_COLS), lambda i: (i // tiles_per_seq, 0, 0)),
                  const((1, R_COLS)), const((1, R_WIDTH)), const((LORA_W, R_WIDTH)), const((1, R_WIDTH)),
                  const((LORA_A, R_WIDTH)), const((LORA_G, R_WIDTH)), const((1, R_WIDTH)), const((1, R_WIDTH)),
                  const((1, R_WIDTH)), const((R_WIDTH, R_WIDTH))],
        out_specs=[out] * 8,
        out_shape=[jax.ShapeDtypeStruct((n_rows, R_WIDTH), jnp.float32)] * 8,
        compiler_params=pltpu.CompilerParams(dimension_semantics=("arbitrary",), vmem_limit_bytes=VMEM_LIMIT),
        name="rwkv_prep",
    )(p_all, p_all, shift_prev.reshape(n_seq, 1, R_COLS), row(mu), row(w0), w_w2, row(a0), w_a2, g_w2,
      row(k_k), row(k_a), row(r_k), head_ones())
    return outs


def _wkv_chunk_kernel(r_ref, lw_ref, k_ref, v_ref, a_ref, b_ref, bonus_ref, g_ref, gng_ref, gnb_ref, bd_ref,
                      s0_ref, y_ref, s_out_ref, s_scr, *, C, H):
    c = pl.program_id(1)
    D = R_HEAD

    @pl.when(c == 0)
    def _():
        s_scr[...] = s0_ref[0]

    row = lax.broadcasted_iota(jnp.int32, (H, C, C), 1)
    col = lax.broadcasted_iota(jnp.int32, (H, C, C), 2)
    incl = row >= col
    strict = row > col
    ltri = jnp.where(incl, 1.0, 0.0).astype(jnp.bfloat16)
    eye = jnp.where(row == col, 1.0, 0.0).astype(jnp.float32)

    heads = lambda ref: jnp.stack([ref[0, :, h * D:(h + 1) * D] for h in range(H)])
    lw = heads(lw_ref)
    r = heads(r_ref)
    k = heads(k_ref)
    v = heads(v_ref)
    a = heads(a_ref)
    b = heads(b_ref)
    hi = _bf(lw)
    rem = lw - hi.astype(jnp.float32)
    mid = _bf(rem)
    lo = _bf(rem - mid.astype(jnp.float32))
    lp = _bmm(ltri, hi) + _bmm(ltri, mid) + _bmm(ltri, lo)
    lp_end = lp[:, C - 1:C, :]
    p_end = jnp.exp(lp_end)
    p_inv = jnp.exp(-lp)
    at = a * jnp.exp(lp - lw)
    rt = r * jnp.exp(lp)
    bt = b * p_inv
    kt = k * p_inv
    p_hat = jnp.exp(lp_end - lp)
    bh = b * p_hat
    kh = k * p_hat

    n_ab = jnp.where(strict, _bmm_nt(at, bt), 0.0)
    a_ak = jnp.where(strict, _bmm_nt(at, kt), 0.0)
    a_rb = jnp.where(incl, _bmm_nt(rt, bt), 0.0)
    a_rk = jnp.where(incl, _bmm_nt(rt, kt), 0.0)

    t_inv = eye + n_ab
    n_pow = n_ab
    span = 2
    while span < C:
        n_pow = _bmm(n_pow, n_pow)
        t_inv = _bmm(t_inv, eye + n_pow)
        span *= 2

    s = s_scr[...]
    rhs = _bmm_nt(at, s) + _bmm(a_ak, v)
    u = _bmm(t_inv, rhs)
    y = _bmm_nt(rt, s) + _bmm(a_rb, u) + _bmm(a_rk, v)
    s_new = s * p_end + _bmm_tn(u, bh) + _bmm_tn(v, kh)
    s_scr[...] = s_new

    yt = jnp.concatenate([y[h] for h in range(H)], axis=1)
    ones_bd = bd_ref[...]
    dev = yt - _head_sum(yt, ones_bd) * (1.0 / D)
    var = _head_sum(dev * dev, ones_bd) * (1.0 / D)
    yn = dev * lax.rsqrt(var + GN_EPS) * gng_ref[...] + gnb_ref[...]
    y_ref[0] = (yn + bonus_ref[0]) * g_ref[0]

    @pl.when(c == pl.num_programs(1) - 1)
    def _():
        s_out_ref[0] = s_new


def wkv_chunked(r, lw, k, v, a, b, bonus, g, gn_g, gn_b, s0, C):
    B, T, W = r.shape
    H, D = R_HEADS, R_HEAD
    assert T % C == 0
    seq = pl.BlockSpec((1, C, W), lambda bi, ci: (bi, ci, 0))
    st = pl.BlockSpec((1, H, D, D), lambda bi, ci: (bi, 0, 0, 0))
    const = lambda shape: pl.BlockSpec(shape, lambda bi, ci: (0,) * len(shape))
    return pl.pallas_call(
        functools.partial(_wkv_chunk_kernel, C=C, H=H),
        grid=(B, T // C),
        in_specs=[seq] * 8 + [const((1, W)), const((1, W)), const((W, W)), st],
        out_specs=[seq, st],
        out_shape=[jax.ShapeDtypeStruct((B, T, W), jnp.float32),
                   jax.ShapeDtypeStruct((B, H, D, D), jnp.float32)],
        scratch_shapes=[pltpu.VMEM((H, D, D), jnp.float32)],
        compiler_params=pltpu.CompilerParams(dimension_semantics=("arbitrary", "arbitrary")),
        name="wkv_chunk",
    )(r, lw, k, v, a, b, bonus, g, gn_g.reshape(1, W), gn_b.reshape(1, W), head_ones(), s0)


def _nsa_prompt_kernel(q_ref, kc_ref, vc_ref, cover_ref, ks_ref, vs_ref, kw_ref, vw_ref, g_ref, o_ref,
                       *, TQ, TK, NC, NCP):
    f32 = jnp.float32
    bf16 = jnp.bfloat16
    qb = pl.program_id(2)
    R = HPG * TQ
    q = q_ref[0, 0].reshape(R, HEAD_DIM)
    t_pos = qb * TQ + lax.broadcasted_iota(jnp.int32, (TQ, 1), 0)

    n_idx = lax.broadcasted_iota(jnp.int32, (1, NCP), 1)
    c_ok = ((n_idx * CMP_STRIDE + (CMP_LEN - 1)) <= t_pos) & (n_idx < NC)
    s_c = _dot_nt(q, kc_ref[0, 0]).reshape(HPG, TQ, NCP)
    s_c = jnp.where(c_ok[None], s_c, NEG)
    m_c = jnp.max(s_c, axis=-1, keepdims=True)
    p_c = jnp.where(c_ok[None], jnp.exp2(s_c - m_c), 0.0)
    l_c = jnp.sum(p_c, axis=-1, keepdims=True)
    p_c = p_c / jnp.where(l_c > 0.0, l_c, 1.0)
    p_cb = p_c.astype(bf16)
    o_c = jnp.dot(p_cb.reshape(R, NCP), vc_ref[0, 0], preferred_element_type=f32)

    cover_t = cover_ref[...]
    imp = _dot_nt(cover_t, p_cb[0])
    for h in range(1, HPG):
        imp = imp + _dot_nt(cover_t, p_cb[h])
    s_col = lax.broadcasted_iota(jnp.int32, (LANES, 1), 0)
    t_row = qb * TQ + lax.broadcasted_iota(jnp.int32, (1, TQ), 1)
    cur = t_row // SEL_BLOCK
    forced = (s_col == 0) | (s_col == cur) | (s_col == cur - 1)
    causal = (s_col * SEL_BLOCK) <= t_row
    score = jnp.where(forced, 1e6, imp)
    score = jnp.where(causal, score, NEG)
    s_col_f = s_col.astype(f32)
    sel_t = jnp.zeros((LANES, TQ), f32)
    for _ in range(SEL_TOP):
        top = jnp.max(score, axis=0, keepdims=True)
        first = jnp.min(jnp.where(score == top, s_col_f, float(LANES)), axis=0, keepdims=True)
        hit = s_col_f == first
        sel_t = jnp.where(hit & (top > 0.5 * NEG), 1.0, sel_t)
        score = jnp.where(hit, -3e38, score)
    sel_neg = ((sel_t.T - 1.0) * (-NEG)).astype(bf16)
    q_aug = jnp.concatenate([jnp.concatenate([sel_neg] * HPG, axis=0), q], axis=1)
    key_off = lax.broadcasted_iota(jnp.int32, (1, TK), 1)

    def sel_tile(j, carry, diagonal):
        m, acc = carry
        start = pl.multiple_of(j * TK, TK)
        k = ks_ref[0, 0, pl.ds(start, TK), :]
        v = vs_ref[0, 0, pl.ds(start, TK), :]
        s = _dot_nt(q_aug, k).reshape(HPG, TQ, TK)
        if diagonal:
            s = jnp.where(((key_off + j * TK) <= t_pos)[None], s, NEG)
        m_new = jnp.maximum(m, jnp.max(s, axis=-1, keepdims=True))
        p = jnp.exp2((s - m_new).astype(bf16))
        alpha = jnp.exp2(m - m_new)
        pv = jnp.dot(p.reshape(R, TK), v, preferred_element_type=f32)
        acc = alpha * acc + pv.reshape(HPG, TQ, LANES)
        return m_new, acc

    m0 = jnp.full((HPG, TQ, 1), NEG, f32)
    a0 = jnp.zeros((HPG, TQ, LANES), f32)
    n_full = (qb * TQ) // TK
    carry = lax.fori_loop(0, n_full, functools.partial(sel_tile, diagonal=False), (m0, a0))
    _, acc_s = sel_tile(n_full, carry, True)
    o_s = acc_s[:, :, :HEAD_DIM] / acc_s[:, :, HEAD_DIM:HEAD_DIM + 1]

    n_w = WINDOW // TQ + 1
    lane_q = lax.broadcasted_iota(jnp.int32, (1, TQ), 1)
    s_w, v_w = [], []
    for i in range(n_w):
        kb = qb - (n_w - 1) + i
        start = pl.multiple_of(jnp.maximum(kb, 0) * TQ, TQ)
        k = kw_ref[0, 0, pl.ds(start, TQ), :]
        v_w.append(vw_ref[0, 0, pl.ds(start, TQ), :])
        kpos = kb * TQ + lane_q
        ok = (kpos <= t_pos) & (kpos >= t_pos - WINDOW) & (kpos >= 0)
        s_w.append(jnp.where(ok[None], _dot_nt(q, k).reshape(HPG, TQ, TQ), NEG))
    m_w = s_w[0].max(axis=-1, keepdims=True)
    for i in range(1, n_w):
        m_w = jnp.maximum(m_w, s_w[i].max(axis=-1, keepdims=True))
    acc_w = jnp.zeros((R, LANES), f32)
    for i in range(n_w):
        p = jnp.exp2((s_w[i] - m_w).astype(bf16))
        acc_w = acc_w + jnp.dot(p.reshape(R, TQ), v_w[i], preferred_element_type=f32)
    acc_w = acc_w.reshape(HPG, TQ, LANES)
    o_w = acc_w[:, :, :HEAD_DIM] / acc_w[:, :, HEAD_DIM:HEAD_DIM + 1]

    g = g_ref[0, 0]
    o_ref[0, 0] = (g[:, :, 0:1] * o_c.reshape(HPG, TQ, HEAD_DIM) + g[:, :, 1:2] * o_s
                   + g[:, :, 2:3] * o_w)


def nsa_prompt_attention(q, kc, vc, ks, vs, kw, vw, gates, TQ=256, TK=1024):
    B, G, _, T, D = q.shape
    NC = kc.shape[2]
    NCP = -(-NC // LANES) * LANES
    NS = T // SEL_BLOCK
    assert NS <= LANES and T % TK == 0 and TK % TQ == 0 and WINDOW % TQ == 0
    kc = jnp.pad(kc, ((0, 0), (0, 0), (0, NCP - NC), (0, 0))).astype(jnp.bfloat16)
    vc = jnp.pad(vc, ((0, 0), (0, 0), (0, NCP - NC), (0, 0))).astype(jnp.bfloat16)
    c_start = jnp.arange(NCP) * CMP_STRIDE
    s_start = jnp.arange(LANES) * SEL_BLOCK
    cover = ((c_start[:, None] < s_start[None, :] + SEL_BLOCK)
             & (c_start[:, None] + CMP_LEN > s_start[None, :])
             & (jnp.arange(NCP)[:, None] < NC) & (jnp.arange(LANES)[None, :] < NS)).astype(jnp.bfloat16)
    onehot = (jnp.arange(T)[:, None] // SEL_BLOCK == jnp.arange(LANES)[None, :]).astype(jnp.bfloat16)
    ks = jnp.concatenate([jnp.broadcast_to(onehot, (B, G, T, LANES)), ks], axis=-1)
    ones_col = (jnp.arange(LANES - D) == 0).astype(jnp.bfloat16)
    with_ones = lambda v: jnp.concatenate([v, jnp.broadcast_to(ones_col, (B, G, T, LANES - D))], axis=-1)
    vs, vw = with_ones(vs), with_ones(vw)
    full = lambda n, d=D: pl.BlockSpec((1, 1, n, d), lambda b, g, i: (b, g, 0, 0))
    qspec = pl.BlockSpec((1, 1, HPG, TQ, D), lambda b, g, i: (b, g, 0, i, 0))
    return pl.pallas_call(
        functools.partial(_nsa_prompt_kernel, TQ=TQ, TK=TK, NC=NC, NCP=NCP),
        grid=(B, G, T // TQ),
        in_specs=[qspec, full(NCP), full(NCP),
                  pl.BlockSpec((LANES, NCP), lambda b, g, i: (0, 0)),
                  full(T, LANES + D), full(T, LANES), full(T), full(T, LANES),
                  pl.BlockSpec((1, 1, HPG, TQ, 3), lambda b, g, i: (b, g, 0, i, 0))],
        out_specs=qspec,
        out_shape=jax.ShapeDtypeStruct((B, G, HPG, T, D), jnp.float32),
        compiler_params=pltpu.CompilerParams(
            dimension_semantics=("arbitrary", "arbitrary", "arbitrary"),
            vmem_limit_bytes=VMEM_LIMIT),
        name="nsa_prompt",
    )(q, kc, vc, cover.T, ks, vs, kw, vw, gates)


def _cmp_sample_kernel(pt_ref, *refs, n_chunk):
    pages = refs[:PAGES_PER_STEP]
    wcat_ref, c1_ref, w2_ref, b2_ref, o_ref, seq = refs[PAGES_PER_STEP:]
    j = pl.program_id(1)
    rows = PAGE_SIZE // CMP_STRIDE
    src = lax.broadcasted_iota(jnp.int32, (PAGE_SIZE, PAGE_SIZE), 0)
    src = (src % rows) * CMP_STRIDE + src // rows
    perm = _bf(jnp.where(src == lax.broadcasted_iota(jnp.int32, (PAGE_SIZE, PAGE_SIZE), 1), 1.0, 0.0))
    for i in range(PAGES_PER_STEP):
        dst = pl.multiple_of((j * PAGES_PER_STEP + i) * rows, rows)
        for kv in range(2):
            for g in range(N_KV):
                q = kv * N_KV + g
                by_pos = _dot_nt(perm, pages[i][0, kv, g])
                for p in range(CMP_STRIDE):
                    seq[q, pl.ds(dst, rows), p * HEAD_DIM:(p + 1) * HEAD_DIM] = by_pos[p * rows:(p + 1) * rows]

    @pl.when(j == pl.num_programs(1) - 1)
    def _():
        for kv in range(2):
            for g in range(N_KV):
                q = kv * N_KV + g
                acc = jnp.dot(_bf(seq[q]), wcat_ref[kv], preferred_element_type=jnp.float32)
                first = acc[:, :CMP_HIDDEN]
                second = acc[:, CMP_HIDDEN:]
                second = jnp.concatenate([second[1:], second[:1]], axis=0)
                hid = jax.nn.gelu(first + second + c1_ref[kv])
                out = jnp.dot(_bf(hid), w2_ref[kv], preferred_element_type=jnp.float32) + b2_ref[kv]
                o_ref[0, kv, g] = _bf(out)


def compress_sample(cache, page_table, pe, w1, b1, w2, b2):
    n_pool = cache.shape[0]
    DB, n_pages = page_table.shape
    assert n_pages % PAGES_PER_STEP == 0
    rows = PAGE_SIZE // CMP_STRIDE
    n_chunk = n_pages * rows
    view = jnp.transpose(cache, (0, 2, 3, 4, 1))
    w1r = w1.reshape(2, CMP_LEN, HEAD_DIM, CMP_HIDDEN)
    wcat = _bf(jnp.concatenate([w1r[:, :CMP_STRIDE], w1r[:, CMP_STRIDE:]], axis=-1))
    wcat = wcat.reshape(2, CMP_STRIDE * HEAD_DIM, 2 * CMP_HIDDEN)
    c1 = (jnp.einsum('kn,knh->kh', pe.reshape(2, CMP_LEN * HEAD_DIM), w1) + b1).reshape(2, 1, CMP_HIDDEN)
    page_spec = lambda i: pl.BlockSpec((1, 2, N_KV, HEAD_DIM, PAGE_SIZE),
                                       lambda b, j, pt: (pt[b, j * PAGES_PER_STEP + i], 0, 0, 0, 0))
    const = lambda shape: pl.BlockSpec(shape, lambda b, j, pt: (0,) * len(shape))
    grid_spec = pltpu.PrefetchScalarGridSpec(
        num_scalar_prefetch=1,
        grid=(DB, n_pages // PAGES_PER_STEP),
        in_specs=[page_spec(i) for i in range(PAGES_PER_STEP)]
        + [const((2, CMP_STRIDE * HEAD_DIM, 2 * CMP_HIDDEN)), const((2, 1, CMP_HIDDEN)),
           const((2, CMP_HIDDEN, HEAD_DIM)), const((2, 1, HEAD_DIM))],
        out_specs=pl.BlockSpec((1, 2, N_KV, n_chunk, HEAD_DIM), lambda b, j, pt: (b, 0, 0, 0, 0)),
        scratch_shapes=[pltpu.VMEM((2 * N_KV, n_chunk, CMP_STRIDE * HEAD_DIM), jnp.float32)],
    )
    return pl.pallas_call(
        functools.partial(_cmp_sample_kernel, n_chunk=n_chunk),
        grid_spec=grid_spec,
        out_shape=jax.ShapeDtypeStruct((DB, 2, N_KV, n_chunk, HEAD_DIM), jnp.bfloat16),
        compiler_params=pltpu.CompilerParams(dimension_semantics=("arbitrary", "arbitrary"),
                                             vmem_limit_bytes=VMEM_LIMIT),
        name="cmp_sample",
    )(page_table, *([view] * PAGES_PER_STEP), wcat, c1, _bf(w2), b2.reshape(2, 1, HEAD_DIM))


def _nsa_sample_kernel(pt_ref, *refs, DS, NC, past):
    pages = refs[:PAGES_PER_STEP]
    (q_ref, kvc_ref, cover_ref, new_sel_ref, win_ref, new_win_ref, g_ref, o_ref,
     qaug, m_s, l_s, acc_s, oc_s) = refs[PAGES_PER_STEP:]
    f32 = jnp.float32
    j = pl.program_id(1)
    R = HPG * DS
    t_row = lax.broadcasted_iota(jnp.int32, (R, 1), 0) % DS
    n_chunk = kvc_ref.shape[3]

    @pl.when(j == 0)
    def _():
        n_idx = lax.broadcasted_iota(jnp.int32, (1, n_chunk), 1)
        s_col = lax.broadcasted_iota(jnp.int32, (LANES, 1), 0)
        s_col_f = s_col.astype(f32)
        last_blk = past // SEL_BLOCK - 1
        for g in range(N_KV):
            q = q_ref[0, g]
            s_c = jnp.where(n_idx < NC, _dot_nt(q, kvc_ref[0, 0, g]), NEG)
            p_c = jnp.exp(s_c - jnp.max(s_c, axis=-1, keepdims=True))
            p_c = p_c / jnp.sum(p_c, axis=-1, keepdims=True)
            p_cb = _bf(p_c)
            oc_s[g] = jnp.dot(p_cb, kvc_ref[0, 1, g], preferred_element_type=f32)
            imp_rows = _dot_nt(cover_ref[...], p_cb)
            imp = imp_rows[:, 0:DS]
            for h in range(1, HPG):
                imp = imp + imp_rows[:, h * DS:(h + 1) * DS]
            forced = (s_col == 0) | (s_col == last_blk)
            score = jnp.where(forced, 1e6, imp)
            score = jnp.where(s_col <= last_blk, score, NEG)
            sel_t = jnp.zeros((LANES, DS), f32)
            for _ in range(SEL_TOP - 1):
                top = jnp.max(score, axis=0, keepdims=True)
                first = jnp.min(jnp.where(score == top, s_col_f, float(LANES)), axis=0, keepdims=True)
                hit = s_col_f == first
                sel_t = jnp.where(hit & (top > 0.5 * NEG), 1.0, sel_t)
                score = jnp.where(hit, -3e38, score)
            sel_neg = _bf((sel_t.T - 1.0) * (-NEG))
            qaug[g] = jnp.concatenate([jnp.concatenate([sel_neg] * HPG, axis=0), q], axis=1)
        m_s[...] = jnp.full(m_s.shape, NEG, f32)
        l_s[...] = jnp.zeros(l_s.shape, f32)
        acc_s[...] = jnp.zeros(acc_s.shape, f32)

    TK = PAGES_PER_STEP * PAGE_SIZE
    blk = (lax.broadcasted_iota(jnp.int32, (LANES, TK), 1) // SEL_BLOCK
           + j * (TK // SEL_BLOCK))
    onehot = _bf(jnp.where(blk == lax.broadcasted_iota(jnp.int32, (LANES, TK), 0), 1.0, 0.0))
    for g in range(N_KV):
        k_t = _bf(jnp.concatenate([pages[i][0, 0, g] for i in range(PAGES_PER_STEP)], axis=1))
        v_t = _bf(jnp.concatenate([pages[i][0, 1, g] for i in range(PAGES_PER_STEP)], axis=1))
        s = jnp.dot(qaug[g], jnp.concatenate([onehot, k_t], axis=0), preferred_element_type=f32)
        m_new = jnp.maximum(m_s[g], jnp.max(s, axis=-1, keepdims=True))
        p = jnp.exp(s - m_new)
        alpha = jnp.exp(m_s[g] - m_new)
        l_s[g] = alpha * l_s[g] + jnp.sum(p, axis=-1, keepdims=True)
        acc_s[g] = alpha * acc_s[g] + _dot_nt(p, v_t)
        m_s[g] = m_new

    @pl.when(j == pl.num_programs(1) - 1)
    def _():
        j_new = lax.broadcasted_iota(jnp.int32, (1, NEW_PAD), 1)
        ok_new = (j_new <= t_row) & (j_new < DS)
        w_idx = lax.broadcasted_iota(jnp.int32, (1, WINDOW), 1)
        ok_win = w_idx >= t_row
        for g in range(N_KV):
            q = q_ref[0, g]
            s = jnp.where(ok_new, _dot_nt(q, new_sel_ref[0, 0, g]), NEG)
            m_new = jnp.maximum(m_s[g], jnp.max(s, axis=-1, keepdims=True))
            p = jnp.where(ok_new, jnp.exp(s - m_new), 0.0)
            alpha = jnp.exp(m_s[g] - m_new)
            l_fin = alpha * l_s[g] + jnp.sum(p, axis=-1, keepdims=True)
            o_sel = (alpha * acc_s[g]
                     + jnp.dot(_bf(p), new_sel_ref[0, 1, g], preferred_element_type=f32)) / l_fin
            s_a = jnp.where(ok_win, jnp.dot(q, _bf(win_ref[0, 0, g]), preferred_element_type=f32), NEG)
            s_b = jnp.where(ok_new, _dot_nt(q, new_win_ref[0, 0, g]), NEG)
            m_w = jnp.maximum(jnp.max(s_a, axis=-1, keepdims=True), jnp.max(s_b, axis=-1, keepdims=True))
            p_a = jnp.where(ok_win, jnp.exp(s_a - m_w), 0.0)
            p_b = jnp.where(ok_new, jnp.exp(s_b - m_w), 0.0)
            l_w = jnp.sum(p_a, axis=-1, keepdims=True) + jnp.sum(p_b, axis=-1, keepdims=True)
            o_win = (_dot_nt(p_a, win_ref[0, 1, g])
                     + jnp.dot(_bf(p_b), new_win_ref[0, 1, g], preferred_element_type=f32)) / l_w
            gt = g_ref[0, g]
            o_ref[0, g] = gt[:, 0:1] * oc_s[g] + gt[:, 1:2] * o_sel + gt[:, 2:3] * o_win


def nsa_sample_attention(q, kvc, sel_cache, win_cache, page_table, new_sel, new_win, gates, past):
    DB, G, R, D = q.shape
    DS = R // HPG
    n_pool = sel_cache.shape[0]
    n_pages = page_table.shape[1]
    n_chunk = kvc.shape[3]
    NC = (past + DS) // CMP_STRIDE - 1
    assert past % SEL_BLOCK == 0 and DS < CMP_STRIDE and DS <= NEW_PAD and past // SEL_BLOCK <= LANES
    assert n_pages % PAGES_PER_STEP == 0 and win_cache.shape[1] == WINDOW and NC < n_chunk + 1
    c_start = jnp.arange(n_chunk) * CMP_STRIDE
    s_start = jnp.arange(LANES) * SEL_BLOCK
    cover_t = _bf((c_start[None, :] < s_start[:, None] + SEL_BLOCK)
                  & (c_start[None, :] + CMP_LEN > s_start[:, None])
                  & (jnp.arange(n_chunk)[None, :] < NC) & (jnp.arange(LANES)[:, None] < past // SEL_BLOCK))
    sel_view = jnp.transpose(sel_cache, (0, 2, 3, 4, 1))
    win_view = jnp.transpose(win_cache, (0, 2, 3, 4, 1))
    page_spec = lambda i: pl.BlockSpec((1, 2, G, D, PAGE_SIZE),
                                       lambda b, j, pt: (pt[b, j * PAGES_PER_STEP + i], 0, 0, 0, 0))
    per_b = lambda shape: pl.BlockSpec((1,) + shape, lambda b, j, pt: (b,) + (0,) * len(shape))
    grid_spec = pltpu.PrefetchScalarGridSpec(
        num_scalar_prefetch=1,
        grid=(DB, n_pages // PAGES_PER_STEP),
        in_specs=[page_spec(i) for i in range(PAGES_PER_STEP)]
        + [per_b((G, R, D)), per_b((2, G, n_chunk, D)),
           pl.BlockSpec((LANES, n_chunk), lambda b, j, pt: (0, 0)),
           per_b((2, G, NEW_PAD, D)), per_b((2, G, D, WINDOW)), per_b((2, G, NEW_PAD, D)), per_b((G, R, 3))],
        out_specs=per_b((G, R, D)),
        scratch_shapes=[pltpu.VMEM((G, R, LANES + D), jnp.bfloat16), pltpu.VMEM((G, R, 1), jnp.float32),
                        pltpu.VMEM((G, R, 1), jnp.float32), pltpu.VMEM((G, R, D), jnp.float32),
                        pltpu.VMEM((G, R, D), jnp.float32)],
    )
    return pl.pallas_call(
        functools.partial(_nsa_sample_kernel, DS=DS, NC=NC, past=past),
        grid_spec=grid_spec,
        out_shape=jax.ShapeDtypeStruct((DB, G, R, D), jnp.float32),
        compiler_params=pltpu.CompilerParams(dimension_semantics=("arbitrary", "arbitrary"),
                                             vmem_limit_bytes=VMEM_LIMIT),
        name="nsa_sample",
    )(page_table, *([sel_view] * PAGES_PER_STEP), q, kvc, cover_t, new_sel, win_view, new_win, gates)


def _merge_kernel(x_ref, yr_ref, ya_ref, gr_ref, ga_ref, wpa_ref, wpb_ref, wo_ref, g1_ref, b1_ref, rw_ref,
                  h_ref, lg_ref):
    m = (jax.nn.sigmoid(gr_ref[...]) * _dot(yr_ref[...], wpa_ref[...])
         + jax.nn.sigmoid(ga_ref[...]) * _dot(ya_ref[...], wpb_ref[...]))
    z = DN_ALPHA * x_ref[...] + _dot(m, wo_ref[...])
    mu = jnp.mean(z, axis=-1, keepdims=True)
    dev = z - mu
    var = jnp.mean(dev * dev, axis=-1, keepdims=True)
    h = dev * lax.rsqrt(var + LN_EPS) * g1_ref[...] + b1_ref[...]
    h_ref[...] = h
    lg_ref[...] = _dot(h, rw_ref[...])


def merge_rows(x, y_r, y_a, p_all, g0, w_pa, w_pb, w_o, ln1_g, ln1_b, router_w):
    N, D = x.shape
    W = y_r.shape[1]
    assert g0 % D == 0 and D == D_MODEL
    tm = next(t for t in (384, 256, 128, 64, 8) if N % t == 0)
    rw = jnp.pad(router_w, ((0, 0), (0, LANES - router_w.shape[1])))
    rows = lambda w, c=0: pl.BlockSpec((tm, w), lambda i, c=c: (i, c))
    const = lambda shape: pl.BlockSpec(shape, lambda i: (0,) * len(shape))
    return pl.pallas_call(
        _merge_kernel,
        grid=(N // tm,),
        in_specs=[rows(D), rows(W), rows(W), rows(D, g0 // D), rows(D, g0 // D + 1),
                  const((W, D)), const((W, D)), const((D, D)), const((1, D)), const((1, D)), const((D, LANES))],
        out_specs=[rows(D), rows(LANES)],
        out_shape=[jax.ShapeDtypeStruct((N, D), jnp.float32), jax.ShapeDtypeStruct((N, LANES), jnp.float32)],
        compiler_params=pltpu.CompilerParams(dimension_semantics=("arbitrary",), vmem_limit_bytes=VMEM_LIMIT),
        name="merge_rows",
    )(x, y_r, y_a, p_all, p_all, _bf(w_pa), _bf(w_pb), _bf(w_o), ln1_g.reshape(1, D), ln1_b.reshape(1, D), _bf(rw))


def _moe_mlp_kernel(be_ref, base_ref, nval_ref, order_ref, x_hbm, w1_ref, b1_ref, w2_ref, b2_ref, o_hbm,
                    w1s, w2s, xbuf, ybuf, in_sem, out_sem, *, BM, T):
    i = pl.program_id(0)
    n = pl.num_programs(0)
    s = i % 2

    def gather(block, sl):
        base = base_ref[block]

        def body(r, c):
            tok = order_ref[base + r] >> K_SHIFT
            pltpu.make_async_copy(x_hbm.at[tok], xbuf.at[sl, r], in_sem.at[sl]).start()
            return c
        lax.fori_loop(0, BM, body, 0, unroll=8)

    @pl.when(i == 0)
    def _():
        gather(0, 0)

    @pl.when(i + 1 < n)
    def _():
        gather(i + 1, 1 - s)

    e = be_ref[i]
    prev = be_ref[jnp.maximum(i - 1, 0)]

    @pl.when((i == 0) | (e != prev))
    def _():
        w1s[...] = w1_ref[0].astype(jnp.bfloat16)
        w2s[...] = w2_ref[0].astype(jnp.bfloat16)

    def rows_in(sl):
        return pltpu.make_async_copy(x_hbm.at[pl.ds(0, BM)], xbuf.at[sl], in_sem.at[sl])

    def rows_out(sl):
        return pltpu.make_async_copy(ybuf.at[sl], o_hbm.at[pl.ds(0, BM)], out_sem.at[sl])

    rows_in(s).wait()

    @pl.when(i >= 2)
    def _():
        rows_out(s).wait()

    x = jnp.concatenate([xbuf[s, :, c, :] for c in range(SUBLANES)], axis=1).astype(jnp.bfloat16)
    h = jnp.dot(x, w1s[...], preferred_element_type=jnp.float32) + b1_ref[0]
    glu = jnp.minimum(h[:, :D_FF], SWIGLU_LIMIT)
    lin = jnp.clip(h[:, D_FF:], -SWIGLU_LIMIT, SWIGLU_LIMIT)
    act = glu * jax.nn.sigmoid(SWIGLU_ALPHA * glu) * (lin + 1.0)
    y = jnp.dot(act.astype(jnp.bfloat16), w2s[...], preferred_element_type=jnp.float32) + b2_ref[0]
    for c in range(SUBLANES):
        ybuf[s, :, c, :] = y[:, c * LANES:(c + 1) * LANES]

    base = base_ref[i]
    nval = nval_ref[i]
    spare = TOP_K * T + i * BM - base - nval

    def scatter(r, c):
        a = order_ref[base + r]
        row = jnp.where(r < nval, (a & (TOP_K - 1)) * T + (a >> K_SHIFT), spare + r)
        pltpu.make_async_copy(ybuf.at[s, r], o_hbm.at[row], out_sem.at[s]).start()
        return c
    lax.fori_loop(0, BM, scatter, 0, unroll=8)

    @pl.when(i == n - 1)
    def _():
        rows_out(s).wait()

    @pl.when((i == n - 1) & (n >= 2))
    def _():
        rows_out(1 - s).wait()


def moe_mlp(x, order, block_e, block_base, block_nval, w1, b1, w2, b2, BM):
    T, D = x.shape
    assert D == SUBLANES * LANES
    n_blocks = block_e.shape[0]
    E = w1.shape[0]
    grid_spec = pltpu.PrefetchScalarGridSpec(
        num_scalar_prefetch=4,
        grid=(n_blocks,),
        in_specs=[pl.BlockSpec(memory_space=pl.ANY),
                  pl.BlockSpec((1, D, 2 * D_FF), lambda i, be, *_: (be[i], 0, 0)),
                  pl.BlockSpec((1, 1, 2 * D_FF), lambda i, be, *_: (be[i], 0, 0)),
                  pl.BlockSpec((1, D_FF, D), lambda i, be, *_: (be[i], 0, 0)),
                  pl.BlockSpec((1, 1, D), lambda i, be, *_: (be[i], 0, 0))],
        out_specs=pl.BlockSpec(memory_space=pl.ANY),
        scratch_shapes=[pltpu.VMEM((D, 2 * D_FF), jnp.bfloat16), pltpu.VMEM((D_FF, D), jnp.bfloat16),
                        pltpu.VMEM((2, BM, SUBLANES, LANES), jnp.float32),
                        pltpu.VMEM((2, BM, SUBLANES, LANES), jnp.float32),
                        pltpu.SemaphoreType.DMA((2,)), pltpu.SemaphoreType.DMA((2,))],
    )
    return pl.pallas_call(
        functools.partial(_moe_mlp_kernel, BM=BM, T=T),
        grid_spec=grid_spec,
        out_shape=jax.ShapeDtypeStruct((n_blocks * BM, SUBLANES, LANES), jnp.float32),
        compiler_params=pltpu.CompilerParams(dimension_semantics=("arbitrary",),
                                             vmem_limit_bytes=VMEM_LIMIT),
        name="moe_mlp",
    )(block_e, block_base, block_nval, order, x.reshape(T, SUBLANES, LANES), w1, b1.reshape(E, 1, -1), w2, b2.reshape(E, 1, -1))


def _combine_kernel(h_ref, gate_ref, *refs):
    slabs = refs[:TOP_K]
    g2_ref, b2_ref, o_ref = refs[TOP_K:]
    gate = gate_ref[...]
    f = None
    for k in range(TOP_K):
        rows = jnp.concatenate([slabs[k][:, c, :] for c in range(SUBLANES)], axis=1)
        term = gate[:, k:k + 1] * rows
        f = term if f is None else f + term
    z = DN_ALPHA * h_ref[...] + f
    mu = jnp.mean(z, axis=-1, keepdims=True)
    dev = z - mu
    var = jnp.mean(dev * dev, axis=-1, keepdims=True)
    o_ref[...] = dev * lax.rsqrt(var + LN_EPS) * g2_ref[...] + b2_ref[...]


def combine_rows(h, gate, yb, ln2_g, ln2_b):
    T, D = h.shape
    tm = next(t for t in (384, 256, 128, 64, 8) if T % t == 0)
    nb = T // tm
    slab = lambda k: pl.BlockSpec((tm, SUBLANES, LANES), lambda i, k=k: (k * nb + i, 0, 0))
    rows = lambda w: pl.BlockSpec((tm, w), lambda i: (i, 0))
    const = lambda shape: pl.BlockSpec(shape, lambda i: (0,) * len(shape))
    return pl.pallas_call(
        _combine_kernel,
        grid=(nb,),
        in_specs=[rows(D), rows(TOP_K)] + [slab(k) for k in range(TOP_K)] + [const((1, D)), const((1, D))],
        out_specs=rows(D),
        out_shape=jax.ShapeDtypeStruct((T, D), jnp.float32),
        compiler_params=pltpu.CompilerParams(dimension_semantics=("arbitrary",), vmem_limit_bytes=VMEM_LIMIT),
        name="combine_rows",
    )(h, gate, *([yb] * TOP_K), ln2_g.reshape(1, D), ln2_b.reshape(1, D))


def moe(x, logits, mlp1_w, mlp1_b, mlp2_w, mlp2_b, BM=256):
    T, D = x.shape
    top_v, top_e = lax.top_k(logits, TOP_K)
    gate = jax.nn.softmax(top_v, axis=-1)
    n_assign = T * TOP_K
    _, order = lax.sort((top_e.reshape(-1).astype(jnp.int32), jnp.arange(n_assign, dtype=jnp.int32)), num_keys=1)
    counts = jnp.sum(top_e.reshape(-1, 1) == jnp.arange(N_EXPERTS)[None, :], axis=0).astype(jnp.int32)
    padded = (counts + BM - 1) // BM * BM
    start = jnp.cumsum(counts) - counts
    pend = jnp.cumsum(padded)
    pstart = pend - padded
    n_blocks = -(-n_assign // BM) + N_EXPERTS
    row0 = jnp.arange(n_blocks, dtype=jnp.int32) * BM
    block_e = jnp.minimum(jnp.sum(pend[None, :] <= row0[:, None], axis=1), N_EXPERTS - 1).astype(jnp.int32)
    off = row0 - pstart[block_e]
    block_nval = jnp.clip(counts[block_e] - off, 0, BM).astype(jnp.int32)
    block_base = jnp.clip(start[block_e] + off, 0, n_assign).astype(jnp.int32)
    order = jnp.concatenate([order, jnp.zeros((BM,), jnp.int32)])
    yb = moe_mlp(x, order, block_e, block_base, block_nval, mlp1_w, mlp1_b, mlp2_w, mlp2_b, BM)
    return gate, yb


def rope(x, pos):
    half = ROT_DIM // 2
    inv = ROPE_THETA ** (-jnp.arange(half, dtype=jnp.float32) * 2.0 / ROT_DIM)
    ang = pos.astype(jnp.float32)[:, None] * inv
    cos, sin = jnp.cos(ang)[:, None, :], jnp.sin(ang)[:, None, :]
    x1, x2 = x[..., :half], x[..., half:ROT_DIM]
    return jnp.concatenate([x1 * cos - x2 * sin, x2 * cos + x1 * sin, x[..., ROT_DIM:]], axis=-1)


G0 = R_COLS + A_WIDTH + 6 * KV_WIDTH


def project(x2, w_in):
    w_main = jnp.concatenate([w_in[:, :G0], w_in[:, G0 + N_GATE:]], axis=1)
    w_gate = jnp.pad(w_in[:, G0:G0 + N_GATE], ((0, 0), (0, LANES - N_GATE)))
    return pallas_matmul(x2, w_main), pallas_matmul(x2, w_gate)[:, :N_GATE]


def split_projection(p, pg, B, T, pos):
    pr = p[:, :R_COLS].reshape(B, T, R_COLS)
    pa = p[:, R_COLS:G0].reshape(B, T, G0 - R_COLS)
    q = rope(pa[..., :A_WIDTH].reshape(B, T, N_HEADS, HEAD_DIM), pos)
    kvs = [pa[..., A_WIDTH + i * KV_WIDTH:A_WIDTH + (i + 1) * KV_WIDTH].reshape(B, T, N_KV, HEAD_DIM)
           for i in range(6)]
    kvs = [rope(z, pos) if i % 2 == 0 else z for i, z in enumerate(kvs)]
    gates = jax.nn.sigmoid(pg).reshape(B, T, N_KV, HPG, 3)
    return pr, q, kvs, gates


def rwkv_prep_rows(pr, shift_prev, mu, w0, w_w2, a0, w_a2, g_w2, k_k, k_a, r_k):
    B, T, _ = pr.shape
    prev = jnp.concatenate([shift_prev[:, None, :], pr[:, :-1]], axis=1)
    xm = pr + (prev - pr) * mu
    o1, o2, o3 = R_WIDTH, 2 * R_WIDTH, 3 * R_WIDTH
    o4 = o3 + LORA_W
    o5 = o4 + LORA_A
    r, k, v = xm[..., :o1], xm[..., o1:o2], xm[..., o2:o3]
    xw, xa, xg = xm[..., o3:o4], xm[..., o4:o5], xm[..., o5:]
    w_log = -jax.nn.softplus(-(w0 + jnp.tanh(xw) @ w_w2)) - 0.5
    a = jax.nn.sigmoid(a0 + xa @ w_a2)
    g = jax.nn.sigmoid(xg) @ g_w2
    heads = lambda z: z.reshape(B, T, R_HEADS, R_HEAD)
    flat = lambda z: z.reshape(B, T, R_WIDTH)
    kk = heads(k * k_k)
    kk = flat(kk / jnp.maximum(jnp.linalg.norm(kk, axis=-1, keepdims=True), 1e-12))
    k_h = k * (1.0 + (a - 1.0) * k_a)
    bonus = flat(jnp.sum(heads(r * k_h * r_k.reshape(-1)), axis=-1, keepdims=True) * heads(v))
    return r, -jnp.exp(w_log), k_h, v, -kk, kk * a, bonus, g


def rwkv_mixer(parts, wkv0, gn_g, gn_b):
    B, T, _ = parts[0].shape
    C = 64 if T % 64 == 0 else 8
    Tp = -(-T // C) * C
    padded = [jnp.pad(z, ((0, 0), (0, Tp - T), (0, 0))) for z in parts]
    y, wkv = wkv_chunked(*padded, gn_g, gn_b, wkv0, C)
    return y[:, :T], wkv


def compress(kv, pe, w1, b1, w2, b2):
    B, L = kv.shape[:2]
    n_chunk = L // CMP_STRIDE
    ch = kv[:, :n_chunk * CMP_STRIDE].reshape(B, n_chunk, CMP_STRIDE, N_KV, HEAD_DIM)
    blk = jnp.concatenate([ch[:, :-1], ch[:, 1:]], axis=2) + pe[:, None, :]
    blk = jnp.transpose(blk, (0, 3, 1, 2, 4)).reshape(B, N_KV, n_chunk - 1, CMP_LEN * HEAD_DIM)
    return jax.nn.gelu(blk @ w1 + b1) @ w2 + b2


def nsa_prompt(q, kvs, gates, pe, w1, b1, w2, b2):
    kc_raw, vc_raw, ks, vs, kw, vw = kvs
    B, T = q.shape[:2]
    kc = compress(kc_raw, pe[0], w1[0], b1[0], w2[0], b2[0])
    vc = compress(vc_raw, pe[1], w1[1], b1[1], w2[1], b2[1])
    qg = _bf(jnp.transpose(q.reshape(B, T, N_KV, HPG, HEAD_DIM), (0, 2, 3, 1, 4)) * (HEAD_DIM ** -0.5 * LOG2E))
    tk = lambda z: _bf(jnp.swapaxes(z, 1, 2))
    o = nsa_prompt_attention(qg, kc, vc, tk(ks), tk(vs), tk(kw), tk(vw),
                             jnp.transpose(gates, (0, 2, 3, 1, 4)))
    return jnp.transpose(o, (0, 3, 1, 2, 4)).reshape(B, T, A_WIDTH)


def nsa_sample(q, kvs, gates, cmp_cache, sel_cache, win_cache, page_table, pe, w1, b1, w2, b2):
    kc_new, vc_new, ks_new, vs_new, kw_new, vw_new = kvs
    DB, DS = q.shape[:2]
    past = page_table.shape[1] * PAGE_SIZE
    kvc = compress_sample(cmp_cache, page_table, pe, w1, b1, w2, b2)
    qg = _bf(jnp.transpose(q.reshape(DB, DS, N_KV, HPG, HEAD_DIM), (0, 2, 3, 1, 4)) * (HEAD_DIM ** -0.5))
    qg = qg.reshape(DB, N_KV, HPG * DS, HEAD_DIM)
    gt = jnp.transpose(gates, (0, 2, 3, 1, 4)).reshape(DB, N_KV, HPG * DS, 3)
    pack = lambda k, v: _bf(jnp.pad(jnp.transpose(jnp.stack([k, v], axis=1), (0, 1, 3, 2, 4)),
                                    ((0, 0), (0, 0), (0, 0), (0, NEW_PAD - DS), (0, 0))))
    o = nsa_sample_attention(qg, kvc, sel_cache, win_cache, page_table, pack(ks_new, vs_new),
                             pack(kw_new, vw_new), gt, past)
    o = jnp.transpose(o.reshape(DB, N_KV, HPG, DS, HEAD_DIM), (0, 3, 1, 2, 4)).reshape(DB, DS, -1)
    new_k = jnp.concatenate([win_cache[:, DS:, 0], kw_new], axis=1)
    new_v = jnp.concatenate([win_cache[:, DS:, 1], vw_new], axis=1)
    return o, jnp.stack([new_k, new_v], axis=2)


def merge_and_ffn(x, y_r, y_a, p_all, w_pa, w_pb, w_o, ln1_g, ln1_b, router_w, router_b,
                  mlp1_w, mlp1_b, mlp2_w, mlp2_b, ln2_g, ln2_b):
    h, logits = merge_rows(x, y_r, y_a, p_all, G0, w_pa, w_pb, w_o, ln1_g, ln1_b, router_w)
    gate, yb = moe(h, logits[:, :N_EXPERTS] + router_b, mlp1_w, mlp1_b, mlp2_w, mlp2_b)
    return combine_rows(h, gate, yb, ln2_g, ln2_b)


def kernel(x_prompt, x_sample, cache_cmp_kv, cache_sel_kv, cache_win_kv, state_wkv, state_shift,
           page_table, w_in, mu_shift, w0, w_w2, a0, w_a2, g_w2, k_k, k_a, r_k, gn_g, gn_b,
           cmp_pe, cmp_w1, cmp_b1, cmp_w2, cmp_b2, w_pa, w_pb, w_o, ln1_g, ln1_b,
           router_w, router_b, mlp1_w, mlp1_b, mlp2_w, mlp2_b, ln2_g, ln2_b):
    B, T, D = x_prompt.shape
    DB, DS, _ = x_sample.shape
    n_p = B * T
    past = page_table.shape[1] * PAGE_SIZE
    pos_p = jnp.arange(T)
    pos_s = past + jnp.arange(DS)
    wb_p = min(WINDOW, T)
    h_all = jnp.concatenate([x_prompt.reshape(n_p, D), x_sample.reshape(DB * DS, D)])
    cmp_p, sel_p, win_p, wkv_p, shift_p = [], [], [], [], []
    cmp_s, sel_s, win_s, wkv_s, shift_s = [], [], [], [], []
    for l in range(DEPTH):
        rwkv_w = (mu_shift[l], w0[l], w_w2[l], a0[l], w_a2[l], g_w2[l], k_k[l], k_a[l], r_k[l],
                  gn_g[l], gn_b[l])
        cmp_w = (cmp_pe[l], cmp_w1[l], cmp_b1[l], cmp_w2[l], cmp_b2[l])
        out_w = (w_pa[l], w_pb[l], w_o[l], ln1_g[l], ln1_b[l], router_w[l], router_b[l],
                 mlp1_w[l], mlp1_b[l], mlp2_w[l], mlp2_b[l], ln2_g[l], ln2_b[l])
        p_all, pg_all = project(h_all, w_in[l])
        pr, q, kvs, gates = split_projection(p_all[:n_p], pg_all[:n_p], B, T, pos_p)
        parts = rwkv_prep(p_all, n_p, T, jnp.zeros((B, R_COLS), jnp.float32), *rwkv_w[:9])
        y_r, wkv = rwkv_mixer([z.reshape(B, T, R_WIDTH) for z in parts],
                              jnp.zeros((B, R_HEADS, R_HEAD, R_HEAD), jnp.float32), *rwkv_w[9:])
        shift = p_all[T - 1:n_p:T, :R_COLS]
        y_a = nsa_prompt(q, kvs, gates, *cmp_w)
        cmp_p.append(jnp.stack([kvs[0], kvs[1]], axis=2))
        sel_p.append(jnp.stack([kvs[2], kvs[3]], axis=2))
        win_p.append(jnp.stack([kvs[4][:, T - wb_p:], kvs[5][:, T - wb_p:]], axis=2))
        wkv_p.append(wkv)
        shift_p.append(shift)
        pr, q, kvs, gates = split_projection(p_all[n_p:], pg_all[n_p:], DB, DS, pos_s)
        y_r_s, wkv = rwkv_mixer(rwkv_prep_rows(pr, state_shift[l], *rwkv_w[:9]), state_wkv[l], *rwkv_w[9:])
        shift = pr[:, -1]
        y_a_s, new_win = nsa_sample(q, kvs, gates, cache_cmp_kv[l], cache_sel_kv[l], cache_win_kv[l],
                                    page_table, *cmp_w)
        cmp_s.append(jnp.stack([kvs[0], kvs[1]], axis=2))
        sel_s.append(jnp.stack([kvs[2], kvs[3]], axis=2))
        win_s.append(new_win)
        wkv_s.append(wkv)
        shift_s.append(shift)
        rows = lambda a, b: jnp.concatenate([a.reshape(n_p, -1), b.reshape(DB * DS, -1)])
        h_all = merge_and_ffn(h_all, rows(y_r, y_r_s), rows(y_a, y_a_s), p_all, *out_w)
    hp = h_all[:n_p].reshape(B, T, D)
    hs = h_all[n_p:].reshape(DB, DS, D)
    return (hp, hs, jnp.stack(cmp_p), jnp.stack(sel_p), jnp.stack(win_p), jnp.stack(wkv_p),
            jnp.stack(shift_p), jnp.stack(cmp_s), jnp.stack(sel_s), jnp.stack(win_s),
            jnp.stack(wkv_s), jnp.stack(shift_s))
```

```python
import functools

import jax
import jax.numpy as jnp
from jax import lax
from jax.experimental import pallas as pl
from jax.experimental.pallas import tpu as pltpu

D_MODEL = 1024
DEPTH = 1
PAGE_SIZE = 128

R_HEADS = 8
R_HEAD = 64
R_WIDTH = R_HEADS * R_HEAD
LORA_W = 64
LORA_A = 64
LORA_G = 128
R_COLS = 3 * R_WIDTH + LORA_W + LORA_A + LORA_G
GN_EPS = 64e-5

N_HEADS = 8
N_KV = 2
HPG = N_HEADS // N_KV
HEAD_DIM = 64
A_WIDTH = N_HEADS * HEAD_DIM
KV_WIDTH = N_KV * HEAD_DIM
N_GATE = 3 * N_HEADS
A_COLS = A_WIDTH + 6 * KV_WIDTH + N_GATE
ROT_DIM = HEAD_DIM // 4
ROPE_THETA = 500000.0
CMP_STRIDE = 16
CMP_LEN = 2 * CMP_STRIDE
CMP_HIDDEN = 256
SEL_BLOCK = 64
SEL_TOP = 16
WINDOW = 512

N_EXPERTS = 32
TOP_K = 4
K_SHIFT = 2
D_FF = 1024
SWIGLU_LIMIT = 7.0
SWIGLU_ALPHA = 1.702

DN_ALPHA = (2 * DEPTH) ** 0.25
LN_EPS = 1e-5
NEG = -1e30
LOG2E = 1.4426950408889634

LANES = 128
SUBLANES = 8
PAGES_PER_STEP = 16
NEW_PAD = 8
VMEM_LIMIT = 56 * 1024 * 1024


def _bf(x):
    return x.astype(jnp.bfloat16)


def _dot(a, b):
    return jnp.dot(_bf(a), _bf(b), preferred_element_type=jnp.float32)


def _dot_nt(a, b):
    return lax.dot_general(_bf(a), _bf(b), (((1,), (1,)), ((), ())),
                           preferred_element_type=jnp.float32)


def _mm_kernel(x_ref, w_ref, o_ref):
    o_ref[...] = _dot(x_ref[...], w_ref[...])


def pallas_matmul(x, w):
    x, w = _bf(x), _bf(w)
    M, K = x.shape
    N = w.shape[1]
    tm = next(t for t in (512, 384, 256, 128, M) if M % t == 0)
    resident = 2 * (2 * K * N + 4 * tm * N + 2 * tm * K) <= VMEM_LIMIT - (8 << 20)
    tn = N if resident else next(t for t in (512, 256, LANES) if N % t == 0)
    return pl.pallas_call(
        _mm_kernel,
        grid=(N // tn, M // tm),
        in_specs=[pl.BlockSpec((tm, K), lambda j, i: (i, 0)),
                  pl.BlockSpec((K, tn), lambda j, i: (0, j))],
        out_specs=pl.BlockSpec((tm, tn), lambda j, i: (i, j)),
        out_shape=jax.ShapeDtypeStruct((M, N), jnp.float32),
        compiler_params=pltpu.CompilerParams(vmem_limit_bytes=VMEM_LIMIT),
        name="mm",
    )(x, w)


def _bmm(a, b):
    return lax.dot_general(_bf(a), _bf(b), (((2,), (1,)), ((0,), (0,))), preferred_element_type=jnp.float32)


def _bmm_nt(a, b):
    return lax.dot_general(_bf(a), _bf(b), (((2,), (2,)), ((0,), (0,))), preferred_element_type=jnp.float32)


def _bmm_tn(a, b):
    return lax.dot_general(_bf(a), _bf(b), (((1,), (1,)), ((0,), (0,))), preferred_element_type=jnp.float32)


def _head_sum(x, ones_bd):
    hi = _bf(x)
    rem = x - hi.astype(jnp.float32)
    mid = _bf(rem)
    lo = _bf(rem - mid.astype(jnp.float32))
    dot = lambda t: jnp.dot(t, ones_bd, preferred_element_type=jnp.float32)
    return dot(hi) + dot(mid) + dot(lo)


def head_ones():
    h = jnp.arange(R_WIDTH) // R_HEAD
    return _bf(h[:, None] == h[None, :])


def _rwkv_prep_kernel(p_ref, pb_ref, sp_ref, mu_ref, w0_ref, ww2_ref, a0_ref, wa2_ref, gw2_ref,
                      kk_ref, ka_ref, rk_ref, bd_ref,
                      r_ref, lw_ref, k_ref, v_ref, a_ref, b_ref, bonus_ref, g_ref, *, tiles_per_seq):
    i = pl.program_id(0)
    pr = p_ref[...]
    first = jnp.where(i % tiles_per_seq == 0, sp_ref[0], pb_ref[SUBLANES - 1:SUBLANES, :])
    rolled = pltpu.roll(pr, shift=1, axis=0)
    prev = jnp.where(lax.broadcasted_iota(jnp.int32, (pr.shape[0], 1), 0) == 0, first, rolled)
    xm = pr + (prev - pr) * mu_ref[...]
    o1, o2, o3 = R_WIDTH, 2 * R_WIDTH, 3 * R_WIDTH
    o4 = o3 + LORA_W
    o5 = o4 + LORA_A
    r, k, v = xm[:, :o1], xm[:, o1:o2], xm[:, o2:o3]
    xw, xa, xg = xm[:, o3:o4], xm[:, o4:o5], xm[:, o5:]
    dot = lambda x, w_ref: jnp.dot(_bf(x), _bf(w_ref[...]), preferred_element_type=jnp.float32)
    w_log = -jax.nn.softplus(-(w0_ref[...] + dot(jnp.tanh(xw), ww2_ref))) - 0.5
    a = jax.nn.sigmoid(a0_ref[...] + dot(xa, wa2_ref))
    g_ref[...] = dot(jax.nn.sigmoid(xg), gw2_ref)
    ones_bd = bd_ref[...]
    kk = k * kk_ref[...]
    kk = kk / jnp.maximum(jnp.sqrt(_head_sum(kk * kk, ones_bd)), 1e-12)
    k_h = k * (1.0 + (a - 1.0) * ka_ref[...])
    r_ref[...] = r
    lw_ref[...] = -jnp.exp(w_log)
    k_ref[...] = k_h
    v_ref[...] = v
    a_ref[...] = -kk
    b_ref[...] = kk * a
    bonus_ref[...] = _head_sum(r * k_h * rk_ref[...], ones_bd) * v


def rwkv_prep(p_all, n_rows, seq_len, shift_prev, mu, w0, w_w2, a0, w_a2, g_w2, k_k, k_a, r_k, tm=256):
    assert seq_len % tm == 0 and n_rows % seq_len == 0
    n_seq = n_rows // seq_len
    tiles_per_seq = seq_len // tm
    row = lambda z: z.reshape(1, -1)
    const = lambda shape: pl.BlockSpec(shape, lambda i: (0,) * len(shape))
    out = pl.BlockSpec((tm, R_WIDTH), lambda i: (i, 0))
    outs = pl.pallas_call(
        functools.partial(_rwkv_prep_kernel, tiles_per_seq=tiles_per_seq),
        grid=(n_rows // tm,),
        in_specs=[pl.BlockSpec((tm, R_COLS), lambda i: (i, 0)),
                  pl.BlockSpec((SUBLANES, R_COLS), lambda i: (jnp.maximum(i * (tm // SUBLANES) - 1, 0), 0)),
                  pl.BlockSpec((1, 1, R_COLS), lambda i: (i // tiles_per_seq, 0, 0)),
                  const((1, R_COLS)), const((1, R_WIDTH)), const((LORA_W, R_WIDTH)), const((1, R_WIDTH)),
                  const((LORA_A, R_WIDTH)), const((LORA_G, R_WIDTH)), const((1, R_WIDTH)), const((1, R_WIDTH)),
                  const((1, R_WIDTH)), const((R_WIDTH, R_WIDTH))],
        out_specs=[out] * 8,
        out_shape=[jax.ShapeDtypeStruct((n_rows, R_WIDTH), jnp.float32)] * 8,
        compiler_params=pltpu.CompilerParams(dimension_semantics=("arbitrary",), vmem_limit_bytes=VMEM_LIMIT),
        name="rwkv_prep",
    )(p_all, p_all, shift_prev.reshape(n_seq, 1, R_COLS), row(mu), row(w0), w_w2, row(a0), w_a2, g_w2,
      row(k_k), row(k_a), row(r_k), head_ones())
    return outs


def _wkv_chunk_kernel(r_ref, lw_ref, k_ref, v_ref, a_ref, b_ref, bonus_ref, g_ref, gng_ref, gnb_ref, bd_ref,
                      s0_ref, y_ref, s_out_ref, s_scr, *, C, H, NB):
    c = pl.program_id(1)
    D = R_HEAD

    @pl.when(c == 0)
    def _():
        s_scr[...] = s0_ref[...].reshape(NB * H, D, D)

    row = lax.broadcasted_iota(jnp.int32, (NB * H, C, C), 1)
    col = lax.broadcasted_iota(jnp.int32, (NB * H, C, C), 2)
    incl = row >= col
    strict = row > col
    ltri = jnp.where(incl, 1.0, 0.0).astype(jnp.bfloat16)
    eye = jnp.where(row == col, 1.0, 0.0).astype(jnp.float32)

    heads = lambda ref: jnp.stack([ref[n, :, h * D:(h + 1) * D] for n in range(NB) for h in range(H)])
    lw = heads(lw_ref)
    r = heads(r_ref)
    k = heads(k_ref)
    v = heads(v_ref)
    a = heads(a_ref)
    b = heads(b_ref)
    hi = _bf(lw)
    rem = lw - hi.astype(jnp.float32)
    mid = _bf(rem)
    lo = _bf(rem - mid.astype(jnp.float32))
    lp = _bmm(ltri, hi) + _bmm(ltri, mid) + _bmm(ltri, lo)
    lp_end = lp[:, C - 1:C, :]
    p_end = jnp.exp(lp_end)
    p_inv = jnp.exp(-lp)
    at = a * jnp.exp(lp - lw)
    rt = r * jnp.exp(lp)
    bt = b * p_inv
    kt = k * p_inv
    p_hat = jnp.exp(lp_end - lp)
    bh = b * p_hat
    kh = k * p_hat

    n_ab = jnp.where(strict, _bmm_nt(at, bt), 0.0)
    a_ak = jnp.where(strict, _bmm_nt(at, kt), 0.0)
    a_rb = jnp.where(incl, _bmm_nt(rt, bt), 0.0)
    a_rk = jnp.where(incl, _bmm_nt(rt, kt), 0.0)

    t_inv = eye + n_ab
    n_pow = n_ab
    span = 2
    while span < C:
        n_pow = _bmm(n_pow, n_pow)
        t_inv = _bmm(t_inv, eye + n_pow)
        span *= 2

    s = s_scr[...]
    rhs = _bmm_nt(at, s) + _bmm(a_ak, v)
    u = _bmm(t_inv, rhs)
    y = _bmm_nt(rt, s) + _bmm(a_rb, u) + _bmm(a_rk, v)
    s_new = s * p_end + _bmm_tn(u, bh) + _bmm_tn(v, kh)
    s_scr[...] = s_new

    ones_bd = bd_ref[...]
    for n in range(NB):
        yt = jnp.concatenate([y[n * H + h] for h in range(H)], axis=1)
        dev = yt - _head_sum(yt, ones_bd) * (1.0 / D)
        var = _head_sum(dev * dev, ones_bd) * (1.0 / D)
        yn = dev * lax.rsqrt(var + GN_EPS) * gng_ref[...] + gnb_ref[...]
        y_ref[n] = (yn + bonus_ref[n]) * g_ref[n]

    @pl.when(c == pl.num_programs(1) - 1)
    def _():
        s_out_ref[...] = s_new.reshape(NB, H, D, D)


def wkv_chunked(r, lw, k, v, a, b, bonus, g, gn_g, gn_b, s0, C):
    B, T, W = r.shape
    H, D = R_HEADS, R_HEAD
    assert T % C == 0
    NB = 2 if B % 2 == 0 else 1
    seq = pl.BlockSpec((NB, C, W), lambda bi, ci: (bi, ci, 0))
    st = pl.BlockSpec((NB, H, D, D), lambda bi, ci: (bi, 0, 0, 0))
    const = lambda shape: pl.BlockSpec(shape, lambda bi, ci: (0,) * len(shape))
    return pl.pallas_call(
        functools.partial(_wkv_chunk_kernel, C=C, H=H, NB=NB),
        grid=(B // NB, T // C),
        in_specs=[seq] * 8 + [const((1, W)), const((1, W)), const((W, W)), st],
        out_specs=[seq, st],
        out_shape=[jax.ShapeDtypeStruct((B, T, W), jnp.float32),
                   jax.ShapeDtypeStruct((B, H, D, D), jnp.float32)],
        scratch_shapes=[pltpu.VMEM((NB * H, D, D), jnp.float32)],
        compiler_params=pltpu.CompilerParams(dimension_semantics=("arbitrary", "arbitrary")),
        name="wkv_chunk",
    )(r, lw, k, v, a, b, bonus, g, gn_g.reshape(1, W), gn_b.reshape(1, W), head_ones(), s0)


def _nsa_prompt_kernel(q_ref, kc_ref, vc_ref, cover_ref, ks_ref, vs_ref, kw_ref, vw_ref, g_ref, o_ref,
                       *, TQ, TK, NC, NCP):
    f32 = jnp.float32
    bf16 = jnp.bfloat16
    qb = pl.program_id(2)
    R = HPG * TQ
    q = q_ref[0, 0].reshape(R, HEAD_DIM)
    t_pos = qb * TQ + lax.broadcasted_iota(jnp.int32, (TQ, 1), 0)

    n_idx = lax.broadcasted_iota(jnp.int32, (1, NCP), 1)
    c_ok = ((n_idx * CMP_STRIDE + (CMP_LEN - 1)) <= t_pos) & (n_idx < NC)
    s_c = _dot_nt(q, kc_ref[0, 0]).reshape(HPG, TQ, NCP)
    s_c = jnp.where(c_ok[None], s_c, NEG)
    m_c = jnp.max(s_c, axis=-1, keepdims=True)
    p_c = jnp.where(c_ok[None], jnp.exp2(s_c - m_c), 0.0)
    l_c = jnp.sum(p_c, axis=-1, keepdims=True)
    p_c = p_c / jnp.where(l_c > 0.0, l_c, 1.0)
    p_cb = p_c.astype(bf16)
    o_c = jnp.dot(p_cb.reshape(R, NCP), vc_ref[0, 0], preferred_element_type=f32)

    cover_t = cover_ref[...]
    imp = _dot_nt(cover_t, p_cb[0])
    for h in range(1, HPG):
        imp = imp + _dot_nt(cover_t, p_cb[h])
    s_col = lax.broadcasted_iota(jnp.int32, (LANES, 1), 0)
    t_row = qb * TQ + lax.broadcasted_iota(jnp.int32, (1, TQ), 1)
    cur = t_row // SEL_BLOCK
    forced = (s_col == 0) | (s_col == cur) | (s_col == cur - 1)
    causal = (s_col * SEL_BLOCK) <= t_row
    score = jnp.where(forced, 1e6, imp)
    score = jnp.where(causal, score, NEG)
    s_col_f = s_col.astype(f32)
    sel_t = jnp.zeros((LANES, TQ), f32)
    for _ in range(SEL_TOP):
        top = jnp.max(score, axis=0, keepdims=True)
        first = jnp.min(jnp.where(score == top, s_col_f, float(LANES)), axis=0, keepdims=True)
        hit = s_col_f == first
        sel_t = jnp.where(hit & (top > 0.5 * NEG), 1.0, sel_t)
        score = jnp.where(hit, -3e38, score)
    sel_neg = ((sel_t.T - 1.0) * (-NEG)).astype(bf16)
    q_aug = jnp.concatenate([jnp.concatenate([sel_neg] * HPG, axis=0), q], axis=1)
    key_off = lax.broadcasted_iota(jnp.int32, (1, TK), 1)

    def sel_tile(j, carry, diagonal):
        m, acc = carry
        start = pl.multiple_of(j * TK, TK)
        k = ks_ref[0, 0, pl.ds(start, TK), :]
        v = vs_ref[0, 0, pl.ds(start, TK), :]
        s = _dot_nt(q_aug, k).reshape(HPG, TQ, TK)
        if diagonal:
            s = jnp.where(((key_off + j * TK) <= t_pos)[None], s, NEG)
        m_new = jnp.maximum(m, jnp.max(s, axis=-1, keepdims=True))
        p = jnp.exp2((s - m_new).astype(bf16))
        alpha = jnp.exp2(m - m_new)
        pv = jnp.dot(p.reshape(R, TK), v, preferred_element_type=f32)
        acc = alpha * acc + pv.reshape(HPG, TQ, LANES)
        return m_new, acc

    m0 = jnp.full((HPG, TQ, 1), NEG, f32)
    a0 = jnp.zeros((HPG, TQ, LANES), f32)
    n_full = (qb * TQ) // TK
    carry = lax.fori_loop(0, n_full, functools.partial(sel_tile, diagonal=False), (m0, a0))
    _, acc_s = sel_tile(n_full, carry, True)
    o_s = acc_s[:, :, :HEAD_DIM] / acc_s[:, :, HEAD_DIM:HEAD_DIM + 1]

    n_w = WINDOW // TQ + 1
    lane_q = lax.broadcasted_iota(jnp.int32, (1, TQ), 1)
    s_w, v_w = [], []
    for i in range(n_w):
        kb = qb - (n_w - 1) + i
        start = pl.multiple_of(jnp.maximum(kb, 0) * TQ, TQ)
        k = kw_ref[0, 0, pl.ds(start, TQ), :]
        v_w.append(vw_ref[0, 0, pl.ds(start, TQ), :])
        kpos = kb * TQ + lane_q
        ok = (kpos <= t_pos) & (kpos >= t_pos - WINDOW) & (kpos >= 0)
        s_w.append(jnp.where(ok[None], _dot_nt(q, k).reshape(HPG, TQ, TQ), NEG))
    m_w = s_w[0].max(axis=-1, keepdims=True)
    for i in range(1, n_w):
        m_w = jnp.maximum(m_w, s_w[i].max(axis=-1, keepdims=True))
    acc_w = jnp.zeros((R, LANES), f32)
    for i in range(n_w):
        p = jnp.exp2((s_w[i] - m_w).astype(bf16))
        acc_w = acc_w + jnp.dot(p.reshape(R, TQ), v_w[i], preferred_element_type=f32)
    acc_w = acc_w.reshape(HPG, TQ, LANES)
    o_w = acc_w[:, :, :HEAD_DIM] / acc_w[:, :, HEAD_DIM:HEAD_DIM + 1]

    g = g_ref[0, 0]
    o_ref[0, 0] = (g[:, :, 0:1] * o_c.reshape(HPG, TQ, HEAD_DIM) + g[:, :, 1:2] * o_s
                   + g[:, :, 2:3] * o_w)


def nsa_prompt_attention(q, kc, vc, ks, vs, kw, vw, gates, TQ=256, TK=1024):
    B, G, _, T, D = q.shape
    NC = kc.shape[2]
    NCP = -(-NC // LANES) * LANES
    NS = T // SEL_BLOCK
    assert NS <= LANES and T % TK == 0 and TK % TQ == 0 and WINDOW % TQ == 0
    kc = jnp.pad(kc, ((0, 0), (0, 0), (0, NCP - NC), (0, 0))).astype(jnp.bfloat16)
    vc = jnp.pad(vc, ((0, 0), (0, 0), (0, NCP - NC), (0, 0))).astype(jnp.bfloat16)
    c_start = jnp.arange(NCP) * CMP_STRIDE
    s_start = jnp.arange(LANES) * SEL_BLOCK
    cover = ((c_start[:, None] < s_start[None, :] + SEL_BLOCK)
             & (c_start[:, None] + CMP_LEN > s_start[None, :])
             & (jnp.arange(NCP)[:, None] < NC) & (jnp.arange(LANES)[None, :] < NS)).astype(jnp.bfloat16)
    onehot = (jnp.arange(T)[:, None] // SEL_BLOCK == jnp.arange(LANES)[None, :]).astype(jnp.bfloat16)
    ks = jnp.concatenate([jnp.broadcast_to(onehot, (B, G, T, LANES)), ks], axis=-1)
    ones_col = (jnp.arange(LANES - D) == 0).astype(jnp.bfloat16)
    with_ones = lambda v: jnp.concatenate([v, jnp.broadcast_to(ones_col, (B, G, T, LANES - D))], axis=-1)
    vs, vw = with_ones(vs), with_ones(vw)
    full = lambda n, d=D: pl.BlockSpec((1, 1, n, d), lambda b, g, i: (b, g, 0, 0))
    qspec = pl.BlockSpec((1, 1, HPG, TQ, D), lambda b, g, i: (b, g, 0, i, 0))
    return pl.pallas_call(
        functools.partial(_nsa_prompt_kernel, TQ=TQ, TK=TK, NC=NC, NCP=NCP),
        grid=(B, G, T // TQ),
        in_specs=[qspec, full(NCP), full(NCP),
                  pl.BlockSpec((LANES, NCP), lambda b, g, i: (0, 0)),
                  full(T, LANES + D), full(T, LANES), full(T), full(T, LANES),
                  pl.BlockSpec((1, 1, HPG, TQ, 3), lambda b, g, i: (b, g, 0, i, 0))],
        out_specs=qspec,
        out_shape=jax.ShapeDtypeStruct((B, G, HPG, T, D), jnp.float32),
        compiler_params=pltpu.CompilerParams(
            dimension_semantics=("arbitrary", "arbitrary", "arbitrary"),
            vmem_limit_bytes=VMEM_LIMIT),
        name="nsa_prompt",
    )(q, kc, vc, cover.T, ks, vs, kw, vw, gates)


def _cmp_sample_kernel(pt_ref, *refs, n_chunk):
    pages = refs[:PAGES_PER_STEP]
    wcat_ref, c1_ref, w2_ref, b2_ref, o_ref, seq = refs[PAGES_PER_STEP:]
    j = pl.program_id(1)
    rows = PAGE_SIZE // CMP_STRIDE
    src = lax.broadcasted_iota(jnp.int32, (PAGE_SIZE, PAGE_SIZE), 0)
    src = (src % rows) * CMP_STRIDE + src // rows
    perm = _bf(jnp.where(src == lax.broadcasted_iota(jnp.int32, (PAGE_SIZE, PAGE_SIZE), 1), 1.0, 0.0))
    for i in range(PAGES_PER_STEP):
        dst = pl.multiple_of((j * PAGES_PER_STEP + i) * rows, rows)
        for kv in range(2):
            for g in range(N_KV):
                q = kv * N_KV + g
                by_pos = _dot_nt(perm, pages[i][0, kv, g])
                for p in range(CMP_STRIDE):
                    seq[q, pl.ds(dst, rows), p * HEAD_DIM:(p + 1) * HEAD_DIM] = by_pos[p * rows:(p + 1) * rows]

    @pl.when(j == pl.num_programs(1) - 1)
    def _():
        for kv in range(2):
            for g in range(N_KV):
                q = kv * N_KV + g
                acc = jnp.dot(_bf(seq[q]), wcat_ref[kv], preferred_element_type=jnp.float32)
                first = acc[:, :CMP_HIDDEN]
                second = acc[:, CMP_HIDDEN:]
                second = jnp.concatenate([second[1:], second[:1]], axis=0)
                hid = jax.nn.gelu(first + second + c1_ref[kv])
                out = jnp.dot(_bf(hid), w2_ref[kv], preferred_element_type=jnp.float32) + b2_ref[kv]
                o_ref[0, kv, g] = _bf(out)


def compress_sample(cache, page_table, pe, w1, b1, w2, b2):
    n_pool = cache.shape[0]
    DB, n_pages = page_table.shape
    assert n_pages % PAGES_PER_STEP == 0
    rows = PAGE_SIZE // CMP_STRIDE
    n_chunk = n_pages * rows
    view = jnp.transpose(cache, (0, 2, 3, 4, 1))
    w1r = w1.reshape(2, CMP_LEN, HEAD_DIM, CMP_HIDDEN)
    wcat = _bf(jnp.concatenate([w1r[:, :CMP_STRIDE], w1r[:, CMP_STRIDE:]], axis=-1))
    wcat = wcat.reshape(2, CMP_STRIDE * HEAD_DIM, 2 * CMP_HIDDEN)
    c1 = (jnp.einsum('kn,knh->kh', pe.reshape(2, CMP_LEN * HEAD_DIM), w1) + b1).reshape(2, 1, CMP_HIDDEN)
    page_spec = lambda i: pl.BlockSpec((1, 2, N_KV, HEAD_DIM, PAGE_SIZE),
                                       lambda b, j, pt: (pt[b, j * PAGES_PER_STEP + i], 0, 0, 0, 0))
    const = lambda shape: pl.BlockSpec(shape, lambda b, j, pt: (0,) * len(shape))
    grid_spec = pltpu.PrefetchScalarGridSpec(
        num_scalar_prefetch=1,
        grid=(DB, n_pages // PAGES_PER_STEP),
        in_specs=[page_spec(i) for i in range(PAGES_PER_STEP)]
        + [const((2, CMP_STRIDE * HEAD_DIM, 2 * CMP_HIDDEN)), const((2, 1, CMP_HIDDEN)),
           const((2, CMP_HIDDEN, HEAD_DIM)), const((2, 1, HEAD_DIM))],
        out_specs=pl.BlockSpec((1, 2, N_KV, n_chunk, HEAD_DIM), lambda b, j, pt: (b, 0, 0, 0, 0)),
        scratch_shapes=[pltpu.VMEM((2 * N_KV, n_chunk, CMP_STRIDE * HEAD_DIM), jnp.float32)],
    )
    return pl.pallas_call(
        functools.partial(_cmp_sample_kernel, n_chunk=n_chunk),
        grid_spec=grid_spec,
        out_shape=jax.ShapeDtypeStruct((DB, 2, N_KV, n_chunk, HEAD_DIM), jnp.bfloat16),
        compiler_params=pltpu.CompilerParams(dimension_semantics=("arbitrary", "arbitrary"),
                                             vmem_limit_bytes=VMEM_LIMIT),
        name="cmp_sample",
    )(page_table, *([view] * PAGES_PER_STEP), wcat, c1, _bf(w2), b2.reshape(2, 1, HEAD_DIM))


def _nsa_sample_kernel(pt_ref, *refs, DS, NC, past):
    pages = refs[:PAGES_PER_STEP]
    (q_ref, kvc_ref, cover_ref, new_sel_ref, win_ref, new_win_ref, g_ref, o_ref,
     qaug, m_s, l_s, acc_s, oc_s) = refs[PAGES_PER_STEP:]
    f32 = jnp.float32
    j = pl.program_id(1)
    R = HPG * DS
    t_row = lax.broadcasted_iota(jnp.int32, (R, 1), 0) % DS
    n_chunk = kvc_ref.shape[3]

    @pl.when(j == 0)
    def _():
        n_idx = lax.broadcasted_iota(jnp.int32, (1, n_chunk), 1)
        s_col = lax.broadcasted_iota(jnp.int32, (LANES, 1), 0)
        s_col_f = s_col.astype(f32)
        last_blk = past // SEL_BLOCK - 1
        for g in range(N_KV):
            q = q_ref[0, g]
            s_c = jnp.where(n_idx < NC, _dot_nt(q, kvc_ref[0, 0, g]), NEG)
            p_c = jnp.exp(s_c - jnp.max(s_c, axis=-1, keepdims=True))
            p_c = p_c / jnp.sum(p_c, axis=-1, keepdims=True)
            p_cb = _bf(p_c)
            oc_s[g] = jnp.dot(p_cb, kvc_ref[0, 1, g], preferred_element_type=f32)
            imp_rows = _dot_nt(cover_ref[...], p_cb)
            imp = imp_rows[:, 0:DS]
            for h in range(1, HPG):
                imp = imp + imp_rows[:, h * DS:(h + 1) * DS]
            forced = (s_col == 0) | (s_col == last_blk)
            score = jnp.where(forced, 1e6, imp)
            score = jnp.where(s_col <= last_blk, score, NEG)
            sel_t = jnp.zeros((LANES, DS), f32)
            for _ in range(SEL_TOP - 1):
                top = jnp.max(score, axis=0, keepdims=True)
                first = jnp.min(jnp.where(score == top, s_col_f, float(LANES)), axis=0, keepdims=True)
                hit = s_col_f == first
                sel_t = jnp.where(hit & (top > 0.5 * NEG), 1.0, sel_t)
                score = jnp.where(hit, -3e38, score)
            sel_neg = _bf((sel_t.T - 1.0) * (-NEG))
            qaug[g] = jnp.concatenate([jnp.concatenate([sel_neg] * HPG, axis=0), q], axis=1)
        m_s[...] = jnp.full(m_s.shape, NEG, f32)
        l_s[...] = jnp.zeros(l_s.shape, f32)
        acc_s[...] = jnp.zeros(acc_s.shape, f32)

    TK = PAGES_PER_STEP * PAGE_SIZE
    blk = (lax.broadcasted_iota(jnp.int32, (LANES, TK), 1) // SEL_BLOCK
           + j * (TK // SEL_BLOCK))
    onehot = _bf(jnp.where(blk == lax.broadcasted_iota(jnp.int32, (LANES, TK), 0), 1.0, 0.0))
    for g in range(N_KV):
        k_t = _bf(jnp.concatenate([pages[i][0, 0, g] for i in range(PAGES_PER_STEP)], axis=1))
        v_t = _bf(jnp.concatenate([pages[i][0, 1, g] for i in range(PAGES_PER_STEP)], axis=1))
        s = jnp.dot(qaug[g], jnp.concatenate([onehot, k_t], axis=0), preferred_element_type=f32)
        m_new = jnp.maximum(m_s[g], jnp.max(s, axis=-1, keepdims=True))
        p = jnp.exp(s - m_new)
        alpha = jnp.exp(m_s[g] - m_new)
        l_s[g] = alpha * l_s[g] + jnp.sum(p, axis=-1, keepdims=True)
        acc_s[g] = alpha * acc_s[g] + _dot_nt(p, v_t)
        m_s[g] = m_new

    @pl.when(j == pl.num_programs(1) - 1)
    def _():
        j_new = lax.broadcasted_iota(jnp.int32, (1, NEW_PAD), 1)
        ok_new = (j_new <= t_row) & (j_new < DS)
        w_idx = lax.broadcasted_iota(jnp.int32, (1, WINDOW), 1)
        ok_win = w_idx >= t_row
        for g in range(N_KV):
            q = q_ref[0, g]
            s = jnp.where(ok_new, _dot_nt(q, new_sel_ref[0, 0, g]), NEG)
            m_new = jnp.maximum(m_s[g], jnp.max(s, axis=-1, keepdims=True))
            p = jnp.where(ok_new, jnp.exp(s - m_new), 0.0)
            alpha = jnp.exp(m_s[g] - m_new)
            l_fin = alpha * l_s[g] + jnp.sum(p, axis=-1, keepdims=True)
            o_sel = (alpha * acc_s[g]
                     + jnp.dot(_bf(p), new_sel_ref[0, 1, g], preferred_element_type=f32)) / l_fin
            s_a = jnp.where(ok_win, jnp.dot(q, _bf(win_ref[0, 0, g]), preferred_element_type=f32), NEG)
            s_b = jnp.where(ok_new, _dot_nt(q, new_win_ref[0, 0, g]), NEG)
            m_w = jnp.maximum(jnp.max(s_a, axis=-1, keepdims=True), jnp.max(s_b, axis=-1, keepdims=True))
            p_a = jnp.where(ok_win, jnp.exp(s_a - m_w), 0.0)
            p_b = jnp.where(ok_new, jnp.exp(s_b - m_w), 0.0)
            l_w = jnp.sum(p_a, axis=-1, keepdims=True) + jnp.sum(p_b, axis=-1, keepdims=True)
            o_win = (_dot_nt(p_a, win_ref[0, 1, g])
                     + jnp.dot(_bf(p_b), new_win_ref[0, 1, g], preferred_element_type=f32)) / l_w
            gt = g_ref[0, g]
            o_ref[0, g] = gt[:, 0:1] * oc_s[g] + gt[:, 1:2] * o_sel + gt[:, 2:3] * o_win


def nsa_sample_attention(q, kvc, sel_cache, win_cache, page_table, new_sel, new_win, gates, past):
    DB, G, R, D = q.shape
    DS = R // HPG
    n_pool = sel_cache.shape[0]
    n_pages = page_table.shape[1]
    n_chunk = kvc.shape[3]
    NC = (past + DS) // CMP_STRIDE - 1
    assert past % SEL_BLOCK == 0 and DS < CMP_STRIDE and DS <= NEW_PAD and past // SEL_BLOCK <= LANES
    assert n_pages % PAGES_PER_STEP == 0 and win_cache.shape[1] == WINDOW and NC < n_chunk + 1
    c_start = jnp.arange(n_chunk) * CMP_STRIDE
    s_start = jnp.arange(LANES) * SEL_BLOCK
    cover_t = _bf((c_start[None, :] < s_start[:, None] + SEL_BLOCK)
                  & (c_start[None, :] + CMP_LEN > s_start[:, None])
                  & (jnp.arange(n_chunk)[None, :] < NC) & (jnp.arange(LANES)[:, None] < past // SEL_BLOCK))
    sel_view = jnp.transpose(sel_cache, (0, 2, 3, 4, 1))
    win_view = jnp.transpose(win_cache, (0, 2, 3, 4, 1))
    page_spec = lambda i: pl.BlockSpec((1, 2, G, D, PAGE_SIZE),
                                       lambda b, j, pt: (pt[b, j * PAGES_PER_STEP + i], 0, 0, 0, 0))
    per_b = lambda shape: pl.BlockSpec((1,) + shape, lambda b, j, pt: (b,) + (0,) * len(shape))
    grid_spec = pltpu.PrefetchScalarGridSpec(
        num_scalar_prefetch=1,
        grid=(DB, n_pages // PAGES_PER_STEP),
        in_specs=[page_spec(i) for i in range(PAGES_PER_STEP)]
        + [per_b((G, R, D)), per_b((2, G, n_chunk, D)),
           pl.BlockSpec((LANES, n_chunk), lambda b, j, pt: (0, 0)),
           per_b((2, G, NEW_PAD, D)), per_b((2, G, D, WINDOW)), per_b((2, G, NEW_PAD, D)), per_b((G, R, 3))],
        out_specs=per_b((G, R, D)),
        scratch_shapes=[pltpu.VMEM((G, R, LANES + D), jnp.bfloat16), pltpu.VMEM((G, R, 1), jnp.float32),
                        pltpu.VMEM((G, R, 1), jnp.float32), pltpu.VMEM((G, R, D), jnp.float32),
                        pltpu.VMEM((G, R, D), jnp.float32)],
    )
    return pl.pallas_call(
        functools.partial(_nsa_sample_kernel, DS=DS, NC=NC, past=past),
        grid_spec=grid_spec,
        out_shape=jax.ShapeDtypeStruct((DB, G, R, D), jnp.float32),
        compiler_params=pltpu.CompilerParams(dimension_semantics=("arbitrary", "arbitrary"),
                                             vmem_limit_bytes=VMEM_LIMIT),
        name="nsa_sample",
    )(page_table, *([sel_view] * PAGES_PER_STEP), q, kvc, cover_t, new_sel, win_view, new_win, gates)


def _merge_kernel(x_ref, yr_ref, ya_ref, gr_ref, ga_ref, wpa_ref, wpb_ref, wo_ref, g1_ref, b1_ref, rw_ref,
                  h_ref, lg_ref):
    m = (jax.nn.sigmoid(gr_ref[...]) * _dot(yr_ref[...], wpa_ref[...])
         + jax.nn.sigmoid(ga_ref[...]) * _dot(ya_ref[...], wpb_ref[...]))
    z = DN_ALPHA * x_ref[...] + _dot(m, wo_ref[...])
    mu = jnp.mean(z, axis=-1, keepdims=True)
    dev = z - mu
    var = jnp.mean(dev * dev, axis=-1, keepdims=True)
    h = dev * lax.rsqrt(var + LN_EPS) * g1_ref[...] + b1_ref[...]
    h_ref[...] = h
    lg_ref[...] = _dot(h, rw_ref[...])


def merge_rows(x, y_r, y_a, p_all, g0, w_pa, w_pb, w_o, ln1_g, ln1_b, router_w):
    N, D = x.shape
    W = y_r.shape[1]
    assert g0 % D == 0 and D == D_MODEL
    tm = next(t for t in (384, 256, 128, 64, 8) if N % t == 0)
    rw = jnp.pad(router_w, ((0, 0), (0, LANES - router_w.shape[1])))
    rows = lambda w, c=0: pl.BlockSpec((tm, w), lambda i, c=c: (i, c))
    const = lambda shape: pl.BlockSpec(shape, lambda i: (0,) * len(shape))
    return pl.pallas_call(
        _merge_kernel,
        grid=(N // tm,),
        in_specs=[rows(D), rows(W), rows(W), rows(D, g0 // D), rows(D, g0 // D + 1),
                  const((W, D)), const((W, D)), const((D, D)), const((1, D)), const((1, D)), const((D, LANES))],
        out_specs=[rows(D), rows(LANES)],
        out_shape=[jax.ShapeDtypeStruct((N, D), jnp.float32), jax.ShapeDtypeStruct((N, LANES), jnp.float32)],
        compiler_params=pltpu.CompilerParams(dimension_semantics=("arbitrary",), vmem_limit_bytes=VMEM_LIMIT),
        name="merge_rows",
    )(x, y_r, y_a, p_all, p_all, _bf(w_pa), _bf(w_pb), _bf(w_o), ln1_g.reshape(1, D), ln1_b.reshape(1, D), _bf(rw))


def _moe_mlp_kernel(be_ref, base_ref, nval_ref, order_ref, x_hbm, w1_ref, b1_ref, w2_ref, b2_ref, o_hbm,
                    w1s, w2s, xbuf, ybuf, in_sem, out_sem, *, BM, T):
    i = pl.program_id(0)
    n = pl.num_programs(0)
    s = i % 2

    def gather(block, sl):
        base = base_ref[block]

        def body(r, c):
            tok = order_ref[base + r] >> K_SHIFT
            pltpu.make_async_copy(x_hbm.at[tok], xbuf.at[sl, r], in_sem.at[sl]).start()
            return c
        lax.fori_loop(0, BM, body, 0, unroll=8)

    @pl.when(i == 0)
    def _():
        gather(0, 0)

    @pl.when(i + 1 < n)
    def _():
        gather(i + 1, 1 - s)

    e = be_ref[i]
    prev = be_ref[jnp.maximum(i - 1, 0)]

    @pl.when((i == 0) | (e != prev))
    def _():
        w1s[...] = w1_ref[0].astype(jnp.bfloat16)
        w2s[...] = w2_ref[0].astype(jnp.bfloat16)

    def rows_in(sl):
        return pltpu.make_async_copy(x_hbm.at[pl.ds(0, BM)], xbuf.at[sl], in_sem.at[sl])

    def rows_out(sl):
        return pltpu.make_async_copy(ybuf.at[sl], o_hbm.at[pl.ds(0, BM)], out_sem.at[sl])

    rows_in(s).wait()

    @pl.when(i >= 2)
    def _():
        rows_out(s).wait()

    x = jnp.concatenate([xbuf[s, :, c, :] for c in range(SUBLANES)], axis=1).astype(jnp.bfloat16)
    h = jnp.dot(x, w1s[...], preferred_element_type=jnp.float32) + b1_ref[0]
    glu = jnp.minimum(h[:, :D_FF], SWIGLU_LIMIT)
    lin = jnp.clip(h[:, D_FF:], -SWIGLU_LIMIT, SWIGLU_LIMIT)
    act = glu * jax.nn.sigmoid(SWIGLU_ALPHA * glu) * (lin + 1.0)
    y = jnp.dot(act.astype(jnp.bfloat16), w2s[...], preferred_element_type=jnp.float32) + b2_ref[0]
    for c in range(SUBLANES):
        ybuf[s, :, c, :] = y[:, c * LANES:(c + 1) * LANES]

    base = base_ref[i]
    nval = nval_ref[i]
    spare = TOP_K * T + i * BM - base - nval

    def scatter(r, c):
        a = order_ref[base + r]
        row = jnp.where(r < nval, (a & (TOP_K - 1)) * T + (a >> K_SHIFT), spare + r)
        pltpu.make_async_copy(ybuf.at[s, r], o_hbm.at[row], out_sem.at[s]).start()
        return c
    lax.fori_loop(0, BM, scatter, 0, unroll=8)

    @pl.when(i == n - 1)
    def _():
        rows_out(s).wait()

    @pl.when((i == n - 1) & (n >= 2))
    def _():
        rows_out(1 - s).wait()


def moe_mlp(x, order, block_e, block_base, block_nval, w1, b1, w2, b2, BM):
    T, D = x.shape
    assert D == SUBLANES * LANES
    n_blocks = block_e.shape[0]
    E = w1.shape[0]
    grid_spec = pltpu.PrefetchScalarGridSpec(
        num_scalar_prefetch=4,
        grid=(n_blocks,),
        in_specs=[pl.BlockSpec(memory_space=pl.ANY),
                  pl.BlockSpec((1, D, 2 * D_FF), lambda i, be, *_: (be[i], 0, 0)),
                  pl.BlockSpec((1, 1, 2 * D_FF), lambda i, be, *_: (be[i], 0, 0)),
                  pl.BlockSpec((1, D_FF, D), lambda i, be, *_: (be[i], 0, 0)),
                  pl.BlockSpec((1, 1, D), lambda i, be, *_: (be[i], 0, 0))],
        out_specs=pl.BlockSpec(memory_space=pl.ANY),
        scratch_shapes=[pltpu.VMEM((D, 2 * D_FF), jnp.bfloat16), pltpu.VMEM((D_FF, D), jnp.bfloat16),
                        pltpu.VMEM((2, BM, SUBLANES, LANES), jnp.float32),
                        pltpu.VMEM((2, BM, SUBLANES, LANES), jnp.float32),
                        pltpu.SemaphoreType.DMA((2,)), pltpu.SemaphoreType.DMA((2,))],
    )
    return pl.pallas_call(
        functools.partial(_moe_mlp_kernel, BM=BM, T=T),
        grid_spec=grid_spec,
        out_shape=jax.ShapeDtypeStruct((n_blocks * BM, SUBLANES, LANES), jnp.float32),
        compiler_params=pltpu.CompilerParams(dimension_semantics=("arbitrary",),
                                             vmem_limit_bytes=VMEM_LIMIT),
        name="moe_mlp",
    )(block_e, block_base, block_nval, order, x.reshape(T, SUBLANES, LANES), w1, b1.reshape(E, 1, -1), w2, b2.reshape(E, 1, -1))


def _combine_kernel(h_ref, gate_ref, *refs):
    slabs = refs[:TOP_K]
    g2_ref, b2_ref, o_ref = refs[TOP_K:]
    gate = gate_ref[...]
    f = None
    for k in range(TOP_K):
        rows = jnp.concatenate([slabs[k][:, c, :] for c in range(SUBLANES)], axis=1)
        term = gate[:, k:k + 1] * rows
        f = term if f is None else f + term
    z = DN_ALPHA * h_ref[...] + f
    mu = jnp.mean(z, axis=-1, keepdims=True)
    dev = z - mu
    var = jnp.mean(dev * dev, axis=-1, keepdims=True)
    o_ref[...] = dev * lax.rsqrt(var + LN_EPS) * g2_ref[...] + b2_ref[...]


def combine_rows(h, gate, yb, ln2_g, ln2_b):
    T, D = h.shape
    tm = next(t for t in (384, 256, 128, 64, 8) if T % t == 0)
    nb = T // tm
    slab = lambda k: pl.BlockSpec((tm, SUBLANES, LANES), lambda i, k=k: (k * nb + i, 0, 0))
    rows = lambda w: pl.BlockSpec((tm, w), lambda i: (i, 0))
    const = lambda shape: pl.BlockSpec(shape, lambda i: (0,) * len(shape))
    return pl.pallas_call(
        _combine_kernel,
        grid=(nb,),
        in_specs=[rows(D), rows(TOP_K)] + [slab(k) for k in range(TOP_K)] + [const((1, D)), const((1, D))],
        out_specs=rows(D),
        out_shape=jax.ShapeDtypeStruct((T, D), jnp.float32),
        compiler_params=pltpu.CompilerParams(dimension_semantics=("arbitrary",), vmem_limit_bytes=VMEM_LIMIT),
        name="combine_rows",
    )(h, gate, *([yb] * TOP_K), ln2_g.reshape(1, D), ln2_b.reshape(1, D))


def moe(x, logits, mlp1_w, mlp1_b, mlp2_w, mlp2_b, BM=256):
    T, D = x.shape
    top_v, top_e = lax.top_k(logits, TOP_K)
    gate = jax.nn.softmax(top_v, axis=-1)
    n_assign = T * TOP_K
    _, order = lax.sort((top_e.reshape(-1).astype(jnp.int32), jnp.arange(n_assign, dtype=jnp.int32)), num_keys=1)
    counts = jnp.sum(top_e.reshape(-1, 1) == jnp.arange(N_EXPERTS)[None, :], axis=0).astype(jnp.int32)
    padded = (counts + BM - 1) // BM * BM
    start = jnp.cumsum(counts) - counts
    pend = jnp.cumsum(padded)
    pstart = pend - padded
    n_blocks = -(-n_assign // BM) + N_EXPERTS
    row0 = jnp.arange(n_blocks, dtype=jnp.int32) * BM
    block_e = jnp.minimum(jnp.sum(pend[None, :] <= row0[:, None], axis=1), N_EXPERTS - 1).astype(jnp.int32)
    off = row0 - pstart[block_e]
    block_nval = jnp.clip(counts[block_e] - off, 0, BM).astype(jnp.int32)
    block_base = jnp.clip(start[block_e] + off, 0, n_assign).astype(jnp.int32)
    order = jnp.concatenate([order, jnp.zeros((BM,), jnp.int32)])
    yb = moe_mlp(x, order, block_e, block_base, block_nval, mlp1_w, mlp1_b, mlp2_w, mlp2_b, BM)
    return gate, yb


def rope(x, pos):
    half = ROT_DIM // 2
    inv = ROPE_THETA ** (-jnp.arange(half, dtype=jnp.float32) * 2.0 / ROT_DIM)
    ang = pos.astype(jnp.float32)[:, None] * inv
    cos, sin = jnp.cos(ang)[:, None, :], jnp.sin(ang)[:, None, :]
    x1, x2 = x[..., :half], x[..., half:ROT_DIM]
    return jnp.concatenate([x1 * cos - x2 * sin, x2 * cos + x1 * sin, x[..., ROT_DIM:]], axis=-1)


G0 = R_COLS + A_WIDTH + 6 * KV_WIDTH


def project(x2, w_in):
    w_main = jnp.concatenate([w_in[:, :G0], w_in[:, G0 + N_GATE:]], axis=1)
    w_gate = jnp.pad(w_in[:, G0:G0 + N_GATE], ((0, 0), (0, LANES - N_GATE)))
    return pallas_matmul(x2, w_main), pallas_matmul(x2, w_gate)[:, :N_GATE]


def split_projection(p, pg, B, T, pos):
    pr = p[:, :R_COLS].reshape(B, T, R_COLS)
    pa = p[:, R_COLS:G0].reshape(B, T, G0 - R_COLS)
    q = rope(pa[..., :A_WIDTH].reshape(B, T, N_HEADS, HEAD_DIM), pos)
    kvs = [pa[..., A_WIDTH + i * KV_WIDTH:A_WIDTH + (i + 1) * KV_WIDTH].reshape(B, T, N_KV, HEAD_DIM)
           for i in range(6)]
    kvs = [rope(z, pos) if i % 2 == 0 else z for i, z in enumerate(kvs)]
    gates = jax.nn.sigmoid(pg).reshape(B, T, N_KV, HPG, 3)
    return pr, q, kvs, gates


def rwkv_prep_rows(pr, shift_prev, mu, w0, w_w2, a0, w_a2, g_w2, k_k, k_a, r_k):
    B, T, _ = pr.shape
    prev = jnp.concatenate([shift_prev[:, None, :], pr[:, :-1]], axis=1)
    xm = pr + (prev - pr) * mu
    o1, o2, o3 = R_WIDTH, 2 * R_WIDTH, 3 * R_WIDTH
    o4 = o3 + LORA_W
    o5 = o4 + LORA_A
    r, k, v = xm[..., :o1], xm[..., o1:o2], xm[..., o2:o3]
    xw, xa, xg = xm[..., o3:o4], xm[..., o4:o5], xm[..., o5:]
    w_log = -jax.nn.softplus(-(w0 + jnp.tanh(xw) @ w_w2)) - 0.5
    a = jax.nn.sigmoid(a0 + xa @ w_a2)
    g = jax.nn.sigmoid(xg) @ g_w2
    heads = lambda z: z.reshape(B, T, R_HEADS, R_HEAD)
    flat = lambda z: z.reshape(B, T, R_WIDTH)
    kk = heads(k * k_k)
    kk = flat(kk / jnp.maximum(jnp.linalg.norm(kk, axis=-1, keepdims=True), 1e-12))
    k_h = k * (1.0 + (a - 1.0) * k_a)
    bonus = flat(jnp.sum(heads(r * k_h * r_k.reshape(-1)), axis=-1, keepdims=True) * heads(v))
    return r, -jnp.exp(w_log), k_h, v, -kk, kk * a, bonus, g


def rwkv_mixer(parts, wkv0, gn_g, gn_b):
    B, T, _ = parts[0].shape
    C = 64 if T % 64 == 0 else 8
    Tp = -(-T // C) * C
    padded = [jnp.pad(z, ((0, 0), (0, Tp - T), (0, 0))) for z in parts]
    y, wkv = wkv_chunked(*padded, gn_g, gn_b, wkv0, C)
    return y[:, :T], wkv


def compress(kv, pe, w1, b1, w2, b2):
    B, L = kv.shape[:2]
    n_chunk = L // CMP_STRIDE
    ch = kv[:, :n_chunk * CMP_STRIDE].reshape(B, n_chunk, CMP_STRIDE, N_KV, HEAD_DIM)
    blk = jnp.concatenate([ch[:, :-1], ch[:, 1:]], axis=2) + pe[:, None, :]
    blk = jnp.transpose(blk, (0, 3, 1, 2, 4)).reshape(B, N_KV, n_chunk - 1, CMP_LEN * HEAD_DIM)
    return jax.nn.gelu(blk @ w1 + b1) @ w2 + b2


def nsa_prompt(q, kvs, gates, pe, w1, b1, w2, b2):
    kc_raw, vc_raw, ks, vs, kw, vw = kvs
    B, T = q.shape[:2]
    kc = compress(kc_raw, pe[0], w1[0], b1[0], w2[0], b2[0])
    vc = compress(vc_raw, pe[1], w1[1], b1[1], w2[1], b2[1])
    qg = _bf(jnp.transpose(q.reshape(B, T, N_KV, HPG, HEAD_DIM), (0, 2, 3, 1, 4)) * (HEAD_DIM ** -0.5 * LOG2E))
    tk = lambda z: _bf(jnp.swapaxes(z, 1, 2))
    o = nsa_prompt_attention(qg, kc, vc, tk(ks), tk(vs), tk(kw), tk(vw),
                             jnp.transpose(gates, (0, 2, 3, 1, 4)))
    return jnp.transpose(o, (0, 3, 1, 2, 4)).reshape(B, T, A_WIDTH)


def nsa_sample(q, kvs, gates, cmp_cache, sel_cache, win_cache, page_table, pe, w1, b1, w2, b2):
    kc_new, vc_new, ks_new, vs_new, kw_new, vw_new = kvs
    DB, DS = q.shape[:2]
    past = page_table.shape[1] * PAGE_SIZE
    kvc = compress_sample(cmp_cache, page_table, pe, w1, b1, w2, b2)
    qg = _bf(jnp.transpose(q.reshape(DB, DS, N_KV, HPG, HEAD_DIM), (0, 2, 3, 1, 4)) * (HEAD_DIM ** -0.5))
    qg = qg.reshape(DB, N_KV, HPG * DS, HEAD_DIM)
    gt = jnp.transpose(gates, (0, 2, 3, 1, 4)).reshape(DB, N_KV, HPG * DS, 3)
    pack = lambda k, v: _bf(jnp.pad(jnp.transpose(jnp.stack([k, v], axis=1), (0, 1, 3, 2, 4)),
                                    ((0, 0), (0, 0), (0, 0), (0, NEW_PAD - DS), (0, 0))))
    o = nsa_sample_attention(qg, kvc, sel_cache, win_cache, page_table, pack(ks_new, vs_new),
                             pack(kw_new, vw_new), gt, past)
    o = jnp.transpose(o.reshape(DB, N_KV, HPG, DS, HEAD_DIM), (0, 3, 1, 2, 4)).reshape(DB, DS, -1)
    new_k = jnp.concatenate([win_cache[:, DS:, 0], kw_new], axis=1)
    new_v = jnp.concatenate([win_cache[:, DS:, 1], vw_new], axis=1)
    return o, jnp.stack([new_k, new_v], axis=2)


def merge_and_ffn(x, y_r, y_a, p_all, w_pa, w_pb, w_o, ln1_g, ln1_b, router_w, router_b,
                  mlp1_w, mlp1_b, mlp2_w, mlp2_b, ln2_g, ln2_b):
    h, logits = merge_rows(x, y_r, y_a, p_all, G0, w_pa, w_pb, w_o, ln1_g, ln1_b, router_w)
    gate, yb = moe(h, logits[:, :N_EXPERTS] + router_b, mlp1_w, mlp1_b, mlp2_w, mlp2_b)
    return combine_rows(h, gate, yb, ln2_g, ln2_b)


def kernel(x_prompt, x_sample, cache_cmp_kv, cache_sel_kv, cache_win_kv, state_wkv, state_shift,
           page_table, w_in, mu_shift, w0, w_w2, a0, w_a2, g_w2, k_k, k_a, r_k, gn_g, gn_b,
           cmp_pe, cmp_w1, cmp_b1, cmp_w2, cmp_b2, w_pa, w_pb, w_o, ln1_g, ln1_b,
           router_w, router_b, mlp1_w, mlp1_b, mlp2_w, mlp2_b, ln2_g, ln2_b):
    B, T, D = x_prompt.shape
    DB, DS, _ = x_sample.shape
    n_p = B * T
    past = page_table.shape[1] * PAGE_SIZE
    pos_p = jnp.arange(T)
    pos_s = past + jnp.arange(DS)
    wb_p = min(WINDOW, T)
    h_all = jnp.concatenate([x_prompt.reshape(n_p, D), x_sample.reshape(DB * DS, D)])
    cmp_p, sel_p, win_p, wkv_p, shift_p = [], [], [], [], []
    cmp_s, sel_s, win_s, wkv_s, shift_s = [], [], [], [], []
    for l in range(DEPTH):
        rwkv_w = (mu_shift[l], w0[l], w_w2[l], a0[l], w_a2[l], g_w2[l], k_k[l], k_a[l], r_k[l],
                  gn_g[l], gn_b[l])
        cmp_w = (cmp_pe[l], cmp_w1[l], cmp_b1[l], cmp_w2[l], cmp_b2[l])
        out_w = (w_pa[l], w_pb[l], w_o[l], ln1_g[l], ln1_b[l], router_w[l], router_b[l],
                 mlp1_w[l], mlp1_b[l], mlp2_w[l], mlp2_b[l], ln2_g[l], ln2_b[l])
        p_all, pg_all = project(h_all, w_in[l])
        pr, q, kvs, gates = split_projection(p_all[:n_p], pg_all[:n_p], B, T, pos_p)
        parts = rwkv_prep(p_all, n_p, T, jnp.zeros((B, R_COLS), jnp.float32), *rwkv_w[:9])
        y_r, wkv = rwkv_mixer([z.reshape(B, T, R_WIDTH) for z in parts],
                              jnp.zeros((B, R_HEADS, R_HEAD, R_HEAD), jnp.float32), *rwkv_w[9:])
        shift = p_all[T - 1:n_p:T, :R_COLS]
        y_a = nsa_prompt(q, kvs, gates, *cmp_w)
        cmp_p.append(jnp.stack([kvs[0], kvs[1]], axis=2))
        sel_p.append(jnp.stack([kvs[2], kvs[3]], axis=2))
        win_p.append(jnp.stack([kvs[4][:, T - wb_p:], kvs[5][:, T - wb_p:]], axis=2))
        wkv_p.append(wkv)
        shift_p.append(shift)
        pr, q, kvs, gates = split_projection(p_all[n_p:], pg_all[n_p:], DB, DS, pos_s)
        y_r_s, wkv = rwkv_mixer(rwkv_prep_rows(pr, state_shift[l], *rwkv_w[:9]), state_wkv[l], *rwkv_w[9:])
        shift = pr[:, -1]
        y_a_s, new_win = nsa_sample(q, kvs, gates, cache_cmp_kv[l], cache_sel_kv[l], cache_win_kv[l],
                                    page_table, *cmp_w)
        cmp_s.append(jnp.stack([kvs[0], kvs[1]], axis=2))
        sel_s.append(jnp.stack([kvs[2], kvs[3]], axis=2))
        win_s.append(new_win)
        wkv_s.append(wkv)
        shift_s.append(shift)
        rows = lambda a, b: jnp.concatenate([a.reshape(n_p, -1), b.reshape(DB * DS, -1)])
        h_all = merge_and_ffn(h_all, rows(y_r, y_r_s), rows(y_a, y_a_s), p_all, *out_w)
    hp = h_all[:n_p].reshape(B, T, D)
    hs = h_all[n_p:].reshape(DB, DS, D)
    return (hp, hs, jnp.stack(cmp_p), jnp.stack(sel_p), jnp.stack(win_p), jnp.stack(wkv_p),
            jnp.stack(shift_p), jnp.stack(cmp_s), jnp.stack(sel_s), jnp.stack(win_s),
            jnp.stack(wkv_s), jnp.stack(shift_s))
```

```python
import functools

import jax
import jax.numpy as jnp
from jax import lax
from jax.experimental import pallas as pl
from jax.experimental.pallas import tpu as pltpu

D_MODEL = 1024
DEPTH = 1
PAGE_SIZE = 128

R_HEADS = 8
R_HEAD = 64
R_WIDTH = R_HEADS * R_HEAD
LORA_W = 64
LORA_A = 64
LORA_G = 128
R_COLS = 3 * R_WIDTH + LORA_W + LORA_A + LORA_G
GN_EPS = 64e-5

N_HEADS = 8
N_KV = 2
HPG = N_HEADS // N_KV
HEAD_DIM = 64
A_WIDTH = N_HEADS * HEAD_DIM
KV_WIDTH = N_KV * HEAD_DIM
N_GATE = 3 * N_HEADS
A_COLS = A_WIDTH + 6 * KV_WIDTH + N_GATE
ROT_DIM = HEAD_DIM // 4
ROPE_THETA = 500000.0
CMP_STRIDE = 16
CMP_LEN = 2 * CMP_STRIDE
CMP_HIDDEN = 256
SEL_BLOCK = 64
SEL_TOP = 16
WINDOW = 512

N_EXPERTS = 32
TOP_K = 4
K_SHIFT = 2
D_FF = 1024
SWIGLU_LIMIT = 7.0
SWIGLU_ALPHA = 1.702

DN_ALPHA = (2 * DEPTH) ** 0.25
LN_EPS = 1e-5
NEG = -1e30
LOG2E = 1.4426950408889634

LANES = 128
SUBLANES = 8
PAGES_PER_STEP = 32
NEW_PAD = 8
VMEM_LIMIT = 56 * 1024 * 1024


def _bf(x):
    return x.astype(jnp.bfloat16)


def _dot(a, b):
    return jnp.dot(_bf(a), _bf(b), preferred_element_type=jnp.float32)


def _dot_nt(a, b):
    return lax.dot_general(_bf(a), _bf(b), (((1,), (1,)), ((), ())),
                           preferred_element_type=jnp.float32)


def _mm_kernel(x_ref, w_ref, o_ref):
    o_ref[...] = _dot(x_ref[...], w_ref[...])


def pallas_matmul(x, w):
    x, w = _bf(x), _bf(w)
    M, K = x.shape
    N = w.shape[1]
    tm = next(t for t in (512, 384, 256, 128, M) if M % t == 0)
    resident = 2 * (2 * K * N + 4 * tm * N + 2 * tm * K) <= VMEM_LIMIT - (8 << 20)
    tn = N if resident else next(t for t in (512, 256, LANES) if N % t == 0)
    return pl.pallas_call(
        _mm_kernel,
        grid=(N // tn, M // tm),
        in_specs=[pl.BlockSpec((tm, K), lambda j, i: (i, 0)),
                  pl.BlockSpec((K, tn), lambda j, i: (0, j))],
        out_specs=pl.BlockSpec((tm, tn), lambda j, i: (i, j)),
        out_shape=jax.ShapeDtypeStruct((M, N), jnp.float32),
        compiler_params=pltpu.CompilerParams(vmem_limit_bytes=VMEM_LIMIT),
        name="mm",
    )(x, w)


def _bmm(a, b):
    return lax.dot_general(_bf(a), _bf(b), (((2,), (1,)), ((0,), (0,))), preferred_element_type=jnp.float32)


def _bmm_nt(a, b):
    return lax.dot_general(_bf(a), _bf(b), (((2,), (2,)), ((0,), (0,))), preferred_element_type=jnp.float32)


def _bmm_tn(a, b):
    return lax.dot_general(_bf(a), _bf(b), (((1,), (1,)), ((0,), (0,))), preferred_element_type=jnp.float32)


def _head_sum(x, ones_bd):
    hi = _bf(x)
    rem = x - hi.astype(jnp.float32)
    mid = _bf(rem)
    lo = _bf(rem - mid.astype(jnp.float32))
    dot = lambda t: jnp.dot(t, ones_bd, preferred_element_type=jnp.float32)
    return dot(hi) + dot(mid) + dot(lo)


def head_ones():
    h = jnp.arange(R_WIDTH) // R_HEAD
    return _bf(h[:, None] == h[None, :])


def _rwkv_prep_kernel(p_ref, pb_ref, sp_ref, mu_ref, w0_ref, ww2_ref, a0_ref, wa2_ref, gw2_ref,
                      kk_ref, ka_ref, rk_ref, bd_ref,
                      r_ref, lw_ref, k_ref, v_ref, a_ref, b_ref, bonus_ref, g_ref, *, tiles_per_seq):
    i = pl.program_id(0)
    pr = p_ref[...]
    first = jnp.where(i % tiles_per_seq == 0, sp_ref[0], pb_ref[SUBLANES - 1:SUBLANES, :])
    rolled = pltpu.roll(pr, shift=1, axis=0)
    prev = jnp.where(lax.broadcasted_iota(jnp.int32, (pr.shape[0], 1), 0) == 0, first, rolled)
    xm = pr + (prev - pr) * mu_ref[...]
    o1, o2, o3 = R_WIDTH, 2 * R_WIDTH, 3 * R_WIDTH
    o4 = o3 + LORA_W
    o5 = o4 + LORA_A
    r, k, v = xm[:, :o1], xm[:, o1:o2], xm[:, o2:o3]
    xw, xa, xg = xm[:, o3:o4], xm[:, o4:o5], xm[:, o5:]
    dot = lambda x, w_ref: jnp.dot(_bf(x), _bf(w_ref[...]), preferred_element_type=jnp.float32)
    w_log = -jax.nn.softplus(-(w0_ref[...] + dot(jnp.tanh(xw), ww2_ref))) - 0.5
    a = jax.nn.sigmoid(a0_ref[...] + dot(xa, wa2_ref))
    g_ref[...] = dot(jax.nn.sigmoid(xg), gw2_ref)
    ones_bd = bd_ref[...]
    kk = k * kk_ref[...]
    kk = kk / jnp.maximum(jnp.sqrt(_head_sum(kk * kk, ones_bd)), 1e-12)
    k_h = k * (1.0 + (a - 1.0) * ka_ref[...])
    r_ref[...] = r
    lw_ref[...] = -jnp.exp(w_log)
    k_ref[...] = k_h
    v_ref[...] = v
    a_ref[...] = -kk
    b_ref[...] = kk * a
    bonus_ref[...] = _head_sum(r * k_h * rk_ref[...], ones_bd) * v


def rwkv_prep(p_all, n_rows, seq_len, shift_prev, mu, w0, w_w2, a0, w_a2, g_w2, k_k, k_a, r_k, tm=256):
    assert seq_len % tm == 0 and n_rows % seq_len == 0
    n_seq = n_rows // seq_len
    tiles_per_seq = seq_len // tm
    row = lambda z: z.reshape(1, -1)
    const = lambda shape: pl.BlockSpec(shape, lambda i: (0,) * len(shape))
    out = pl.BlockSpec((tm, R_WIDTH), lambda i: (i, 0))
    outs = pl.pallas_call(
        functools.partial(_rwkv_prep_kernel, tiles_per_seq=tiles_per_seq),
        grid=(n_rows // tm,),
        in_specs=[pl.BlockSpec((tm, R_COLS), lambda i: (i, 0)),
                  pl.BlockSpec((SUBLANES, R_COLS), lambda i: (jnp.maximum(i * (tm // SUBLANES) - 1, 0), 0)),
                  pl.BlockSpec((1, 1, R_COLS), lambda i: (i // tiles_per_seq, 0, 0)),
                  const((1, R_COLS)), const((1, R_WIDTH)), const((LORA_W, R_WIDTH)), const((1, R_WIDTH)),
                  const((LORA_A, R_WIDTH)), const((LORA_G, R_WIDTH)), const((1, R_WIDTH)), const((1, R_WIDTH)),
                  const((1, R_WIDTH)), const((R_WIDTH, R_WIDTH))],
        out_specs=[out] * 8,
        out_shape=[jax.ShapeDtypeStruct((n_rows, R_WIDTH), jnp.float32)] * 8,
        compiler_params=pltpu.CompilerParams(dimension_semantics=("arbitrary",), vmem_limit_bytes=VMEM_LIMIT),
        name="rwkv_prep",
    )(p_all, p_all, shift_prev.reshape(n_seq, 1, R_COLS), row(mu), row(w0), w_w2, row(a0), w_a2, g_w2,
      row(k_k), row(k_a), row(r_k), head_ones())
    return outs


def _wkv_chunk_kernel(r_ref, lw_ref, k_ref, v_ref, a_ref, b_ref, bonus_ref, g_ref, gng_ref, gnb_ref, bd_ref,
                      s0_ref, y_ref, s_out_ref, s_scr, *, C, H, NB):
    c = pl.program_id(1)
    D = R_HEAD

    @pl.when(c == 0)
    def _():
        s_scr[...] = s0_ref[...].reshape(NB * H, D, D)

    row = lax.broadcasted_iota(jnp.int32, (NB * H, C, C), 1)
    col = lax.broadcasted_iota(jnp.int32, (NB * H, C, C), 2)
    incl = row >= col
    strict = row > col
    ltri = jnp.where(incl, 1.0, 0.0).astype(jnp.bfloat16)
    eye = jnp.where(row == col, 1.0, 0.0).astype(jnp.float32)

    heads = lambda ref: jnp.stack([ref[n, :, h * D:(h + 1) * D] for n in range(NB) for h in range(H)])
    lw = heads(lw_ref)
    r = heads(r_ref)
    k = heads(k_ref)
    v = heads(v_ref)
    a = heads(a_ref)
    b = heads(b_ref)
    hi = _bf(lw)
    rem = lw - hi.astype(jnp.float32)
    mid = _bf(rem)
    lo = _bf(rem - mid.astype(jnp.float32))
    lp = _bmm(ltri, hi) + _bmm(ltri, mid) + _bmm(ltri, lo)
    lp_end = lp[:, C - 1:C, :]
    p_end = jnp.exp(lp_end)
    p_inv = jnp.exp(-lp)
    at = a * jnp.exp(lp - lw)
    rt = r * jnp.exp(lp)
    bt = b * p_inv
    kt = k * p_inv
    p_hat = jnp.exp(lp_end - lp)
    bh = b * p_hat
    kh = k * p_hat

    n_ab = jnp.where(strict, _bmm_nt(at, bt), 0.0)
    a_ak = jnp.where(strict, _bmm_nt(at, kt), 0.0)
    a_rb = jnp.where(incl, _bmm_nt(rt, bt), 0.0)
    a_rk = jnp.where(incl, _bmm_nt(rt, kt), 0.0)

    t_inv = eye + n_ab
    n_pow = n_ab
    span = 2
    while span < C:
        n_pow = _bmm(n_pow, n_pow)
        t_inv = _bmm(t_inv, eye + n_pow)
        span *= 2

    s = s_scr[...]
    rhs = _bmm_nt(at, s) + _bmm(a_ak, v)
    u = _bmm(t_inv, rhs)
    y = _bmm_nt(rt, s) + _bmm(a_rb, u) + _bmm(a_rk, v)
    s_new = s * p_end + _bmm_tn(u, bh) + _bmm_tn(v, kh)
    s_scr[...] = s_new

    ones_bd = bd_ref[...]
    for n in range(NB):
        yt = jnp.concatenate([y[n * H + h] for h in range(H)], axis=1)
        dev = yt - _head_sum(yt, ones_bd) * (1.0 / D)
        var = _head_sum(dev * dev, ones_bd) * (1.0 / D)
        yn = dev * lax.rsqrt(var + GN_EPS) * gng_ref[...] + gnb_ref[...]
        y_ref[n] = (yn + bonus_ref[n]) * g_ref[n]

    @pl.when(c == pl.num_programs(1) - 1)
    def _():
        s_out_ref[...] = s_new.reshape(NB, H, D, D)


def wkv_chunked(r, lw, k, v, a, b, bonus, g, gn_g, gn_b, s0, C):
    B, T, W = r.shape
    H, D = R_HEADS, R_HEAD
    assert T % C == 0
    NB = 2 if B % 2 == 0 else 1
    seq = pl.BlockSpec((NB, C, W), lambda bi, ci: (bi, ci, 0))
    st = pl.BlockSpec((NB, H, D, D), lambda bi, ci: (bi, 0, 0, 0))
    const = lambda shape: pl.BlockSpec(shape, lambda bi, ci: (0,) * len(shape))
    return pl.pallas_call(
        functools.partial(_wkv_chunk_kernel, C=C, H=H, NB=NB),
        grid=(B // NB, T // C),
        in_specs=[seq] * 8 + [const((1, W)), const((1, W)), const((W, W)), st],
        out_specs=[seq, st],
        out_shape=[jax.ShapeDtypeStruct((B, T, W), jnp.float32),
                   jax.ShapeDtypeStruct((B, H, D, D), jnp.float32)],
        scratch_shapes=[pltpu.VMEM((NB * H, D, D), jnp.float32)],
        compiler_params=pltpu.CompilerParams(dimension_semantics=("arbitrary", "arbitrary")),
        name="wkv_chunk",
    )(r, lw, k, v, a, b, bonus, g, gn_g.reshape(1, W), gn_b.reshape(1, W), head_ones(), s0)


def _nsa_prompt_kernel(q_ref, kc_ref, vc_ref, cover_ref, ks_ref, vs_ref, kw_ref, vw_ref, g_ref, o_ref,
                       *, TQ, TK, NC, NCP):
    f32 = jnp.float32
    bf16 = jnp.bfloat16
    qb = pl.program_id(2)
    R = HPG * TQ
    q = q_ref[0, 0].reshape(R, HEAD_DIM)
    t_pos = qb * TQ + lax.broadcasted_iota(jnp.int32, (TQ, 1), 0)

    n_idx = lax.broadcasted_iota(jnp.int32, (1, NCP), 1)
    c_ok = ((n_idx * CMP_STRIDE + (CMP_LEN - 1)) <= t_pos) & (n_idx < NC)
    s_c = _dot_nt(q, kc_ref[0, 0]).reshape(HPG, TQ, NCP)
    s_c = jnp.where(c_ok[None], s_c, NEG)
    m_c = jnp.max(s_c, axis=-1, keepdims=True)
    p_c = jnp.where(c_ok[None], jnp.exp2(s_c - m_c), 0.0)
    l_c = jnp.sum(p_c, axis=-1, keepdims=True)
    p_c = p_c / jnp.where(l_c > 0.0, l_c, 1.0)
    p_cb = p_c.astype(bf16)
    o_c = jnp.dot(p_cb.reshape(R, NCP), vc_ref[0, 0], preferred_element_type=f32)

    cover_t = cover_ref[...]
    imp = _dot_nt(cover_t, p_cb[0])
    for h in range(1, HPG):
        imp = imp + _dot_nt(cover_t, p_cb[h])
    s_col = lax.broadcasted_iota(jnp.int32, (LANES, 1), 0)
    t_row = qb * TQ + lax.broadcasted_iota(jnp.int32, (1, TQ), 1)
    cur = t_row // SEL_BLOCK
    forced = (s_col == 0) | (s_col == cur) | (s_col == cur - 1)
    causal = (s_col * SEL_BLOCK) <= t_row
    score = jnp.where(forced, 1e6, imp)
    score = jnp.where(causal, score, NEG)
    s_col_f = s_col.astype(f32)
    sel_t = jnp.zeros((LANES, TQ), f32)
    for _ in range(SEL_TOP):
        top = jnp.max(score, axis=0, keepdims=True)
        first = jnp.min(jnp.where(score == top, s_col_f, float(LANES)), axis=0, keepdims=True)
        hit = s_col_f == first
        sel_t = jnp.where(hit & (top > 0.5 * NEG), 1.0, sel_t)
        score = jnp.where(hit, -3e38, score)
    sel_neg = ((sel_t.T - 1.0) * (-NEG)).astype(bf16)
    q_aug = jnp.concatenate([jnp.concatenate([sel_neg] * HPG, axis=0), q], axis=1)
    key_off = lax.broadcasted_iota(jnp.int32, (1, TK), 1)

    def sel_tile(j, carry, diagonal):
        m, acc = carry
        start = pl.multiple_of(j * TK, TK)
        k = ks_ref[0, 0, pl.ds(start, TK), :]
        v = vs_ref[0, 0, pl.ds(start, TK), :]
        s = _dot_nt(q_aug, k).reshape(HPG, TQ, TK)
        if diagonal:
            s = jnp.where(((key_off + j * TK) <= t_pos)[None], s, NEG)
        m_new = jnp.maximum(m, jnp.max(s, axis=-1, keepdims=True))
        p = jnp.exp2((s - m_new).astype(bf16))
        alpha = jnp.exp2(m - m_new)
        pv = jnp.dot(p.reshape(R, TK), v, preferred_element_type=f32)
        acc = alpha * acc + pv.reshape(HPG, TQ, LANES)
        return m_new, acc

    m0 = jnp.full((HPG, TQ, 1), NEG, f32)
    a0 = jnp.zeros((HPG, TQ, LANES), f32)
    n_full = (qb * TQ) // TK
    carry = lax.fori_loop(0, n_full, functools.partial(sel_tile, diagonal=False), (m0, a0))
    _, acc_s = sel_tile(n_full, carry, True)
    o_s = acc_s[:, :, :HEAD_DIM] / acc_s[:, :, HEAD_DIM:HEAD_DIM + 1]

    n_w = WINDOW // TQ + 1
    lane_q = lax.broadcasted_iota(jnp.int32, (1, TQ), 1)
    s_w, v_w = [], []
    for i in range(n_w):
        kb = qb - (n_w - 1) + i
        start = pl.multiple_of(jnp.maximum(kb, 0) * TQ, TQ)
        k = kw_ref[0, 0, pl.ds(start, TQ), :]
        v_w.append(vw_ref[0, 0, pl.ds(start, TQ), :])
        kpos = kb * TQ + lane_q
        ok = (kpos <= t_pos) & (kpos >= t_pos - WINDOW) & (kpos >= 0)
        s_w.append(jnp.where(ok[None], _dot_nt(q, k).reshape(HPG, TQ, TQ), NEG))
    m_w = s_w[0].max(axis=-1, keepdims=True)
    for i in range(1, n_w):
        m_w = jnp.maximum(m_w, s_w[i].max(axis=-1, keepdims=True))
    acc_w = jnp.zeros((R, LANES), f32)
    for i in range(n_w):
        p = jnp.exp2((s_w[i] - m_w).astype(bf16))
        acc_w = acc_w + jnp.dot(p.reshape(R, TQ), v_w[i], preferred_element_type=f32)
    acc_w = acc_w.reshape(HPG, TQ, LANES)
    o_w = acc_w[:, :, :HEAD_DIM] / acc_w[:, :, HEAD_DIM:HEAD_DIM + 1]

    g = g_ref[0, 0]
    o_ref[0, 0] = (g[:, :, 0:1] * o_c.reshape(HPG, TQ, HEAD_DIM) + g[:, :, 1:2] * o_s
                   + g[:, :, 2:3] * o_w)


def nsa_prompt_attention(q, kc, vc, ks, vs, kw, vw, gates, TQ=256, TK=1024):
    B, G, _, T, D = q.shape
    NC = kc.shape[2]
    NCP = -(-NC // LANES) * LANES
    NS = T // SEL_BLOCK
    assert NS <= LANES and T % TK == 0 and TK % TQ == 0 and WINDOW % TQ == 0
    kc = jnp.pad(kc, ((0, 0), (0, 0), (0, NCP - NC), (0, 0))).astype(jnp.bfloat16)
    vc = jnp.pad(vc, ((0, 0), (0, 0), (0, NCP - NC), (0, 0))).astype(jnp.bfloat16)
    c_start = jnp.arange(NCP) * CMP_STRIDE
    s_start = jnp.arange(LANES) * SEL_BLOCK
    cover = ((c_start[:, None] < s_start[None, :] + SEL_BLOCK)
             & (c_start[:, None] + CMP_LEN > s_start[None, :])
             & (jnp.arange(NCP)[:, None] < NC) & (jnp.arange(LANES)[None, :] < NS)).astype(jnp.bfloat16)
    onehot = (jnp.arange(T)[:, None] // SEL_BLOCK == jnp.arange(LANES)[None, :]).astype(jnp.bfloat16)
    ks = jnp.concatenate([jnp.broadcast_to(onehot, (B, G, T, LANES)), ks], axis=-1)
    ones_col = (jnp.arange(LANES - D) == 0).astype(jnp.bfloat16)
    with_ones = lambda v: jnp.concatenate([v, jnp.broadcast_to(ones_col, (B, G, T, LANES - D))], axis=-1)
    vs, vw = with_ones(vs), with_ones(vw)
    full = lambda n, d=D: pl.BlockSpec((1, 1, n, d), lambda b, g, i: (b, g, 0, 0))
    qspec = pl.BlockSpec((1, 1, HPG, TQ, D), lambda b, g, i: (b, g, 0, i, 0))
    return pl.pallas_call(
        functools.partial(_nsa_prompt_kernel, TQ=TQ, TK=TK, NC=NC, NCP=NCP),
        grid=(B, G, T // TQ),
        in_specs=[qspec, full(NCP), full(NCP),
                  pl.BlockSpec((LANES, NCP), lambda b, g, i: (0, 0)),
                  full(T, LANES + D), full(T, LANES), full(T), full(T, LANES),
                  pl.BlockSpec((1, 1, HPG, TQ, 3), lambda b, g, i: (b, g, 0, i, 0))],
        out_specs=qspec,
        out_shape=jax.ShapeDtypeStruct((B, G, HPG, T, D), jnp.float32),
        compiler_params=pltpu.CompilerParams(
            dimension_semantics=("arbitrary", "arbitrary", "arbitrary"),
            vmem_limit_bytes=VMEM_LIMIT),
        name="nsa_prompt",
    )(q, kc, vc, cover.T, ks, vs, kw, vw, gates)


def _cmp_sample_kernel(pt_ref, *refs, n_chunk):
    pages = refs[:PAGES_PER_STEP]
    wcat_ref, c1_ref, w2_ref, b2_ref, o_ref, seq = refs[PAGES_PER_STEP:]
    j = pl.program_id(1)
    rows = PAGE_SIZE // CMP_STRIDE
    src = lax.broadcasted_iota(jnp.int32, (PAGE_SIZE, PAGE_SIZE), 0)
    src = (src % rows) * CMP_STRIDE + src // rows
    perm = _bf(jnp.where(src == lax.broadcasted_iota(jnp.int32, (PAGE_SIZE, PAGE_SIZE), 1), 1.0, 0.0))
    for i in range(PAGES_PER_STEP):
        dst = pl.multiple_of((j * PAGES_PER_STEP + i) * rows, rows)
        for kv in range(2):
            for g in range(N_KV):
                q = kv * N_KV + g
                by_pos = _dot_nt(perm, pages[i][0, kv, g])
                for p in range(CMP_STRIDE):
                    seq[q, pl.ds(dst, rows), p * HEAD_DIM:(p + 1) * HEAD_DIM] = by_pos[p * rows:(p + 1) * rows]

    @pl.when(j == pl.num_programs(1) - 1)
    def _():
        for kv in range(2):
            for g in range(N_KV):
                q = kv * N_KV + g
                acc = jnp.dot(_bf(seq[q]), wcat_ref[kv], preferred_element_type=jnp.float32)
                first = acc[:, :CMP_HIDDEN]
                second = acc[:, CMP_HIDDEN:]
                second = jnp.concatenate([second[1:], second[:1]], axis=0)
                hid = jax.nn.gelu(first + second + c1_ref[kv])
                out = jnp.dot(_bf(hid), w2_ref[kv], preferred_element_type=jnp.float32) + b2_ref[kv]
                o_ref[0, kv, g] = _bf(out)


def compress_sample(cache, page_table, pe, w1, b1, w2, b2):
    n_pool = cache.shape[0]
    DB, n_pages = page_table.shape
    assert n_pages % PAGES_PER_STEP == 0
    rows = PAGE_SIZE // CMP_STRIDE
    n_chunk = n_pages * rows
    view = jnp.transpose(cache, (0, 2, 3, 4, 1))
    w1r = w1.reshape(2, CMP_LEN, HEAD_DIM, CMP_HIDDEN)
    wcat = _bf(jnp.concatenate([w1r[:, :CMP_STRIDE], w1r[:, CMP_STRIDE:]], axis=-1))
    wcat = wcat.reshape(2, CMP_STRIDE * HEAD_DIM, 2 * CMP_HIDDEN)
    c1 = (jnp.einsum('kn,knh->kh', pe.reshape(2, CMP_LEN * HEAD_DIM), w1) + b1).reshape(2, 1, CMP_HIDDEN)
    page_spec = lambda i: pl.BlockSpec((1, 2, N_KV, HEAD_DIM, PAGE_SIZE),
                                       lambda b, j, pt: (pt[b, j * PAGES_PER_STEP + i], 0, 0, 0, 0))
    const = lambda shape: pl.BlockSpec(shape, lambda b, j, pt: (0,) * len(shape))
    grid_spec = pltpu.PrefetchScalarGridSpec(
        num_scalar_prefetch=1,
        grid=(DB, n_pages // PAGES_PER_STEP),
        in_specs=[page_spec(i) for i in range(PAGES_PER_STEP)]
        + [const((2, CMP_STRIDE * HEAD_DIM, 2 * CMP_HIDDEN)), const((2, 1, CMP_HIDDEN)),
           const((2, CMP_HIDDEN, HEAD_DIM)), const((2, 1, HEAD_DIM))],
        out_specs=pl.BlockSpec((1, 2, N_KV, n_chunk, HEAD_DIM), lambda b, j, pt: (b, 0, 0, 0, 0)),
        scratch_shapes=[pltpu.VMEM((2 * N_KV, n_chunk, CMP_STRIDE * HEAD_DIM), jnp.float32)],
    )
    return pl.pallas_call(
        functools.partial(_cmp_sample_kernel, n_chunk=n_chunk),
        grid_spec=grid_spec,
        out_shape=jax.ShapeDtypeStruct((DB, 2, N_KV, n_chunk, HEAD_DIM), jnp.bfloat16),
        compiler_params=pltpu.CompilerParams(dimension_semantics=("arbitrary", "arbitrary"),
                                             vmem_limit_bytes=VMEM_LIMIT),
        name="cmp_sample",
    )(page_table, *([view] * PAGES_PER_STEP), wcat, c1, _bf(w2), b2.reshape(2, 1, HEAD_DIM))


def _nsa_sample_kernel(pt_ref, *refs, DS, NC, past):
    pages = refs[:PAGES_PER_STEP]
    (q_ref, kvc_ref, cover_ref, new_sel_ref, win_ref, new_win_ref, g_ref, o_ref,
     qaug, m_s, l_s, acc_s, oc_s) = refs[PAGES_PER_STEP:]
    f32 = jnp.float32
    j = pl.program_id(1)
    R = HPG * DS
    t_row = lax.broadcasted_iota(jnp.int32, (R, 1), 0) % DS
    n_chunk = kvc_ref.shape[3]

    @pl.when(j == 0)
    def _():
        n_idx = lax.broadcasted_iota(jnp.int32, (1, n_chunk), 1)
        s_col = lax.broadcasted_iota(jnp.int32, (LANES, 1), 0)
        s_col_f = s_col.astype(f32)
        last_blk = past // SEL_BLOCK - 1
        for g in range(N_KV):
            q = q_ref[0, g]
            s_c = jnp.where(n_idx < NC, _dot_nt(q, kvc_ref[0, 0, g]), NEG)
            p_c = jnp.exp(s_c - jnp.max(s_c, axis=-1, keepdims=True))
            p_c = p_c / jnp.sum(p_c, axis=-1, keepdims=True)
            p_cb = _bf(p_c)
            oc_s[g] = jnp.dot(p_cb, kvc_ref[0, 1, g], preferred_element_type=f32)
            imp_rows = _dot_nt(cover_ref[...], p_cb)
            imp = imp_rows[:, 0:DS]
            for h in range(1, HPG):
                imp = imp + imp_rows[:, h * DS:(h + 1) * DS]
            forced = (s_col == 0) | (s_col == last_blk)
            score = jnp.where(forced, 1e6, imp)
            score = jnp.where(s_col <= last_blk, score, NEG)
            sel_t = jnp.zeros((LANES, DS), f32)
            for _ in range(SEL_TOP - 1):
                top = jnp.max(score, axis=0, keepdims=True)
                first = jnp.min(jnp.where(score == top, s_col_f, float(LANES)), axis=0, keepdims=True)
                hit = s_col_f == first
                sel_t = jnp.where(hit & (top > 0.5 * NEG), 1.0, sel_t)
                score = jnp.where(hit, -3e38, score)
            sel_neg = _bf((sel_t.T - 1.0) * (-NEG))
            qaug[g] = jnp.concatenate([jnp.concatenate([sel_neg] * HPG, axis=0), q], axis=1)
        m_s[...] = jnp.full(m_s.shape, NEG, f32)
        l_s[...] = jnp.zeros(l_s.shape, f32)
        acc_s[...] = jnp.zeros(acc_s.shape, f32)

    TK = PAGES_PER_STEP * PAGE_SIZE
    blk = (lax.broadcasted_iota(jnp.int32, (LANES, TK), 1) // SEL_BLOCK
           + j * (TK // SEL_BLOCK))
    onehot = _bf(jnp.where(blk == lax.broadcasted_iota(jnp.int32, (LANES, TK), 0), 1.0, 0.0))
    for g in range(N_KV):
        k_t = _bf(jnp.concatenate([pages[i][0, 0, g] for i in range(PAGES_PER_STEP)], axis=1))
        v_t = _bf(jnp.concatenate([pages[i][0, 1, g] for i in range(PAGES_PER_STEP)], axis=1))
        s = jnp.dot(qaug[g], jnp.concatenate([onehot, k_t], axis=0), preferred_element_type=f32)
        m_new = jnp.maximum(m_s[g], jnp.max(s, axis=-1, keepdims=True))
        p = jnp.exp(s - m_new)
        alpha = jnp.exp(m_s[g] - m_new)
        l_s[g] = alpha * l_s[g] + jnp.sum(p, axis=-1, keepdims=True)
        acc_s[g] = alpha * acc_s[g] + _dot_nt(p, v_t)
        m_s[g] = m_new

    @pl.when(j == pl.num_programs(1) - 1)
    def _():
        j_new = lax.broadcasted_iota(jnp.int32, (1, NEW_PAD), 1)
        ok_new = (j_new <= t_row) & (j_new < DS)
        w_idx = lax.broadcasted_iota(jnp.int32, (1, WINDOW), 1)
        ok_win = w_idx >= t_row
        for g in range(N_KV):
            q = q_ref[0, g]
            s = jnp.where(ok_new, _dot_nt(q, new_sel_ref[0, 0, g]), NEG)
            m_new = jnp.maximum(m_s[g], jnp.max(s, axis=-1, keepdims=True))
            p = jnp.where(ok_new, jnp.exp(s - m_new), 0.0)
            alpha = jnp.exp(m_s[g] - m_new)
            l_fin = alpha * l_s[g] + jnp.sum(p, axis=-1, keepdims=True)
            o_sel = (alpha * acc_s[g]
                     + jnp.dot(_bf(p), new_sel_ref[0, 1, g], preferred_element_type=f32)) / l_fin
            s_a = jnp.where(ok_win, jnp.dot(q, _bf(win_ref[0, 0, g]), preferred_element_type=f32), NEG)
            s_b = jnp.where(ok_new, _dot_nt(q, new_win_ref[0, 0, g]), NEG)
            m_w = jnp.maximum(jnp.max(s_a, axis=-1, keepdims=True), jnp.max(s_b, axis=-1, keepdims=True))
            p_a = jnp.where(ok_win, jnp.exp(s_a - m_w), 0.0)
            p_b = jnp.where(ok_new, jnp.exp(s_b - m_w), 0.0)
            l_w = jnp.sum(p_a, axis=-1, keepdims=True) + jnp.sum(p_b, axis=-1, keepdims=True)
            o_win = (_dot_nt(p_a, win_ref[0, 1, g])
                     + jnp.dot(_bf(p_b), new_win_ref[0, 1, g], preferred_element_type=f32)) / l_w
            gt = g_ref[0, g]
            o_ref[0, g] = gt[:, 0:1] * oc_s[g] + gt[:, 1:2] * o_sel + gt[:, 2:3] * o_win


def nsa_sample_attention(q, kvc, sel_cache, win_cache, page_table, new_sel, new_win, gates, past):
    DB, G, R, D = q.shape
    DS = R // HPG
    n_pool = sel_cache.shape[0]
    n_pages = page_table.shape[1]
    n_chunk = kvc.shape[3]
    NC = (past + DS) // CMP_STRIDE - 1
    assert past % SEL_BLOCK == 0 and DS < CMP_STRIDE and DS <= NEW_PAD and past // SEL_BLOCK <= LANES
    assert n_pages % PAGES_PER_STEP == 0 and win_cache.shape[1] == WINDOW and NC < n_chunk + 1
    c_start = jnp.arange(n_chunk) * CMP_STRIDE
    s_start = jnp.arange(LANES) * SEL_BLOCK
    cover_t = _bf((c_start[None, :] < s_start[:, None] + SEL_BLOCK)
                  & (c_start[None, :] + CMP_LEN > s_start[:, None])
                  & (jnp.arange(n_chunk)[None, :] < NC) & (jnp.arange(LANES)[:, None] < past // SEL_BLOCK))
    sel_view = jnp.transpose(sel_cache, (0, 2, 3, 4, 1))
    win_view = jnp.transpose(win_cache, (0, 2, 3, 4, 1))
    page_spec = lambda i: pl.BlockSpec((1, 2, G, D, PAGE_SIZE),
                                       lambda b, j, pt: (pt[b, j * PAGES_PER_STEP + i], 0, 0, 0, 0))
    per_b = lambda shape: pl.BlockSpec((1,) + shape, lambda b, j, pt: (b,) + (0,) * len(shape))
    grid_spec = pltpu.PrefetchScalarGridSpec(
        num_scalar_prefetch=1,
        grid=(DB, n_pages // PAGES_PER_STEP),
        in_specs=[page_spec(i) for i in range(PAGES_PER_STEP)]
        + [per_b((G, R, D)), per_b((2, G, n_chunk, D)),
           pl.BlockSpec((LANES, n_chunk), lambda b, j, pt: (0, 0)),
           per_b((2, G, NEW_PAD, D)), per_b((2, G, D, WINDOW)), per_b((2, G, NEW_PAD, D)), per_b((G, R, 3))],
        out_specs=per_b((G, R, D)),
        scratch_shapes=[pltpu.VMEM((G, R, LANES + D), jnp.bfloat16), pltpu.VMEM((G, R, 1), jnp.float32),
                        pltpu.VMEM((G, R, 1), jnp.float32), pltpu.VMEM((G, R, D), jnp.float32),
                        pltpu.VMEM((G, R, D), jnp.float32)],
    )
    return pl.pallas_call(
        functools.partial(_nsa_sample_kernel, DS=DS, NC=NC, past=past),
        grid_spec=grid_spec,
        out_shape=jax.ShapeDtypeStruct((DB, G, R, D), jnp.float32),
        compiler_params=pltpu.CompilerParams(dimension_semantics=("arbitrary", "arbitrary"),
                                             vmem_limit_bytes=VMEM_LIMIT),
        name="nsa_sample",
    )(page_table, *([sel_view] * PAGES_PER_STEP), q, kvc, cover_t, new_sel, win_view, new_win, gates)


def _merge_kernel(x_ref, yr_ref, ya_ref, gr_ref, ga_ref, wpa_ref, wpb_ref, wo_ref, g1_ref, b1_ref, rw_ref,
                  h_ref, lg_ref):
    m = (jax.nn.sigmoid(gr_ref[...]) * _dot(yr_ref[...], wpa_ref[...])
         + jax.nn.sigmoid(ga_ref[...]) * _dot(ya_ref[...], wpb_ref[...]))
    z = DN_ALPHA * x_ref[...] + _dot(m, wo_ref[...])
    mu = jnp.mean(z, axis=-1, keepdims=True)
    dev = z - mu
    var = jnp.mean(dev * dev, axis=-1, keepdims=True)
    h = dev * lax.rsqrt(var + LN_EPS) * g1_ref[...] + b1_ref[...]
    h_ref[...] = h
    lg_ref[...] = _dot(h, rw_ref[...])


def merge_rows(x, y_r, y_a, p_all, g0, w_pa, w_pb, w_o, ln1_g, ln1_b, router_w):
    N, D = x.shape
    W = y_r.shape[1]
    assert g0 % D == 0 and D == D_MODEL
    tm = next(t for t in (384, 256, 128, 64, 8) if N % t == 0)
    rw = jnp.pad(router_w, ((0, 0), (0, LANES - router_w.shape[1])))
    rows = lambda w, c=0: pl.BlockSpec((tm, w), lambda i, c=c: (i, c))
    const = lambda shape: pl.BlockSpec(shape, lambda i: (0,) * len(shape))
    return pl.pallas_call(
        _merge_kernel,
        grid=(N // tm,),
        in_specs=[rows(D), rows(W), rows(W), rows(D, g0 // D), rows(D, g0 // D + 1),
                  const((W, D)), const((W, D)), const((D, D)), const((1, D)), const((1, D)), const((D, LANES))],
        out_specs=[rows(D), rows(LANES)],
        out_shape=[jax.ShapeDtypeStruct((N, D), jnp.float32), jax.ShapeDtypeStruct((N, LANES), jnp.float32)],
        compiler_params=pltpu.CompilerParams(dimension_semantics=("arbitrary",), vmem_limit_bytes=VMEM_LIMIT),
        name="merge_rows",
    )(x, y_r, y_a, p_all, p_all, _bf(w_pa), _bf(w_pb), _bf(w_o), ln1_g.reshape(1, D), ln1_b.reshape(1, D), _bf(rw))


def _moe_mlp_kernel(be_ref, base_ref, nval_ref, order_ref, x_hbm, w1_ref, b1_ref, w2_ref, b2_ref, o_hbm,
                    w1s, w2s, xbuf, ybuf, in_sem, out_sem, *, BM, T):
    i = pl.program_id(0)
    n = pl.num_programs(0)
    s = i % 2

    def gather(block, sl):
        base = base_ref[block]

        def body(r, c):
            tok = order_ref[base + r] >> K_SHIFT
            pltpu.make_async_copy(x_hbm.at[tok], xbuf.at[sl, r], in_sem.at[sl]).start()
            return c
        lax.fori_loop(0, BM, body, 0, unroll=8)

    @pl.when(i == 0)
    def _():
        gather(0, 0)

    @pl.when(i + 1 < n)
    def _():
        gather(i + 1, 1 - s)

    e = be_ref[i]
    prev = be_ref[jnp.maximum(i - 1, 0)]

    @pl.when((i == 0) | (e != prev))
    def _():
        w1s[...] = w1_ref[0].astype(jnp.bfloat16)
        w2s[...] = w2_ref[0].astype(jnp.bfloat16)

    def rows_in(sl):
        return pltpu.make_async_copy(x_hbm.at[pl.ds(0, BM)], xbuf.at[sl], in_sem.at[sl])

    def rows_out(sl):
        return pltpu.make_async_copy(ybuf.at[sl], o_hbm.at[pl.ds(0, BM)], out_sem.at[sl])

    rows_in(s).wait()

    @pl.when(i >= 2)
    def _():
        rows_out(s).wait()

    x = jnp.concatenate([xbuf[s, :, c, :] for c in range(SUBLANES)], axis=1).astype(jnp.bfloat16)
    h = jnp.dot(x, w1s[...], preferred_element_type=jnp.float32) + b1_ref[0]
    glu = jnp.minimum(h[:, :D_FF], SWIGLU_LIMIT)
    lin = jnp.clip(h[:, D_FF:], -SWIGLU_LIMIT, SWIGLU_LIMIT)
    act = glu * jax.nn.sigmoid(SWIGLU_ALPHA * glu) * (lin + 1.0)
    y = jnp.dot(act.astype(jnp.bfloat16), w2s[...], preferred_element_type=jnp.float32) + b2_ref[0]
    for c in range(SUBLANES):
        ybuf[s, :, c, :] = y[:, c * LANES:(c + 1) * LANES]

    base = base_ref[i]
    nval = nval_ref[i]
    spare = TOP_K * T + i * BM - base - nval

    def scatter(r, c):
        a = order_ref[base + r]
        row = jnp.where(r < nval, (a & (TOP_K - 1)) * T + (a >> K_SHIFT), spare + r)
        pltpu.make_async_copy(ybuf.at[s, r], o_hbm.at[row], out_sem.at[s]).start()
        return c
    lax.fori_loop(0, BM, scatter, 0, unroll=8)

    @pl.when(i == n - 1)
    def _():
        rows_out(s).wait()

    @pl.when((i == n - 1) & (n >= 2))
    def _():
        rows_out(1 - s).wait()


def moe_mlp(x, order, block_e, block_base, block_nval, w1, b1, w2, b2, BM):
    T, D = x.shape
    assert D == SUBLANES * LANES
    n_blocks = block_e.shape[0]
    E = w1.shape[0]
    grid_spec = pltpu.PrefetchScalarGridSpec(
        num_scalar_prefetch=4,
        grid=(n_blocks,),
        in_specs=[pl.BlockSpec(memory_space=pl.ANY),
                  pl.BlockSpec((1, D, 2 * D_FF), lambda i, be, *_: (be[i], 0, 0)),
                  pl.BlockSpec((1, 1, 2 * D_FF), lambda i, be, *_: (be[i], 0, 0)),
                  pl.BlockSpec((1, D_FF, D), lambda i, be, *_: (be[i], 0, 0)),
                  pl.BlockSpec((1, 1, D), lambda i, be, *_: (be[i], 0, 0))],
        out_specs=pl.BlockSpec(memory_space=pl.ANY),
        scratch_shapes=[pltpu.VMEM((D, 2 * D_FF), jnp.bfloat16), pltpu.VMEM((D_FF, D), jnp.bfloat16),
                        pltpu.VMEM((2, BM, SUBLANES, LANES), jnp.float32),
                        pltpu.VMEM((2, BM, SUBLANES, LANES), jnp.float32),
                        pltpu.SemaphoreType.DMA((2,)), pltpu.SemaphoreType.DMA((2,))],
    )
    return pl.pallas_call(
        functools.partial(_moe_mlp_kernel, BM=BM, T=T),
        grid_spec=grid_spec,
        out_shape=jax.ShapeDtypeStruct((n_blocks * BM, SUBLANES, LANES), jnp.float32),
        compiler_params=pltpu.CompilerParams(dimension_semantics=("arbitrary",),
                                             vmem_limit_bytes=VMEM_LIMIT),
        name="moe_mlp",
    )(block_e, block_base, block_nval, order, x.reshape(T, SUBLANES, LANES), w1, b1.reshape(E, 1, -1), w2, b2.reshape(E, 1, -1))


def _combine_kernel(h_ref, gate_ref, *refs):
    slabs = refs[:TOP_K]
    g2_ref, b2_ref, o_ref = refs[TOP_K:]
    gate = gate_ref[...]
    f = None
    for k in range(TOP_K):
        rows = jnp.concatenate([slabs[k][:, c, :] for c in range(SUBLANES)], axis=1)
        term = gate[:, k:k + 1] * rows
        f = term if f is None else f + term
    z = DN_ALPHA * h_ref[...] + f
    mu = jnp.mean(z, axis=-1, keepdims=True)
    dev = z - mu
    var = jnp.mean(dev * dev, axis=-1, keepdims=True)
    o_ref[...] = dev * lax.rsqrt(var + LN_EPS) * g2_ref[...] + b2_ref[...]


def combine_rows(h, gate, yb, ln2_g, ln2_b):
    T, D = h.shape
    tm = next(t for t in (384, 256, 128, 64, 8) if T % t == 0)
    nb = T // tm
    slab = lambda k: pl.BlockSpec((tm, SUBLANES, LANES), lambda i, k=k: (k * nb + i, 0, 0))
    rows = lambda w: pl.BlockSpec((tm, w), lambda i: (i, 0))
    const = lambda shape: pl.BlockSpec(shape, lambda i: (0,) * len(shape))
    return pl.pallas_call(
        _combine_kernel,
        grid=(nb,),
        in_specs=[rows(D), rows(TOP_K)] + [slab(k) for k in range(TOP_K)] + [const((1, D)), const((1, D))],
        out_specs=rows(D),
        out_shape=jax.ShapeDtypeStruct((T, D), jnp.float32),
        compiler_params=pltpu.CompilerParams(dimension_semantics=("arbitrary",), vmem_limit_bytes=VMEM_LIMIT),
        name="combine_rows",
    )(h, gate, *([yb] * TOP_K), ln2_g.reshape(1, D), ln2_b.reshape(1, D))


def moe(x, logits, mlp1_w, mlp1_b, mlp2_w, mlp2_b, BM=256):
    T, D = x.shape
    top_v, top_e = lax.top_k(logits, TOP_K)
    gate = jax.nn.softmax(top_v, axis=-1)
    n_assign = T * TOP_K
    _, order = lax.sort((top_e.reshape(-1).astype(jnp.int32), jnp.arange(n_assign, dtype=jnp.int32)), num_keys=1)
    counts = jnp.sum(top_e.reshape(-1, 1) == jnp.arange(N_EXPERTS)[None, :], axis=0).astype(jnp.int32)
    padded = (counts + BM - 1) // BM * BM
    start = jnp.cumsum(counts) - counts
    pend = jnp.cumsum(padded)
    pstart = pend - padded
    n_blocks = -(-n_assign // BM) + N_EXPERTS
    row0 = jnp.arange(n_blocks, dtype=jnp.int32) * BM
    block_e = jnp.minimum(jnp.sum(pend[None, :] <= row0[:, None], axis=1), N_EXPERTS - 1).astype(jnp.int32)
    off = row0 - pstart[block_e]
    block_nval = jnp.clip(counts[block_e] - off, 0, BM).astype(jnp.int32)
    block_base = jnp.clip(start[block_e] + off, 0, n_assign).astype(jnp.int32)
    order = jnp.concatenate([order, jnp.zeros((BM,), jnp.int32)])
    yb = moe_mlp(x, order, block_e, block_base, block_nval, mlp1_w, mlp1_b, mlp2_w, mlp2_b, BM)
    return gate, yb


def rope(x, pos):
    half = ROT_DIM // 2
    inv = ROPE_THETA ** (-jnp.arange(half, dtype=jnp.float32) * 2.0 / ROT_DIM)
    ang = pos.astype(jnp.float32)[:, None] * inv
    cos, sin = jnp.cos(ang)[:, None, :], jnp.sin(ang)[:, None, :]
    x1, x2 = x[..., :half], x[..., half:ROT_DIM]
    return jnp.concatenate([x1 * cos - x2 * sin, x2 * cos + x1 * sin, x[..., ROT_DIM:]], axis=-1)


G0 = R_COLS + A_WIDTH + 6 * KV_WIDTH


def project(x2, w_in):
    w_main = jnp.concatenate([w_in[:, :G0], w_in[:, G0 + N_GATE:]], axis=1)
    w_gate = jnp.pad(w_in[:, G0:G0 + N_GATE], ((0, 0), (0, LANES - N_GATE)))
    return pallas_matmul(x2, w_main), pallas_matmul(x2, w_gate)[:, :N_GATE]


def split_projection(p, pg, B, T, pos):
    pr = p[:, :R_COLS].reshape(B, T, R_COLS)
    pa = p[:, R_COLS:G0].reshape(B, T, G0 - R_COLS)
    q = rope(pa[..., :A_WIDTH].reshape(B, T, N_HEADS, HEAD_DIM), pos)
    kvs = [pa[..., A_WIDTH + i * KV_WIDTH:A_WIDTH + (i + 1) * KV_WIDTH].reshape(B, T, N_KV, HEAD_DIM)
           for i in range(6)]
    kvs = [rope(z, pos) if i % 2 == 0 else z for i, z in enumerate(kvs)]
    gates = jax.nn.sigmoid(pg).reshape(B, T, N_KV, HPG, 3)
    return pr, q, kvs, gates


def rwkv_prep_rows(pr, shift_prev, mu, w0, w_w2, a0, w_a2, g_w2, k_k, k_a, r_k):
    B, T, _ = pr.shape
    prev = jnp.concatenate([shift_prev[:, None, :], pr[:, :-1]], axis=1)
    xm = pr + (prev - pr) * mu
    o1, o2, o3 = R_WIDTH, 2 * R_WIDTH, 3 * R_WIDTH
    o4 = o3 + LORA_W
    o5 = o4 + LORA_A
    r, k, v = xm[..., :o1], xm[..., o1:o2], xm[..., o2:o3]
    xw, xa, xg = xm[..., o3:o4], xm[..., o4:o5], xm[..., o5:]
    w_log = -jax.nn.softplus(-(w0 + jnp.tanh(xw) @ w_w2)) - 0.5
    a = jax.nn.sigmoid(a0 + xa @ w_a2)
    g = jax.nn.sigmoid(xg) @ g_w2
    heads = lambda z: z.reshape(B, T, R_HEADS, R_HEAD)
    flat = lambda z: z.reshape(B, T, R_WIDTH)
    kk = heads(k * k_k)
    kk = flat(kk / jnp.maximum(jnp.linalg.norm(kk, axis=-1, keepdims=True), 1e-12))
    k_h = k * (1.0 + (a - 1.0) * k_a)
    bonus = flat(jnp.sum(heads(r * k_h * r_k.reshape(-1)), axis=-1, keepdims=True) * heads(v))
    return r, -jnp.exp(w_log), k_h, v, -kk, kk * a, bonus, g


def rwkv_mixer(parts, wkv0, gn_g, gn_b):
    B, T, _ = parts[0].shape
    C = 64 if T % 64 == 0 else 8
    Tp = -(-T // C) * C
    padded = [jnp.pad(z, ((0, 0), (0, Tp - T), (0, 0))) for z in parts]
    y, wkv = wkv_chunked(*padded, gn_g, gn_b, wkv0, C)
    return y[:, :T], wkv


def compress(kv, pe, w1, b1, w2, b2):
    B, L = kv.shape[:2]
    n_chunk = L // CMP_STRIDE
    ch = kv[:, :n_chunk * CMP_STRIDE].reshape(B, n_chunk, CMP_STRIDE, N_KV, HEAD_DIM)
    blk = jnp.concatenate([ch[:, :-1], ch[:, 1:]], axis=2) + pe[:, None, :]
    blk = jnp.transpose(blk, (0, 3, 1, 2, 4)).reshape(B, N_KV, n_chunk - 1, CMP_LEN * HEAD_DIM)
    return jax.nn.gelu(blk @ w1 + b1) @ w2 + b2


def nsa_prompt(q, kvs, gates, pe, w1, b1, w2, b2):
    kc_raw, vc_raw, ks, vs, kw, vw = kvs
    B, T = q.shape[:2]
    kc = compress(kc_raw, pe[0], w1[0], b1[0], w2[0], b2[0])
    vc = compress(vc_raw, pe[1], w1[1], b1[1], w2[1], b2[1])
    qg = _bf(jnp.transpose(q.reshape(B, T, N_KV, HPG, HEAD_DIM), (0, 2, 3, 1, 4)) * (HEAD_DIM ** -0.5 * LOG2E))
    tk = lambda z: _bf(jnp.swapaxes(z, 1, 2))
    o = nsa_prompt_attention(qg, kc, vc, tk(ks), tk(vs), tk(kw), tk(vw),
                             jnp.transpose(gates, (0, 2, 3, 1, 4)))
    return jnp.transpose(o, (0, 3, 1, 2, 4)).reshape(B, T, A_WIDTH)


def nsa_sample(q, kvs, gates, cmp_cache, sel_cache, win_cache, page_table, pe, w1, b1, w2, b2):
    kc_new, vc_new, ks_new, vs_new, kw_new, vw_new = kvs
    DB, DS = q.shape[:2]
    past = page_table.shape[1] * PAGE_SIZE
    kvc = compress_sample(cmp_cache, page_table, pe, w1, b1, w2, b2)
    qg = _bf(jnp.transpose(q.reshape(DB, DS, N_KV, HPG, HEAD_DIM), (0, 2, 3, 1, 4)) * (HEAD_DIM ** -0.5))
    qg = qg.reshape(DB, N_KV, HPG * DS, HEAD_DIM)
    gt = jnp.transpose(gates, (0, 2, 3, 1, 4)).reshape(DB, N_KV, HPG * DS, 3)
    pack = lambda k, v: _bf(jnp.pad(jnp.transpose(jnp.stack([k, v], axis=1), (0, 1, 3, 2, 4)),
                                    ((0, 0), (0, 0), (0, 0), (0, NEW_PAD - DS), (0, 0))))
    o = nsa_sample_attention(qg, kvc, sel_cache, win_cache, page_table, pack(ks_new, vs_new),
                             pack(kw_new, vw_new), gt, past)
    o = jnp.transpose(o.reshape(DB, N_KV, HPG, DS, HEAD_DIM), (0, 3, 1, 2, 4)).reshape(DB, DS, -1)
    new_k = jnp.concatenate([win_cache[:, DS:, 0], kw_new], axis=1)
    new_v = jnp.concatenate([win_cache[:, DS:, 1], vw_new], axis=1)
    return o, jnp.stack([new_k, new_v], axis=2)


def merge_and_ffn(x, y_r, y_a, p_all, w_pa, w_pb, w_o, ln1_g, ln1_b, router_w, router_b,
                  mlp1_w, mlp1_b, mlp2_w, mlp2_b, ln2_g, ln2_b):
    h, logits = merge_rows(x, y_r, y_a, p_all, G0, w_pa, w_pb, w_o, ln1_g, ln1_b, router_w)
    gate, yb = moe(h, logits[:, :N_EXPERTS] + router_b, mlp1_w, mlp1_b, mlp2_w, mlp2_b)
    return combine_rows(h, gate, yb, ln2_g, ln2_b)


def kernel(x_prompt, x_sample, cache_cmp_kv, cache_sel_kv, cache_win_kv, state_wkv, state_shift,
           page_table, w_in, mu_shift, w0, w_w2, a0, w_a2, g_w2, k_k, k_a, r_k, gn_g, gn_b,
           cmp_pe, cmp_w1, cmp_b1, cmp_w2, cmp_b2, w_pa, w_pb, w_o, ln1_g, ln1_b,
           router_w, router_b, mlp1_w, mlp1_b, mlp2_w, mlp2_b, ln2_g, ln2_b):
    B, T, D = x_prompt.shape
    DB, DS, _ = x_sample.shape
    n_p = B * T
    past = page_table.shape[1] * PAGE_SIZE
    pos_p = jnp.arange(T)
    pos_s = past + jnp.arange(DS)
    wb_p = min(WINDOW, T)
    h_all = jnp.concatenate([x_prompt.reshape(n_p, D), x_sample.reshape(DB * DS, D)])
    cmp_p, sel_p, win_p, wkv_p, shift_p = [], [], [], [], []
    cmp_s, sel_s, win_s, wkv_s, shift_s = [], [], [], [], []
    for l in range(DEPTH):
        rwkv_w = (mu_shift[l], w0[l], w_w2[l], a0[l], w_a2[l], g_w2[l], k_k[l], k_a[l], r_k[l],
                  gn_g[l], gn_b[l])
        cmp_w = (cmp_pe[l], cmp_w1[l], cmp_b1[l], cmp_w2[l], cmp_b2[l])
        out_w = (w_pa[l], w_pb[l], w_o[l], ln1_g[l], ln1_b[l], router_w[l], router_b[l],
                 mlp1_w[l], mlp1_b[l], mlp2_w[l], mlp2_b[l], ln2_g[l], ln2_b[l])
        p_all, pg_all = project(h_all, w_in[l])
        pr, q, kvs, gates = split_projection(p_all[:n_p], pg_all[:n_p], B, T, pos_p)
        parts = rwkv_prep(p_all, n_p, T, jnp.zeros((B, R_COLS), jnp.float32), *rwkv_w[:9])
        y_r, wkv = rwkv_mixer([z.reshape(B, T, R_WIDTH) for z in parts],
                              jnp.zeros((B, R_HEADS, R_HEAD, R_HEAD), jnp.float32), *rwkv_w[9:])
        shift = p_all[T - 1:n_p:T, :R_COLS]
        y_a = nsa_prompt(q, kvs, gates, *cmp_w)
        cmp_p.append(jnp.stack([kvs[0], kvs[1]], axis=2))
        sel_p.append(jnp.stack([kvs[2], kvs[3]], axis=2))
        win_p.append(jnp.stack([kvs[4][:, T - wb_p:], kvs[5][:, T - wb_p:]], axis=2))
        wkv_p.append(wkv)
        shift_p.append(shift)
        pr, q, kvs, gates = split_projection(p_all[n_p:], pg_all[n_p:], DB, DS, pos_s)
        y_r_s, wkv = rwkv_mixer(rwkv_prep_rows(pr, state_shift[l], *rwkv_w[:9]), state_wkv[l], *rwkv_w[9:])
        shift = pr[:, -1]
        y_a_s, new_win = nsa_sample(q, kvs, gates, cache_cmp_kv[l], cache_sel_kv[l], cache_win_kv[l],
                                    page_table, *cmp_w)
        cmp_s.append(jnp.stack([kvs[0], kvs[1]], axis=2))
        sel_s.append(jnp.stack([kvs[2], kvs[3]], axis=2))
        win_s.append(new_win)
        wkv_s.append(wkv)
        shift_s.append(shift)
        rows = lambda a, b: jnp.concatenate([a.reshape(n_p, -1), b.reshape(DB * DS, -1)])
        h_all = merge_and_ffn(h_all, rows(y_r, y_r_s), rows(y_a, y_a_s), p_all, *out_w)
    hp = h_all[:n_p].reshape(B, T, D)
    hs = h_all[n_p:].reshape(DB, DS, D)
    return (hp, hs, jnp.stack(cmp_p), jnp.stack(sel_p), jnp.stack(win_p), jnp.stack(wkv_p),
            jnp.stack(shift_p), jnp.stack(cmp_s), jnp.stack(sel_s), jnp.stack(win_s),
            jnp.stack(wkv_s), jnp.stack(shift_s))
```

```python
import functools

import jax
import jax.numpy as jnp
from jax import lax
from jax.experimental import pallas as pl
from jax.experimental.pallas import tpu as pltpu

D_MODEL = 1024
DEPTH = 1
PAGE_SIZE = 128

R_HEADS = 8
R_HEAD = 64
R_WIDTH = R_HEADS * R_HEAD
LORA_W = 64
LORA_A = 64
LORA_G = 128
R_COLS = 3 * R_WIDTH + LORA_W + LORA_A + LORA_G
GN_EPS = 64e-5

N_HEADS = 8
N_KV = 2
HPG = N_HEADS // N_KV
HEAD_DIM = 64
A_WIDTH = N_HEADS * HEAD_DIM
KV_WIDTH = N_KV * HEAD_DIM
N_GATE = 3 * N_HEADS
A_COLS = A_WIDTH + 6 * KV_WIDTH + N_GATE
ROT_DIM = HEAD_DIM // 4
ROPE_THETA = 500000.0
CMP_STRIDE = 16
CMP_LEN = 2 * CMP_STRIDE
CMP_HIDDEN = 256
SEL_BLOCK = 64
SEL_TOP = 16
WINDOW = 512

N_EXPERTS = 32
TOP_K = 4
K_SHIFT = 2
D_FF = 1024
SWIGLU_LIMIT = 7.0
SWIGLU_ALPHA = 1.702

DN_ALPHA = (2 * DEPTH) ** 0.25
LN_EPS = 1e-5
NEG = -1e30
LOG2E = 1.4426950408889634

LANES = 128
SUBLANES = 8
PAGES_PER_STEP = 64
NEW_PAD = 8
VMEM_LIMIT = 56 * 1024 * 1024


def _bf(x):
    return x.astype(jnp.bfloat16)


def _dot(a, b):
    return jnp.dot(_bf(a), _bf(b), preferred_element_type=jnp.float32)


def _dot_nt(a, b):
    return lax.dot_general(_bf(a), _bf(b), (((1,), (1,)), ((), ())),
                           preferred_element_type=jnp.float32)


def _mm_kernel(x_ref, w_ref, o_ref):
    o_ref[...] = _dot(x_ref[...], w_ref[...])


def pallas_matmul(x, w):
    x, w = _bf(x), _bf(w)
    M, K = x.shape
    N = w.shape[1]
    tm = next(t for t in (512, 384, 256, 128, M) if M % t == 0)
    resident = 2 * (2 * K * N + 4 * tm * N + 2 * tm * K) <= VMEM_LIMIT - (8 << 20)
    tn = N if resident else next(t for t in (512, 256, LANES) if N % t == 0)
    return pl.pallas_call(
        _mm_kernel,
        grid=(N // tn, M // tm),
        in_specs=[pl.BlockSpec((tm, K), lambda j, i: (i, 0)),
                  pl.BlockSpec((K, tn), lambda j, i: (0, j))],
        out_specs=pl.BlockSpec((tm, tn), lambda j, i: (i, j)),
        out_shape=jax.ShapeDtypeStruct((M, N), jnp.float32),
        compiler_params=pltpu.CompilerParams(vmem_limit_bytes=VMEM_LIMIT),
        name="mm",
    )(x, w)


def _bmm(a, b):
    return lax.dot_general(_bf(a), _bf(b), (((2,), (1,)), ((0,), (0,))), preferred_element_type=jnp.float32)


def _bmm_nt(a, b):
    return lax.dot_general(_bf(a), _bf(b), (((2,), (2,)), ((0,), (0,))), preferred_element_type=jnp.float32)


def _bmm_tn(a, b):
    return lax.dot_general(_bf(a), _bf(b), (((1,), (1,)), ((0,), (0,))), preferred_element_type=jnp.float32)


def _head_sum(x, ones_bd):
    hi = _bf(x)
    rem = x - hi.astype(jnp.float32)
    mid = _bf(rem)
    lo = _bf(rem - mid.astype(jnp.float32))
    dot = lambda t: jnp.dot(t, ones_bd, preferred_element_type=jnp.float32)
    return dot(hi) + dot(mid) + dot(lo)


def head_ones():
    h = jnp.arange(R_WIDTH) // R_HEAD
    return _bf(h[:, None] == h[None, :])


def _rwkv_prep_kernel(p_ref, pb_ref, sp_ref, mu_ref, w0_ref, ww2_ref, a0_ref, wa2_ref, gw2_ref,
                      kk_ref, ka_ref, rk_ref, bd_ref,
                      r_ref, lw_ref, k_ref, v_ref, a_ref, b_ref, bonus_ref, g_ref, *, tiles_per_seq):
    i = pl.program_id(0)
    pr = p_ref[...]
    first = jnp.where(i % tiles_per_seq == 0, sp_ref[0], pb_ref[SUBLANES - 1:SUBLANES, :])
    rolled = pltpu.roll(pr, shift=1, axis=0)
    prev = jnp.where(lax.broadcasted_iota(jnp.int32, (pr.shape[0], 1), 0) == 0, first, rolled)
    xm = pr + (prev - pr) * mu_ref[...]
    o1, o2, o3 = R_WIDTH, 2 * R_WIDTH, 3 * R_WIDTH
    o4 = o3 + LORA_W
    o5 = o4 + LORA_A
    r, k, v = xm[:, :o1], xm[:, o1:o2], xm[:, o2:o3]
    xw, xa, xg = xm[:, o3:o4], xm[:, o4:o5], xm[:, o5:]
    dot = lambda x, w_ref: jnp.dot(_bf(x), _bf(w_ref[...]), preferred_element_type=jnp.float32)
    w_log = -jax.nn.softplus(-(w0_ref[...] + dot(jnp.tanh(xw), ww2_ref))) - 0.5
    a = jax.nn.sigmoid(a0_ref[...] + dot(xa, wa2_ref))
    g_ref[...] = dot(jax.nn.sigmoid(xg), gw2_ref)
    ones_bd = bd_ref[...]
    kk = k * kk_ref[...]
    kk = kk / jnp.maximum(jnp.sqrt(_head_sum(kk * kk, ones_bd)), 1e-12)
    k_h = k * (1.0 + (a - 1.0) * ka_ref[...])
    r_ref[...] = r
    lw_ref[...] = -jnp.exp(w_log)
    k_ref[...] = k_h
    v_ref[...] = v
    a_ref[...] = -kk
    b_ref[...] = kk * a
    bonus_ref[...] = _head_sum(r * k_h * rk_ref[...], ones_bd) * v


def rwkv_prep(p_all, n_rows, seq_len, shift_prev, mu, w0, w_w2, a0, w_a2, g_w2, k_k, k_a, r_k, tm=256):
    assert seq_len % tm == 0 and n_rows % seq_len == 0
    n_seq = n_rows // seq_len
    tiles_per_seq = seq_len // tm
    row = lambda z: z.reshape(1, -1)
    const = lambda shape: pl.BlockSpec(shape, lambda i: (0,) * len(shape))
    out = pl.BlockSpec((tm, R_WIDTH), lambda i: (i, 0))
    outs = pl.pallas_call(
        functools.partial(_rwkv_prep_kernel, tiles_per_seq=tiles_per_seq),
        grid=(n_rows // tm,),
        in_specs=[pl.BlockSpec((tm, R_COLS), lambda i: (i, 0)),
                  pl.BlockSpec((SUBLANES, R_COLS), lambda i: (jnp.maximum(i * (tm // SUBLANES) - 1, 0), 0)),
                  pl.BlockSpec((1, 1, R_COLS), lambda i: (i // tiles_per_seq, 0, 0)),
                  const((1, R_COLS)), const((1, R_WIDTH)), const((LORA_W, R_WIDTH)), const((1, R_WIDTH)),
                  const((LORA_A, R_WIDTH)), const((LORA_G, R_WIDTH)), const((1, R_WIDTH)), const((1, R_WIDTH)),
                  const((1, R_WIDTH)), const((R_WIDTH, R_WIDTH))],
        out_specs=[out] * 8,
        out_shape=[jax.ShapeDtypeStruct((n_rows, R_WIDTH), jnp.float32)] * 8,
        compiler_params=pltpu.CompilerParams(dimension_semantics=("arbitrary",), vmem_limit_bytes=VMEM_LIMIT),
        name="rwkv_prep",
    )(p_all, p_all, shift_prev.reshape(n_seq, 1, R_COLS), row(mu), row(w0), w_w2, row(a0), w_a2, g_w2,
      row(k_k), row(k_a), row(r_k), head_ones())
    return outs


def _wkv_chunk_kernel(r_ref, lw_ref, k_ref, v_ref, a_ref, b_ref, bonus_ref, g_ref, gng_ref, gnb_ref, bd_ref,
                      s0_ref, y_ref, s_out_ref, s_scr, *, C, H, NB):
    c = pl.program_id(1)
    D = R_HEAD

    @pl.when(c == 0)
    def _():
        s_scr[...] = s0_ref[...].reshape(NB * H, D, D)

    row = lax.broadcasted_iota(jnp.int32, (NB * H, C, C), 1)
    col = lax.broadcasted_iota(jnp.int32, (NB * H, C, C), 2)
    incl = row >= col
    strict = row > col
    ltri = jnp.where(incl, 1.0, 0.0).astype(jnp.bfloat16)
    eye = jnp.where(row == col, 1.0, 0.0).astype(jnp.float32)

    heads = lambda ref: jnp.stack([ref[n, :, h * D:(h + 1) * D] for n in range(NB) for h in range(H)])
    lw = heads(lw_ref)
    r = heads(r_ref)
    k = heads(k_ref)
    v = heads(v_ref)
    a = heads(a_ref)
    b = heads(b_ref)
    hi = _bf(lw)
    rem = lw - hi.astype(jnp.float32)
    mid = _bf(rem)
    lo = _bf(rem - mid.astype(jnp.float32))
    lp = _bmm(ltri, hi) + _bmm(ltri, mid) + _bmm(ltri, lo)
    lp_end = lp[:, C - 1:C, :]
    p_end = jnp.exp(lp_end)
    p_inv = jnp.exp(-lp)
    at = a * jnp.exp(lp - lw)
    rt = r * jnp.exp(lp)
    bt = b * p_inv
    kt = k * p_inv
    p_hat = jnp.exp(lp_end - lp)
    bh = b * p_hat
    kh = k * p_hat

    n_ab = jnp.where(strict, _bmm_nt(at, bt), 0.0)
    a_ak = jnp.where(strict, _bmm_nt(at, kt), 0.0)
    a_rb = jnp.where(incl, _bmm_nt(rt, bt), 0.0)
    a_rk = jnp.where(incl, _bmm_nt(rt, kt), 0.0)

    t_inv = eye + n_ab
    n_pow = n_ab
    span = 2
    while span < C:
        n_pow = _bmm(n_pow, n_pow)
        t_inv = _bmm(t_inv, eye + n_pow)
        span *= 2

    s = s_scr[...]
    rhs = _bmm_nt(at, s) + _bmm(a_ak, v)
    u = _bmm(t_inv, rhs)
    y = _bmm_nt(rt, s) + _bmm(a_rb, u) + _bmm(a_rk, v)
    s_new = s * p_end + _bmm_tn(u, bh) + _bmm_tn(v, kh)
    s_scr[...] = s_new

    ones_bd = bd_ref[...]
    for n in range(NB):
        yt = jnp.concatenate([y[n * H + h] for h in range(H)], axis=1)
        dev = yt - _head_sum(yt, ones_bd) * (1.0 / D)
        var = _head_sum(dev * dev, ones_bd) * (1.0 / D)
        yn = dev * lax.rsqrt(var + GN_EPS) * gng_ref[...] + gnb_ref[...]
        y_ref[n] = (yn + bonus_ref[n]) * g_ref[n]

    @pl.when(c == pl.num_programs(1) - 1)
    def _():
        s_out_ref[...] = s_new.reshape(NB, H, D, D)


def wkv_chunked(r, lw, k, v, a, b, bonus, g, gn_g, gn_b, s0, C):
    B, T, W = r.shape
    H, D = R_HEADS, R_HEAD
    assert T % C == 0
    NB = 2 if B % 2 == 0 else 1
    seq = pl.BlockSpec((NB, C, W), lambda bi, ci: (bi, ci, 0))
    st = pl.BlockSpec((NB, H, D, D), lambda bi, ci: (bi, 0, 0, 0))
    const = lambda shape: pl.BlockSpec(shape, lambda bi, ci: (0,) * len(shape))
    return pl.pallas_call(
        functools.partial(_wkv_chunk_kernel, C=C, H=H, NB=NB),
        grid=(B // NB, T // C),
        in_specs=[seq] * 8 + [const((1, W)), const((1, W)), const((W, W)), st],
        out_specs=[seq, st],
        out_shape=[jax.ShapeDtypeStruct((B, T, W), jnp.float32),
                   jax.ShapeDtypeStruct((B, H, D, D), jnp.float32)],
        scratch_shapes=[pltpu.VMEM((NB * H, D, D), jnp.float32)],
        compiler_params=pltpu.CompilerParams(dimension_semantics=("arbitrary", "arbitrary")),
        name="wkv_chunk",
    )(r, lw, k, v, a, b, bonus, g, gn_g.reshape(1, W), gn_b.reshape(1, W), head_ones(), s0)


def _nsa_prompt_kernel(q_ref, kc_ref, vc_ref, cover_ref, ks_ref, vs_ref, kw_ref, vw_ref, g_ref, o_ref,
                       *, TQ, TK, NC, NCP):
    f32 = jnp.float32
    bf16 = jnp.bfloat16
    qb = pl.program_id(2)
    R = HPG * TQ
    q = q_ref[0, 0].reshape(R, HEAD_DIM)
    t_pos = qb * TQ + lax.broadcasted_iota(jnp.int32, (TQ, 1), 0)

    n_idx = lax.broadcasted_iota(jnp.int32, (1, NCP), 1)
    c_ok = ((n_idx * CMP_STRIDE + (CMP_LEN - 1)) <= t_pos) & (n_idx < NC)
    s_c = _dot_nt(q, kc_ref[0, 0]).reshape(HPG, TQ, NCP)
    s_c = jnp.where(c_ok[None], s_c, NEG)
    m_c = jnp.max(s_c, axis=-1, keepdims=True)
    p_c = jnp.where(c_ok[None], jnp.exp2(s_c - m_c), 0.0)
    l_c = jnp.sum(p_c, axis=-1, keepdims=True)
    p_c = p_c / jnp.where(l_c > 0.0, l_c, 1.0)
    p_cb = p_c.astype(bf16)
    o_c = jnp.dot(p_cb.reshape(R, NCP), vc_ref[0, 0], preferred_element_type=f32)

    cover_t = cover_ref[...]
    imp = _dot_nt(cover_t, p_cb[0])
    for h in range(1, HPG):
        imp = imp + _dot_nt(cover_t, p_cb[h])
    s_col = lax.broadcasted_iota(jnp.int32, (LANES, 1), 0)
    t_row = qb * TQ + lax.broadcasted_iota(jnp.int32, (1, TQ), 1)
    cur = t_row // SEL_BLOCK
    forced = (s_col == 0) | (s_col == cur) | (s_col == cur - 1)
    causal = (s_col * SEL_BLOCK) <= t_row
    score = jnp.where(forced, 1e6, imp)
    score = jnp.where(causal, score, NEG)
    s_col_f = s_col.astype(f32)
    sel_t = jnp.zeros((LANES, TQ), f32)
    for _ in range(SEL_TOP):
        top = jnp.max(score, axis=0, keepdims=True)
        first = jnp.min(jnp.where(score == top, s_col_f, float(LANES)), axis=0, keepdims=True)
        hit = s_col_f == first
        sel_t = jnp.where(hit & (top > 0.5 * NEG), 1.0, sel_t)
        score = jnp.where(hit, -3e38, score)
    sel_neg = ((sel_t.T - 1.0) * (-NEG)).astype(bf16)
    q_aug = jnp.concatenate([jnp.concatenate([sel_neg] * HPG, axis=0), q], axis=1)
    key_off = lax.broadcasted_iota(jnp.int32, (1, TK), 1)

    def sel_tile(j, carry, diagonal):
        m, acc = carry
        start = pl.multiple_of(j * TK, TK)
        k = ks_ref[0, 0, pl.ds(start, TK), :]
        v = vs_ref[0, 0, pl.ds(start, TK), :]
        s = _dot_nt(q_aug, k).reshape(HPG, TQ, TK)
        if diagonal:
            s = jnp.where(((key_off + j * TK) <= t_pos)[None], s, NEG)
        m_new = jnp.maximum(m, jnp.max(s, axis=-1, keepdims=True))
        p = jnp.exp2((s - m_new).astype(bf16))
        alpha = jnp.exp2(m - m_new)
        pv = jnp.dot(p.reshape(R, TK), v, preferred_element_type=f32)
        acc = alpha * acc + pv.reshape(HPG, TQ, LANES)
        return m_new, acc

    m0 = jnp.full((HPG, TQ, 1), NEG, f32)
    a0 = jnp.zeros((HPG, TQ, LANES), f32)
    n_full = (qb * TQ) // TK
    carry = lax.fori_loop(0, n_full, functools.partial(sel_tile, diagonal=False), (m0, a0))
    _, acc_s = sel_tile(n_full, carry, True)
    o_s = acc_s[:, :, :HEAD_DIM] / acc_s[:, :, HEAD_DIM:HEAD_DIM + 1]

    n_w = WINDOW // TQ + 1
    lane_q = lax.broadcasted_iota(jnp.int32, (1, TQ), 1)
    s_w, v_w = [], []
    for i in range(n_w):
        kb = qb - (n_w - 1) + i
        start = pl.multiple_of(jnp.maximum(kb, 0) * TQ, TQ)
        k = kw_ref[0, 0, pl.ds(start, TQ), :]
        v_w.append(vw_ref[0, 0, pl.ds(start, TQ), :])
        kpos = kb * TQ + lane_q
        ok = (kpos <= t_pos) & (kpos >= t_pos - WINDOW) & (kpos >= 0)
        s_w.append(jnp.where(ok[None], _dot_nt(q, k).reshape(HPG, TQ, TQ), NEG))
    m_w = s_w[0].max(axis=-1, keepdims=True)
    for i in range(1, n_w):
        m_w = jnp.maximum(m_w, s_w[i].max(axis=-1, keepdims=True))
    acc_w = jnp.zeros((R, LANES), f32)
    for i in range(n_w):
        p = jnp.exp2((s_w[i] - m_w).astype(bf16))
        acc_w = acc_w + jnp.dot(p.reshape(R, TQ), v_w[i], preferred_element_type=f32)
    acc_w = acc_w.reshape(HPG, TQ, LANES)
    o_w = acc_w[:, :, :HEAD_DIM] / acc_w[:, :, HEAD_DIM:HEAD_DIM + 1]

    g = g_ref[0, 0]
    o_ref[0, 0] = (g[:, :, 0:1] * o_c.reshape(HPG, TQ, HEAD_DIM) + g[:, :, 1:2] * o_s
                   + g[:, :, 2:3] * o_w)


def nsa_prompt_attention(q, kc, vc, ks, vs, kw, vw, gates, TQ=256, TK=1024):
    B, G, _, T, D = q.shape
    NC = kc.shape[2]
    NCP = -(-NC // LANES) * LANES
    NS = T // SEL_BLOCK
    assert NS <= LANES and T % TK == 0 and TK % TQ == 0 and WINDOW % TQ == 0
    kc = jnp.pad(kc, ((0, 0), (0, 0), (0, NCP - NC), (0, 0))).astype(jnp.bfloat16)
    vc = jnp.pad(vc, ((0, 0), (0, 0), (0, NCP - NC), (0, 0))).astype(jnp.bfloat16)
    c_start = jnp.arange(NCP) * CMP_STRIDE
    s_start = jnp.arange(LANES) * SEL_BLOCK
    cover = ((c_start[:, None] < s_start[None, :] + SEL_BLOCK)
             & (c_start[:, None] + CMP_LEN > s_start[None, :])
             & (jnp.arange(NCP)[:, None] < NC) & (jnp.arange(LANES)[None, :] < NS)).astype(jnp.bfloat16)
    onehot = (jnp.arange(T)[:, None] // SEL_BLOCK == jnp.arange(LANES)[None, :]).astype(jnp.bfloat16)
    ks = jnp.concatenate([jnp.broadcast_to(onehot, (B, G, T, LANES)), ks], axis=-1)
    ones_col = (jnp.arange(LANES - D) == 0).astype(jnp.bfloat16)
    with_ones = lambda v: jnp.concatenate([v, jnp.broadcast_to(ones_col, (B, G, T, LANES - D))], axis=-1)
    vs, vw = with_ones(vs), with_ones(vw)
    full = lambda n, d=D: pl.BlockSpec((1, 1, n, d), lambda b, g, i: (b, g, 0, 0))
    qspec = pl.BlockSpec((1, 1, HPG, TQ, D), lambda b, g, i: (b, g, 0, i, 0))
    return pl.pallas_call(
        functools.partial(_nsa_prompt_kernel, TQ=TQ, TK=TK, NC=NC, NCP=NCP),
        grid=(B, G, T // TQ),
        in_specs=[qspec, full(NCP), full(NCP),
                  pl.BlockSpec((LANES, NCP), lambda b, g, i: (0, 0)),
                  full(T, LANES + D), full(T, LANES), full(T), full(T, LANES),
                  pl.BlockSpec((1, 1, HPG, TQ, 3), lambda b, g, i: (b, g, 0, i, 0))],
        out_specs=qspec,
        out_shape=jax.ShapeDtypeStruct((B, G, HPG, T, D), jnp.float32),
        compiler_params=pltpu.CompilerParams(
            dimension_semantics=("arbitrary", "arbitrary", "arbitrary"),
            vmem_limit_bytes=VMEM_LIMIT),
        name="nsa_prompt",
    )(q, kc, vc, cover.T, ks, vs, kw, vw, gates)


def _cmp_sample_kernel(pt_ref, *refs, n_chunk):
    pages = refs[:PAGES_PER_STEP]
    wcat_ref, c1_ref, w2_ref, b2_ref, o_ref, seq = refs[PAGES_PER_STEP:]
    j = pl.program_id(1)
    rows = PAGE_SIZE // CMP_STRIDE
    src = lax.broadcasted_iota(jnp.int32, (PAGE_SIZE, PAGE_SIZE), 0)
    src = (src % rows) * CMP_STRIDE + src // rows
    perm = _bf(jnp.where(src == lax.broadcasted_iota(jnp.int32, (PAGE_SIZE, PAGE_SIZE), 1), 1.0, 0.0))
    for i in range(PAGES_PER_STEP):
        dst = pl.multiple_of((j * PAGES_PER_STEP + i) * rows, rows)
        for kv in range(2):
            for g in range(N_KV):
                q = kv * N_KV + g
                by_pos = _dot_nt(perm, pages[i][0, kv, g])
                for p in range(CMP_STRIDE):
                    seq[q, pl.ds(dst, rows), p * HEAD_DIM:(p + 1) * HEAD_DIM] = by_pos[p * rows:(p + 1) * rows]

    @pl.when(j == pl.num_programs(1) - 1)
    def _():
        for kv in range(2):
            for g in range(N_KV):
                q = kv * N_KV + g
                acc = jnp.dot(_bf(seq[q]), wcat_ref[kv], preferred_element_type=jnp.float32)
                first = acc[:, :CMP_HIDDEN]
                second = acc[:, CMP_HIDDEN:]
                second = jnp.concatenate([second[1:], second[:1]], axis=0)
                hid = jax.nn.gelu(first + second + c1_ref[kv])
                out = jnp.dot(_bf(hid), w2_ref[kv], preferred_element_type=jnp.float32) + b2_ref[kv]
                o_ref[0, kv, g] = _bf(out)


def compress_sample(cache, page_table, pe, w1, b1, w2, b2):
    n_pool = cache.shape[0]
    DB, n_pages = page_table.shape
    assert n_pages % PAGES_PER_STEP == 0
    rows = PAGE_SIZE // CMP_STRIDE
    n_chunk = n_pages * rows
    view = jnp.transpose(cache, (0, 2, 3, 4, 1))
    w1r = w1.reshape(2, CMP_LEN, HEAD_DIM, CMP_HIDDEN)
    wcat = _bf(jnp.concatenate([w1r[:, :CMP_STRIDE], w1r[:, CMP_STRIDE:]], axis=-1))
    wcat = wcat.reshape(2, CMP_STRIDE * HEAD_DIM, 2 * CMP_HIDDEN)
    c1 = (jnp.einsum('kn,knh->kh', pe.reshape(2, CMP_LEN * HEAD_DIM), w1) + b1).reshape(2, 1, CMP_HIDDEN)
    page_spec = lambda i: pl.BlockSpec((1, 2, N_KV, HEAD_DIM, PAGE_SIZE),
                                       lambda b, j, pt: (pt[b, j * PAGES_PER_STEP + i], 0, 0, 0, 0))
    const = lambda shape: pl.BlockSpec(shape, lambda b, j, pt: (0,) * len(shape))
    grid_spec = pltpu.PrefetchScalarGridSpec(
        num_scalar_prefetch=1,
        grid=(DB, n_pages // PAGES_PER_STEP),
        in_specs=[page_spec(i) for i in range(PAGES_PER_STEP)]
        + [const((2, CMP_STRIDE * HEAD_DIM, 2 * CMP_HIDDEN)), const((2, 1, CMP_HIDDEN)),
           const((2, CMP_HIDDEN, HEAD_DIM)), const((2, 1, HEAD_DIM))],
        out_specs=pl.BlockSpec((1, 2, N_KV, n_chunk, HEAD_DIM), lambda b, j, pt: (b, 0, 0, 0, 0)),
        scratch_shapes=[pltpu.VMEM((2 * N_KV, n_chunk, CMP_STRIDE * HEAD_DIM), jnp.float32)],
    )
    return pl.pallas_call(
        functools.partial(_cmp_sample_kernel, n_chunk=n_chunk),
        grid_spec=grid_spec,
        out_shape=jax.ShapeDtypeStruct((DB, 2, N_KV, n_chunk, HEAD_DIM), jnp.bfloat16),
        compiler_params=pltpu.CompilerParams(dimension_semantics=("arbitrary", "arbitrary"),
                                             vmem_limit_bytes=VMEM_LIMIT),
        name="cmp_sample",
    )(page_table, *([view] * PAGES_PER_STEP), wcat, c1, _bf(w2), b2.reshape(2, 1, HEAD_DIM))


def _nsa_sample_kernel(pt_ref, *refs, DS, NC, past):
    pages = refs[:PAGES_PER_STEP]
    (q_ref, kvc_ref, cover_ref, new_sel_ref, win_ref, new_win_ref, g_ref, o_ref,
     qaug, m_s, l_s, acc_s, oc_s) = refs[PAGES_PER_STEP:]
    f32 = jnp.float32
    j = pl.program_id(1)
    R = HPG * DS
    t_row = lax.broadcasted_iota(jnp.int32, (R, 1), 0) % DS
    n_chunk = kvc_ref.shape[3]

    @pl.when(j == 0)
    def _():
        n_idx = lax.broadcasted_iota(jnp.int32, (1, n_chunk), 1)
        s_col = lax.broadcasted_iota(jnp.int32, (LANES, 1), 0)
        s_col_f = s_col.astype(f32)
        last_blk = past // SEL_BLOCK - 1
        for g in range(N_KV):
            q = q_ref[0, g]
            s_c = jnp.where(n_idx < NC, _dot_nt(q, kvc_ref[0, 0, g]), NEG)
            p_c = jnp.exp(s_c - jnp.max(s_c, axis=-1, keepdims=True))
            p_c = p_c / jnp.sum(p_c, axis=-1, keepdims=True)
            p_cb = _bf(p_c)
            oc_s[g] = jnp.dot(p_cb, kvc_ref[0, 1, g], preferred_element_type=f32)
            imp_rows = _dot_nt(cover_ref[...], p_cb)
            imp = imp_rows[:, 0:DS]
            for h in range(1, HPG):
                imp = imp + imp_rows[:, h * DS:(h + 1) * DS]
            forced = (s_col == 0) | (s_col == last_blk)
            score = jnp.where(forced, 1e6, imp)
            score = jnp.where(s_col <= last_blk, score, NEG)
            sel_t = jnp.zeros((LANES, DS), f32)
            for _ in range(SEL_TOP - 1):
                top = jnp.max(score, axis=0, keepdims=True)
                first = jnp.min(jnp.where(score == top, s_col_f, float(LANES)), axis=0, keepdims=True)
                hit = s_col_f == first
                sel_t = jnp.where(hit & (top > 0.5 * NEG), 1.0, sel_t)
                score = jnp.where(hit, -3e38, score)
            sel_neg = _bf((sel_t.T - 1.0) * (-NEG))
            qaug[g] = jnp.concatenate([jnp.concatenate([sel_neg] * HPG, axis=0), q], axis=1)
        m_s[...] = jnp.full(m_s.shape, NEG, f32)
        l_s[...] = jnp.zeros(l_s.shape, f32)
        acc_s[...] = jnp.zeros(acc_s.shape, f32)

    TK = PAGES_PER_STEP * PAGE_SIZE
    blk = (lax.broadcasted_iota(jnp.int32, (LANES, TK), 1) // SEL_BLOCK
           + j * (TK // SEL_BLOCK))
    onehot = _bf(jnp.where(blk == lax.broadcasted_iota(jnp.int32, (LANES, TK), 0), 1.0, 0.0))
    for g in range(N_KV):
        k_t = _bf(jnp.concatenate([pages[i][0, 0, g] for i in range(PAGES_PER_STEP)], axis=1))
        v_t = _bf(jnp.concatenate([pages[i][0, 1, g] for i in range(PAGES_PER_STEP)], axis=1))
        s = jnp.dot(qaug[g], jnp.concatenate([onehot, k_t], axis=0), preferred_element_type=f32)
        m_new = jnp.maximum(m_s[g], jnp.max(s, axis=-1, keepdims=True))
        p = jnp.exp(s - m_new)
        alpha = jnp.exp(m_s[g] - m_new)
        l_s[g] = alpha * l_s[g] + jnp.sum(p, axis=-1, keepdims=True)
        acc_s[g] = alpha * acc_s[g] + _dot_nt(p, v_t)
        m_s[g] = m_new

    @pl.when(j == pl.num_programs(1) - 1)
    def _():
        j_new = lax.broadcasted_iota(jnp.int32, (1, NEW_PAD), 1)
        ok_new = (j_new <= t_row) & (j_new < DS)
        w_idx = lax.broadcasted_iota(jnp.int32, (1, WINDOW), 1)
        ok_win = w_idx >= t_row
        for g in range(N_KV):
            q = q_ref[0, g]
            s = jnp.where(ok_new, _dot_nt(q, new_sel_ref[0, 0, g]), NEG)
            m_new = jnp.maximum(m_s[g], jnp.max(s, axis=-1, keepdims=True))
            p = jnp.where(ok_new, jnp.exp(s - m_new), 0.0)
            alpha = jnp.exp(m_s[g] - m_new)
            l_fin = alpha * l_s[g] + jnp.sum(p, axis=-1, keepdims=True)
            o_sel = (alpha * acc_s[g]
                     + jnp.dot(_bf(p), new_sel_ref[0, 1, g], preferred_element_type=f32)) / l_fin
            s_a = jnp.where(ok_win, jnp.dot(q, _bf(win_ref[0, 0, g]), preferred_element_type=f32), NEG)
            s_b = jnp.where(ok_new, _dot_nt(q, new_win_ref[0, 0, g]), NEG)
            m_w = jnp.maximum(jnp.max(s_a, axis=-1, keepdims=True), jnp.max(s_b, axis=-1, keepdims=True))
            p_a = jnp.where(ok_win, jnp.exp(s_a - m_w), 0.0)
            p_b = jnp.where(ok_new, jnp.exp(s_b - m_w), 0.0)
            l_w = jnp.sum(p_a, axis=-1, keepdims=True) + jnp.sum(p_b, axis=-1, keepdims=True)
            o_win = (_dot_nt(p_a, win_ref[0, 1, g])
                     + jnp.dot(_bf(p_b), new_win_ref[0, 1, g], preferred_element_type=f32)) / l_w
            gt = g_ref[0, g]
            o_ref[0, g] = gt[:, 0:1] * oc_s[g] + gt[:, 1:2] * o_sel + gt[:, 2:3] * o_win


def nsa_sample_attention(q, kvc, sel_cache, win_cache, page_table, new_sel, new_win, gates, past):
    DB, G, R, D = q.shape
    DS = R // HPG
    n_pool = sel_cache.shape[0]
    n_pages = page_table.shape[1]
    n_chunk = kvc.shape[3]
    NC = (past + DS) // CMP_STRIDE - 1
    assert past % SEL_BLOCK == 0 and DS < CMP_STRIDE and DS <= NEW_PAD and past // SEL_BLOCK <= LANES
    assert n_pages % PAGES_PER_STEP == 0 and win_cache.shape[1] == WINDOW and NC < n_chunk + 1
    c_start = jnp.arange(n_chunk) * CMP_STRIDE
    s_start = jnp.arange(LANES) * SEL_BLOCK
    cover_t = _bf((c_start[None, :] < s_start[:, None] + SEL_BLOCK)
                  & (c_start[None, :] + CMP_LEN > s_start[:, None])
                  & (jnp.arange(n_chunk)[None, :] < NC) & (jnp.arange(LANES)[:, None] < past // SEL_BLOCK))
    sel_view = jnp.transpose(sel_cache, (0, 2, 3, 4, 1))
    win_view = jnp.transpose(win_cache, (0, 2, 3, 4, 1))
    page_spec = lambda i: pl.BlockSpec((1, 2, G, D, PAGE_SIZE),
                                       lambda b, j, pt: (pt[b, j * PAGES_PER_STEP + i], 0, 0, 0, 0))
    per_b = lambda shape: pl.BlockSpec((1,) + shape, lambda b, j, pt: (b,) + (0,) * len(shape))
    grid_spec = pltpu.PrefetchScalarGridSpec(
        num_scalar_prefetch=1,
        grid=(DB, n_pages // PAGES_PER_STEP),
        in_specs=[page_spec(i) for i in range(PAGES_PER_STEP)]
        + [per_b((G, R, D)), per_b((2, G, n_chunk, D)),
           pl.BlockSpec((LANES, n_chunk), lambda b, j, pt: (0, 0)),
           per_b((2, G, NEW_PAD, D)), per_b((2, G, D, WINDOW)), per_b((2, G, NEW_PAD, D)), per_b((G, R, 3))],
        out_specs=per_b((G, R, D)),
        scratch_shapes=[pltpu.VMEM((G, R, LANES + D), jnp.bfloat16), pltpu.VMEM((G, R, 1), jnp.float32),
                        pltpu.VMEM((G, R, 1), jnp.float32), pltpu.VMEM((G, R, D), jnp.float32),
                        pltpu.VMEM((G, R, D), jnp.float32)],
    )
    return pl.pallas_call(
        functools.partial(_nsa_sample_kernel, DS=DS, NC=NC, past=past),
        grid_spec=grid_spec,
        out_shape=jax.ShapeDtypeStruct((DB, G, R, D), jnp.float32),
        compiler_params=pltpu.CompilerParams(dimension_semantics=("arbitrary", "arbitrary"),
                                             vmem_limit_bytes=VMEM_LIMIT),
        name="nsa_sample",
    )(page_table, *([sel_view] * PAGES_PER_STEP), q, kvc, cover_t, new_sel, win_view, new_win, gates)


def _merge_kernel(x_ref, yr_ref, ya_ref, gr_ref, ga_ref, wpa_ref, wpb_ref, wo_ref, g1_ref, b1_ref, rw_ref,
                  h_ref, lg_ref):
    m = (jax.nn.sigmoid(gr_ref[...]) * _dot(yr_ref[...], wpa_ref[...])
         + jax.nn.sigmoid(ga_ref[...]) * _dot(ya_ref[...], wpb_ref[...]))
    z = DN_ALPHA * x_ref[...] + _dot(m, wo_ref[...])
    mu = jnp.mean(z, axis=-1, keepdims=True)
    dev = z - mu
    var = jnp.mean(dev * dev, axis=-1, keepdims=True)
    h = dev * lax.rsqrt(var + LN_EPS) * g1_ref[...] + b1_ref[...]
    h_ref[...] = h
    lg_ref[...] = _dot(h, rw_ref[...])


def merge_rows(x, y_r, y_a, p_all, g0, w_pa, w_pb, w_o, ln1_g, ln1_b, router_w):
    N, D = x.shape
    W = y_r.shape[1]
    assert g0 % D == 0 and D == D_MODEL
    tm = next(t for t in (384, 256, 128, 64, 8) if N % t == 0)
    rw = jnp.pad(router_w, ((0, 0), (0, LANES - router_w.shape[1])))
    rows = lambda w, c=0: pl.BlockSpec((tm, w), lambda i, c=c: (i, c))
    const = lambda shape: pl.BlockSpec(shape, lambda i: (0,) * len(shape))
    return pl.pallas_call(
        _merge_kernel,
        grid=(N // tm,),
        in_specs=[rows(D), rows(W), rows(W), rows(D, g0 // D), rows(D, g0 // D + 1),
                  const((W, D)), const((W, D)), const((D, D)), const((1, D)), const((1, D)), const((D, LANES))],
        out_specs=[rows(D), rows(LANES)],
        out_shape=[jax.ShapeDtypeStruct((N, D), jnp.float32), jax.ShapeDtypeStruct((N, LANES), jnp.float32)],
        compiler_params=pltpu.CompilerParams(dimension_semantics=("arbitrary",), vmem_limit_bytes=VMEM_LIMIT),
        name="merge_rows",
    )(x, y_r, y_a, p_all, p_all, _bf(w_pa), _bf(w_pb), _bf(w_o), ln1_g.reshape(1, D), ln1_b.reshape(1, D), _bf(rw))


def _moe_mlp_kernel(be_ref, base_ref, nval_ref, order_ref, x_hbm, w1_ref, b1_ref, w2_ref, b2_ref, o_hbm,
                    w1s, w2s, xbuf, ybuf, in_sem, out_sem, *, BM, T):
    i = pl.program_id(0)
    n = pl.num_programs(0)
    s = i % 2

    def gather(block, sl):
        base = base_ref[block]

        def body(r, c):
            tok = order_ref[base + r] >> K_SHIFT
            pltpu.make_async_copy(x_hbm.at[tok], xbuf.at[sl, r], in_sem.at[sl]).start()
            return c
        lax.fori_loop(0, BM, body, 0, unroll=8)

    @pl.when(i == 0)
    def _():
        gather(0, 0)

    @pl.when(i + 1 < n)
    def _():
        gather(i + 1, 1 - s)

    e = be_ref[i]
    prev = be_ref[jnp.maximum(i - 1, 0)]

    @pl.when((i == 0) | (e != prev))
    def _():
        w1s[...] = w1_ref[0].astype(jnp.bfloat16)
        w2s[...] = w2_ref[0].astype(jnp.bfloat16)

    def rows_in(sl):
        return pltpu.make_async_copy(x_hbm.at[pl.ds(0, BM)], xbuf.at[sl], in_sem.at[sl])

    def rows_out(sl):
        return pltpu.make_async_copy(ybuf.at[sl], o_hbm.at[pl.ds(0, BM)], out_sem.at[sl])

    rows_in(s).wait()

    @pl.when(i >= 2)
    def _():
        rows_out(s).wait()

    x = jnp.concatenate([xbuf[s, :, c, :] for c in range(SUBLANES)], axis=1).astype(jnp.bfloat16)
    h = jnp.dot(x, w1s[...], preferred_element_type=jnp.float32) + b1_ref[0]
    glu = jnp.minimum(h[:, :D_FF], SWIGLU_LIMIT)
    lin = jnp.clip(h[:, D_FF:], -SWIGLU_LIMIT, SWIGLU_LIMIT)
    act = glu * jax.nn.sigmoid(SWIGLU_ALPHA * glu) * (lin + 1.0)
    y = jnp.dot(act.astype(jnp.bfloat16), w2s[...], preferred_element_type=jnp.float32) + b2_ref[0]
    for c in range(SUBLANES):
        ybuf[s, :, c, :] = y[:, c * LANES:(c + 1) * LANES]

    base = base_ref[i]
    nval = nval_ref[i]
    spare = TOP_K * T + i * BM - base - nval

    def scatter(r, c):
        a = order_ref[base + r]
        row = jnp.where(r < nval, (a & (TOP_K - 1)) * T + (a >> K_SHIFT), spare + r)
        pltpu.make_async_copy(ybuf.at[s, r], o_hbm.at[row], out_sem.at[s]).start()
        return c
    lax.fori_loop(0, BM, scatter, 0, unroll=8)

    @pl.when(i == n - 1)
    def _():
        rows_out(s).wait()

    @pl.when((i == n - 1) & (n >= 2))
    def _():
        rows_out(1 - s).wait()


def moe_mlp(x, order, block_e, block_base, block_nval, w1, b1, w2, b2, BM):
    T, D = x.shape
    assert D == SUBLANES * LANES
    n_blocks = block_e.shape[0]
    E = w1.shape[0]
    grid_spec = pltpu.PrefetchScalarGridSpec(
        num_scalar_prefetch=4,
        grid=(n_blocks,),
        in_specs=[pl.BlockSpec(memory_space=pl.ANY),
                  pl.BlockSpec((1, D, 2 * D_FF), lambda i, be, *_: (be[i], 0, 0)),
                  pl.BlockSpec((1, 1, 2 * D_FF), lambda i, be, *_: (be[i], 0, 0)),
                  pl.BlockSpec((1, D_FF, D), lambda i, be, *_: (be[i], 0, 0)),
                  pl.BlockSpec((1, 1, D), lambda i, be, *_: (be[i], 0, 0))],
        out_specs=pl.BlockSpec(memory_space=pl.ANY),
        scratch_shapes=[pltpu.VMEM((D, 2 * D_FF), jnp.bfloat16), pltpu.VMEM((D_FF, D), jnp.bfloat16),
                        pltpu.VMEM((2, BM, SUBLANES, LANES), jnp.float32),
                        pltpu.VMEM((2, BM, SUBLANES, LANES), jnp.float32),
                        pltpu.SemaphoreType.DMA((2,)), pltpu.SemaphoreType.DMA((2,))],
    )
    return pl.pallas_call(
        functools.partial(_moe_mlp_kernel, BM=BM, T=T),
        grid_spec=grid_spec,
        out_shape=jax.ShapeDtypeStruct((n_blocks * BM, SUBLANES, LANES), jnp.float32),
        compiler_params=pltpu.CompilerParams(dimension_semantics=("arbitrary",),
                                             vmem_limit_bytes=VMEM_LIMIT),
        name="moe_mlp",
    )(block_e, block_base, block_nval, order, x.reshape(T, SUBLANES, LANES), w1, b1.reshape(E, 1, -1), w2, b2.reshape(E, 1, -1))


def _combine_kernel(h_ref, gate_ref, *refs):
    slabs = refs[:TOP_K]
    g2_ref, b2_ref, o_ref = refs[TOP_K:]
    gate = gate_ref[...]
    f = None
    for k in range(TOP_K):
        rows = jnp.concatenate([slabs[k][:, c, :] for c in range(SUBLANES)], axis=1)
        term = gate[:, k:k + 1] * rows
        f = term if f is None else f + term
    z = DN_ALPHA * h_ref[...] + f
    mu = jnp.mean(z, axis=-1, keepdims=True)
    dev = z - mu
    var = jnp.mean(dev * dev, axis=-1, keepdims=True)
    o_ref[...] = dev * lax.rsqrt(var + LN_EPS) * g2_ref[...] + b2_ref[...]


def combine_rows(h, gate, yb, ln2_g, ln2_b):
    T, D = h.shape
    tm = next(t for t in (384, 256, 128, 64, 8) if T % t == 0)
    nb = T // tm
    slab = lambda k: pl.BlockSpec((tm, SUBLANES, LANES), lambda i, k=k: (k * nb + i, 0, 0))
    rows = lambda w: pl.BlockSpec((tm, w), lambda i: (i, 0))
    const = lambda shape: pl.BlockSpec(shape, lambda i: (0,) * len(shape))
    return pl.pallas_call(
        _combine_kernel,
        grid=(nb,),
        in_specs=[rows(D), rows(TOP_K)] + [slab(k) for k in range(TOP_K)] + [const((1, D)), const((1, D))],
        out_specs=rows(D),
        out_shape=jax.ShapeDtypeStruct((T, D), jnp.float32),
        compiler_params=pltpu.CompilerParams(dimension_semantics=("arbitrary",), vmem_limit_bytes=VMEM_LIMIT),
        name="combine_rows",
    )(h, gate, *([yb] * TOP_K), ln2_g.reshape(1, D), ln2_b.reshape(1, D))


def moe(x, logits, mlp1_w, mlp1_b, mlp2_w, mlp2_b, BM=256):
    T, D = x.shape
    top_v, top_e = lax.top_k(logits, TOP_K)
    gate = jax.nn.softmax(top_v, axis=-1)
    n_assign = T * TOP_K
    _, order = lax.sort((top_e.reshape(-1).astype(jnp.int32), jnp.arange(n_assign, dtype=jnp.int32)), num_keys=1)
    counts = jnp.sum(top_e.reshape(-1, 1) == jnp.arange(N_EXPERTS)[None, :], axis=0).astype(jnp.int32)
    padded = (counts + BM - 1) // BM * BM
    start = jnp.cumsum(counts) - counts
    pend = jnp.cumsum(padded)
    pstart = pend - padded
    n_blocks = -(-n_assign // BM) + N_EXPERTS
    row0 = jnp.arange(n_blocks, dtype=jnp.int32) * BM
    block_e = jnp.minimum(jnp.sum(pend[None, :] <= row0[:, None], axis=1), N_EXPERTS - 1).astype(jnp.int32)
    off = row0 - pstart[block_e]
    block_nval = jnp.clip(counts[block_e] - off, 0, BM).astype(jnp.int32)
    block_base = jnp.clip(start[block_e] + off, 0, n_assign).astype(jnp.int32)
    order = jnp.concatenate([order, jnp.zeros((BM,), jnp.int32)])
    yb = moe_mlp(x, order, block_e, block_base, block_nval, mlp1_w, mlp1_b, mlp2_w, mlp2_b, BM)
    return gate, yb


def rope(x, pos):
    half = ROT_DIM // 2
    inv = ROPE_THETA ** (-jnp.arange(half, dtype=jnp.float32) * 2.0 / ROT_DIM)
    ang = pos.astype(jnp.float32)[:, None] * inv
    cos, sin = jnp.cos(ang)[:, None, :], jnp.sin(ang)[:, None, :]
    x1, x2 = x[..., :half], x[..., half:ROT_DIM]
    return jnp.concatenate([x1 * cos - x2 * sin, x2 * cos + x1 * sin, x[..., ROT_DIM:]], axis=-1)


G0 = R_COLS + A_WIDTH + 6 * KV_WIDTH


def project(x2, w_in):
    w_main = jnp.concatenate([w_in[:, :G0], w_in[:, G0 + N_GATE:]], axis=1)
    w_gate = jnp.pad(w_in[:, G0:G0 + N_GATE], ((0, 0), (0, LANES - N_GATE)))
    return pallas_matmul(x2, w_main), pallas_matmul(x2, w_gate)[:, :N_GATE]


def split_projection(p, pg, B, T, pos):
    pr = p[:, :R_COLS].reshape(B, T, R_COLS)
    pa = p[:, R_COLS:G0].reshape(B, T, G0 - R_COLS)
    q = rope(pa[..., :A_WIDTH].reshape(B, T, N_HEADS, HEAD_DIM), pos)
    kvs = [pa[..., A_WIDTH + i * KV_WIDTH:A_WIDTH + (i + 1) * KV_WIDTH].reshape(B, T, N_KV, HEAD_DIM)
           for i in range(6)]
    kvs = [rope(z, pos) if i % 2 == 0 else z for i, z in enumerate(kvs)]
    gates = jax.nn.sigmoid(pg).reshape(B, T, N_KV, HPG, 3)
    return pr, q, kvs, gates


def rwkv_prep_rows(pr, shift_prev, mu, w0, w_w2, a0, w_a2, g_w2, k_k, k_a, r_k):
    B, T, _ = pr.shape
    prev = jnp.concatenate([shift_prev[:, None, :], pr[:, :-1]], axis=1)
    xm = pr + (prev - pr) * mu
    o1, o2, o3 = R_WIDTH, 2 * R_WIDTH, 3 * R_WIDTH
    o4 = o3 + LORA_W
    o5 = o4 + LORA_A
    r, k, v = xm[..., :o1], xm[..., o1:o2], xm[..., o2:o3]
    xw, xa, xg = xm[..., o3:o4], xm[..., o4:o5], xm[..., o5:]
    w_log = -jax.nn.softplus(-(w0 + jnp.tanh(xw) @ w_w2)) - 0.5
    a = jax.nn.sigmoid(a0 + xa @ w_a2)
    g = jax.nn.sigmoid(xg) @ g_w2
    heads = lambda z: z.reshape(B, T, R_HEADS, R_HEAD)
    flat = lambda z: z.reshape(B, T, R_WIDTH)
    kk = heads(k * k_k)
    kk = flat(kk / jnp.maximum(jnp.linalg.norm(kk, axis=-1, keepdims=True), 1e-12))
    k_h = k * (1.0 + (a - 1.0) * k_a)
    bonus = flat(jnp.sum(heads(r * k_h * r_k.reshape(-1)), axis=-1, keepdims=True) * heads(v))
    return r, -jnp.exp(w_log), k_h, v, -kk, kk * a, bonus, g


def rwkv_mixer(parts, wkv0, gn_g, gn_b):
    B, T, _ = parts[0].shape
    C = 64 if T % 64 == 0 else 8
    Tp = -(-T // C) * C
    padded = [jnp.pad(z, ((0, 0), (0, Tp - T), (0, 0))) for z in parts]
    y, wkv = wkv_chunked(*padded, gn_g, gn_b, wkv0, C)
    return y[:, :T], wkv


def compress(kv, pe, w1, b1, w2, b2):
    B, L = kv.shape[:2]
    n_chunk = L // CMP_STRIDE
    ch = kv[:, :n_chunk * CMP_STRIDE].reshape(B, n_chunk, CMP_STRIDE, N_KV, HEAD_DIM)
    blk = jnp.concatenate([ch[:, :-1], ch[:, 1:]], axis=2) + pe[:, None, :]
    blk = jnp.transpose(blk, (0, 3, 1, 2, 4)).reshape(B, N_KV, n_chunk - 1, CMP_LEN * HEAD_DIM)
    return jax.nn.gelu(blk @ w1 + b1) @ w2 + b2


def nsa_prompt(q, kvs, gates, pe, w1, b1, w2, b2):
    kc_raw, vc_raw, ks, vs, kw, vw = kvs
    B, T = q.shape[:2]
    kc = compress(kc_raw, pe[0], w1[0], b1[0], w2[0], b2[0])
    vc = compress(vc_raw, pe[1], w1[1], b1[1], w2[1], b2[1])
    qg = _bf(jnp.transpose(q.reshape(B, T, N_KV, HPG, HEAD_DIM), (0, 2, 3, 1, 4)) * (HEAD_DIM ** -0.5 * LOG2E))
    tk = lambda z: _bf(jnp.swapaxes(z, 1, 2))
    o = nsa_prompt_attention(qg, kc, vc, tk(ks), tk(vs), tk(kw), tk(vw),
                             jnp.transpose(gates, (0, 2, 3, 1, 4)))
    return jnp.transpose(o, (0, 3, 1, 2, 4)).reshape(B, T, A_WIDTH)


def nsa_sample(q, kvs, gates, cmp_cache, sel_cache, win_cache, page_table, pe, w1, b1, w2, b2):
    kc_new, vc_new, ks_new, vs_new, kw_new, vw_new = kvs
    DB, DS = q.shape[:2]
    past = page_table.shape[1] * PAGE_SIZE
    kvc = compress_sample(cmp_cache, page_table, pe, w1, b1, w2, b2)
    qg = _bf(jnp.transpose(q.reshape(DB, DS, N_KV, HPG, HEAD_DIM), (0, 2, 3, 1, 4)) * (HEAD_DIM ** -0.5))
    qg = qg.reshape(DB, N_KV, HPG * DS, HEAD_DIM)
    gt = jnp.transpose(gates, (0, 2, 3, 1, 4)).reshape(DB, N_KV, HPG * DS, 3)
    pack = lambda k, v: _bf(jnp.pad(jnp.transpose(jnp.stack([k, v], axis=1), (0, 1, 3, 2, 4)),
                                    ((0, 0), (0, 0), (0, 0), (0, NEW_PAD - DS), (0, 0))))
    o = nsa_sample_attention(qg, kvc, sel_cache, win_cache, page_table, pack(ks_new, vs_new),
                             pack(kw_new, vw_new), gt, past)
    o = jnp.transpose(o.reshape(DB, N_KV, HPG, DS, HEAD_DIM), (0, 3, 1, 2, 4)).reshape(DB, DS, -1)
    new_k = jnp.concatenate([win_cache[:, DS:, 0], kw_new], axis=1)
    new_v = jnp.concatenate([win_cache[:, DS:, 1], vw_new], axis=1)
    return o, jnp.stack([new_k, new_v], axis=2)


def merge_and_ffn(x, y_r, y_a, p_all, w_pa, w_pb, w_o, ln1_g, ln1_b, router_w, router_b,
                  mlp1_w, mlp1_b, mlp2_w, mlp2_b, ln2_g, ln2_b):
    h, logits = merge_rows(x, y_r, y_a, p_all, G0, w_pa, w_pb, w_o, ln1_g, ln1_b, router_w)
    gate, yb = moe(h, logits[:, :N_EXPERTS] + router_b, mlp1_w, mlp1_b, mlp2_w, mlp2_b)
    return combine_rows(h, gate, yb, ln2_g, ln2_b)


def kernel(x_prompt, x_sample, cache_cmp_kv, cache_sel_kv, cache_win_kv, state_wkv, state_shift,
           page_table, w_in, mu_shift, w0, w_w2, a0, w_a2, g_w2, k_k, k_a, r_k, gn_g, gn_b,
           cmp_pe, cmp_w1, cmp_b1, cmp_w2, cmp_b2, w_pa, w_pb, w_o, ln1_g, ln1_b,
           router_w, router_b, mlp1_w, mlp1_b, mlp2_w, mlp2_b, ln2_g, ln2_b):
    B, T, D = x_prompt.shape
    DB, DS, _ = x_sample.shape
    n_p = B * T
    past = page_table.shape[1] * PAGE_SIZE
    pos_p = jnp.arange(T)
    pos_s = past + jnp.arange(DS)
    wb_p = min(WINDOW, T)
    h_all = jnp.concatenate([x_prompt.reshape(n_p, D), x_sample.reshape(DB * DS, D)])
    cmp_p, sel_p, win_p, wkv_p, shift_p = [], [], [], [], []
    cmp_s, sel_s, win_s, wkv_s, shift_s = [], [], [], [], []
    for l in range(DEPTH):
        rwkv_w = (mu_shift[l], w0[l], w_w2[l], a0[l], w_a2[l], g_w2[l], k_k[l], k_a[l], r_k[l],
                  gn_g[l], gn_b[l])
        cmp_w = (cmp_pe[l], cmp_w1[l], cmp_b1[l], cmp_w2[l], cmp_b2[l])
        out_w = (w_pa[l], w_pb[l], w_o[l], ln1_g[l], ln1_b[l], router_w[l], router_b[l],
                 mlp1_w[l], mlp1_b[l], mlp2_w[l], mlp2_b[l], ln2_g[l], ln2_b[l])
        p_all, pg_all = project(h_all, w_in[l])
        pr, q, kvs, gates = split_projection(p_all[:n_p], pg_all[:n_p], B, T, pos_p)
        parts = rwkv_prep(p_all, n_p, T, jnp.zeros((B, R_COLS), jnp.float32), *rwkv_w[:9])
        y_r, wkv = rwkv_mixer([z.reshape(B, T, R_WIDTH) for z in parts],
                              jnp.zeros((B, R_HEADS, R_HEAD, R_HEAD), jnp.float32), *rwkv_w[9:])
        shift = p_all[T - 1:n_p:T, :R_COLS]
        y_a = nsa_prompt(q, kvs, gates, *cmp_w)
        cmp_p.append(jnp.stack([kvs[0], kvs[1]], axis=2))
        sel_p.append(jnp.stack([kvs[2], kvs[3]], axis=2))
        win_p.append(jnp.stack([kvs[4][:, T - wb_p:], kvs[5][:, T - wb_p:]], axis=2))
        wkv_p.append(wkv)
        shift_p.append(shift)
        pr, q, kvs, gates = split_projection(p_all[n_p:], pg_all[n_p:], DB, DS, pos_s)
        y_r_s, wkv = rwkv_mixer(rwkv_prep_rows(pr, state_shift[l], *rwkv_w[:9]), state_wkv[l], *rwkv_w[9:])
        shift = pr[:, -1]
        y_a_s, new_win = nsa_sample(q, kvs, gates, cache_cmp_kv[l], cache_sel_kv[l], cache_win_kv[l],
                                    page_table, *cmp_w)
        cmp_s.append(jnp.stack([kvs[0], kvs[1]], axis=2))
        sel_s.append(jnp.stack([kvs[2], kvs[3]], axis=2))
        win_s.append(new_win)
        wkv_s.append(wkv)
        shift_s.append(shift)
        rows = lambda a, b: jnp.concatenate([a.reshape(n_p, -1), b.reshape(DB * DS, -1)])
        h_all = merge_and_ffn(h_all, rows(y_r, y_r_s), rows(y_a, y_a_s), p_all, *out_w)
    hp = h_all[:n_p].reshape(B, T, D)
    hs = h_all[n_p:].reshape(DB, DS, D)
    return (hp, hs, jnp.stack(cmp_p), jnp.stack(sel_p), jnp.stack(win_p), jnp.stack(wkv_p),
            jnp.stack(shift_p), jnp.stack(cmp_s), jnp.stack(sel_s), jnp.stack(win_s),
            jnp.stack(wkv_s), jnp.stack(shift_s))
```
